```python
import jax, jax.numpy as jnp
from jax import lax
import numpy as np

D_MODEL = 1024
BATCH = 16
SEQ = 2048
DEPTH = 1
DEC_BATCH = 8
DEC_SEQ = 64
PAST_LEN = 1024

CHUNK = 64
POOL_WIDTH = D_MODEL
POOL_WINDOWS = (2, 4, 8, 16)
N_POOL_GROUPS = 4
POOL_GROUP_DIM = POOL_WIDTH // N_POOL_GROUPS
POOL_HIST = max(POOL_WINDOWS) - 1
LRU_WIDTH = D_MODEL
LRU_HEADS = 16
LRU_HEAD_DIM = LRU_WIDTH // LRU_HEADS
CONV_WIDTH = 4
CONV_HIST = CONV_WIDTH - 1
LRU_C = 8.0
N_BRANCHES = 2
IN_COLS = POOL_WIDTH + 2 * LRU_WIDTH + N_BRANCHES * D_MODEL
N_EXPERTS = 32
TOP_K = 4
D_FF = D_MODEL
SWIGLU_LIMIT = 7.0
SWIGLU_ALPHA = 1.702
NORM_EPS = 1e-6

kernel_name = 'hybrid_pool_rglru_moe_stream_step'


def rms_norm(x, g):
    xf = x.astype(jnp.float32)
    y = xf * lax.rsqrt(jnp.mean(xf * xf, axis=-1, keepdims=True) + NORM_EPS)
    return (y * g.astype(jnp.float32)).astype(x.dtype)


def pool_mixer(u, hist, start_pos, w_grp, scale):
    bsz, t_len, _ = u.shape
    full = jnp.concatenate([hist.astype(u.dtype), u], axis=1)
    cs = jnp.cumsum(full.astype(jnp.float32), axis=1)
    cs = jnp.pad(cs, ((0, 0), (1, 0), (0, 0)))
    pos = start_pos + jnp.arange(t_len, dtype=jnp.float32)
    uf = u.astype(jnp.float32)
    groups = []
    for g, w in enumerate(POOL_WINDOWS):
        sl = slice(g * POOL_GROUP_DIM, (g + 1) * POOL_GROUP_DIM)
        hi = cs[:, POOL_HIST + 1:POOL_HIST + 1 + t_len, sl]
        lo = cs[:, POOL_HIST + 1 - w:POOL_HIST + 1 - w + t_len, sl]
        cnt = jnp.minimum(jnp.float32(w), pos + 1.0)[None, :, None]
        groups.append((hi - lo) / cnt - uf[..., sl])
    pooled = jnp.stack(groups, axis=2)
    mixed = jnp.einsum('btgc,gcd->btgd', pooled, w_grp.astype(jnp.float32))
    y = mixed.reshape(bsz, t_len, POOL_WIDTH) * scale.astype(jnp.float32)
    return y.astype(u.dtype), full[:, -POOL_HIST:]


def causal_depthwise_conv(u, hist, w_conv, b_conv):
    t_len = u.shape[1]
    full = jnp.concatenate([hist.astype(u.dtype), u], axis=1)
    ff = full.astype(jnp.float32)
    wf = w_conv.astype(jnp.float32)
    y = b_conv.astype(jnp.float32)
    for k in range(CONV_WIDTH):
        y = y + ff[:, k:k + t_len] * wf[k]
    return y, full[:, -CONV_HIST:]


def rg_lru(xc, h0, w_a, b_a, w_i, b_i, lam):
    bsz, t_len, _ = xc.shape
    xh = xc.reshape(bsz, t_len, LRU_HEADS, LRU_HEAD_DIM)
    r = jax.nn.sigmoid(jnp.einsum('bthi,hij->bthj', xh, w_a.astype(jnp.float32)).reshape(bsz, t_len, LRU_WIDTH)
                       + b_a.astype(jnp.float32))
    i = jax.nn.sigmoid(jnp.einsum('bthi,hij->bthj', xh, w_i.astype(jnp.float32)).reshape(bsz, t_len, LRU_WIDTH)
                       + b_i.astype(jnp.float32))
    log_a = -LRU_C * r * jax.nn.softplus(-lam.astype(jnp.float32))
    a = jnp.exp(log_a)
    mult = jnp.sqrt(-jnp.expm1(2.0 * log_a))
    b = mult * (i * xc)
    b = b.at[:, 0].add(a[:, 0] * h0.astype(jnp.float32))

    def combine(lhs, rhs):
        a1, b1 = lhs
        a2, b2 = rhs
        return a1 * a2, a2 * b1 + b2

    _, h = lax.associative_scan(combine, (a, b), axis=1)
    return h, h[:, -1]


def mixer_block(xn, pool_hist, conv_hist, lru_h, start_pos, w_in, b_gate, pool_w, pool_scale,
                conv_w, conv_b, lru_wa, lru_ba, lru_wi, lru_bi, lru_lambda, w_br_pool, w_br_lru, w_out):
    bsz, t_len, _ = xn.shape
    proj = jnp.einsum('btd,dc->btc', xn, w_in)
    o1 = POOL_WIDTH
    o2 = o1 + LRU_WIDTH
    o3 = o2 + LRU_WIDTH
    u_pool, u_x, u_gate, g_logit = proj[..., :o1], proj[..., o1:o2], proj[..., o2:o3], proj[..., o3:]
    y_pool, new_pool = pool_mixer(u_pool, pool_hist, start_pos, pool_w, pool_scale)
    xc, new_conv = causal_depthwise_conv(u_x, conv_hist, conv_w, conv_b)
    h, h_last = rg_lru(xc, lru_h, lru_wa, lru_ba, lru_wi, lru_bi, lru_lambda)
    y_lru = (h * jax.nn.gelu(u_gate.astype(jnp.float32), approximate=True)).astype(xn.dtype)
    gates = jax.nn.sigmoid((g_logit + b_gate).astype(jnp.float32)).reshape(bsz, t_len, N_BRANCHES, D_MODEL)
    br_pool = jnp.einsum('btp,pd->btd', y_pool, w_br_pool).astype(jnp.float32)
    br_lru = jnp.einsum('btr,rd->btd', y_lru, w_br_lru).astype(jnp.float32)
    merged = (gates[..., 0, :] * br_pool + gates[..., 1, :] * br_lru).astype(xn.dtype)
    out = jnp.einsum('btd,de->bte', merged, w_out)
    return out.astype(xn.dtype), new_pool, new_conv.astype(xn.dtype), h_last.astype(xn.dtype)


def moe_ffn(xn, w_router, b_router, w_gu, b_gu, w_dn, b_dn):
    bsz, t_len, d = xn.shape
    xt = xn.reshape(bsz * t_len, d)
    logits = (xt @ w_router + b_router).astype(jnp.float32)
    top_v, top_i = lax.top_k(logits, TOP_K)
    top_w = jax.nn.softmax(top_v, axis=-1)
    combine = jnp.sum(jax.nn.one_hot(top_i, N_EXPERTS, dtype=jnp.float32) * top_w[..., None], axis=1)
    out = jnp.zeros((bsz * t_len, d), jnp.float32)
    for e in range(N_EXPERTS):
        gu = (xt @ w_gu[e] + b_gu[e]).astype(jnp.float32)
        gate = jnp.minimum(gu[:, :D_FF], SWIGLU_LIMIT)
        up = jnp.clip(gu[:, D_FF:], -SWIGLU_LIMIT, SWIGLU_LIMIT)
        hid = (up + 1.0) * gate * jax.nn.sigmoid(SWIGLU_ALPHA * gate)
        y_e = (hid.astype(xn.dtype) @ w_dn[e] + b_dn[e]).astype(jnp.float32)
        out = out + combine[:, e:e + 1] * y_e
    return out.reshape(bsz, t_len, d).astype(xn.dtype)


def trunk(x, pool_hist, conv_hist, lru_h, start_pos, layer_params, norm_final):
    (norm_mix, w_in, b_gate, pool_w, pool_scale, conv_w, conv_b, lru_wa, lru_ba, lru_wi, lru_bi,
     lru_lambda, w_br_pool, w_br_lru, w_out, norm_ffn, w_router, b_router, w_gu, b_gu, w_dn, b_dn) = layer_params
    new_pool, new_conv, new_lru = [], [], []
    for l in range(DEPTH):
        xn = rms_norm(x, norm_mix[l])
        mix, ph, ch, hl = mixer_block(xn, pool_hist[l], conv_hist[l], lru_h[l], start_pos, w_in[l], b_gate[l],
                                      pool_w[l], pool_scale[l], conv_w[l], conv_b[l], lru_wa[l], lru_ba[l],
                                      lru_wi[l], lru_bi[l], lru_lambda[l], w_br_pool[l], w_br_lru[l], w_out[l])
        x = x + mix
        x = x + moe_ffn(rms_norm(x, norm_ffn[l]), w_router[l], b_router[l], w_gu[l], b_gu[l], w_dn[l], b_dn[l])
        new_pool.append(ph)
        new_conv.append(ch)
        new_lru.append(hl)
    return rms_norm(x, norm_final), jnp.stack(new_pool), jnp.stack(new_conv), jnp.stack(new_lru)


def setup_inputs(seed: int = 0) -> dict:
    key = jax.random.key(seed)
    ks = jax.random.split(key, 32)
    f32 = jnp.float32

    def nrm(k, shape, fan_in):
        return jax.random.normal(k, shape, f32) * (fan_in ** -0.5)

    def gain(k, shape):
        return 1.0 + 0.05 * jax.random.normal(k, shape, f32)

    def small(k, shape, s=0.02):
        return s * jax.random.normal(k, shape, f32)

    u = jax.random.uniform(ks[12], (DEPTH, LRU_WIDTH), f32, 0.9, 0.999)
    a_base = u ** (1.0 / LRU_C)
    lru_lambda = jnp.log(a_base) - jnp.log1p(-a_base)
    return {
        'x_prompt': jax.random.normal(ks[0], (BATCH, SEQ, D_MODEL), f32),
        'x_sample': jax.random.normal(ks[1], (DEC_BATCH, DEC_SEQ, D_MODEL), f32),
        'state_pool': jax.random.normal(ks[2], (DEPTH, DEC_BATCH, POOL_HIST, POOL_WIDTH), f32),
        'state_conv': jax.random.normal(ks[3], (DEPTH, DEC_BATCH, CONV_HIST, LRU_WIDTH), f32),
        'state_lru': 0.5 * jax.random.normal(ks[4], (DEPTH, DEC_BATCH, LRU_WIDTH), f32),
        'norm_mix': gain(ks[5], (DEPTH, D_MODEL)),
        'w_in': nrm(ks[6], (DEPTH, D_MODEL, IN_COLS), D_MODEL),
        'b_gate': small(ks[7], (DEPTH, N_BRANCHES * D_MODEL), 0.1),
        'pool_w': nrm(ks[8], (DEPTH, N_POOL_GROUPS, POOL_GROUP_DIM, POOL_GROUP_DIM), POOL_GROUP_DIM),
        'pool_scale': gain(ks[9], (DEPTH, POOL_WIDTH)),
        'conv_w': nrm(ks[10], (DEPTH, CONV_WIDTH, LRU_WIDTH), CONV_WIDTH),
        'conv_b': small(ks[11], (DEPTH, LRU_WIDTH)),
        'lru_wa': nrm(ks[13], (DEPTH, LRU_HEADS, LRU_HEAD_DIM, LRU_HEAD_DIM), LRU_HEAD_DIM),
        'lru_ba': small(ks[14], (DEPTH, LRU_WIDTH), 0.1),
        'lru_wi': nrm(ks[15], (DEPTH, LRU_HEADS, LRU_HEAD_DIM, LRU_HEAD_DIM), LRU_HEAD_DIM),
        'lru_bi': small(ks[16], (DEPTH, LRU_WIDTH), 0.1),
        'lru_lambda': lru_lambda,
        'w_br_pool': nrm(ks[17], (DEPTH, POOL_WIDTH, D_MODEL), POOL_WIDTH),
        'w_br_lru': nrm(ks[18], (DEPTH, LRU_WIDTH, D_MODEL), LRU_WIDTH),
        'w_out': nrm(ks[19], (DEPTH, D_MODEL, D_MODEL), D_MODEL),
        'norm_ffn': gain(ks[20], (DEPTH, D_MODEL)),
        'w_router': nrm(ks[21], (DEPTH, D_MODEL, N_EXPERTS), D_MODEL),
        'b_router': small(ks[22], (DEPTH, N_EXPERTS), 0.01),
        'w_gu': nrm(ks[23], (DEPTH, N_EXPERTS, D_MODEL, 2 * D_FF), D_MODEL),
        'b_gu': small(ks[24], (DEPTH, N_EXPERTS, 2 * D_FF)),
        'w_dn': nrm(ks[25], (DEPTH, N_EXPERTS, D_FF, D_MODEL), D_FF),
        'b_dn': small(ks[26], (DEPTH, N_EXPERTS, D_MODEL)),
        'norm_final': gain(ks[27], (D_MODEL,)),
    }


def reference(x_prompt, x_sample, state_pool, state_conv, state_lru, norm_mix, w_in, b_gate, pool_w,
              pool_scale, conv_w, conv_b, lru_wa, lru_ba, lru_wi, lru_bi, lru_lambda, w_br_pool, w_br_lru,
              w_out, norm_ffn, w_router, b_router, w_gu, b_gu, w_dn, b_dn, norm_final):
    layer_params = (norm_mix, w_in, b_gate, pool_w, pool_scale, conv_w, conv_b, lru_wa, lru_ba, lru_wi,
                    lru_bi, lru_lambda, w_br_pool, w_br_lru, w_out, norm_ffn, w_router, b_router,
                    w_gu, b_gu, w_dn, b_dn)
    bp = x_prompt.shape[0]
    zero_pool = jnp.zeros((DEPTH, bp, POOL_HIST, POOL_WIDTH), x_prompt.dtype)
    zero_conv = jnp.zeros((DEPTH, bp, CONV_HIST, LRU_WIDTH), x_prompt.dtype)
    zero_lru = jnp.zeros((DEPTH, bp, LRU_WIDTH), x_prompt.dtype)
    y_prompt, new_pool_prompt, new_conv_prompt, new_lru_prompt = trunk(
        x_prompt, zero_pool, zero_conv, zero_lru, 0, layer_params, norm_final)
    y_sample, new_pool_sample, new_conv_sample, new_lru_sample = trunk(
        x_sample, state_pool, state_conv, state_lru, PAST_LEN, layer_params, norm_final)
    return (y_prompt, y_sample, new_pool_prompt, new_conv_prompt, new_lru_prompt,
            new_pool_sample, new_conv_sample, new_lru_sample)
```

```python
import functools

import jax
import jax.numpy as jnp
from jax import lax
from jax.experimental import pallas as pl
from jax.experimental.pallas import tpu as pltpu

BF16 = jnp.bfloat16
F32 = jnp.float32

POOL_WINDOWS = (2, 4, 8, 16)
POOL_HIST = max(POOL_WINDOWS) - 1
CONV_WIDTH = 4
CONV_HIST = CONV_WIDTH - 1
LRU_HEADS = 16
LRU_C = 8.0
N_EXPERTS = 32
TOP_K = 4
SWIGLU_LIMIT = 7.0
SWIGLU_ALPHA = 1.702
NORM_EPS = 1e-6

SUBLANES = 8
LANES = 128
MXU_WIDTH = 256
POOL_PAD = 16
CONV_PAD = 8
VMEM_LIMIT_BYTES = 56 * 1024 * 1024

PAST_LEN = 1024
MIXER_TILE = 256
TOKEN_TILE = 512
EXPERT_TILE = 256


def _rms_norm(x, g):
    ms = jnp.mean(x * x, axis=-1, keepdims=True)
    return (x * lax.rsqrt(ms + NORM_EPS)) * g


def _dot(a, b):
    return jnp.dot(a, b, preferred_element_type=F32)


def _softplus(x):
    return jnp.maximum(x, 0.0) + jnp.log1p(jnp.exp(-jnp.abs(x)))


def _gelu_tanh(x):
    c = 0.7978845608028654
    return 0.5 * x * (1.0 + jnp.tanh(c * (x + 0.044715 * (x * x * x))))


def _lru_scan(a, b, h0):
    t_len, c = a.shape
    groups = t_len // SUBLANES
    a3 = a.reshape(groups, SUBLANES, c)
    b3 = b.reshape(groups, SUBLANES, c)
    sub = lax.broadcasted_iota(jnp.int32, (groups, SUBLANES, c), 1)
    for k in (1, 2, 4):
        a_prev = pltpu.roll(a3, k, axis=1)
        b_prev = pltpu.roll(b3, k, axis=1)
        valid = sub >= k
        b3 = jnp.where(valid, a3 * b_prev + b3, b3)
        a3 = jnp.where(valid, a3 * a_prev, a3)
    h = h0
    outs = []
    for g in range(groups):
        hg = a3[g] * h + b3[g]
        outs.append(hg)
        h = hg[SUBLANES - 1:SUBLANES, :]
    return jnp.concatenate(outs, axis=0), h


def _mixer_kernel(x_ref, hp_ref, hc_ref, hl_ref, nmix_ref, win_ref, bgate_ref, poolw_ref, pscale_ref,
                  convw_ref, convb_ref, wai_ref, ba_ref, bi_ref, lam_ref, wbrp_ref, wbrl_ref, wout_ref,
                  x1_ref, npool_ref, nconv_ref, nlru_ref,
                  pool_ext, conv_ext, h_carry, *, start_pos, bblk, tt, d):
    j = pl.program_id(1)

    @pl.when(j == 0)
    def _():
        pool_ext[:, 0:POOL_PAD, :] = hp_ref[...]
        conv_ext[:, 0:CONV_PAD, :] = hc_ref[...]
        h_carry[...] = hl_ref[...]

    rows = bblk * tt
    x = x_ref[...].reshape(rows, d)
    xnb = _rms_norm(x, nmix_ref[...]).astype(BF16)

    u_pool = _dot(xnb, win_ref[:, 0:d])
    pos1 = lax.broadcasted_iota(jnp.int32, (tt, 1), 0) + (start_pos + 1) + j * tt
    gd = d // len(POOL_WINDOWS)
    y_pool_rows = []
    for b in range(bblk):
        pool_ext[b, POOL_PAD:POOL_PAD + tt, :] = u_pool[b * tt:(b + 1) * tt, :]
        parts = []
        for g, w in enumerate(POOL_WINDOWS):
            sl = slice(g * gd, (g + 1) * gd)
            s = pool_ext[b, :, sl]
            width = 1
            while width < w:
                s = s + pltpu.roll(s, width, axis=0)
                width *= 2
            cur = s[POOL_PAD:POOL_PAD + tt, :]
            inv_cnt = 1.0 / jnp.minimum(pos1, w).astype(F32)
            pooled = cur * inv_cnt - u_pool[b * tt:(b + 1) * tt, sl]
            mixed = _dot(pooled.astype(BF16), poolw_ref[g])
            parts.append(mixed * pscale_ref[:, sl])
        y_pool_rows.append(jnp.concatenate(parts, axis=1))
        npool_ref[b] = pool_ext[b, POOL_PAD + tt - POOL_HIST:POOL_PAD + tt, :]
        pool_ext[b, 0:POOL_PAD, :] = pool_ext[b, tt:tt + POOL_PAD, :]
    y_pool = jnp.concatenate(y_pool_rows, axis=0) if bblk > 1 else y_pool_rows[0]
    br_pool = _dot(y_pool.astype(BF16), wbrp_ref[...])
    g_pool = jax.nn.sigmoid(_dot(xnb, win_ref[:, 3 * d:4 * d]) + bgate_ref[:, 0:d])
    acc = g_pool * br_pool

    u_x = _dot(xnb, win_ref[:, d:2 * d])
    xc_rows = []
    for b in range(bblk):
        conv_ext[b, CONV_PAD:CONV_PAD + tt, :] = u_x[b * tt:(b + 1) * tt, :]
        ce = conv_ext[b]
        y = convb_ref[...] + ce * convw_ref[CONV_WIDTH - 1:CONV_WIDTH, :]
        for k in range(1, CONV_WIDTH):
            y = y + pltpu.roll(ce, k, axis=0) * convw_ref[CONV_WIDTH - 1 - k:CONV_WIDTH - k, :]
        xc_rows.append(y[CONV_PAD:CONV_PAD + tt, :])
        nconv_ref[b] = conv_ext[b, CONV_PAD + tt - CONV_HIST:CONV_PAD + tt, :]
        conv_ext[b, 0:CONV_PAD, :] = conv_ext[b, tt:tt + CONV_PAD, :]
    xc = jnp.concatenate(xc_rows, axis=0) if bblk > 1 else xc_rows[0]

    n_chunks = d // MXU_WIDTH
    pre_a, pre_i = [], []
    for c in range(n_chunks):
        ai = _dot(xc[:, c * MXU_WIDTH:(c + 1) * MXU_WIDTH].astype(BF16), wai_ref[c])
        pre_a.append(ai[:, 0:MXU_WIDTH])
        pre_i.append(ai[:, MXU_WIDTH:2 * MXU_WIDTH])
    r_gate = jax.nn.sigmoid(jnp.concatenate(pre_a, axis=1) + ba_ref[...])
    i_gate = jax.nn.sigmoid(jnp.concatenate(pre_i, axis=1) + bi_ref[...])
    log_a = (-LRU_C) * r_gate * _softplus(-lam_ref[...])
    a = jnp.exp(log_a)
    mult = jnp.sqrt(-jnp.tanh(log_a) * (a * a + 1.0))
    bb = mult * (i_gate * xc)
    h_rows = []
    for b in range(bblk):
        h_b, h_last = _lru_scan(a[b * tt:(b + 1) * tt, :], bb[b * tt:(b + 1) * tt, :], h_carry[b])
        h_rows.append(h_b)
        h_carry[b] = h_last
        nlru_ref[b] = h_last
    h = jnp.concatenate(h_rows, axis=0) if bblk > 1 else h_rows[0]

    u_gate = _dot(xnb, win_ref[:, 2 * d:3 * d])
    y_lru = h * _gelu_tanh(u_gate)
    br_lru = _dot(y_lru.astype(BF16), wbrl_ref[...])
    g_lru = jax.nn.sigmoid(_dot(xnb, win_ref[:, 4 * d:5 * d]) + bgate_ref[:, d:2 * d])
    acc = acc + g_lru * br_lru

    out = _dot(acc.astype(BF16), wout_ref[...])
    x1_ref[...] = (x + out).reshape(bblk, tt, d)


def _const_spec(shape):
    nd = len(shape)
    return pl.BlockSpec(shape, lambda *_: (0,) * nd, pipeline_mode=pl.Buffered(1))


def _mixer_call(x, hist_pool, hist_conv, hist_lru, weights, *, start_pos, bblk, tt):
    bsz, t_len, d = x.shape
    grid = (bsz // bblk, t_len // tt)
    kern = functools.partial(_mixer_kernel, start_pos=start_pos, bblk=bblk, tt=tt, d=d)
    seq_spec = lambda rows: pl.BlockSpec((bblk, rows, d), lambda b, j: (b, 0, 0))
    in_specs = [
        pl.BlockSpec((bblk, tt, d), lambda b, j: (b, j, 0)),
        seq_spec(POOL_PAD), seq_spec(CONV_PAD), seq_spec(1),
    ] + [_const_spec(w.shape) for w in weights]
    out_shape = (
        jax.ShapeDtypeStruct((bsz, t_len, d), F32),
        jax.ShapeDtypeStruct((bsz, POOL_HIST, d), F32),
        jax.ShapeDtypeStruct((bsz, CONV_HIST, d), F32),
        jax.ShapeDtypeStruct((bsz, 1, d), F32),
    )
    out_specs = (
        pl.BlockSpec((bblk, tt, d), lambda b, j: (b, j, 0)),
        seq_spec(POOL_HIST), seq_spec(CONV_HIST), seq_spec(1),
    )
    return pl.pallas_call(
        kern,
        grid=grid,
        in_specs=in_specs,
        out_specs=out_specs,
        out_shape=out_shape,
        scratch_shapes=[
            pltpu.VMEM((bblk, POOL_PAD + tt, d), F32),
            pltpu.VMEM((bblk, CONV_PAD + tt, d), F32),
            pltpu.VMEM((bblk, 1, d), F32),
        ],
        compiler_params=pltpu.CompilerParams(
            dimension_semantics=("arbitrary", "arbitrary"), vmem_limit_bytes=VMEM_LIMIT_BYTES),
        name="mixer",
    )(x, hist_pool, hist_conv, hist_lru, *weights)


def _router_kernel(x1_ref, nffn_ref, wrt_ref, br_ref, tri_ref,
                   xn_ref, topi_ref, topw_ref, rank_ref, cnt_ref, carry, *, tb):
    i = pl.program_id(0)

    @pl.when(i == 0)
    def _():
        carry[...] = jnp.zeros_like(carry)

    xnb = _rms_norm(x1_ref[...], nffn_ref[...]).astype(BF16)
    xn_ref[...] = xnb
    logits = lax.dot_general(wrt_ref[...], xnb, (((1,), (1,)), ((), ())),
                             preferred_element_type=F32) + br_ref[...]
    n_exp = logits.shape[0]
    iota_e = lax.broadcasted_iota(jnp.int32, (n_exp, tb), 0)
    work = logits
    vals, idxs, hots = [], [], []
    for _ in range(TOP_K):
        m = jnp.max(work, axis=0, keepdims=True)
        idx = jnp.min(jnp.where(work == m, iota_e, n_exp), axis=0, keepdims=True)
        hot = iota_e == idx
        vals.append(m)
        idxs.append(idx)
        hots.append(hot)
        work = jnp.where(hot, -jnp.inf, work)
    exps = [jnp.exp(v - vals[0]) for v in vals]
    denom = exps[0] + exps[1] + exps[2] + exps[3]
    topi_ref[...] = jnp.concatenate(idxs, axis=0)
    topw_ref[...] = jnp.concatenate([e / denom for e in exps], axis=0)

    sel = sum(jnp.where(h, 1.0, 0.0) for h in hots)
    before = _dot(sel.astype(BF16), tri_ref[...]) + carry[:, 0:1]
    ranks = [jnp.sum(jnp.where(h, before, 0.0), axis=0, keepdims=True) for h in hots]
    rank_ref[...] = jnp.concatenate(ranks, axis=0).astype(jnp.int32)
    carry[...] = carry[...] + jnp.sum(sel, axis=1, keepdims=True)
    cnt_ref[...] = carry[...]


def _router_call(x1, norm_ffn, w_router_t, b_router, *, tb):
    n, d = x1.shape
    n_exp = w_router_t.shape[0]
    tri = jnp.triu(jnp.ones((tb, tb), BF16), k=1)
    tok = lambda rows, dt: (pl.BlockSpec((rows, tb), lambda i: (0, i)), jax.ShapeDtypeStruct((rows, n), dt))
    (topi_spec, topi_shape), (topw_spec, topw_shape), (rank_spec, rank_shape) = (
        tok(TOP_K, jnp.int32), tok(TOP_K, F32), tok(TOP_K, jnp.int32))
    return pl.pallas_call(
        functools.partial(_router_kernel, tb=tb),
        grid=(n // tb,),
        in_specs=[
            pl.BlockSpec((tb, d), lambda i: (i, 0)),
            _const_spec((1, d)), _const_spec((n_exp, d)), _const_spec((n_exp, 1)), _const_spec((tb, tb)),
        ],
        out_specs=(
            pl.BlockSpec((tb, d), lambda i: (i, 0)),
            topi_spec, topw_spec, rank_spec,
            pl.BlockSpec((n_exp, LANES), lambda i: (0, 0)),
        ),
        out_shape=(
            jax.ShapeDtypeStruct((n, d), BF16),
            topi_shape, topw_shape, rank_shape,
            jax.ShapeDtypeStruct((n_exp, LANES), F32),
        ),
        scratch_shapes=[pltpu.VMEM((n_exp, LANES), F32)],
        compiler_params=pltpu.CompilerParams(
            dimension_semantics=("arbitrary",), vmem_limit_bytes=VMEM_LIMIT_BYTES),
        name="router",
    )(x1, norm_ffn.reshape(1, d), w_router_t, b_router.reshape(n_exp, 1), tri)


def _expert_kernel(te_ref, nu_ref, xs_ref, wgu_ref, bgu_ref, wdn_ref, bdn_ref, ys_ref, *, dff):
    del te_ref

    @pl.when(pl.program_id(0) < nu_ref[0])
    def _():
        gu = _dot(xs_ref[...], wgu_ref[0]) + bgu_ref[0]
        gate = jnp.minimum(gu[:, :dff], SWIGLU_LIMIT)
        up = jnp.clip(gu[:, dff:], -SWIGLU_LIMIT, SWIGLU_LIMIT)
        hid = (up + 1.0) * gate * jax.nn.sigmoid(SWIGLU_ALPHA * gate)
        ys_ref[...] = _dot(hid.astype(BF16), wdn_ref[0]) + bdn_ref[0]


def _expert_call(tile_expert, n_used, xs, w_gu, b_gu, w_dn, b_dn, *, tm):
    p_rows, d = xs.shape
    n_exp, _, dff2 = w_gu.shape
    dff = dff2 // 2
    row_blk = lambda i, te, nu: (jnp.minimum(i, nu[0] - 1), 0)
    exp_blk = lambda i, te, nu: (te[i], 0, 0)
    return pl.pallas_call(
        functools.partial(_expert_kernel, dff=dff),
        grid_spec=pltpu.PrefetchScalarGridSpec(
            num_scalar_prefetch=2,
            grid=(p_rows // tm,),
            in_specs=[
                pl.BlockSpec((tm, d), row_blk),
                pl.BlockSpec((1, d, dff2), exp_blk),
                pl.BlockSpec((1, 1, dff2), exp_blk),
                pl.BlockSpec((1, dff, d), exp_blk),
                pl.BlockSpec((1, 1, d), exp_blk),
            ],
            out_specs=pl.BlockSpec((tm, d), row_blk),
        ),
        out_shape=jax.ShapeDtypeStruct((p_rows, d), F32),
        compiler_params=pltpu.CompilerParams(
            dimension_semantics=("arbitrary",), vmem_limit_bytes=VMEM_LIMIT_BYTES),
        name="experts",
    )(tile_expert, n_used, xs, w_gu, b_gu.reshape(n_exp, 1, dff2), w_dn, b_dn.reshape(n_exp, 1, d))


def _final_kernel(x1_ref, yk_ref, w_ref, nfin_ref, out_ref):
    w = w_ref[...]
    moe = yk_ref[0] * w[:, 0:1]
    for k in range(1, TOP_K):
        moe = moe + yk_ref[k] * w[:, k:k + 1]
    out_ref[...] = _rms_norm(x1_ref[...] + moe, nfin_ref[...])


def _final_call(x1, yk, topw_t, norm_final, *, tb):
    n, d = x1.shape
    return pl.pallas_call(
        _final_kernel,
        grid=(n // tb,),
        in_specs=[
            pl.BlockSpec((tb, d), lambda i: (i, 0)),
            pl.BlockSpec((TOP_K, tb, d), lambda i: (0, i, 0)),
            pl.BlockSpec((tb, TOP_K), lambda i: (i, 0)),
            _const_spec((1, d)),
        ],
        out_specs=pl.BlockSpec((tb, d), lambda i: (i, 0)),
        out_shape=jax.ShapeDtypeStruct((n, d), F32),
        compiler_params=pltpu.CompilerParams(
            dimension_semantics=("arbitrary",), vmem_limit_bytes=VMEM_LIMIT_BYTES),
        name="final",
    )(x1, yk, topw_t, norm_final.reshape(1, d))


def _block_diag(w, per_block):
    heads, n, _ = w.shape
    w4 = w.reshape(heads // per_block, per_block, n, n)
    eye = jnp.eye(per_block, dtype=w.dtype)
    bd = jnp.einsum('chij,hg->chigj', w4, eye)
    return bd.reshape(heads // per_block, per_block * n, per_block * n)


def _mixer_weights(norm_mix, w_in, b_gate, pool_w, pool_scale, conv_w, conv_b, lru_wa, lru_ba, lru_wi,
                   lru_bi, lru_lambda, w_br_pool, w_br_lru, w_out):
    row = lambda v: v.reshape(1, -1)
    head_dim = lru_wa.shape[-1]
    per_block = MXU_WIDTH // head_dim
    w_ai = jnp.concatenate([_block_diag(lru_wa, per_block), _block_diag(lru_wi, per_block)], axis=-1)
    return (row(norm_mix), w_in.astype(BF16), row(b_gate), pool_w.astype(BF16), row(pool_scale),
            conv_w, row(conv_b), w_ai.astype(BF16), row(lru_ba), row(lru_bi), row(lru_lambda),
            w_br_pool.astype(BF16), w_br_lru.astype(BF16), w_out.astype(BF16))


def _run_mixer(x, state_pool, state_conv, state_lru, weights, *, start_pos, bblk, tt):
    hp = jnp.pad(state_pool, ((0, 0), (POOL_PAD - POOL_HIST, 0), (0, 0)))
    hc = jnp.pad(state_conv, ((0, 0), (CONV_PAD - CONV_HIST, 0), (0, 0)))
    hl = state_lru[:, None, :]
    x1, npool, nconv, nlru = _mixer_call(x, hp, hc, hl, weights, start_pos=start_pos, bblk=bblk, tt=tt)
    return x1, npool, nconv, nlru[:, 0, :]


def _routing_plan(topi, rank, counts, *, tm, n_tiles):
    tiles_per = (counts + tm - 1) // tm
    tiles_cum = jnp.cumsum(tiles_per)
    base = (tiles_cum - tiles_per) * tm
    pos = base[topi] + rank
    n_used = tiles_cum[-1]
    tile_ids = jnp.minimum(jnp.arange(n_tiles, dtype=jnp.int32), n_used - 1)
    tile_expert = jnp.sum((tiles_cum[None, :] <= tile_ids[:, None]).astype(jnp.int32), axis=1)
    return pos, tile_expert, n_used.reshape(1).astype(jnp.int32)


def kernel(x_prompt, x_sample, state_pool, state_conv, state_lru, norm_mix, w_in, b_gate, pool_w, pool_scale,
           conv_w, conv_b, lru_wa, lru_ba, lru_wi, lru_bi, lru_lambda, w_br_pool, w_br_lru, w_out, norm_ffn,
           w_router, b_router, w_gu, b_gu, w_dn, b_dn, norm_final):
    bp, tp, d = x_prompt.shape
    bs, ts, _ = x_sample.shape
    n_exp = w_router.shape[-1]
    mw = _mixer_weights(norm_mix[0], w_in[0], b_gate[0], pool_w[0], pool_scale[0], conv_w[0], conv_b[0],
                        lru_wa[0], lru_ba[0], lru_wi[0], lru_bi[0], lru_lambda[0], w_br_pool[0], w_br_lru[0],
                        w_out[0])
    zeros = lambda *shape: jnp.zeros(shape, x_prompt.dtype)
    x1_p, pool_p, conv_p, lru_p = _run_mixer(
        x_prompt, zeros(bp, POOL_HIST, d), zeros(bp, CONV_HIST, d), zeros(bp, d), mw,
        start_pos=0, bblk=1, tt=MIXER_TILE)
    x1_s, pool_s, conv_s, lru_s = _run_mixer(
        x_sample, state_pool[0], state_conv[0], state_lru[0], mw, start_pos=PAST_LEN, bblk=bs, tt=ts)

    n_p, n_s = bp * tp, bs * ts
    n = n_p + n_s
    x1 = jnp.concatenate([x1_p.reshape(n_p, d), x1_s.reshape(n_s, d)], axis=0)
    xn2, topi, topw, rank, cnt = _router_call(
        x1, norm_ffn[0], w_router[0].T.astype(BF16), b_router[0], tb=TOKEN_TILE)

    n_tiles = (n * TOP_K + n_exp * (EXPERT_TILE - 1)) // EXPERT_TILE
    pos, tile_expert, n_used = _routing_plan(
        topi, rank, cnt[:, 0].astype(jnp.int32), tm=EXPERT_TILE, n_tiles=n_tiles)

    tok = jnp.broadcast_to(jnp.arange(n, dtype=jnp.int32), (TOP_K, n))
    src = jnp.zeros((n_tiles * EXPERT_TILE,), jnp.int32).at[pos.reshape(-1)].set(tok.reshape(-1))
    xs = jnp.take(xn2, src, axis=0)
    ys = _expert_call(tile_expert, n_used, xs, w_gu[0].astype(BF16), b_gu[0], w_dn[0].astype(BF16), b_dn[0],
                      tm=EXPERT_TILE)
    yk = jnp.take(ys, pos, axis=0)
    y = _final_call(x1, yk, topw.T, norm_final, tb=TOKEN_TILE)

    return (y[:n_p].reshape(bp, tp, d), y[n_p:].reshape(bs, ts, d),
            pool_p[None], conv_p[None], lru_p[None], pool_s[None], conv_s[None], lru_s[None])
```

```python
import functools

import jax
import jax.numpy as jnp
from jax import lax
from jax.experimental import pallas as pl
from jax.experimental.pallas import tpu as pltpu
from jax.experimental.pallas import tpu_sc as plsc

BF16 = jnp.bfloat16
F32 = jnp.float32

POOL_WINDOWS = (2, 4, 8, 16)
POOL_HIST = max(POOL_WINDOWS) - 1
CONV_WIDTH = 4
CONV_HIST = CONV_WIDTH - 1
LRU_HEADS = 16
LRU_C = 8.0
N_EXPERTS = 32
TOP_K = 4
SWIGLU_LIMIT = 7.0
SWIGLU_ALPHA = 1.702
NORM_EPS = 1e-6

SUBLANES = 8
LANES = 128
MXU_WIDTH = 256
POOL_PAD = 16
CONV_PAD = 8
VMEM_LIMIT_BYTES = 56 * 1024 * 1024

PAST_LEN = 1024
MIXER_TILE = 256
TOKEN_TILE = 512
EXPERT_TILE = 256

SC_CORES = 2
SC_SUBCORES = 16
SC_WORKERS = SC_CORES * SC_SUBCORES
SC_MAX_STEP_ROWS = 64


def _rms_norm(x, g):
    ms = jnp.mean(x * x, axis=-1, keepdims=True)
    return (x * lax.rsqrt(ms + NORM_EPS)) * g


def _dot(a, b):
    return jnp.dot(a, b, preferred_element_type=F32)


def _softplus(x):
    return jnp.maximum(x, 0.0) + jnp.log1p(jnp.exp(-jnp.abs(x)))


def _gelu_tanh(x):
    c = 0.7978845608028654
    return 0.5 * x * (1.0 + jnp.tanh(c * (x + 0.044715 * (x * x * x))))


def _lru_scan(a, b, h0):
    t_len, c = a.shape
    groups = t_len // SUBLANES
    a3 = a.reshape(groups, SUBLANES, c)
    b3 = b.reshape(groups, SUBLANES, c)
    sub = lax.broadcasted_iota(jnp.int32, (groups, SUBLANES, c), 1)
    for k in (1, 2, 4):
        a_prev = pltpu.roll(a3, k, axis=1)
        b_prev = pltpu.roll(b3, k, axis=1)
        valid = sub >= k
        b3 = jnp.where(valid, a3 * b_prev + b3, b3)
        a3 = jnp.where(valid, a3 * a_prev, a3)
    h = h0
    outs = []
    for g in range(groups):
        hg = a3[g] * h + b3[g]
        outs.append(hg)
        h = hg[SUBLANES - 1:SUBLANES, :]
    return jnp.concatenate(outs, axis=0), h


def _mixer_kernel(x_ref, hp_ref, hc_ref, hl_ref, nmix_ref, win_ref, bgate_ref, poolw_ref, pscale_ref,
                  convw_ref, convb_ref, wai_ref, ba_ref, bi_ref, lam_ref, wbrp_ref, wbrl_ref, wout_ref,
                  x1_ref, npool_ref, nconv_ref, nlru_ref,
                  pool_ext, conv_ext, h_carry, *, start_pos, bblk, tt, d):
    j = pl.program_id(1)

    @pl.when(j == 0)
    def _():
        pool_ext[:, 0:POOL_PAD, :] = hp_ref[...]
        conv_ext[:, 0:CONV_PAD, :] = hc_ref[...]
        h_carry[...] = hl_ref[...]

    rows = bblk * tt
    x = x_ref[...].reshape(rows, d)
    xnb = _rms_norm(x, nmix_ref[...]).astype(BF16)

    u_pool = _dot(xnb, win_ref[:, 0:d])
    pos1 = lax.broadcasted_iota(jnp.int32, (tt, 1), 0) + (start_pos + 1) + j * tt
    gd = d // len(POOL_WINDOWS)
    y_pool_rows = []
    for b in range(bblk):
        pool_ext[b, POOL_PAD:POOL_PAD + tt, :] = u_pool[b * tt:(b + 1) * tt, :]
        parts = []
        for g, w in enumerate(POOL_WINDOWS):
            sl = slice(g * gd, (g + 1) * gd)
            s = pool_ext[b, :, sl]
            width = 1
            while width < w:
                s = s + pltpu.roll(s, width, axis=0)
                width *= 2
            cur = s[POOL_PAD:POOL_PAD + tt, :]
            inv_cnt = 1.0 / jnp.minimum(pos1, w).astype(F32)
            pooled = cur * inv_cnt - u_pool[b * tt:(b + 1) * tt, sl]
            mixed = _dot(pooled.astype(BF16), poolw_ref[g])
            parts.append(mixed * pscale_ref[:, sl])
        y_pool_rows.append(jnp.concatenate(parts, axis=1))
        npool_ref[b] = pool_ext[b, POOL_PAD + tt - POOL_HIST:POOL_PAD + tt, :]
        pool_ext[b, 0:POOL_PAD, :] = pool_ext[b, tt:tt + POOL_PAD, :]
    y_pool = jnp.concatenate(y_pool_rows, axis=0) if bblk > 1 else y_pool_rows[0]
    br_pool = _dot(y_pool.astype(BF16), wbrp_ref[...])
    g_pool = jax.nn.sigmoid(_dot(xnb, win_ref[:, 3 * d:4 * d]) + bgate_ref[:, 0:d])
    acc = g_pool * br_pool

    u_x = _dot(xnb, win_ref[:, d:2 * d])
    xc_rows = []
    for b in range(bblk):
        conv_ext[b, CONV_PAD:CONV_PAD + tt, :] = u_x[b * tt:(b + 1) * tt, :]
        ce = conv_ext[b]
        y = convb_ref[...] + ce * convw_ref[CONV_WIDTH - 1:CONV_WIDTH, :]
        for k in range(1, CONV_WIDTH):
            y = y + pltpu.roll(ce, k, axis=0) * convw_ref[CONV_WIDTH - 1 - k:CONV_WIDTH - k, :]
        xc_rows.append(y[CONV_PAD:CONV_PAD + tt, :])
        nconv_ref[b] = conv_ext[b, CONV_PAD + tt - CONV_HIST:CONV_PAD + tt, :]
        conv_ext[b, 0:CONV_PAD, :] = conv_ext[b, tt:tt + CONV_PAD, :]
    xc = jnp.concatenate(xc_rows, axis=0) if bblk > 1 else xc_rows[0]

    n_chunks = d // MXU_WIDTH
    pre_a, pre_i = [], []
    for c in range(n_chunks):
        ai = _dot(xc[:, c * MXU_WIDTH:(c + 1) * MXU_WIDTH].astype(BF16), wai_ref[c])
        pre_a.append(ai[:, 0:MXU_WIDTH])
        pre_i.append(ai[:, MXU_WIDTH:2 * MXU_WIDTH])
    r_gate = jax.nn.sigmoid(jnp.concatenate(pre_a, axis=1) + ba_ref[...])
    i_gate = jax.nn.sigmoid(jnp.concatenate(pre_i, axis=1) + bi_ref[...])
    log_a = (-LRU_C) * r_gate * _softplus(-lam_ref[...])
    a = jnp.exp(log_a)
    mult = jnp.sqrt(-jnp.tanh(log_a) * (a * a + 1.0))
    bb = mult * (i_gate * xc)
    h_rows = []
    for b in range(bblk):
        h_b, h_last = _lru_scan(a[b * tt:(b + 1) * tt, :], bb[b * tt:(b + 1) * tt, :], h_carry[b])
        h_rows.append(h_b)
        h_carry[b] = h_last
        nlru_ref[b] = h_last
    h = jnp.concatenate(h_rows, axis=0) if bblk > 1 else h_rows[0]

    u_gate = _dot(xnb, win_ref[:, 2 * d:3 * d])
    y_lru = h * _gelu_tanh(u_gate)
    br_lru = _dot(y_lru.astype(BF16), wbrl_ref[...])
    g_lru = jax.nn.sigmoid(_dot(xnb, win_ref[:, 4 * d:5 * d]) + bgate_ref[:, d:2 * d])
    acc = acc + g_lru * br_lru

    out = _dot(acc.astype(BF16), wout_ref[...])
    x1_ref[...] = (x + out).reshape(bblk, tt, d)


def _const_spec(shape):
    nd = len(shape)
    return pl.BlockSpec(shape, lambda *_: (0,) * nd, pipeline_mode=pl.Buffered(1))


def _mixer_call(x, hist_pool, hist_conv, hist_lru, weights, *, start_pos, bblk, tt):
    bsz, t_len, d = x.shape
    grid = (bsz // bblk, t_len // tt)
    kern = functools.partial(_mixer_kernel, start_pos=start_pos, bblk=bblk, tt=tt, d=d)
    seq_spec = lambda rows: pl.BlockSpec((bblk, rows, d), lambda b, j: (b, 0, 0))
    in_specs = [
        pl.BlockSpec((bblk, tt, d), lambda b, j: (b, j, 0)),
        seq_spec(POOL_PAD), seq_spec(CONV_PAD), seq_spec(1),
    ] + [_const_spec(w.shape) for w in weights]
    out_shape = (
        jax.ShapeDtypeStruct((bsz, t_len, d), F32),
        jax.ShapeDtypeStruct((bsz, POOL_HIST, d), F32),
        jax.ShapeDtypeStruct((bsz, CONV_HIST, d), F32),
        jax.ShapeDtypeStruct((bsz, 1, d), F32),
    )
    out_specs = (
        pl.BlockSpec((bblk, tt, d), lambda b, j: (b, j, 0)),
        seq_spec(POOL_HIST), seq_spec(CONV_HIST), seq_spec(1),
    )
    return pl.pallas_call(
        kern,
        grid=grid,
        in_specs=in_specs,
        out_specs=out_specs,
        out_shape=out_shape,
        scratch_shapes=[
            pltpu.VMEM((bblk, POOL_PAD + tt, d), F32),
            pltpu.VMEM((bblk, CONV_PAD + tt, d), F32),
            pltpu.VMEM((bblk, 1, d), F32),
        ],
        compiler_params=pltpu.CompilerParams(
            dimension_semantics=("arbitrary", "arbitrary"), vmem_limit_bytes=VMEM_LIMIT_BYTES),
        name="mixer",
    )(x, hist_pool, hist_conv, hist_lru, *weights)


def _load_rows(ref, rows, chunks):
    return jnp.concatenate([ref[pl.ds(s, rows, stride=chunks), :] for s in range(chunks)], axis=1)


def _store_rows(ref, val, rows, chunks):
    for s in range(chunks):
        ref[pl.ds(s, rows, stride=chunks), :] = val[:, s * LANES:(s + 1) * LANES]


def _router_kernel(x1p_ref, x1s_ref, nffn_ref, wrt_ref, br_ref, tri_ref,
                   xn_ref, topi_ref, topw_ref, rank_ref, cnt_ref, carry, *, tb, d, p_tiles):
    i = pl.program_id(0)

    @pl.when(i == 0)
    def _():
        carry[...] = jnp.zeros_like(carry)

    x1 = jnp.where(i < p_tiles, x1p_ref[...], x1s_ref[...])
    xnb = _rms_norm(x1, nffn_ref[...]).astype(BF16)
    _store_rows(xn_ref, xnb.astype(F32), tb, d // LANES)
    logits = lax.dot_general(wrt_ref[...], xnb, (((1,), (1,)), ((), ())),
                             preferred_element_type=F32) + br_ref[...]
    n_exp = logits.shape[0]
    iota_e = lax.broadcasted_iota(jnp.int32, (n_exp, tb), 0)
    work = logits
    vals, idxs, hots = [], [], []
    for _ in range(TOP_K):
        m = jnp.max(work, axis=0, keepdims=True)
        idx = jnp.min(jnp.where(work == m, iota_e, n_exp), axis=0, keepdims=True)
        hot = iota_e == idx
        vals.append(m)
        idxs.append(idx)
        hots.append(hot)
        work = jnp.where(hot, -jnp.inf, work)
    exps = [jnp.exp(v - vals[0]) for v in vals]
    denom = exps[0] + exps[1] + exps[2] + exps[3]
    topi_ref[...] = jnp.concatenate(idxs, axis=0)
    topw_ref[...] = jnp.concatenate([e / denom for e in exps], axis=0)

    sel = sum(jnp.where(h, 1.0, 0.0) for h in hots)
    before = _dot(sel.astype(BF16), tri_ref[...]) + carry[:, 0:1]
    ranks = [jnp.sum(jnp.where(h, before, 0.0), axis=0, keepdims=True) for h in hots]
    rank_ref[...] = jnp.concatenate(ranks, axis=0).astype(jnp.int32)
    carry[...] = carry[...] + jnp.sum(sel, axis=1, keepdims=True)
    cnt_ref[...] = carry[...]


def _segment_specs(tb, d, p_tiles):
    prompt = pl.BlockSpec((tb, d), lambda i: (jnp.minimum(i, p_tiles - 1), 0))
    sample = pl.BlockSpec((tb, d), lambda i: (0, 0))
    return prompt, sample


def _router_call(x1_p, x1_s, norm_ffn, w_router_t, b_router, *, tb):
    n_p, d = x1_p.shape
    n_s, _ = x1_s.shape
    assert n_p % tb == 0 and n_s == tb
    p_tiles = n_p // tb
    n = n_p + n_s
    chunks = d // LANES
    n_exp = w_router_t.shape[0]
    tri = jnp.triu(jnp.ones((tb, tb), BF16), k=1)
    tok = lambda rows, dt: (pl.BlockSpec((rows, tb), lambda i: (0, i)), jax.ShapeDtypeStruct((rows, n), dt))
    (topi_spec, topi_shape), (topw_spec, topw_shape), (rank_spec, rank_shape) = (
        tok(TOP_K, jnp.int32), tok(TOP_K, F32), tok(TOP_K, jnp.int32))
    return pl.pallas_call(
        functools.partial(_router_kernel, tb=tb, d=d, p_tiles=p_tiles),
        grid=(p_tiles + 1,),
        in_specs=[
            *_segment_specs(tb, d, p_tiles),
            _const_spec((1, d)), _const_spec((n_exp, d)), _const_spec((n_exp, 1)), _const_spec((tb, tb)),
        ],
        out_specs=(
            pl.BlockSpec((tb * chunks, LANES), lambda i: (i, 0)),
            topi_spec, topw_spec, rank_spec,
            pl.BlockSpec((n_exp, LANES), lambda i: (0, 0)),
        ),
        out_shape=(
            jax.ShapeDtypeStruct((n * chunks, LANES), F32),
            topi_shape, topw_shape, rank_shape,
            jax.ShapeDtypeStruct((n_exp, LANES), F32),
        ),
        scratch_shapes=[pltpu.VMEM((n_exp, LANES), F32)],
        compiler_params=pltpu.CompilerParams(
            dimension_semantics=("arbitrary",), vmem_limit_bytes=VMEM_LIMIT_BYTES),
        name="router",
    )(x1_p, x1_s, norm_ffn.reshape(1, d), w_router_t, b_router.reshape(n_exp, 1), tri)


def _expert_kernel(te_ref, nu_ref, xs_ref, wgu_ref, bgu_ref, wdn_ref, bdn_ref, ys_ref, *, tm, d, dff):
    del te_ref

    @pl.when(pl.program_id(0) < nu_ref[0])
    def _():
        x = _load_rows(xs_ref, tm, d // LANES).astype(BF16)
        gu = _dot(x, wgu_ref[0]) + bgu_ref[0]
        gate = jnp.minimum(gu[:, :dff], SWIGLU_LIMIT)
        up = jnp.clip(gu[:, dff:], -SWIGLU_LIMIT, SWIGLU_LIMIT)
        hid = (up + 1.0) * gate * jax.nn.sigmoid(SWIGLU_ALPHA * gate)
        y = _dot(hid.astype(BF16), wdn_ref[0]) + bdn_ref[0]
        _store_rows(ys_ref, y, tm, d // LANES)


def _expert_call(tile_expert, n_used, xs, w_gu, b_gu, w_dn, b_dn, *, tm):
    n_exp, d, dff2 = w_gu.shape
    dff = dff2 // 2
    chunks = d // LANES
    p_rows = xs.shape[0] // chunks
    row_blk = lambda i, te, nu: (jnp.minimum(i, nu[0] - 1), 0)
    exp_blk = lambda i, te, nu: (te[i], 0, 0)
    return pl.pallas_call(
        functools.partial(_expert_kernel, tm=tm, d=d, dff=dff),
        grid_spec=pltpu.PrefetchScalarGridSpec(
            num_scalar_prefetch=2,
            grid=(p_rows // tm,),
            in_specs=[
                pl.BlockSpec((tm * chunks, LANES), row_blk),
                pl.BlockSpec((1, d, dff2), exp_blk),
                pl.BlockSpec((1, 1, dff2), exp_blk),
                pl.BlockSpec((1, dff, d), exp_blk),
                pl.BlockSpec((1, 1, d), exp_blk),
            ],
            out_specs=pl.BlockSpec((tm * chunks, LANES), row_blk),
        ),
        out_shape=jax.ShapeDtypeStruct((p_rows * chunks, LANES), F32),
        compiler_params=pltpu.CompilerParams(
            dimension_semantics=("arbitrary",), vmem_limit_bytes=VMEM_LIMIT_BYTES),
        name="experts",
    )(tile_expert, n_used, xs, w_gu, b_gu.reshape(n_exp, 1, dff2), w_dn, b_dn.reshape(n_exp, 1, d))


def _final_kernel(x1p_ref, x1s_ref, yk_ref, w_ref, nfin_ref, yp_ref, ys_ref, *, tb, d, p_tiles):
    i = pl.program_id(0)
    x1 = jnp.where(i < p_tiles, x1p_ref[...], x1s_ref[...])
    w = w_ref[...]
    moe = _load_rows(yk_ref.at[0], tb, d // LANES) * w[:, 0:1]
    for k in range(1, TOP_K):
        moe = moe + _load_rows(yk_ref.at[k], tb, d // LANES) * w[:, k:k + 1]
    out = _rms_norm(x1 + moe, nfin_ref[...])

    @pl.when(i < p_tiles)
    def _():
        yp_ref[...] = out

    @pl.when(i == p_tiles)
    def _():
        ys_ref[...] = out


def _final_call(x1_p, x1_s, yk, topw_t, norm_final, *, tb):
    n_p, d = x1_p.shape
    n_s, _ = x1_s.shape
    p_tiles = n_p // tb
    chunks = d // LANES
    return pl.pallas_call(
        functools.partial(_final_kernel, tb=tb, d=d, p_tiles=p_tiles),
        grid=(p_tiles + 1,),
        in_specs=[
            *_segment_specs(tb, d, p_tiles),
            pl.BlockSpec((TOP_K, tb * chunks, LANES), lambda i: (0, i, 0)),
            pl.BlockSpec((tb, TOP_K), lambda i: (i, 0)),
            _const_spec((1, d)),
        ],
        out_specs=_segment_specs(tb, d, p_tiles),
        out_shape=(jax.ShapeDtypeStruct((n_p, d), F32), jax.ShapeDtypeStruct((n_s, d), F32)),
        compiler_params=pltpu.CompilerParams(
            dimension_semantics=("arbitrary",), vmem_limit_bytes=VMEM_LIMIT_BYTES),
        name="final",
    )(x1_p, x1_s, yk, topw_t, norm_final.reshape(1, d))


def _sc_step_rows(n):
    per_worker = n // SC_WORKERS
    assert per_worker * SC_WORKERS == n
    w = max(c for c in range(SUBLANES, SC_MAX_STEP_ROWS + 1, SUBLANES) if per_worker % c == 0)
    return w, per_worker // w


def _sc_mesh():
    return plsc.VectorSubcoreMesh(core_axis_name="c", subcore_axis_name="s",
                                  num_cores=SC_CORES, num_subcores=SC_SUBCORES)


def _dispatch_rows(x_tiles, pos_steps, p_rows):
    n, chunks, _ = x_tiles.shape
    w, steps = _sc_step_rows(n)

    @functools.partial(
        pl.kernel, mesh=_sc_mesh(),
        out_type=jax.ShapeDtypeStruct((p_rows, chunks, LANES), F32),
        scratch_types=[pltpu.VMEM((w, chunks, LANES), F32), pltpu.VMEM((TOP_K, w), jnp.int32)],
        name="dispatch",
    )
    def run(x_hbm, pos_hbm, out_hbm, buf, idx):
        wid = lax.axis_index("s") * SC_CORES + lax.axis_index("c")

        @pl.loop(0, steps)
        def _(i):
            step = wid * steps + i
            pltpu.sync_copy(x_hbm.at[pl.ds(pl.multiple_of(step * w, SUBLANES), w)], buf)
            pltpu.sync_copy(pos_hbm.at[step], idx)
            for k in range(TOP_K):
                pltpu.sync_copy(buf, out_hbm.at[idx.at[k]])

    return run(x_tiles, pos_steps)


def _combine_rows(y_tiles, pos_steps, n):
    _, chunks, _ = y_tiles.shape
    w, steps = _sc_step_rows(n)

    @functools.partial(
        pl.kernel, mesh=_sc_mesh(),
        out_type=jax.ShapeDtypeStruct((TOP_K, n, chunks, LANES), F32),
        scratch_types=[pltpu.VMEM((w, chunks, LANES), F32), pltpu.VMEM((TOP_K, w), jnp.int32)],
        name="combine",
    )
    def run(y_hbm, pos_hbm, out_hbm, buf, idx):
        wid = lax.axis_index("s") * SC_CORES + lax.axis_index("c")

        @pl.loop(0, steps)
        def _(i):
            step = wid * steps + i
            pltpu.sync_copy(pos_hbm.at[step], idx)
            for k in range(TOP_K):
                pltpu.sync_copy(y_hbm.at[idx.at[k]], buf)
                pltpu.sync_copy(buf, out_hbm.at[k, pl.ds(pl.multiple_of(step * w, SUBLANES), w)])

    return run(y_tiles, pos_steps)


def _block_diag(w, per_block):
    heads, n, _ = w.shape
    w4 = w.reshape(heads // per_block, per_block, n, n)
    eye = jnp.eye(per_block, dtype=w.dtype)
    bd = jnp.einsum('chij,hg->chigj', w4, eye)
    return bd.reshape(heads // per_block, per_block * n, per_block * n)


def _mixer_weights(norm_mix, w_in, b_gate, pool_w, pool_scale, conv_w, conv_b, lru_wa, lru_ba, lru_wi,
                   lru_bi, lru_lambda, w_br_pool, w_br_lru, w_out):
    row = lambda v: v.reshape(1, -1)
    head_dim = lru_wa.shape[-1]
    per_block = MXU_WIDTH // head_dim
    w_ai = jnp.concatenate([_block_diag(lru_wa, per_block), _block_diag(lru_wi, per_block)], axis=-1)
    return (row(norm_mix), w_in.astype(BF16), row(b_gate), pool_w.astype(BF16), row(pool_scale),
            conv_w, row(conv_b), w_ai.astype(BF16), row(lru_ba), row(lru_bi), row(lru_lambda),
            w_br_pool.astype(BF16), w_br_lru.astype(BF16), w_out.astype(BF16))


def _run_mixer(x, state_pool, state_conv, state_lru, weights, *, start_pos, bblk, tt):
    hp = jnp.pad(state_pool, ((0, 0), (POOL_PAD - POOL_HIST, 0), (0, 0)))
    hc = jnp.pad(state_conv, ((0, 0), (CONV_PAD - CONV_HIST, 0), (0, 0)))
    hl = state_lru[:, None, :]
    x1, npool, nconv, nlru = _mixer_call(x, hp, hc, hl, weights, start_pos=start_pos, bblk=bblk, tt=tt)
    return x1, npool, nconv, nlru[:, 0, :]


def _routing_plan(topi, rank, counts, *, tm, n_tiles):
    n_exp = counts.shape[0]
    tiles_per = (counts + tm - 1) // tm
    tiles_cum = jnp.cumsum(tiles_per)
    base = (tiles_cum - tiles_per) * tm
    experts = jnp.arange(n_exp, dtype=jnp.int32)[:, None, None]
    pos = rank + jnp.sum(jnp.where(topi[None] == experts, base[:, None, None], 0), axis=0)
    n_used = tiles_cum[-1]
    tile_ids = jnp.minimum(jnp.arange(n_tiles, dtype=jnp.int32), n_used - 1)
    tile_expert = jnp.sum((tiles_cum[None, :] <= tile_ids[:, None]).astype(jnp.int32), axis=1)
    return pos, tile_expert, n_used.reshape(1).astype(jnp.int32)


def kernel(x_prompt, x_sample, state_pool, state_conv, state_lru, norm_mix, w_in, b_gate, pool_w, pool_scale,
           conv_w, conv_b, lru_wa, lru_ba, lru_wi, lru_bi, lru_lambda, w_br_pool, w_br_lru, w_out, norm_ffn,
           w_router, b_router, w_gu, b_gu, w_dn, b_dn, norm_final):
    bp, tp, d = x_prompt.shape
    bs, ts, _ = x_sample.shape
    n_exp = w_router.shape[-1]
    mw = _mixer_weights(norm_mix[0], w_in[0], b_gate[0], pool_w[0], pool_scale[0], conv_w[0], conv_b[0],
                        lru_wa[0], lru_ba[0], lru_wi[0], lru_bi[0], lru_lambda[0], w_br_pool[0], w_br_lru[0],
                        w_out[0])
    zeros = lambda *shape: jnp.zeros(shape, x_prompt.dtype)
    x1_p, pool_p, conv_p, lru_p = _run_mixer(
        x_prompt, zeros(bp, POOL_HIST, d), zeros(bp, CONV_HIST, d), zeros(bp, d), mw,
        start_pos=0, bblk=1, tt=MIXER_TILE)
    x1_s, pool_s, conv_s, lru_s = _run_mixer(
        x_sample, state_pool[0], state_conv[0], state_lru[0], mw, start_pos=PAST_LEN, bblk=bs, tt=ts)

    n_p, n_s = bp * tp, bs * ts
    n = n_p + n_s
    chunks = d // LANES
    x1_p, x1_s = x1_p.reshape(n_p, d), x1_s.reshape(n_s, d)
    xn2, topi, topw, rank, cnt = _router_call(
        x1_p, x1_s, norm_ffn[0], w_router[0].T.astype(BF16), b_router[0], tb=TOKEN_TILE)

    n_tiles = (n * TOP_K + n_exp * (EXPERT_TILE - 1)) // EXPERT_TILE
    p_rows = n_tiles * EXPERT_TILE
    pos, tile_expert, n_used = _routing_plan(
        topi, rank, cnt[:, 0].astype(jnp.int32), tm=EXPERT_TILE, n_tiles=n_tiles)
    w, steps = _sc_step_rows(n)
    pos_steps = pos.reshape(TOP_K, SC_WORKERS * steps, w).transpose(1, 0, 2)

    xs = _dispatch_rows(xn2.reshape(n, chunks, LANES), pos_steps, p_rows)
    ys = _expert_call(tile_expert, n_used, xs.reshape(p_rows * chunks, LANES), w_gu[0].astype(BF16), b_gu[0],
                      w_dn[0].astype(BF16), b_dn[0], tm=EXPERT_TILE)
    yk = _combine_rows(ys.reshape(p_rows, chunks, LANES), pos_steps, n)
    y_p, y_s = _final_call(x1_p, x1_s, yk.reshape(TOP_K, n * chunks, LANES), topw.T, norm_final, tb=TOKEN_TILE)

    return (y_p.reshape(bp, tp, d), y_s.reshape(bs, ts, d),
            pool_p[None], conv_p[None], lru_p[None], pool_s[None], conv_s[None], lru_s[None])
```

```python
import functools

import jax
import jax.numpy as jnp
from jax import lax
from jax.experimental import pallas as pl
from jax.experimental.pallas import tpu as pltpu
from jax.experimental.pallas import tpu_sc as plsc

BF16 = jnp.bfloat16
F32 = jnp.float32

POOL_WINDOWS = (2, 4, 8, 16)
POOL_HIST = max(POOL_WINDOWS) - 1
CONV_WIDTH = 4
CONV_HIST = CONV_WIDTH - 1
LRU_HEADS = 16
LRU_C = 8.0
N_EXPERTS = 32
TOP_K = 4
SWIGLU_LIMIT = 7.0
SWIGLU_ALPHA = 1.702
NORM_EPS = 1e-6

SUBLANES = 8
LANES = 128
MXU_WIDTH = 256
POOL_PAD = 16
CONV_PAD = 8
VMEM_LIMIT_BYTES = 56 * 1024 * 1024

PAST_LEN = 1024
MIXER_TILE = 256
TOKEN_TILE = 512
EXPERT_TILE = 256

SC_CORES = 2
SC_SUBCORES = 16
SC_WORKERS = SC_CORES * SC_SUBCORES
SC_MAX_STEP_ROWS = 64


def _rms_norm(x, g):
    ms = jnp.mean(x * x, axis=-1, keepdims=True)
    return (x * lax.rsqrt(ms + NORM_EPS)) * g


def _dot(a, b):
    return jnp.dot(a, b, preferred_element_type=F32)


def _softplus(x):
    return jnp.maximum(x, 0.0) + jnp.log1p(jnp.exp(-jnp.abs(x)))


def _gelu_tanh(x):
    c = 0.7978845608028654
    return 0.5 * x * (1.0 + jnp.tanh(c * (x + 0.044715 * (x * x * x))))


def _lru_scan(a, b, h0):
    t_len, c = a.shape
    groups = t_len // SUBLANES
    a3 = a.reshape(groups, SUBLANES, c)
    b3 = b.reshape(groups, SUBLANES, c)
    sub = lax.broadcasted_iota(jnp.int32, (groups, SUBLANES, c), 1)
    for k in (1, 2, 4):
        a_prev = pltpu.roll(a3, k, axis=1)
        b_prev = pltpu.roll(b3, k, axis=1)
        valid = sub >= k
        b3 = jnp.where(valid, a3 * b_prev + b3, b3)
        a3 = jnp.where(valid, a3 * a_prev, a3)
    h = h0
    outs = []
    for g in range(groups):
        hg = a3[g] * h + b3[g]
        outs.append(hg)
        h = hg[SUBLANES - 1:SUBLANES, :]
    return jnp.concatenate(outs, axis=0), h


def _mixer_kernel(x_ref, hp_ref, hc_ref, hl_ref, nmix_ref, win_ref, bgate_ref, poolw_ref, pscale_ref,
                  convw_ref, convb_ref, wai_ref, ba_ref, bi_ref, lam_ref, wbrp_ref, wbrl_ref, wout_ref,
                  x1_ref, npool_ref, nconv_ref, nlru_ref,
                  pool_ext, conv_ext, h_carry, *, start_pos, bblk, tt, d):
    j = pl.program_id(1)

    @pl.when(j == 0)
    def _():
        pool_ext[:, 0:POOL_PAD, :] = hp_ref[...]
        conv_ext[:, 0:CONV_PAD, :] = hc_ref[...]
        h_carry[...] = hl_ref[...]

    rows = bblk * tt
    x = x_ref[...].reshape(rows, d)
    xnb = _rms_norm(x, nmix_ref[...]).astype(BF16)

    u_pool = _dot(xnb, win_ref[:, 0:d])
    pos1 = lax.broadcasted_iota(jnp.int32, (tt, 1), 0) + (start_pos + 1) + j * tt
    gd = d // len(POOL_WINDOWS)
    y_pool_rows = []
    for b in range(bblk):
        pool_ext[b, POOL_PAD:POOL_PAD + tt, :] = u_pool[b * tt:(b + 1) * tt, :]
        parts = []
        for g, w in enumerate(POOL_WINDOWS):
            sl = slice(g * gd, (g + 1) * gd)
            s = pool_ext[b, :, sl]
            width = 1
            while width < w:
                s = s + pltpu.roll(s, width, axis=0)
                width *= 2
            cur = s[POOL_PAD:POOL_PAD + tt, :]
            inv_cnt = 1.0 / jnp.minimum(pos1, w).astype(F32)
            pooled = cur * inv_cnt - u_pool[b * tt:(b + 1) * tt, sl]
            mixed = _dot(pooled.astype(BF16), poolw_ref[g])
            parts.append(mixed * pscale_ref[:, sl])
        y_pool_rows.append(jnp.concatenate(parts, axis=1))
        npool_ref[b] = pool_ext[b, POOL_PAD + tt - POOL_HIST:POOL_PAD + tt, :]
        pool_ext[b, 0:POOL_PAD, :] = pool_ext[b, tt:tt + POOL_PAD, :]
    y_pool = jnp.concatenate(y_pool_rows, axis=0) if bblk > 1 else y_pool_rows[0]
    br_pool = _dot(y_pool.astype(BF16), wbrp_ref[...])
    g_pool = jax.nn.sigmoid(_dot(xnb, win_ref[:, 3 * d:4 * d]) + bgate_ref[:, 0:d])
    acc = g_pool * br_pool

    u_x = _dot(xnb, win_ref[:, d:2 * d])
    xc_rows = []
    for b in range(bblk):
        conv_ext[b, CONV_PAD:CONV_PAD + tt, :] = u_x[b * tt:(b + 1) * tt, :]
        ce = conv_ext[b]
        y = convb_ref[...] + ce * convw_ref[CONV_WIDTH - 1:CONV_WIDTH, :]
        for k in range(1, CONV_WIDTH):
            y = y + pltpu.roll(ce, k, axis=0) * convw_ref[CONV_WIDTH - 1 - k:CONV_WIDTH - k, :]
        xc_rows.append(y[CONV_PAD:CONV_PAD + tt, :])
        nconv_ref[b] = conv_ext[b, CONV_PAD + tt - CONV_HIST:CONV_PAD + tt, :]
        conv_ext[b, 0:CONV_PAD, :] = conv_ext[b, tt:tt + CONV_PAD, :]
    xc = jnp.concatenate(xc_rows, axis=0) if bblk > 1 else xc_rows[0]

    n_chunks = d // MXU_WIDTH
    pre_a, pre_i = [], []
    for c in range(n_chunks):
        ai = _dot(xc[:, c * MXU_WIDTH:(c + 1) * MXU_WIDTH].astype(BF16), wai_ref[c])
        pre_a.append(ai[:, 0:MXU_WIDTH])
        pre_i.append(ai[:, MXU_WIDTH:2 * MXU_WIDTH])
    r_gate = jax.nn.sigmoid(jnp.concatenate(pre_a, axis=1) + ba_ref[...])
    i_gate = jax.nn.sigmoid(jnp.concatenate(pre_i, axis=1) + bi_ref[...])
    log_a = (-LRU_C) * r_gate * _softplus(-lam_ref[...])
    a = jnp.exp(log_a)
    mult = jnp.sqrt(-jnp.tanh(log_a) * (a * a + 1.0))
    bb = mult * (i_gate * xc)
    h_rows = []
    for b in range(bblk):
        h_b, h_last = _lru_scan(a[b * tt:(b + 1) * tt, :], bb[b * tt:(b + 1) * tt, :], h_carry[b])
        h_rows.append(h_b)
        h_carry[b] = h_last
        nlru_ref[b] = h_last
    h = jnp.concatenate(h_rows, axis=0) if bblk > 1 else h_rows[0]

    u_gate = _dot(xnb, win_ref[:, 2 * d:3 * d])
    y_lru = h * _gelu_tanh(u_gate)
    br_lru = _dot(y_lru.astype(BF16), wbrl_ref[...])
    g_lru = jax.nn.sigmoid(_dot(xnb, win_ref[:, 4 * d:5 * d]) + bgate_ref[:, d:2 * d])
    acc = acc + g_lru * br_lru

    out = _dot(acc.astype(BF16), wout_ref[...])
    x1_ref[...] = (x + out).reshape(bblk, tt, d)


def _const_spec(shape):
    nd = len(shape)
    return pl.BlockSpec(shape, lambda *_: (0,) * nd, pipeline_mode=pl.Buffered(1))


def _mixer_call(x, hist_pool, hist_conv, hist_lru, weights, *, start_pos, bblk, tt):
    bsz, t_len, d = x.shape
    grid = (bsz // bblk, t_len // tt)
    kern = functools.partial(_mixer_kernel, start_pos=start_pos, bblk=bblk, tt=tt, d=d)
    seq_spec = lambda rows: pl.BlockSpec((bblk, rows, d), lambda b, j: (b, 0, 0))
    in_specs = [
        pl.BlockSpec((bblk, tt, d), lambda b, j: (b, j, 0)),
        seq_spec(POOL_PAD), seq_spec(CONV_PAD), seq_spec(1),
    ] + [_const_spec(w.shape) for w in weights]
    out_shape = (
        jax.ShapeDtypeStruct((bsz, t_len, d), F32),
        jax.ShapeDtypeStruct((bsz, POOL_HIST, d), F32),
        jax.ShapeDtypeStruct((bsz, CONV_HIST, d), F32),
        jax.ShapeDtypeStruct((bsz, 1, d), F32),
    )
    out_specs = (
        pl.BlockSpec((bblk, tt, d), lambda b, j: (b, j, 0)),
        seq_spec(POOL_HIST), seq_spec(CONV_HIST), seq_spec(1),
    )
    return pl.pallas_call(
        kern,
        grid=grid,
        in_specs=in_specs,
        out_specs=out_specs,
        out_shape=out_shape,
        scratch_shapes=[
            pltpu.VMEM((bblk, POOL_PAD + tt, d), F32),
            pltpu.VMEM((bblk, CONV_PAD + tt, d), F32),
            pltpu.VMEM((bblk, 1, d), F32),
        ],
        compiler_params=pltpu.CompilerParams(
            dimension_semantics=("arbitrary", "arbitrary"), vmem_limit_bytes=VMEM_LIMIT_BYTES),
        name="mixer",
    )(x, hist_pool, hist_conv, hist_lru, *weights)


def _load_rows(ref, rows, chunks):
    return jnp.concatenate([ref[pl.ds(s, rows, stride=chunks), :] for s in range(chunks)], axis=1)


def _store_rows(ref, val, rows, chunks):
    for s in range(chunks):
        ref[pl.ds(s, rows, stride=chunks), :] = val[:, s * LANES:(s + 1) * LANES]


def _router_kernel(x1p_ref, x1s_ref, nffn_ref, wrt_ref, br_ref, tri_ref,
                   xn_ref, topi_ref, topw_ref, rank_ref, cnt_ref, carry, *, tb, d, p_tiles):
    i = pl.program_id(0)

    @pl.when(i == 0)
    def _():
        carry[...] = jnp.zeros_like(carry)

    x1 = jnp.where(i < p_tiles, x1p_ref[...], x1s_ref[...])
    xnb = _rms_norm(x1, nffn_ref[...]).astype(BF16)
    _store_rows(xn_ref, xnb.astype(F32), tb, d // LANES)
    logits = lax.dot_general(wrt_ref[...], xnb, (((1,), (1,)), ((), ())),
                             preferred_element_type=F32) + br_ref[...]
    n_exp = logits.shape[0]
    iota_e = lax.broadcasted_iota(jnp.int32, (n_exp, tb), 0)
    work = logits
    vals, idxs, hots = [], [], []
    for _ in range(TOP_K):
        m = jnp.max(work, axis=0, keepdims=True)
        idx = jnp.min(jnp.where(work == m, iota_e, n_exp), axis=0, keepdims=True)
        hot = iota_e == idx
        vals.append(m)
        idxs.append(idx)
        hots.append(hot)
        work = jnp.where(hot, -jnp.inf, work)
    exps = [jnp.exp(v - vals[0]) for v in vals]
    denom = exps[0] + exps[1] + exps[2] + exps[3]
    topi_ref[...] = jnp.concatenate(idxs, axis=0)
    topw_ref[...] = jnp.concatenate([e / denom for e in exps], axis=0)

    sel = sum(jnp.where(h, 1.0, 0.0) for h in hots)
    before = _dot(sel.astype(BF16), tri_ref[...]) + carry[:, 0:1]
    ranks = [jnp.sum(jnp.where(h, before, 0.0), axis=0, keepdims=True) for h in hots]
    rank_ref[...] = jnp.concatenate(ranks, axis=0).astype(jnp.int32)
    carry[...] = carry[...] + jnp.sum(sel, axis=1, keepdims=True)
    cnt_ref[...] = carry[...]


def _segment_specs(tb, d, p_tiles):
    prompt = pl.BlockSpec((tb, d), lambda i: (jnp.minimum(i, p_tiles - 1), 0))
    sample = pl.BlockSpec((tb, d), lambda i: (0, 0))
    return prompt, sample


def _router_call(x1_p, x1_s, norm_ffn, w_router_t, b_router, *, tb):
    n_p, d = x1_p.shape
    n_s, _ = x1_s.shape
    assert n_p % tb == 0 and n_s == tb
    p_tiles = n_p // tb
    n = n_p + n_s
    chunks = d // LANES
    n_exp = w_router_t.shape[0]
    tri = jnp.triu(jnp.ones((tb, tb), BF16), k=1)
    tok = lambda rows, dt: (pl.BlockSpec((rows, tb), lambda i: (0, i)), jax.ShapeDtypeStruct((rows, n), dt))
    (topi_spec, topi_shape), (topw_spec, topw_shape), (rank_spec, rank_shape) = (
        tok(TOP_K, jnp.int32), tok(TOP_K, F32), tok(TOP_K, jnp.int32))
    return pl.pallas_call(
        functools.partial(_router_kernel, tb=tb, d=d, p_tiles=p_tiles),
        grid=(p_tiles + 1,),
        in_specs=[
            *_segment_specs(tb, d, p_tiles),
            _const_spec((1, d)), _const_spec((n_exp, d)), _const_spec((n_exp, 1)), _const_spec((tb, tb)),
        ],
        out_specs=(
            pl.BlockSpec((tb * chunks, LANES), lambda i: (i, 0)),
            topi_spec, topw_spec, rank_spec,
            pl.BlockSpec((n_exp, LANES), lambda i: (0, 0)),
        ),
        out_shape=(
            jax.ShapeDtypeStruct((n * chunks, LANES), F32),
            topi_shape, topw_shape, rank_shape,
            jax.ShapeDtypeStruct((n_exp, LANES), F32),
        ),
        scratch_shapes=[pltpu.VMEM((n_exp, LANES), F32)],
        compiler_params=pltpu.CompilerParams(
            dimension_semantics=("arbitrary",), vmem_limit_bytes=VMEM_LIMIT_BYTES),
        name="router",
    )(x1_p, x1_s, norm_ffn.reshape(1, d), w_router_t, b_router.reshape(n_exp, 1), tri)


def _expert_kernel(te_ref, nu_ref, xs_ref, wgu_ref, bgu_ref, wdn_ref, bdn_ref, ys_ref, wgu_bf, wdn_bf,
                   *, tm, d, dff):
    i = pl.program_id(0)
    prev = te_ref[jnp.maximum(i - 1, 0)]

    @pl.when((i == 0) | (te_ref[i] != prev))
    def _():
        wgu_bf[...] = wgu_ref[0].astype(BF16)
        wdn_bf[...] = wdn_ref[0].astype(BF16)

    @pl.when(i < nu_ref[0])
    def _():
        x = _load_rows(xs_ref, tm, d // LANES).astype(BF16)
        gu = _dot(x, wgu_bf[...]) + bgu_ref[0]
        gate = jnp.minimum(gu[:, :dff], SWIGLU_LIMIT)
        up = jnp.clip(gu[:, dff:], -SWIGLU_LIMIT, SWIGLU_LIMIT)
        hid = (up + 1.0) * gate * jax.nn.sigmoid(SWIGLU_ALPHA * gate)
        y = _dot(hid.astype(BF16), wdn_bf[...]) + bdn_ref[0]
        _store_rows(ys_ref, y, tm, d // LANES)


def _expert_call(tile_expert, n_used, xs, w_gu, b_gu, w_dn, b_dn, *, tm):
    n_exp, d, dff2 = w_gu.shape
    dff = dff2 // 2
    chunks = d // LANES
    p_rows = xs.shape[0] // chunks
    row_blk = lambda i, te, nu: (jnp.minimum(i, nu[0] - 1), 0)
    exp_blk = lambda i, te, nu: (te[i], 0, 0)
    return pl.pallas_call(
        functools.partial(_expert_kernel, tm=tm, d=d, dff=dff),
        grid_spec=pltpu.PrefetchScalarGridSpec(
            num_scalar_prefetch=2,
            grid=(p_rows // tm,),
            in_specs=[
                pl.BlockSpec((tm * chunks, LANES), row_blk),
                pl.BlockSpec((1, d, dff2), exp_blk),
                pl.BlockSpec((1, 1, dff2), exp_blk),
                pl.BlockSpec((1, dff, d), exp_blk),
                pl.BlockSpec((1, 1, d), exp_blk),
            ],
            out_specs=pl.BlockSpec((tm * chunks, LANES), row_blk),
            scratch_shapes=[pltpu.VMEM((d, dff2), BF16), pltpu.VMEM((dff, d), BF16)],
        ),
        out_shape=jax.ShapeDtypeStruct((p_rows * chunks, LANES), F32),
        compiler_params=pltpu.CompilerParams(
            dimension_semantics=("arbitrary",), vmem_limit_bytes=VMEM_LIMIT_BYTES),
        name="experts",
    )(tile_expert, n_used, xs, w_gu, b_gu.reshape(n_exp, 1, dff2), w_dn, b_dn.reshape(n_exp, 1, d))


def _final_kernel(x1p_ref, x1s_ref, yk_ref, w_ref, nfin_ref, yp_ref, ys_ref, *, tb, d, p_tiles):
    i = pl.program_id(0)
    x1 = jnp.where(i < p_tiles, x1p_ref[...], x1s_ref[...])
    w = w_ref[...]
    moe = _load_rows(yk_ref.at[0], tb, d // LANES) * w[:, 0:1]
    for k in range(1, TOP_K):
        moe = moe + _load_rows(yk_ref.at[k], tb, d // LANES) * w[:, k:k + 1]
    out = _rms_norm(x1 + moe, nfin_ref[...])

    @pl.when(i < p_tiles)
    def _():
        yp_ref[...] = out

    @pl.when(i == p_tiles)
    def _():
        ys_ref[...] = out


def _final_call(x1_p, x1_s, yk, topw_t, norm_final, *, tb):
    n_p, d = x1_p.shape
    n_s, _ = x1_s.shape
    p_tiles = n_p // tb
    chunks = d // LANES
    return pl.pallas_call(
        functools.partial(_final_kernel, tb=tb, d=d, p_tiles=p_tiles),
        grid=(p_tiles + 1,),
        in_specs=[
            *_segment_specs(tb, d, p_tiles),
            pl.BlockSpec((TOP_K, tb * chunks, LANES), lambda i: (0, i, 0)),
            pl.BlockSpec((tb, TOP_K), lambda i: (i, 0)),
            _const_spec((1, d)),
        ],
        out_specs=_segment_specs(tb, d, p_tiles),
        out_shape=(jax.ShapeDtypeStruct((n_p, d), F32), jax.ShapeDtypeStruct((n_s, d), F32)),
        compiler_params=pltpu.CompilerParams(
            dimension_semantics=("arbitrary",), vmem_limit_bytes=VMEM_LIMIT_BYTES),
        name="final",
    )(x1_p, x1_s, yk, topw_t, norm_final.reshape(1, d))


def _sc_step_rows(n):
    per_worker = n // SC_WORKERS
    assert per_worker * SC_WORKERS == n
    w = max(c for c in range(SUBLANES, SC_MAX_STEP_ROWS + 1, SUBLANES) if per_worker % c == 0)
    return w, per_worker // w


def _sc_mesh():
    return plsc.VectorSubcoreMesh(core_axis_name="c", subcore_axis_name="s",
                                  num_cores=SC_CORES, num_subcores=SC_SUBCORES)


def _dispatch_rows(x_tiles, pos_steps, p_rows):
    n, chunks, _ = x_tiles.shape
    w, steps = _sc_step_rows(n)

    @functools.partial(
        pl.kernel, mesh=_sc_mesh(),
        out_type=jax.ShapeDtypeStruct((p_rows, chunks, LANES), F32),
        scratch_types=[pltpu.VMEM((w, chunks, LANES), F32), pltpu.VMEM((TOP_K, w), jnp.int32)],
        name="dispatch",
    )
    def run(x_hbm, pos_hbm, out_hbm, buf, idx):
        wid = lax.axis_index("s") * SC_CORES + lax.axis_index("c")

        @pl.loop(0, steps)
        def _(i):
            step = wid * steps + i
            pltpu.sync_copy(x_hbm.at[pl.ds(pl.multiple_of(step * w, SUBLANES), w)], buf)
            pltpu.sync_copy(pos_hbm.at[step], idx)
            for k in range(TOP_K):
                pltpu.sync_copy(buf, out_hbm.at[idx.at[k]])

    return run(x_tiles, pos_steps)


def _combine_rows(y_tiles, pos_steps, n):
    _, chunks, _ = y_tiles.shape
    w, steps = _sc_step_rows(n)

    @functools.partial(
        pl.kernel, mesh=_sc_mesh(),
        out_type=jax.ShapeDtypeStruct((TOP_K, n, chunks, LANES), F32),
        scratch_types=[pltpu.VMEM((w, chunks, LANES), F32), pltpu.VMEM((TOP_K, w), jnp.int32)],
        name="combine",
    )
    def run(y_hbm, pos_hbm, out_hbm, buf, idx):
        wid = lax.axis_index("s") * SC_CORES + lax.axis_index("c")

        @pl.loop(0, steps)
        def _(i):
            step = wid * steps + i
            pltpu.sync_copy(pos_hbm.at[step], idx)
            for k in range(TOP_K):
                pltpu.sync_copy(y_hbm.at[idx.at[k]], buf)
                pltpu.sync_copy(buf, out_hbm.at[k, pl.ds(pl.multiple_of(step * w, SUBLANES), w)])

    return run(y_tiles, pos_steps)


def _block_diag(w, per_block):
    heads, n, _ = w.shape
    w4 = w.reshape(heads // per_block, per_block, n, n)
    eye = jnp.eye(per_block, dtype=w.dtype)
    bd = jnp.einsum('chij,hg->chigj', w4, eye)
    return bd.reshape(heads // per_block, per_block * n, per_block * n)


def _mixer_weights(norm_mix, w_in, b_gate, pool_w, pool_scale, conv_w, conv_b, lru_wa, lru_ba, lru_wi,
                   lru_bi, lru_lambda, w_br_pool, w_br_lru, w_out):
    row = lambda v: v.reshape(1, -1)
    head_dim = lru_wa.shape[-1]
    per_block = MXU_WIDTH // head_dim
    w_ai = jnp.concatenate([_block_diag(lru_wa, per_block), _block_diag(lru_wi, per_block)], axis=-1)
    return (row(norm_mix), w_in.astype(BF16), row(b_gate), pool_w.astype(BF16), row(pool_scale),
            conv_w, row(conv_b), w_ai.astype(BF16), row(lru_ba), row(lru_bi), row(lru_lambda),
            w_br_pool.astype(BF16), w_br_lru.astype(BF16), w_out.astype(BF16))


def _run_mixer(x, state_pool, state_conv, state_lru, weights, *, start_pos, bblk, tt):
    hp = jnp.pad(state_pool, ((0, 0), (POOL_PAD - POOL_HIST, 0), (0, 0)))
    hc = jnp.pad(state_conv, ((0, 0), (CONV_PAD - CONV_HIST, 0), (0, 0)))
    hl = state_lru[:, None, :]
    x1, npool, nconv, nlru = _mixer_call(x, hp, hc, hl, weights, start_pos=start_pos, bblk=bblk, tt=tt)
    return x1, npool, nconv, nlru[:, 0, :]


def _routing_plan(topi, rank, counts, *, tm, n_tiles):
    n_exp = counts.shape[0]
    tiles_per = (counts + tm - 1) // tm
    tiles_cum = jnp.cumsum(tiles_per)
    base = (tiles_cum - tiles_per) * tm
    experts = jnp.arange(n_exp, dtype=jnp.int32)[:, None, None]
    pos = rank + jnp.sum(jnp.where(topi[None] == experts, base[:, None, None], 0), axis=0)
    n_used = tiles_cum[-1]
    tile_ids = jnp.minimum(jnp.arange(n_tiles, dtype=jnp.int32), n_used - 1)
    tile_expert = jnp.sum((tiles_cum[None, :] <= tile_ids[:, None]).astype(jnp.int32), axis=1)
    return pos, tile_expert, n_used.reshape(1).astype(jnp.int32)


def kernel(x_prompt, x_sample, state_pool, state_conv, state_lru, norm_mix, w_in, b_gate, pool_w, pool_scale,
           conv_w, conv_b, lru_wa, lru_ba, lru_wi, lru_bi, lru_lambda, w_br_pool, w_br_lru, w_out, norm_ffn,
           w_router, b_router, w_gu, b_gu, w_dn, b_dn, norm_final):
    bp, tp, d = x_prompt.shape
    bs, ts, _ = x_sample.shape
    n_exp = w_router.shape[-1]
    mw = _mixer_weights(norm_mix[0], w_in[0], b_gate[0], pool_w[0], pool_scale[0], conv_w[0], conv_b[0],
                        lru_wa[0], lru_ba[0], lru_wi[0], lru_bi[0], lru_lambda[0], w_br_pool[0], w_br_lru[0],
                        w_out[0])
    zeros = lambda *shape: jnp.zeros(shape, x_prompt.dtype)
    x1_p, pool_p, conv_p, lru_p = _run_mixer(
        x_prompt, zeros(bp, POOL_HIST, d), zeros(bp, CONV_HIST, d), zeros(bp, d), mw,
        start_pos=0, bblk=1, tt=MIXER_TILE)
    x1_s, pool_s, conv_s, lru_s = _run_mixer(
        x_sample, state_pool[0], state_conv[0], state_lru[0], mw, start_pos=PAST_LEN, bblk=bs, tt=ts)

    n_p, n_s = bp * tp, bs * ts
    n = n_p + n_s
    chunks = d // LANES
    x1_p, x1_s = x1_p.reshape(n_p, d), x1_s.reshape(n_s, d)
    xn2, topi, topw, rank, cnt = _router_call(
        x1_p, x1_s, norm_ffn[0], w_router[0].T.astype(BF16), b_router[0], tb=TOKEN_TILE)

    n_tiles = (n * TOP_K + n_exp * (EXPERT_TILE - 1)) // EXPERT_TILE
    p_rows = n_tiles * EXPERT_TILE
    pos, tile_expert, n_used = _routing_plan(
        topi, rank, cnt[:, 0].astype(jnp.int32), tm=EXPERT_TILE, n_tiles=n_tiles)
    w, steps = _sc_step_rows(n)
    pos_steps = pos.reshape(TOP_K, SC_WORKERS * steps, w).transpose(1, 0, 2)

    xs = _dispatch_rows(xn2.reshape(n, chunks, LANES), pos_steps, p_rows)
    ys = _expert_call(tile_expert, n_used, xs.reshape(p_rows * chunks, LANES), w_gu[0], b_gu[0], w_dn[0], b_dn[0],
                      tm=EXPERT_TILE)
    yk = _combine_rows(ys.reshape(p_rows, chunks, LANES), pos_steps, n)
    y_p, y_s = _final_call(x1_p, x1_s, yk.reshape(TOP_K, n * chunks, LANES), topw.T, norm_final, tb=TOKEN_TILE)

    return (y_p.reshape(bp, tp, d), y_s.reshape(bs, ts, d),
            pool_p[None], conv_p[None], lru_p[None], pool_s[None], conv_s[None], lru_s[None])
```

```python
import functools
from typing import NamedTuple

import jax
import jax.numpy as jnp
from jax import lax
from jax.experimental import pallas as pl
from jax.experimental.pallas import tpu as pltpu
from jax.experimental.pallas import tpu_sc as plsc

BF16 = jnp.bfloat16
F32 = jnp.float32

POOL_WINDOWS = (2, 4, 8, 16)
POOL_HIST = max(POOL_WINDOWS) - 1
CONV_WIDTH = 4
CONV_HIST = CONV_WIDTH - 1
LRU_HEADS = 16
LRU_C = 8.0
N_EXPERTS = 32
TOP_K = 4
SWIGLU_LIMIT = 7.0
SWIGLU_ALPHA = 1.702
NORM_EPS = 1e-6

SUBLANES = 8
LANES = 128
MXU_WIDTH = 256
POOL_PAD = 16
CONV_PAD = 8
VMEM_LIMIT_BYTES = 56 * 1024 * 1024

PAST_LEN = 1024
MIXER_TILE = 256
TOKEN_TILE = 512
EXPERT_TILE = 256
PROMPT_GROUPS = 2

SC_CORES = 2
SC_SUBCORES = 16
SC_WORKERS = SC_CORES * SC_SUBCORES
SC_MAX_STEP_ROWS = 64


def _rms_norm(x, g):
    ms = jnp.mean(x * x, axis=-1, keepdims=True)
    return (x * lax.rsqrt(ms + NORM_EPS)) * g


def _dot(a, b):
    return jnp.dot(a, b, preferred_element_type=F32)


def _softplus(x):
    return jnp.maximum(x, 0.0) + jnp.log1p(jnp.exp(-jnp.abs(x)))


def _gelu_tanh(x):
    c = 0.7978845608028654
    return 0.5 * x * (1.0 + jnp.tanh(c * (x + 0.044715 * (x * x * x))))


def _lru_scan(a, b, h0):
    t_len, c = a.shape
    groups = t_len // SUBLANES
    a3 = a.reshape(groups, SUBLANES, c)
    b3 = b.reshape(groups, SUBLANES, c)
    sub = lax.broadcasted_iota(jnp.int32, (groups, SUBLANES, c), 1)
    for k in (1, 2, 4):
        a_prev = pltpu.roll(a3, k, axis=1)
        b_prev = pltpu.roll(b3, k, axis=1)
        valid = sub >= k
        b3 = jnp.where(valid, a3 * b_prev + b3, b3)
        a3 = jnp.where(valid, a3 * a_prev, a3)
    h = h0
    outs = []
    for g in range(groups):
        hg = a3[g] * h + b3[g]
        outs.append(hg)
        h = hg[SUBLANES - 1:SUBLANES, :]
    return jnp.concatenate(outs, axis=0), h


def _mixer_kernel(x_ref, hp_ref, hc_ref, hl_ref, nmix_ref, win_ref, bgate_ref, poolw_ref, pscale_ref,
                  convw_ref, convb_ref, wai_ref, ba_ref, bi_ref, lam_ref, wbrp_ref, wbrl_ref, wout_ref,
                  x1_ref, npool_ref, nconv_ref, nlru_ref,
                  pool_ext, conv_ext, h_carry, *, start_pos, bblk, tt, d):
    j = pl.program_id(1)

    @pl.when(j == 0)
    def _():
        pool_ext[:, 0:POOL_PAD, :] = hp_ref[...]
        conv_ext[:, 0:CONV_PAD, :] = hc_ref[...]
        h_carry[...] = hl_ref[...]

    rows = bblk * tt
    x = x_ref[...].reshape(rows, d)
    xnb = _rms_norm(x, nmix_ref[...]).astype(BF16)

    u_pool = _dot(xnb, win_ref[:, 0:d])
    pos1 = lax.broadcasted_iota(jnp.int32, (tt, 1), 0) + (start_pos + 1) + j * tt
    gd = d // len(POOL_WINDOWS)
    y_pool_rows = []
    for b in range(bblk):
        pool_ext[b, POOL_PAD:POOL_PAD + tt, :] = u_pool[b * tt:(b + 1) * tt, :]
        parts = []
        for g, w in enumerate(POOL_WINDOWS):
            sl = slice(g * gd, (g + 1) * gd)
            s = pool_ext[b, :, sl]
            width = 1
            while width < w:
                s = s + pltpu.roll(s, width, axis=0)
                width *= 2
            cur = s[POOL_PAD:POOL_PAD + tt, :]
            inv_cnt = 1.0 / jnp.minimum(pos1, w).astype(F32)
            pooled = cur * inv_cnt - u_pool[b * tt:(b + 1) * tt, sl]
            mixed = _dot(pooled.astype(BF16), poolw_ref[g])
            parts.append(mixed * pscale_ref[:, sl])
        y_pool_rows.append(jnp.concatenate(parts, axis=1))
        npool_ref[b] = pool_ext[b, POOL_PAD + tt - POOL_HIST:POOL_PAD + tt, :]
        pool_ext[b, 0:POOL_PAD, :] = pool_ext[b, tt:tt + POOL_PAD, :]
    y_pool = jnp.concatenate(y_pool_rows, axis=0) if bblk > 1 else y_pool_rows[0]
    br_pool = _dot(y_pool.astype(BF16), wbrp_ref[...])
    g_pool = jax.nn.sigmoid(_dot(xnb, win_ref[:, 3 * d:4 * d]) + bgate_ref[:, 0:d])
    acc = g_pool * br_pool

    u_x = _dot(xnb, win_ref[:, d:2 * d])
    xc_rows = []
    for b in range(bblk):
        conv_ext[b, CONV_PAD:CONV_PAD + tt, :] = u_x[b * tt:(b + 1) * tt, :]
        ce = conv_ext[b]
        y = convb_ref[...] + ce * convw_ref[CONV_WIDTH - 1:CONV_WIDTH, :]
        for k in range(1, CONV_WIDTH):
            y = y + pltpu.roll(ce, k, axis=0) * convw_ref[CONV_WIDTH - 1 - k:CONV_WIDTH - k, :]
        xc_rows.append(y[CONV_PAD:CONV_PAD + tt, :])
        nconv_ref[b] = conv_ext[b, CONV_PAD + tt - CONV_HIST:CONV_PAD + tt, :]
        conv_ext[b, 0:CONV_PAD, :] = conv_ext[b, tt:tt + CONV_PAD, :]
    xc = jnp.concatenate(xc_rows, axis=0) if bblk > 1 else xc_rows[0]

    n_chunks = d // MXU_WIDTH
    pre_a, pre_i = [], []
    for c in range(n_chunks):
        ai = _dot(xc[:, c * MXU_WIDTH:(c + 1) * MXU_WIDTH].astype(BF16), wai_ref[c])
        pre_a.append(ai[:, 0:MXU_WIDTH])
        pre_i.append(ai[:, MXU_WIDTH:2 * MXU_WIDTH])
    r_gate = jax.nn.sigmoid(jnp.concatenate(pre_a, axis=1) + ba_ref[...])
    i_gate = jax.nn.sigmoid(jnp.concatenate(pre_i, axis=1) + bi_ref[...])
    log_a = (-LRU_C) * r_gate * _softplus(-lam_ref[...])
    a = jnp.exp(log_a)
    mult = jnp.sqrt(-jnp.tanh(log_a) * (a * a + 1.0))
    bb = mult * (i_gate * xc)
    h_rows = []
    for b in range(bblk):
        h_b, h_last = _lru_scan(a[b * tt:(b + 1) * tt, :], bb[b * tt:(b + 1) * tt, :], h_carry[b])
        h_rows.append(h_b)
        h_carry[b] = h_last
        nlru_ref[b] = h_last
    h = jnp.concatenate(h_rows, axis=0) if bblk > 1 else h_rows[0]

    u_gate = _dot(xnb, win_ref[:, 2 * d:3 * d])
    y_lru = h * _gelu_tanh(u_gate)
    br_lru = _dot(y_lru.astype(BF16), wbrl_ref[...])
    g_lru = jax.nn.sigmoid(_dot(xnb, win_ref[:, 4 * d:5 * d]) + bgate_ref[:, d:2 * d])
    acc = acc + g_lru * br_lru

    out = _dot(acc.astype(BF16), wout_ref[...])
    x1_ref[...] = (x + out).reshape(bblk, tt, d)


def _const_spec(shape):
    nd = len(shape)
    return pl.BlockSpec(shape, lambda *_: (0,) * nd, pipeline_mode=pl.Buffered(1))


def _mixer_call(x, hist_pool, hist_conv, hist_lru, weights, *, start_pos, bblk, tt, seq0=0):
    _, t_len, d = x.shape
    bsz = hist_pool.shape[0]
    assert seq0 % bblk == 0
    grid = (bsz // bblk, t_len // tt)
    kern = functools.partial(_mixer_kernel, start_pos=start_pos, bblk=bblk, tt=tt, d=d)
    seq_spec = lambda rows: pl.BlockSpec((bblk, rows, d), lambda b, j: (b, 0, 0))
    in_specs = [
        pl.BlockSpec((bblk, tt, d), lambda b, j: (b + seq0 // bblk, j, 0)),
        seq_spec(POOL_PAD), seq_spec(CONV_PAD), seq_spec(1),
    ] + [_const_spec(w.shape) for w in weights]
    out_shape = (
        jax.ShapeDtypeStruct((bsz, t_len, d), F32),
        jax.ShapeDtypeStruct((bsz, POOL_HIST, d), F32),
        jax.ShapeDtypeStruct((bsz, CONV_HIST, d), F32),
        jax.ShapeDtypeStruct((bsz, 1, d), F32),
    )
    out_specs = (
        pl.BlockSpec((bblk, tt, d), lambda b, j: (b, j, 0)),
        seq_spec(POOL_HIST), seq_spec(CONV_HIST), seq_spec(1),
    )
    return pl.pallas_call(
        kern,
        grid=grid,
        in_specs=in_specs,
        out_specs=out_specs,
        out_shape=out_shape,
        scratch_shapes=[
            pltpu.VMEM((bblk, POOL_PAD + tt, d), F32),
            pltpu.VMEM((bblk, CONV_PAD + tt, d), F32),
            pltpu.VMEM((bblk, 1, d), F32),
        ],
        compiler_params=pltpu.CompilerParams(
            dimension_semantics=("arbitrary", "arbitrary"), vmem_limit_bytes=VMEM_LIMIT_BYTES),
        name="mixer",
    )(x, hist_pool, hist_conv, hist_lru, *weights)


def _load_rows(ref, rows, chunks):
    return jnp.concatenate([ref[pl.ds(s, rows, stride=chunks), :] for s in range(chunks)], axis=1)


def _store_rows(ref, val, rows, chunks):
    for s in range(chunks):
        ref[pl.ds(s, rows, stride=chunks), :] = val[:, s * LANES:(s + 1) * LANES]


class _Window(NamedTuple):
    first: int
    count: int


def _window_specs(windows, tb, d):
    specs, start = [], 0
    for win in windows:
        specs.append(pl.BlockSpec(
            (tb, d), lambda i, win=win, start=start: (win.first + jnp.clip(i - start, 0, win.count - 1), 0)))
        start += win.count
    return specs


def _window_load(i, windows, refs):
    val, start = refs[-1][...], sum(w.count for w in windows[:-1])
    for win, ref in zip(reversed(windows[:-1]), reversed(refs[:-1])):
        val = jnp.where(i < start, ref[...], val)
        start -= win.count
    return val


def _window_store(i, windows, refs, val):
    start = 0
    for win, ref in zip(windows, refs):
        @pl.when((i >= start) & (i < start + win.count))
        def _(ref=ref):
            ref[...] = val
        start += win.count


def _router_kernel(*refs, tb, d, windows):
    x1_refs = refs[:len(windows)]
    (nffn_ref, wrt_ref, br_ref, tri_ref, xn_ref, topi_ref, topw_ref, rank_ref, cnt_ref, carry) = refs[len(windows):]
    i = pl.program_id(0)

    @pl.when(i == 0)
    def _():
        carry[...] = jnp.zeros_like(carry)

    xnb = _rms_norm(_window_load(i, windows, x1_refs), nffn_ref[...]).astype(BF16)
    _store_rows(xn_ref, xnb.astype(F32), tb, d // LANES)
    logits = lax.dot_general(wrt_ref[...], xnb, (((1,), (1,)), ((), ())),
                             preferred_element_type=F32) + br_ref[...]
    n_exp = logits.shape[0]
    iota_e = lax.broadcasted_iota(jnp.int32, (n_exp, tb), 0)
    work = logits
    vals, idxs, hots = [], [], []
    for _ in range(TOP_K):
        m = jnp.max(work, axis=0, keepdims=True)
        idx = jnp.min(jnp.where(work == m, iota_e, n_exp), axis=0, keepdims=True)
        hot = iota_e == idx
        vals.append(m)
        idxs.append(idx)
        hots.append(hot)
        work = jnp.where(hot, -jnp.inf, work)
    exps = [jnp.exp(v - vals[0]) for v in vals]
    denom = exps[0] + exps[1] + exps[2] + exps[3]
    topi_ref[...] = jnp.concatenate(idxs, axis=0)
    topw_ref[...] = jnp.concatenate([e / denom for e in exps], axis=0)

    sel = sum(jnp.where(h, 1.0, 0.0) for h in hots)
    before = _dot(sel.astype(BF16), tri_ref[...]) + carry[:, 0:1]
    ranks = [jnp.sum(jnp.where(h, before, 0.0), axis=0, keepdims=True) for h in hots]
    rank_ref[...] = jnp.concatenate(ranks, axis=0).astype(jnp.int32)
    carry[...] = carry[...] + jnp.sum(sel, axis=1, keepdims=True)
    cnt_ref[...] = carry[...]


def _router_call(x1_parts, windows, norm_ffn, w_router_t, b_router, *, tb):
    d = x1_parts[0].shape[1]
    n_tiles = sum(w.count for w in windows)
    n = n_tiles * tb
    chunks = d // LANES
    n_exp = w_router_t.shape[0]
    tri = jnp.triu(jnp.ones((tb, tb), BF16), k=1)
    tok = lambda rows, dt: (pl.BlockSpec((rows, tb), lambda i: (0, i)), jax.ShapeDtypeStruct((rows, n), dt))
    (topi_spec, topi_shape), (topw_spec, topw_shape), (rank_spec, rank_shape) = (
        tok(TOP_K, jnp.int32), tok(TOP_K, F32), tok(TOP_K, jnp.int32))
    return pl.pallas_call(
        functools.partial(_router_kernel, tb=tb, d=d, windows=tuple(windows)),
        grid=(n_tiles,),
        in_specs=[
            *_window_specs(windows, tb, d),
            _const_spec((1, d)), _const_spec((n_exp, d)), _const_spec((n_exp, 1)), _const_spec((tb, tb)),
        ],
        out_specs=(
            pl.BlockSpec((tb * chunks, LANES), lambda i: (i, 0)),
            topi_spec, topw_spec, rank_spec,
            pl.BlockSpec((n_exp, LANES), lambda i: (0, 0)),
        ),
        out_shape=(
            jax.ShapeDtypeStruct((n * chunks, LANES), F32),
            topi_shape, topw_shape, rank_shape,
            jax.ShapeDtypeStruct((n_exp, LANES), F32),
        ),
        scratch_shapes=[pltpu.VMEM((n_exp, LANES), F32)],
        compiler_params=pltpu.CompilerParams(
            dimension_semantics=("arbitrary",), vmem_limit_bytes=VMEM_LIMIT_BYTES),
        name="router",
    )(*x1_parts, norm_ffn.reshape(1, d), w_router_t, b_router.reshape(n_exp, 1), tri)


def _expert_kernel(te_ref, nu_ref, xs_ref, wgu_ref, bgu_ref, wdn_ref, bdn_ref, ys_ref, wgu_bf, wdn_bf,
                   *, tm, d, dff):
    i = pl.program_id(0)
    prev = te_ref[jnp.maximum(i - 1, 0)]

    @pl.when((i == 0) | (te_ref[i] != prev))
    def _():
        wgu_bf[...] = wgu_ref[0].astype(BF16)
        wdn_bf[...] = wdn_ref[0].astype(BF16)

    @pl.when(i < nu_ref[0])
    def _():
        x = _load_rows(xs_ref, tm, d // LANES).astype(BF16)
        gu = _dot(x, wgu_bf[...]) + bgu_ref[0]
        gate = jnp.minimum(gu[:, :dff], SWIGLU_LIMIT)
        up = jnp.clip(gu[:, dff:], -SWIGLU_LIMIT, SWIGLU_LIMIT)
        hid = (up + 1.0) * gate * jax.nn.sigmoid(SWIGLU_ALPHA * gate)
        y = _dot(hid.astype(BF16), wdn_bf[...]) + bdn_ref[0]
        _store_rows(ys_ref, y, tm, d // LANES)


def _expert_call(tile_expert, n_used, xs, w_gu, b_gu, w_dn, b_dn, *, tm):
    n_exp, d, dff2 = w_gu.shape
    dff = dff2 // 2
    chunks = d // LANES
    p_rows = xs.shape[0] // chunks
    row_blk = lambda i, te, nu: (jnp.minimum(i, nu[0] - 1), 0)
    exp_blk = lambda i, te, nu: (te[i], 0, 0)
    return pl.pallas_call(
        functools.partial(_expert_kernel, tm=tm, d=d, dff=dff),
        grid_spec=pltpu.PrefetchScalarGridSpec(
            num_scalar_prefetch=2,
            grid=(p_rows // tm,),
            in_specs=[
                pl.BlockSpec((tm * chunks, LANES), row_blk),
                pl.BlockSpec((1, d, dff2), exp_blk),
                pl.BlockSpec((1, 1, dff2), exp_blk),
                pl.BlockSpec((1, dff, d), exp_blk),
                pl.BlockSpec((1, 1, d), exp_blk),
            ],
            out_specs=pl.BlockSpec((tm * chunks, LANES), row_blk),
            scratch_shapes=[pltpu.VMEM((d, dff2), BF16), pltpu.VMEM((dff, d), BF16)],
        ),
        out_shape=jax.ShapeDtypeStruct((p_rows * chunks, LANES), F32),
        compiler_params=pltpu.CompilerParams(
            dimension_semantics=("arbitrary",), vmem_limit_bytes=VMEM_LIMIT_BYTES),
        name="experts",
    )(tile_expert, n_used, xs, w_gu, b_gu.reshape(n_exp, 1, dff2), w_dn, b_dn.reshape(n_exp, 1, d))


def _final_kernel(*refs, tb, d, in_windows, out_windows, n_alias):
    n_in, n_out = len(in_windows), len(out_windows)
    x1_refs = refs[:n_in]
    yk_ref, w_ref, nfin_ref = refs[n_in:n_in + 3]
    out_refs = refs[n_in + 3 + n_alias:n_in + 3 + n_alias + n_out]
    i = pl.program_id(0)
    w = w_ref[...]
    moe = _load_rows(yk_ref.at[0], tb, d // LANES) * w[:, 0:1]
    for k in range(1, TOP_K):
        moe = moe + _load_rows(yk_ref.at[k], tb, d // LANES) * w[:, k:k + 1]
    out = _rms_norm(_window_load(i, in_windows, x1_refs) + moe, nfin_ref[...])
    _window_store(i, out_windows, out_refs, out)


def _final_call(x1_parts, in_windows, yk, topw_t, norm_final, out_shapes, out_windows, out_init, *, tb):
    d = x1_parts[0].shape[1]
    n_tiles = sum(w.count for w in in_windows)
    chunks = d // LANES
    alias_in = [a for a in out_init if a is not None]
    n_fixed = len(x1_parts) + 3
    aliases, j = {}, 0
    for k, a in enumerate(out_init):
        if a is not None:
            aliases[n_fixed + j] = k
            j += 1
    return pl.pallas_call(
        functools.partial(_final_kernel, tb=tb, d=d, in_windows=tuple(in_windows),
                          out_windows=tuple(out_windows), n_alias=len(alias_in)),
        grid=(n_tiles,),
        in_specs=[
            *_window_specs(in_windows, tb, d),
            pl.BlockSpec((TOP_K, tb * chunks, LANES), lambda i: (0, i, 0)),
            pl.BlockSpec((tb, TOP_K), lambda i: (i, 0)),
            _const_spec((1, d)),
            *[pl.BlockSpec(memory_space=pl.ANY) for _ in alias_in],
        ],
        out_specs=_window_specs(out_windows, tb, d),
        out_shape=[jax.ShapeDtypeStruct(s, F32) for s in out_shapes],
        input_output_aliases=aliases,
        compiler_params=pltpu.CompilerParams(
            dimension_semantics=("arbitrary",), vmem_limit_bytes=VMEM_LIMIT_BYTES),
        name="final",
    )(*x1_parts, yk, topw_t, norm_final.reshape(1, d), *alias_in)


def _sc_step_rows(n):
    per_worker = n // SC_WORKERS
    assert per_worker * SC_WORKERS == n
    w = max(c for c in range(SUBLANES, SC_MAX_STEP_ROWS + 1, SUBLANES) if per_worker % c == 0)
    return w, per_worker // w


def _sc_mesh():
    return plsc.VectorSubcoreMesh(core_axis_name="c", subcore_axis_name="s",
                                  num_cores=SC_CORES, num_subcores=SC_SUBCORES)


def _dispatch_rows(x_tiles, pos_steps, p_rows):
    n, chunks, _ = x_tiles.shape
    w, steps = _sc_step_rows(n)

    @functools.partial(
        pl.kernel, mesh=_sc_mesh(),
        out_type=jax.ShapeDtypeStruct((p_rows, chunks, LANES), F32),
        scratch_types=[pltpu.VMEM((w, chunks, LANES), F32), pltpu.VMEM((TOP_K, w), jnp.int32)],
        name="dispatch",
    )
    def run(x_hbm, pos_hbm, out_hbm, buf, idx):
        wid = lax.axis_index("s") * SC_CORES + lax.axis_index("c")

        @pl.loop(0, steps)
        def _(i):
            step = wid * steps + i
            pltpu.sync_copy(x_hbm.at[pl.ds(pl.multiple_of(step * w, SUBLANES), w)], buf)
            pltpu.sync_copy(pos_hbm.at[step], idx)
            for k in range(TOP_K):
                pltpu.sync_copy(buf, out_hbm.at[idx.at[k]])

    return run(x_tiles, pos_steps)


def _combine_rows(y_tiles, pos_steps, n):
    _, chunks, _ = y_tiles.shape
    w, steps = _sc_step_rows(n)

    @functools.partial(
        pl.kernel, mesh=_sc_mesh(),
        out_type=jax.ShapeDtypeStruct((TOP_K, n, chunks, LANES), F32),
        scratch_types=[pltpu.VMEM((w, chunks, LANES), F32), pltpu.VMEM((TOP_K, w), jnp.int32)],
        name="combine",
    )
    def run(y_hbm, pos_hbm, out_hbm, buf, idx):
        wid = lax.axis_index("s") * SC_CORES + lax.axis_index("c")

        @pl.loop(0, steps)
        def _(i):
            step = wid * steps + i
            pltpu.sync_copy(pos_hbm.at[step], idx)
            for k in range(TOP_K):
                pltpu.sync_copy(y_hbm.at[idx.at[k]], buf)
                pltpu.sync_copy(buf, out_hbm.at[k, pl.ds(pl.multiple_of(step * w, SUBLANES), w)])

    return run(y_tiles, pos_steps)


def _block_diag(w, per_block):
    heads, n, _ = w.shape
    w4 = w.reshape(heads // per_block, per_block, n, n)
    eye = jnp.eye(per_block, dtype=w.dtype)
    bd = jnp.einsum('chij,hg->chigj', w4, eye)
    return bd.reshape(heads // per_block, per_block * n, per_block * n)


def _mixer_weights(norm_mix, w_in, b_gate, pool_w, pool_scale, conv_w, conv_b, lru_wa, lru_ba, lru_wi,
                   lru_bi, lru_lambda, w_br_pool, w_br_lru, w_out):
    row = lambda v: v.reshape(1, -1)
    head_dim = lru_wa.shape[-1]
    per_block = MXU_WIDTH // head_dim
    w_ai = jnp.concatenate([_block_diag(lru_wa, per_block), _block_diag(lru_wi, per_block)], axis=-1)
    return (row(norm_mix), w_in.astype(BF16), row(b_gate), pool_w.astype(BF16), row(pool_scale),
            conv_w, row(conv_b), w_ai.astype(BF16), row(lru_ba), row(lru_bi), row(lru_lambda),
            w_br_pool.astype(BF16), w_br_lru.astype(BF16), w_out.astype(BF16))


def _run_mixer(x, state_pool, state_conv, state_lru, weights, *, start_pos, bblk, tt, seq0=0):
    hp = jnp.pad(state_pool, ((0, 0), (POOL_PAD - POOL_HIST, 0), (0, 0)))
    hc = jnp.pad(state_conv, ((0, 0), (CONV_PAD - CONV_HIST, 0), (0, 0)))
    hl = state_lru[:, None, :]
    x1, npool, nconv, nlru = _mixer_call(x, hp, hc, hl, weights, start_pos=start_pos, bblk=bblk, tt=tt,
                                         seq0=seq0)
    return x1, npool, nconv, nlru[:, 0, :]


def _routing_plan(topi, rank, counts, *, tm, n_tiles):
    n_exp = counts.shape[0]
    tiles_per = (counts + tm - 1) // tm
    tiles_cum = jnp.cumsum(tiles_per)
    base = (tiles_cum - tiles_per) * tm
    experts = jnp.arange(n_exp, dtype=jnp.int32)[:, None, None]
    pos = rank + jnp.sum(jnp.where(topi[None] == experts, base[:, None, None], 0), axis=0)
    n_used = tiles_cum[-1]
    tile_ids = jnp.minimum(jnp.arange(n_tiles, dtype=jnp.int32), n_used - 1)
    tile_expert = jnp.sum((tiles_cum[None, :] <= tile_ids[:, None]).astype(jnp.int32), axis=1)
    return pos, tile_expert, n_used.reshape(1).astype(jnp.int32)


def _moe_rows(x1_parts, windows, norm_ffn, w_router_t, b_router, w_gu, b_gu, w_dn, b_dn):
    d = x1_parts[0].shape[1]
    chunks = d // LANES
    n_exp = w_router_t.shape[0]
    n = sum(w.count for w in windows) * TOKEN_TILE
    xn2, topi, topw, rank, cnt = _router_call(x1_parts, windows, norm_ffn, w_router_t, b_router, tb=TOKEN_TILE)
    n_tiles = (n * TOP_K + n_exp * (EXPERT_TILE - 1)) // EXPERT_TILE
    p_rows = n_tiles * EXPERT_TILE
    pos, tile_expert, n_used = _routing_plan(
        topi, rank, cnt[:, 0].astype(jnp.int32), tm=EXPERT_TILE, n_tiles=n_tiles)
    w, steps = _sc_step_rows(n)
    pos_steps = pos.reshape(TOP_K, SC_WORKERS * steps, w).transpose(1, 0, 2)
    xs = _dispatch_rows(xn2.reshape(n, chunks, LANES), pos_steps, p_rows)
    ys = _expert_call(tile_expert, n_used, xs.reshape(p_rows * chunks, LANES), w_gu, b_gu, w_dn, b_dn,
                      tm=EXPERT_TILE)
    yk = _combine_rows(ys.reshape(p_rows, chunks, LANES), pos_steps, n)
    return yk.reshape(TOP_K, n * chunks, LANES), topw


def kernel(x_prompt, x_sample, state_pool, state_conv, state_lru, norm_mix, w_in, b_gate, pool_w, pool_scale,
           conv_w, conv_b, lru_wa, lru_ba, lru_wi, lru_bi, lru_lambda, w_br_pool, w_br_lru, w_out, norm_ffn,
           w_router, b_router, w_gu, b_gu, w_dn, b_dn, norm_final):
    bp, tp, d = x_prompt.shape
    bs, ts, _ = x_sample.shape
    n_p, n_s = bp * tp, bs * ts
    mw = _mixer_weights(norm_mix[0], w_in[0], b_gate[0], pool_w[0], pool_scale[0], conv_w[0], conv_b[0],
                        lru_wa[0], lru_ba[0], lru_wi[0], lru_bi[0], lru_lambda[0], w_br_pool[0], w_br_lru[0],
                        w_out[0])
    w_router_t = w_router[0].T.astype(BF16)
    zeros = lambda *shape: jnp.zeros(shape, x_prompt.dtype)

    seqs = bp // PROMPT_GROUPS
    g_tiles = seqs * tp // TOKEN_TILE
    s_tiles = n_s // TOKEN_TILE
    assert seqs * PROMPT_GROUPS == bp and g_tiles * TOKEN_TILE == seqs * tp and s_tiles * TOKEN_TILE == n_s
    x1_s, pool_s, conv_s, lru_s = _run_mixer(
        x_sample, state_pool[0], state_conv[0], state_lru[0], mw, start_pos=PAST_LEN, bblk=bs, tt=ts)
    y_p, y_s = None, None
    pools, convs, lrus = [], [], []
    for g in range(PROMPT_GROUPS):
        x1_g, pool_g, conv_g, lru_g = _run_mixer(
            x_prompt, zeros(seqs, POOL_HIST, d), zeros(seqs, CONV_HIST, d), zeros(seqs, d), mw,
            start_pos=0, bblk=1, tt=MIXER_TILE, seq0=g * seqs)
        pools.append(pool_g)
        convs.append(conv_g)
        lrus.append(lru_g)
        last = g == PROMPT_GROUPS - 1
        parts, windows = [x1_g.reshape(seqs * tp, d)], [_Window(0, g_tiles)]
        out_shapes, out_windows, out_init = [(n_p, d)], [_Window(g * g_tiles, g_tiles)], [y_p]
        if last:
            parts.append(x1_s.reshape(n_s, d))
            windows.append(_Window(0, s_tiles))
            out_shapes.append((n_s, d))
            out_windows.append(_Window(0, s_tiles))
            out_init.append(None)
        yk, topw = _moe_rows(parts, windows, norm_ffn[0], w_router_t, b_router[0], w_gu[0], b_gu[0], w_dn[0],
                             b_dn[0])
        outs = _final_call(parts, windows, yk, topw.T, norm_final, out_shapes, out_windows, out_init,
                           tb=TOKEN_TILE)
        y_p = outs[0]
        if last:
            y_s = outs[1]

    cat = lambda xs: jnp.concatenate(xs, axis=0)[None]
    return (y_p.reshape(bp, tp, d), y_s.reshape(bs, ts, d), cat(pools), cat(convs), cat(lrus),
            pool_s[None], conv_s[None], lru_s[None])
```

```python
import functools
from typing import NamedTuple

import jax
import jax.numpy as jnp
from jax import lax
from jax.experimental import pallas as pl
from jax.experimental.pallas import tpu as pltpu
from jax.experimental.pallas import tpu_sc as plsc

BF16 = jnp.bfloat16
F32 = jnp.float32

POOL_WINDOWS = (2, 4, 8, 16)
POOL_HIST = max(POOL_WINDOWS) - 1
CONV_WIDTH = 4
CONV_HIST = CONV_WIDTH - 1
LRU_HEADS = 16
LRU_C = 8.0
N_EXPERTS = 32
TOP_K = 4
SWIGLU_LIMIT = 7.0
SWIGLU_ALPHA = 1.702
NORM_EPS = 1e-6

SUBLANES = 8
LANES = 128
MXU_WIDTH = 256
POOL_PAD = 16
CONV_PAD = 8
VMEM_LIMIT_BYTES = 56 * 1024 * 1024

PAST_LEN = 1024
MIXER_TILE = 256
TOKEN_TILE = 512
EXPERT_TILE = 256
PROMPT_GROUPS = 2

SC_CORES = 2
SC_SUBCORES = 16
SC_WORKERS = SC_CORES * SC_SUBCORES
SC_MAX_STEP_ROWS = 64


def _rms_norm(x, g):
    ms = jnp.mean(x * x, axis=-1, keepdims=True)
    return (x * lax.rsqrt(ms + NORM_EPS)) * g


def _dot(a, b):
    return jnp.dot(a, b, preferred_element_type=F32)


def _softplus(x):
    return jnp.maximum(x, 0.0) + jnp.log1p(jnp.exp(-jnp.abs(x)))


def _gelu_tanh(x):
    c = 0.7978845608028654
    return 0.5 * x * (1.0 + jnp.tanh(c * (x + 0.044715 * (x * x * x))))


def _lru_scan(a, b, h0):
    t_len, c = a.shape
    groups = t_len // SUBLANES
    a3 = a.reshape(groups, SUBLANES, c)
    b3 = b.reshape(groups, SUBLANES, c)
    sub = lax.broadcasted_iota(jnp.int32, (groups, SUBLANES, c), 1)
    for k in (1, 2, 4):
        a_prev = pltpu.roll(a3, k, axis=1)
        b_prev = pltpu.roll(b3, k, axis=1)
        valid = sub >= k
        b3 = jnp.where(valid, a3 * b_prev + b3, b3)
        a3 = jnp.where(valid, a3 * a_prev, a3)
    h = h0
    outs = []
    for g in range(groups):
        hg = a3[g] * h + b3[g]
        outs.append(hg)
        h = hg[SUBLANES - 1:SUBLANES, :]
    return jnp.concatenate(outs, axis=0), h


def _mixer_kernel(x_ref, hp_ref, hc_ref, hl_ref, nmix_ref, win_ref, bgate_ref, poolw_ref, pscale_ref,
                  convw_ref, convb_ref, wai_ref, ba_ref, bi_ref, lam_ref, wbrp_ref, wbrl_ref, wout_ref,
                  x1_ref, npool_ref, nconv_ref, nlru_ref,
                  pool_ext, conv_ext, h_carry, *, start_pos, bblk, tt, d):
    j = pl.program_id(1)

    @pl.when(j == 0)
    def _():
        pool_ext[:, 0:POOL_PAD, :] = hp_ref[...]
        conv_ext[:, 0:CONV_PAD, :] = hc_ref[...]
        h_carry[...] = hl_ref[...]

    rows = bblk * tt
    x = x_ref[...].reshape(rows, d)
    xnb = _rms_norm(x, nmix_ref[...]).astype(BF16)

    u_pool = _dot(xnb, win_ref[:, 0:d])
    pos1 = lax.broadcasted_iota(jnp.int32, (tt, 1), 0) + (start_pos + 1) + j * tt
    gd = d // len(POOL_WINDOWS)
    y_pool_rows = []
    for b in range(bblk):
        pool_ext[b, POOL_PAD:POOL_PAD + tt, :] = u_pool[b * tt:(b + 1) * tt, :]
        parts = []
        for g, w in enumerate(POOL_WINDOWS):
            sl = slice(g * gd, (g + 1) * gd)
            s = pool_ext[b, :, sl]
            width = 1
            while width < w:
                s = s + pltpu.roll(s, width, axis=0)
                width *= 2
            cur = s[POOL_PAD:POOL_PAD + tt, :]
            inv_cnt = 1.0 / jnp.minimum(pos1, w).astype(F32)
            pooled = cur * inv_cnt - u_pool[b * tt:(b + 1) * tt, sl]
            mixed = _dot(pooled.astype(BF16), poolw_ref[g])
            parts.append(mixed * pscale_ref[:, sl])
        y_pool_rows.append(jnp.concatenate(parts, axis=1))
        npool_ref[b] = pool_ext[b, POOL_PAD + tt - POOL_HIST:POOL_PAD + tt, :]
        pool_ext[b, 0:POOL_PAD, :] = pool_ext[b, tt:tt + POOL_PAD, :]
    y_pool = jnp.concatenate(y_pool_rows, axis=0) if bblk > 1 else y_pool_rows[0]
    br_pool = _dot(y_pool.astype(BF16), wbrp_ref[...])
    g_pool = jax.nn.sigmoid(_dot(xnb, win_ref[:, 3 * d:4 * d]) + bgate_ref[:, 0:d])
    acc = g_pool * br_pool

    u_x = _dot(xnb, win_ref[:, d:2 * d])
    xc_rows = []
    for b in range(bblk):
        conv_ext[b, CONV_PAD:CONV_PAD + tt, :] = u_x[b * tt:(b + 1) * tt, :]
        ce = conv_ext[b]
        y = convb_ref[...] + ce * convw_ref[CONV_WIDTH - 1:CONV_WIDTH, :]
        for k in range(1, CONV_WIDTH):
            y = y + pltpu.roll(ce, k, axis=0) * convw_ref[CONV_WIDTH - 1 - k:CONV_WIDTH - k, :]
        xc_rows.append(y[CONV_PAD:CONV_PAD + tt, :])
        nconv_ref[b] = conv_ext[b, CONV_PAD + tt - CONV_HIST:CONV_PAD + tt, :]
        conv_ext[b, 0:CONV_PAD, :] = conv_ext[b, tt:tt + CONV_PAD, :]
    xc = jnp.concatenate(xc_rows, axis=0) if bblk > 1 else xc_rows[0]

    n_chunks = d // MXU_WIDTH
    pre_a, pre_i = [], []
    for c in range(n_chunks):
        ai = _dot(xc[:, c * MXU_WIDTH:(c + 1) * MXU_WIDTH].astype(BF16), wai_ref[c])
        pre_a.append(ai[:, 0:MXU_WIDTH])
        pre_i.append(ai[:, MXU_WIDTH:2 * MXU_WIDTH])
    r_gate = jax.nn.sigmoid(jnp.concatenate(pre_a, axis=1) + ba_ref[...])
    i_gate = jax.nn.sigmoid(jnp.concatenate(pre_i, axis=1) + bi_ref[...])
    log_a = (-LRU_C) * r_gate * _softplus(-lam_ref[...])
    a = jnp.exp(log_a)
    mult = jnp.sqrt(-jnp.tanh(log_a) * (a * a + 1.0))
    bb = mult * (i_gate * xc)
    h_rows = []
    for b in range(bblk):
        h_b, h_last = _lru_scan(a[b * tt:(b + 1) * tt, :], bb[b * tt:(b + 1) * tt, :], h_carry[b])
        h_rows.append(h_b)
        h_carry[b] = h_last
        nlru_ref[b] = h_last
    h = jnp.concatenate(h_rows, axis=0) if bblk > 1 else h_rows[0]

    u_gate = _dot(xnb, win_ref[:, 2 * d:3 * d])
    y_lru = h * _gelu_tanh(u_gate)
    br_lru = _dot(y_lru.astype(BF16), wbrl_ref[...])
    g_lru = jax.nn.sigmoid(_dot(xnb, win_ref[:, 4 * d:5 * d]) + bgate_ref[:, d:2 * d])
    acc = acc + g_lru * br_lru

    out = _dot(acc.astype(BF16), wout_ref[...])
    x1_ref[...] = (x + out).reshape(bblk, tt, d)


def _const_spec(shape):
    nd = len(shape)
    return pl.BlockSpec(shape, lambda *_: (0,) * nd, pipeline_mode=pl.Buffered(1))


def _mixer_call(x, hist_pool, hist_conv, hist_lru, weights, *, start_pos, bblk, tt, seq0=0):
    _, t_len, d = x.shape
    bsz = hist_pool.shape[0]
    assert seq0 % bblk == 0
    grid = (bsz // bblk, t_len // tt)
    kern = functools.partial(_mixer_kernel, start_pos=start_pos, bblk=bblk, tt=tt, d=d)
    seq_spec = lambda rows: pl.BlockSpec((bblk, rows, d), lambda b, j: (b, 0, 0))
    in_specs = [
        pl.BlockSpec((bblk, tt, d), lambda b, j: (b + seq0 // bblk, j, 0)),
        seq_spec(POOL_PAD), seq_spec(CONV_PAD), seq_spec(1),
    ] + [_const_spec(w.shape) for w in weights]
    out_shape = (
        jax.ShapeDtypeStruct((bsz, t_len, d), F32),
        jax.ShapeDtypeStruct((bsz, POOL_HIST, d), F32),
        jax.ShapeDtypeStruct((bsz, CONV_HIST, d), F32),
        jax.ShapeDtypeStruct((bsz, 1, d), F32),
    )
    out_specs = (
        pl.BlockSpec((bblk, tt, d), lambda b, j: (b, j, 0)),
        seq_spec(POOL_HIST), seq_spec(CONV_HIST), seq_spec(1),
    )
    return pl.pallas_call(
        kern,
        grid=grid,
        in_specs=in_specs,
        out_specs=out_specs,
        out_shape=out_shape,
        scratch_shapes=[
            pltpu.VMEM((bblk, POOL_PAD + tt, d), F32),
            pltpu.VMEM((bblk, CONV_PAD + tt, d), F32),
            pltpu.VMEM((bblk, 1, d), F32),
        ],
        compiler_params=pltpu.CompilerParams(
            dimension_semantics=("arbitrary", "arbitrary"), vmem_limit_bytes=VMEM_LIMIT_BYTES),
        name="mixer",
    )(x, hist_pool, hist_conv, hist_lru, *weights)


def _load_rows(ref, rows, chunks):
    return jnp.concatenate([ref[pl.ds(s, rows, stride=chunks), :] for s in range(chunks)], axis=1)


def _store_rows(ref, val, rows, chunks):
    for s in range(chunks):
        ref[pl.ds(s, rows, stride=chunks), :] = val[:, s * LANES:(s + 1) * LANES]


def _pack_halves(x):
    half = x.shape[1] // 2
    bits = lax.bitcast_convert_type(x, jnp.uint32)
    return (bits[:, :half] >> 16) | (bits[:, half:] & jnp.uint32(0xFFFF0000))


def _unpack_halves(p):
    lo = lax.bitcast_convert_type(p << 16, F32)
    hi = lax.bitcast_convert_type(p & jnp.uint32(0xFFFF0000), F32)
    return jnp.concatenate([lo, hi], axis=1)


class _Window(NamedTuple):
    first: int
    count: int


def _window_specs(windows, tb, d):
    specs, start = [], 0
    for win in windows:
        specs.append(pl.BlockSpec(
            (tb, d), lambda i, win=win, start=start: (win.first + jnp.clip(i - start, 0, win.count - 1), 0)))
        start += win.count
    return specs


def _window_load(i, windows, refs):
    val, start = refs[-1][...], sum(w.count for w in windows[:-1])
    for win, ref in zip(reversed(windows[:-1]), reversed(refs[:-1])):
        val = jnp.where(i < start, ref[...], val)
        start -= win.count
    return val


def _window_store(i, windows, refs, val):
    start = 0
    for win, ref in zip(windows, refs):
        @pl.when((i >= start) & (i < start + win.count))
        def _(ref=ref):
            ref[...] = val
        start += win.count


def _router_kernel(*refs, tb, d, windows):
    x1_refs = refs[:len(windows)]
    (nffn_ref, wrt_ref, br_ref, tri_ref, xn_ref, topi_ref, topw_ref, rank_ref, cnt_ref, carry) = refs[len(windows):]
    i = pl.program_id(0)

    @pl.when(i == 0)
    def _():
        carry[...] = jnp.zeros_like(carry)

    xnb = _rms_norm(_window_load(i, windows, x1_refs), nffn_ref[...]).astype(BF16)
    xn_ref_chunks = d // (2 * LANES)
    _store_rows(xn_ref, _pack_halves(xnb.astype(F32)), tb, xn_ref_chunks)
    logits = lax.dot_general(wrt_ref[...], xnb, (((1,), (1,)), ((), ())),
                             preferred_element_type=F32) + br_ref[...]
    n_exp = logits.shape[0]
    iota_e = lax.broadcasted_iota(jnp.int32, (n_exp, tb), 0)
    work = logits
    vals, idxs, hots = [], [], []
    for _ in range(TOP_K):
        m = jnp.max(work, axis=0, keepdims=True)
        idx = jnp.min(jnp.where(work == m, iota_e, n_exp), axis=0, keepdims=True)
        hot = iota_e == idx
        vals.append(m)
        idxs.append(idx)
        hots.append(hot)
        work = jnp.where(hot, -jnp.inf, work)
    exps = [jnp.exp(v - vals[0]) for v in vals]
    denom = exps[0] + exps[1] + exps[2] + exps[3]
    topi_ref[...] = jnp.concatenate(idxs, axis=0)
    topw_ref[...] = jnp.concatenate([e / denom for e in exps], axis=0)

    sel = sum(jnp.where(h, 1.0, 0.0) for h in hots)
    before = _dot(sel.astype(BF16), tri_ref[...]) + carry[:, 0:1]
    ranks = [jnp.sum(jnp.where(h, before, 0.0), axis=0, keepdims=True) for h in hots]
    rank_ref[...] = jnp.concatenate(ranks, axis=0).astype(jnp.int32)
    carry[...] = carry[...] + jnp.sum(sel, axis=1, keepdims=True)
    cnt_ref[...] = carry[...]


def _router_call(x1_parts, windows, norm_ffn, w_router_t, b_router, *, tb):
    d = x1_parts[0].shape[1]
    n_tiles = sum(w.count for w in windows)
    n = n_tiles * tb
    chunks = d // (2 * LANES)
    n_exp = w_router_t.shape[0]
    tri = jnp.triu(jnp.ones((tb, tb), BF16), k=1)
    tok = lambda rows, dt: (pl.BlockSpec((rows, tb), lambda i: (0, i)), jax.ShapeDtypeStruct((rows, n), dt))
    (topi_spec, topi_shape), (topw_spec, topw_shape), (rank_spec, rank_shape) = (
        tok(TOP_K, jnp.int32), tok(TOP_K, F32), tok(TOP_K, jnp.int32))
    return pl.pallas_call(
        functools.partial(_router_kernel, tb=tb, d=d, windows=tuple(windows)),
        grid=(n_tiles,),
        in_specs=[
            *_window_specs(windows, tb, d),
            _const_spec((1, d)), _const_spec((n_exp, d)), _const_spec((n_exp, 1)), _const_spec((tb, tb)),
        ],
        out_specs=(
            pl.BlockSpec((tb * chunks, LANES), lambda i: (i, 0)),
            topi_spec, topw_spec, rank_spec,
            pl.BlockSpec((n_exp, LANES), lambda i: (0, 0)),
        ),
        out_shape=(
            jax.ShapeDtypeStruct((n * chunks, LANES), jnp.uint32),
            topi_shape, topw_shape, rank_shape,
            jax.ShapeDtypeStruct((n_exp, LANES), F32),
        ),
        scratch_shapes=[pltpu.VMEM((n_exp, LANES), F32)],
        compiler_params=pltpu.CompilerParams(
            dimension_semantics=("arbitrary",), vmem_limit_bytes=VMEM_LIMIT_BYTES),
        name="router",
    )(*x1_parts, norm_ffn.reshape(1, d), w_router_t, b_router.reshape(n_exp, 1), tri)


def _expert_kernel(te_ref, nu_ref, xs_ref, wgu_ref, bgu_ref, wdn_ref, bdn_ref, ys_ref, wgu_bf, wdn_bf,
                   *, tm, d, dff):
    i = pl.program_id(0)
    prev = te_ref[jnp.maximum(i - 1, 0)]

    @pl.when((i == 0) | (te_ref[i] != prev))
    def _():
        wgu_bf[...] = wgu_ref[0].astype(BF16)
        wdn_bf[...] = wdn_ref[0].astype(BF16)

    @pl.when(i < nu_ref[0])
    def _():
        x = _unpack_halves(_load_rows(xs_ref, tm, d // (2 * LANES))).astype(BF16)
        gu = _dot(x, wgu_bf[...]) + bgu_ref[0]
        gate = jnp.minimum(gu[:, :dff], SWIGLU_LIMIT)
        up = jnp.clip(gu[:, dff:], -SWIGLU_LIMIT, SWIGLU_LIMIT)
        hid = (up + 1.0) * gate * jax.nn.sigmoid(SWIGLU_ALPHA * gate)
        y = _dot(hid.astype(BF16), wdn_bf[...]) + bdn_ref[0]
        _store_rows(ys_ref, y, tm, d // LANES)


def _expert_call(tile_expert, n_used, xs, w_gu, b_gu, w_dn, b_dn, *, tm):
    n_exp, d, dff2 = w_gu.shape
    dff = dff2 // 2
    chunks = d // LANES
    xchunks = d // (2 * LANES)
    p_rows = xs.shape[0] // xchunks
    row_blk = lambda i, te, nu: (jnp.minimum(i, nu[0] - 1), 0)
    exp_blk = lambda i, te, nu: (te[i], 0, 0)
    return pl.pallas_call(
        functools.partial(_expert_kernel, tm=tm, d=d, dff=dff),
        grid_spec=pltpu.PrefetchScalarGridSpec(
            num_scalar_prefetch=2,
            grid=(p_rows // tm,),
            in_specs=[
                pl.BlockSpec((tm * xchunks, LANES), row_blk),
                pl.BlockSpec((1, d, dff2), exp_blk),
                pl.BlockSpec((1, 1, dff2), exp_blk),
                pl.BlockSpec((1, dff, d), exp_blk),
                pl.BlockSpec((1, 1, d), exp_blk),
            ],
            out_specs=pl.BlockSpec((tm * chunks, LANES), row_blk),
            scratch_shapes=[pltpu.VMEM((d, dff2), BF16), pltpu.VMEM((dff, d), BF16)],
        ),
        out_shape=jax.ShapeDtypeStruct((p_rows * chunks, LANES), F32),
        compiler_params=pltpu.CompilerParams(
            dimension_semantics=("arbitrary",), vmem_limit_bytes=VMEM_LIMIT_BYTES),
        name="experts",
    )(tile_expert, n_used, xs, w_gu, b_gu.reshape(n_exp, 1, dff2), w_dn, b_dn.reshape(n_exp, 1, d))


def _final_kernel(*refs, tb, d, in_windows, out_windows, n_alias):
    n_in, n_out = len(in_windows), len(out_windows)
    x1_refs = refs[:n_in]
    yk_ref, w_ref, nfin_ref = refs[n_in:n_in + 3]
    out_refs = refs[n_in + 3 + n_alias:n_in + 3 + n_alias + n_out]
    i = pl.program_id(0)
    w = w_ref[...]
    moe = _load_rows(yk_ref.at[0], tb, d // LANES) * w[:, 0:1]
    for k in range(1, TOP_K):
        moe = moe + _load_rows(yk_ref.at[k], tb, d // LANES) * w[:, k:k + 1]
    out = _rms_norm(_window_load(i, in_windows, x1_refs) + moe, nfin_ref[...])
    _window_store(i, out_windows, out_refs, out)


def _final_call(x1_parts, in_windows, yk, topw_t, norm_final, out_shapes, out_windows, out_init, *, tb):
    d = x1_parts[0].shape[1]
    n_tiles = sum(w.count for w in in_windows)
    chunks = d // LANES
    alias_in = [a for a in out_init if a is not None]
    n_fixed = len(x1_parts) + 3
    aliases, j = {}, 0
    for k, a in enumerate(out_init):
        if a is not None:
            aliases[n_fixed + j] = k
            j += 1
    return pl.pallas_call(
        functools.partial(_final_kernel, tb=tb, d=d, in_windows=tuple(in_windows),
                          out_windows=tuple(out_windows), n_alias=len(alias_in)),
        grid=(n_tiles,),
        in_specs=[
            *_window_specs(in_windows, tb, d),
            pl.BlockSpec((TOP_K, tb * chunks, LANES), lambda i: (0, i, 0)),
            pl.BlockSpec((tb, TOP_K), lambda i: (i, 0)),
            _const_spec((1, d)),
            *[pl.BlockSpec(memory_space=pl.ANY) for _ in alias_in],
        ],
        out_specs=_window_specs(out_windows, tb, d),
        out_shape=[jax.ShapeDtypeStruct(s, F32) for s in out_shapes],
        input_output_aliases=aliases,
        compiler_params=pltpu.CompilerParams(
            dimension_semantics=("arbitrary",), vmem_limit_bytes=VMEM_LIMIT_BYTES),
        name="final",
    )(*x1_parts, yk, topw_t, norm_final.reshape(1, d), *alias_in)


def _sc_step_rows(n):
    per_worker = n // SC_WORKERS
    assert per_worker * SC_WORKERS == n
    w = max(c for c in range(SUBLANES, SC_MAX_STEP_ROWS + 1, SUBLANES) if per_worker % c == 0)
    return w, per_worker // w


def _sc_mesh():
    return plsc.VectorSubcoreMesh(core_axis_name="c", subcore_axis_name="s",
                                  num_cores=SC_CORES, num_subcores=SC_SUBCORES)


def _dispatch_rows(x_tiles, pos_steps, p_rows):
    n, chunks, _ = x_tiles.shape
    w, steps = _sc_step_rows(n)

    @functools.partial(
        pl.kernel, mesh=_sc_mesh(),
        out_type=jax.ShapeDtypeStruct((p_rows, chunks, LANES), x_tiles.dtype),
        scratch_types=[pltpu.VMEM((w, chunks, LANES), x_tiles.dtype), pltpu.VMEM((TOP_K, w), jnp.int32)],
        name="dispatch",
    )
    def run(x_hbm, pos_hbm, out_hbm, buf, idx):
        wid = lax.axis_index("s") * SC_CORES + lax.axis_index("c")

        @pl.loop(0, steps)
        def _(i):
            step = wid * steps + i
            pltpu.sync_copy(x_hbm.at[pl.ds(pl.multiple_of(step * w, SUBLANES), w)], buf)
            pltpu.sync_copy(pos_hbm.at[step], idx)
            for k in range(TOP_K):
                pltpu.sync_copy(buf, out_hbm.at[idx.at[k]])

    return run(x_tiles, pos_steps)


def _combine_rows(y_tiles, pos_steps, n):
    _, chunks, _ = y_tiles.shape
    w, steps = _sc_step_rows(n)

    @functools.partial(
        pl.kernel, mesh=_sc_mesh(),
        out_type=jax.ShapeDtypeStruct((TOP_K, n, chunks, LANES), F32),
        scratch_types=[pltpu.VMEM((w, chunks, LANES), F32), pltpu.VMEM((TOP_K, w), jnp.int32)],
        name="combine",
    )
    def run(y_hbm, pos_hbm, out_hbm, buf, idx):
        wid = lax.axis_index("s") * SC_CORES + lax.axis_index("c")

        @pl.loop(0, steps)
        def _(i):
            step = wid * steps + i
            pltpu.sync_copy(pos_hbm.at[step], idx)
            for k in range(TOP_K):
                pltpu.sync_copy(y_hbm.at[idx.at[k]], buf)
                pltpu.sync_copy(buf, out_hbm.at[k, pl.ds(pl.multiple_of(step * w, SUBLANES), w)])

    return run(y_tiles, pos_steps)


def _block_diag(w, per_block):
    heads, n, _ = w.shape
    w4 = w.reshape(heads // per_block, per_block, n, n)
    eye = jnp.eye(per_block, dtype=w.dtype)
    bd = jnp.einsum('chij,hg->chigj', w4, eye)
    return bd.reshape(heads // per_block, per_block * n, per_block * n)


def _mixer_weights(norm_mix, w_in, b_gate, pool_w, pool_scale, conv_w, conv_b, lru_wa, lru_ba, lru_wi,
                   lru_bi, lru_lambda, w_br_pool, w_br_lru, w_out):
    row = lambda v: v.reshape(1, -1)
    head_dim = lru_wa.shape[-1]
    per_block = MXU_WIDTH // head_dim
    w_ai = jnp.concatenate([_block_diag(lru_wa, per_block), _block_diag(lru_wi, per_block)], axis=-1)
    return (row(norm_mix), w_in.astype(BF16), row(b_gate), pool_w.astype(BF16), row(pool_scale),
            conv_w, row(conv_b), w_ai.astype(BF16), row(lru_ba), row(lru_bi), row(lru_lambda),
            w_br_pool.astype(BF16), w_br_lru.astype(BF16), w_out.astype(BF16))


def _run_mixer(x, state_pool, state_conv, state_lru, weights, *, start_pos, bblk, tt, seq0=0):
    hp = jnp.pad(state_pool, ((0, 0), (POOL_PAD - POOL_HIST, 0), (0, 0)))
    hc = jnp.pad(state_conv, ((0, 0), (CONV_PAD - CONV_HIST, 0), (0, 0)))
    hl = state_lru[:, None, :]
    x1, npool, nconv, nlru = _mixer_call(x, hp, hc, hl, weights, start_pos=start_pos, bblk=bblk, tt=tt,
                                         seq0=seq0)
    return x1, npool, nconv, nlru[:, 0, :]


def _routing_plan(topi, rank, counts, *, tm, n_tiles):
    n_exp = counts.shape[0]
    tiles_per = (counts + tm - 1) // tm
    tiles_cum = jnp.cumsum(tiles_per)
    base = (tiles_cum - tiles_per) * tm
    experts = jnp.arange(n_exp, dtype=jnp.int32)[:, None, None]
    pos = rank + jnp.sum(jnp.where(topi[None] == experts, base[:, None, None], 0), axis=0)
    n_used = tiles_cum[-1]
    tile_ids = jnp.minimum(jnp.arange(n_tiles, dtype=jnp.int32), n_used - 1)
    tile_expert = jnp.sum((tiles_cum[None, :] <= tile_ids[:, None]).astype(jnp.int32), axis=1)
    return pos, tile_expert, n_used.reshape(1).astype(jnp.int32)


def _moe_rows(x1_parts, windows, norm_ffn, w_router_t, b_router, w_gu, b_gu, w_dn, b_dn):
    d = x1_parts[0].shape[1]
    chunks = d // LANES
    n_exp = w_router_t.shape[0]
    n = sum(w.count for w in windows) * TOKEN_TILE
    xn2, topi, topw, rank, cnt = _router_call(x1_parts, windows, norm_ffn, w_router_t, b_router, tb=TOKEN_TILE)
    n_tiles = (n * TOP_K + n_exp * (EXPERT_TILE - 1)) // EXPERT_TILE
    p_rows = n_tiles * EXPERT_TILE
    pos, tile_expert, n_used = _routing_plan(
        topi, rank, cnt[:, 0].astype(jnp.int32), tm=EXPERT_TILE, n_tiles=n_tiles)
    w, steps = _sc_step_rows(n)
    pos_steps = pos.reshape(TOP_K, SC_WORKERS * steps, w).transpose(1, 0, 2)
    xchunks = d // (2 * LANES)
    xs = _dispatch_rows(xn2.reshape(n, xchunks, LANES), pos_steps, p_rows)
    ys = _expert_call(tile_expert, n_used, xs.reshape(p_rows * xchunks, LANES), w_gu, b_gu, w_dn, b_dn,
                      tm=EXPERT_TILE)
    yk = _combine_rows(ys.reshape(p_rows, chunks, LANES), pos_steps, n)
    return yk.reshape(TOP_K, n * chunks, LANES), topw


def kernel(x_prompt, x_sample, state_pool, state_conv, state_lru, norm_mix, w_in, b_gate, pool_w, pool_scale,
           conv_w, conv_b, lru_wa, lru_ba, lru_wi, lru_bi, lru_lambda, w_br_pool, w_br_lru, w_out, norm_ffn,
           w_router, b_router, w_gu, b_gu, w_dn, b_dn, norm_final):
    bp, tp, d = x_prompt.shape
    bs, ts, _ = x_sample.shape
    n_p, n_s = bp * tp, bs * ts
    mw = _mixer_weights(norm_mix[0], w_in[0], b_gate[0], pool_w[0], pool_scale[0], conv_w[0], conv_b[0],
                        lru_wa[0], lru_ba[0], lru_wi[0], lru_bi[0], lru_lambda[0], w_br_pool[0], w_br_lru[0],
                        w_out[0])
    w_router_t = w_router[0].T.astype(BF16)
    zeros = lambda *shape: jnp.zeros(shape, x_prompt.dtype)

    seqs = bp // PROMPT_GROUPS
    g_tiles = seqs * tp // TOKEN_TILE
    s_tiles = n_s // TOKEN_TILE
    assert seqs * PROMPT_GROUPS == bp and g_tiles * TOKEN_TILE == seqs * tp and s_tiles * TOKEN_TILE == n_s
    x1_s, pool_s, conv_s, lru_s = _run_mixer(
        x_sample, state_pool[0], state_conv[0], state_lru[0], mw, start_pos=PAST_LEN, bblk=bs, tt=ts)
    y_p, y_s = None, None
    pools, convs, lrus = [], [], []
    for g in range(PROMPT_GROUPS):
        x1_g, pool_g, conv_g, lru_g = _run_mixer(
            x_prompt, zeros(seqs, POOL_HIST, d), zeros(seqs, CONV_HIST, d), zeros(seqs, d), mw,
            start_pos=0, bblk=1, tt=MIXER_TILE, seq0=g * seqs)
        pools.append(pool_g)
        convs.append(conv_g)
        lrus.append(lru_g)
        last = g == PROMPT_GROUPS - 1
        parts, windows = [x1_g.reshape(seqs * tp, d)], [_Window(0, g_tiles)]
        out_shapes, out_windows, out_init = [(n_p, d)], [_Window(g * g_tiles, g_tiles)], [y_p]
        if last:
            parts.append(x1_s.reshape(n_s, d))
            windows.append(_Window(0, s_tiles))
            out_shapes.append((n_s, d))
            out_windows.append(_Window(0, s_tiles))
            out_init.append(None)
        yk, topw = _moe_rows(parts, windows, norm_ffn[0], w_router_t, b_router[0], w_gu[0], b_gu[0], w_dn[0],
                             b_dn[0])
        outs = _final_call(parts, windows, yk, topw.T, norm_final, out_shapes, out_windows, out_init,
                           tb=TOKEN_TILE)
        y_p = outs[0]
        if last:
            y_s = outs[1]

    cat = lambda xs: jnp.concatenate(xs, axis=0)[None]
    return (y_p.reshape(bp, tp, d), y_s.reshape(bs, ts, d), cat(pools), cat(convs), cat(lrus),
            pool_s[None], conv_s[None], lru_s[None])
```

```python
import functools
from typing import NamedTuple

import jax
import jax.numpy as jnp
from jax import lax
from jax.experimental import pallas as pl
from jax.experimental.pallas import tpu as pltpu
from jax.experimental.pallas import tpu_sc as plsc

BF16 = jnp.bfloat16
F32 = jnp.float32

POOL_WINDOWS = (2, 4, 8, 16)
POOL_HIST = max(POOL_WINDOWS) - 1
CONV_WIDTH = 4
CONV_HIST = CONV_WIDTH - 1
LRU_HEADS = 16
LRU_C = 8.0
N_EXPERTS = 32
TOP_K = 4
SWIGLU_LIMIT = 7.0
SWIGLU_ALPHA = 1.702
NORM_EPS = 1e-6

SUBLANES = 8
LANES = 128
MXU_WIDTH = 256
POOL_PAD = 16
CONV_PAD = 8
VMEM_LIMIT_BYTES = 56 * 1024 * 1024

PAST_LEN = 1024
MIXER_TILE = 256
TOKEN_TILE = 512
EXPERT_TILE = 256
PROMPT_GROUP_SEQS = (12, 4)

SC_CORES = 2
SC_SUBCORES = 16
SC_WORKERS = SC_CORES * SC_SUBCORES
SC_MAX_STEP_ROWS = 64


def _rms_norm(x, g):
    ms = jnp.mean(x * x, axis=-1, keepdims=True)
    return (x * lax.rsqrt(ms + NORM_EPS)) * g


def _dot(a, b):
    return jnp.dot(a, b, preferred_element_type=F32)


def _softplus(x):
    return jnp.maximum(x, 0.0) + jnp.log1p(jnp.exp(-jnp.abs(x)))


def _gelu_tanh(x):
    c = 0.7978845608028654
    return 0.5 * x * (1.0 + jnp.tanh(c * (x + 0.044715 * (x * x * x))))


def _lru_scan(a, b, h0):
    t_len, c = a.shape
    groups = t_len // SUBLANES
    a3 = a.reshape(groups, SUBLANES, c)
    b3 = b.reshape(groups, SUBLANES, c)
    sub = lax.broadcasted_iota(jnp.int32, (groups, SUBLANES, c), 1)
    for k in (1, 2, 4):
        a_prev = pltpu.roll(a3, k, axis=1)
        b_prev = pltpu.roll(b3, k, axis=1)
        valid = sub >= k
        b3 = jnp.where(valid, a3 * b_prev + b3, b3)
        a3 = jnp.where(valid, a3 * a_prev, a3)
    h = h0
    outs = []
    for g in range(groups):
        hg = a3[g] * h + b3[g]
        outs.append(hg)
        h = hg[SUBLANES - 1:SUBLANES, :]
    return jnp.concatenate(outs, axis=0), h


def _mixer_kernel(x_ref, hp_ref, hc_ref, hl_ref, nmix_ref, win_ref, bgate_ref, poolw_ref, pscale_ref,
                  convw_ref, convb_ref, wai_ref, ba_ref, bi_ref, lam_ref, wbrp_ref, wbrl_ref, wout_ref,
                  x1_ref, npool_ref, nconv_ref, nlru_ref,
                  pool_ext, conv_ext, h_carry, *, start_pos, bblk, tt, d):
    j = pl.program_id(1)

    @pl.when(j == 0)
    def _():
        pool_ext[:, 0:POOL_PAD, :] = hp_ref[...]
        conv_ext[:, 0:CONV_PAD, :] = hc_ref[...]
        h_carry[...] = hl_ref[...]

    rows = bblk * tt
    x = x_ref[...].reshape(rows, d)
    xnb = _rms_norm(x, nmix_ref[...]).astype(BF16)

    u_pool = _dot(xnb, win_ref[:, 0:d])
    pos1 = lax.broadcasted_iota(jnp.int32, (tt, 1), 0) + (start_pos + 1) + j * tt
    gd = d // len(POOL_WINDOWS)
    y_pool_rows = []
    for b in range(bblk):
        pool_ext[b, POOL_PAD:POOL_PAD + tt, :] = u_pool[b * tt:(b + 1) * tt, :]
        parts = []
        for g, w in enumerate(POOL_WINDOWS):
            sl = slice(g * gd, (g + 1) * gd)
            s = pool_ext[b, :, sl]
            width = 1
            while width < w:
                s = s + pltpu.roll(s, width, axis=0)
                width *= 2
            cur = s[POOL_PAD:POOL_PAD + tt, :]
            inv_cnt = 1.0 / jnp.minimum(pos1, w).astype(F32)
            pooled = cur * inv_cnt - u_pool[b * tt:(b + 1) * tt, sl]
            mixed = _dot(pooled.astype(BF16), poolw_ref[g])
            parts.append(mixed * pscale_ref[:, sl])
        y_pool_rows.append(jnp.concatenate(parts, axis=1))
        npool_ref[b] = pool_ext[b, POOL_PAD + tt - POOL_HIST:POOL_PAD + tt, :]
        pool_ext[b, 0:POOL_PAD, :] = pool_ext[b, tt:tt + POOL_PAD, :]
    y_pool = jnp.concatenate(y_pool_rows, axis=0) if bblk > 1 else y_pool_rows[0]
    br_pool = _dot(y_pool.astype(BF16), wbrp_ref[...])
    g_pool = jax.nn.sigmoid(_dot(xnb, win_ref[:, 3 * d:4 * d]) + bgate_ref[:, 0:d])
    acc = g_pool * br_pool

    u_x = _dot(xnb, win_ref[:, d:2 * d])
    xc_rows = []
    for b in range(bblk):
        conv_ext[b, CONV_PAD:CONV_PAD + tt, :] = u_x[b * tt:(b + 1) * tt, :]
        ce = conv_ext[b]
        y = convb_ref[...] + ce * convw_ref[CONV_WIDTH - 1:CONV_WIDTH, :]
        for k in range(1, CONV_WIDTH):
            y = y + pltpu.roll(ce, k, axis=0) * convw_ref[CONV_WIDTH - 1 - k:CONV_WIDTH - k, :]
        xc_rows.append(y[CONV_PAD:CONV_PAD + tt, :])
        nconv_ref[b] = conv_ext[b, CONV_PAD + tt - CONV_HIST:CONV_PAD + tt, :]
        conv_ext[b, 0:CONV_PAD, :] = conv_ext[b, tt:tt + CONV_PAD, :]
    xc = jnp.concatenate(xc_rows, axis=0) if bblk > 1 else xc_rows[0]

    n_chunks = d // MXU_WIDTH
    pre_a, pre_i = [], []
    for c in range(n_chunks):
        ai = _dot(xc[:, c * MXU_WIDTH:(c + 1) * MXU_WIDTH].astype(BF16), wai_ref[c])
        pre_a.append(ai[:, 0:MXU_WIDTH])
        pre_i.append(ai[:, MXU_WIDTH:2 * MXU_WIDTH])
    r_gate = jax.nn.sigmoid(jnp.concatenate(pre_a, axis=1) + ba_ref[...])
    i_gate = jax.nn.sigmoid(jnp.concatenate(pre_i, axis=1) + bi_ref[...])
    log_a = (-LRU_C) * r_gate * _softplus(-lam_ref[...])
    a = jnp.exp(log_a)
    mult = jnp.sqrt(-jnp.tanh(log_a) * (a * a + 1.0))
    bb = mult * (i_gate * xc)
    h_rows = []
    for b in range(bblk):
        h_b, h_last = _lru_scan(a[b * tt:(b + 1) * tt, :], bb[b * tt:(b + 1) * tt, :], h_carry[b])
        h_rows.append(h_b)
        h_carry[b] = h_last
        nlru_ref[b] = h_last
    h = jnp.concatenate(h_rows, axis=0) if bblk > 1 else h_rows[0]

    u_gate = _dot(xnb, win_ref[:, 2 * d:3 * d])
    y_lru = h * _gelu_tanh(u_gate)
    br_lru = _dot(y_lru.astype(BF16), wbrl_ref[...])
    g_lru = jax.nn.sigmoid(_dot(xnb, win_ref[:, 4 * d:5 * d]) + bgate_ref[:, d:2 * d])
    acc = acc + g_lru * br_lru

    out = _dot(acc.astype(BF16), wout_ref[...])
    x1_ref[...] = (x + out).reshape(bblk, tt, d)


def _const_spec(shape):
    nd = len(shape)
    return pl.BlockSpec(shape, lambda *_: (0,) * nd, pipeline_mode=pl.Buffered(1))


def _mixer_call(x, hist_pool, hist_conv, hist_lru, weights, *, start_pos, bblk, tt, seq0=0):
    _, t_len, d = x.shape
    bsz = hist_pool.shape[0]
    assert seq0 % bblk == 0
    grid = (bsz // bblk, t_len // tt)
    kern = functools.partial(_mixer_kernel, start_pos=start_pos, bblk=bblk, tt=tt, d=d)
    seq_spec = lambda rows: pl.BlockSpec((bblk, rows, d), lambda b, j: (b, 0, 0))
    in_specs = [
        pl.BlockSpec((bblk, tt, d), lambda b, j: (b + seq0 // bblk, j, 0)),
        seq_spec(POOL_PAD), seq_spec(CONV_PAD), seq_spec(1),
    ] + [_const_spec(w.shape) for w in weights]
    out_shape = (
        jax.ShapeDtypeStruct((bsz, t_len, d), F32),
        jax.ShapeDtypeStruct((bsz, POOL_HIST, d), F32),
        jax.ShapeDtypeStruct((bsz, CONV_HIST, d), F32),
        jax.ShapeDtypeStruct((bsz, 1, d), F32),
    )
    out_specs = (
        pl.BlockSpec((bblk, tt, d), lambda b, j: (b, j, 0)),
        seq_spec(POOL_HIST), seq_spec(CONV_HIST), seq_spec(1),
    )
    return pl.pallas_call(
        kern,
        grid=grid,
        in_specs=in_specs,
        out_specs=out_specs,
        out_shape=out_shape,
        scratch_shapes=[
            pltpu.VMEM((bblk, POOL_PAD + tt, d), F32),
            pltpu.VMEM((bblk, CONV_PAD + tt, d), F32),
            pltpu.VMEM((bblk, 1, d), F32),
        ],
        compiler_params=pltpu.CompilerParams(
            dimension_semantics=("arbitrary", "arbitrary"), vmem_limit_bytes=VMEM_LIMIT_BYTES),
        name="mixer",
    )(x, hist_pool, hist_conv, hist_lru, *weights)


def _load_rows(ref, rows, chunks):
    return jnp.concatenate([ref[pl.ds(s, rows, stride=chunks), :] for s in range(chunks)], axis=1)


def _store_rows(ref, val, rows, chunks):
    for s in range(chunks):
        ref[pl.ds(s, rows, stride=chunks), :] = val[:, s * LANES:(s + 1) * LANES]


def _pack_halves(x):
    half = x.shape[1] // 2
    bits = lax.bitcast_convert_type(x, jnp.uint32)
    return (bits[:, :half] >> 16) | (bits[:, half:] & jnp.uint32(0xFFFF0000))


def _unpack_halves(p):
    lo = lax.bitcast_convert_type(p << 16, F32)
    hi = lax.bitcast_convert_type(p & jnp.uint32(0xFFFF0000), F32)
    return jnp.concatenate([lo, hi], axis=1)


class _Window(NamedTuple):
    first: int
    count: int


def _window_specs(windows, tb, d):
    specs, start = [], 0
    for win in windows:
        specs.append(pl.BlockSpec(
            (tb, d), lambda i, win=win, start=start: (win.first + jnp.clip(i - start, 0, win.count - 1), 0)))
        start += win.count
    return specs


def _window_load(i, windows, refs):
    val, start = refs[-1][...], sum(w.count for w in windows[:-1])
    for win, ref in zip(reversed(windows[:-1]), reversed(refs[:-1])):
        val = jnp.where(i < start, ref[...], val)
        start -= win.count
    return val


def _window_store(i, windows, refs, val):
    start = 0
    for win, ref in zip(windows, refs):
        @pl.when((i >= start) & (i < start + win.count))
        def _(ref=ref):
            ref[...] = val
        start += win.count


def _router_kernel(*refs, tb, d, windows):
    x1_refs = refs[:len(windows)]
    (nffn_ref, wrt_ref, br_ref, tri_ref, xn_ref, topi_ref, topw_ref, rank_ref, cnt_ref, carry) = refs[len(windows):]
    i = pl.program_id(0)

    @pl.when(i == 0)
    def _():
        carry[...] = jnp.zeros_like(carry)

    xnb = _rms_norm(_window_load(i, windows, x1_refs), nffn_ref[...]).astype(BF16)
    xn_ref_chunks = d // (2 * LANES)
    _store_rows(xn_ref, _pack_halves(xnb.astype(F32)), tb, xn_ref_chunks)
    logits = lax.dot_general(wrt_ref[...], xnb, (((1,), (1,)), ((), ())),
                             preferred_element_type=F32) + br_ref[...]
    n_exp = logits.shape[0]
    iota_e = lax.broadcasted_iota(jnp.int32, (n_exp, tb), 0)
    work = logits
    vals, idxs, hots = [], [], []
    for _ in range(TOP_K):
        m = jnp.max(work, axis=0, keepdims=True)
        idx = jnp.min(jnp.where(work == m, iota_e, n_exp), axis=0, keepdims=True)
        hot = iota_e == idx
        vals.append(m)
        idxs.append(idx)
        hots.append(hot)
        work = jnp.where(hot, -jnp.inf, work)
    exps = [jnp.exp(v - vals[0]) for v in vals]
    denom = exps[0] + exps[1] + exps[2] + exps[3]
    topi_ref[...] = jnp.concatenate(idxs, axis=0)
    topw_ref[...] = jnp.concatenate([e / denom for e in exps], axis=0)

    sel = sum(jnp.where(h, 1.0, 0.0) for h in hots)
    before = _dot(sel.astype(BF16), tri_ref[...]) + carry[:, 0:1]
    ranks = [jnp.sum(jnp.where(h, before, 0.0), axis=0, keepdims=True) for h in hots]
    rank_ref[...] = jnp.concatenate(ranks, axis=0).astype(jnp.int32)
    carry[...] = carry[...] + jnp.sum(sel, axis=1, keepdims=True)
    cnt_ref[...] = carry[...]


def _router_call(x1_parts, windows, norm_ffn, w_router_t, b_router, *, tb):
    d = x1_parts[0].shape[1]
    n_tiles = sum(w.count for w in windows)
    n = n_tiles * tb
    chunks = d // (2 * LANES)
    n_exp = w_router_t.shape[0]
    tri = jnp.triu(jnp.ones((tb, tb), BF16), k=1)
    tok = lambda rows, dt: (pl.BlockSpec((rows, tb), lambda i: (0, i)), jax.ShapeDtypeStruct((rows, n), dt))
    (topi_spec, topi_shape), (topw_spec, topw_shape), (rank_spec, rank_shape) = (
        tok(TOP_K, jnp.int32), tok(TOP_K, F32), tok(TOP_K, jnp.int32))
    return pl.pallas_call(
        functools.partial(_router_kernel, tb=tb, d=d, windows=tuple(windows)),
        grid=(n_tiles,),
        in_specs=[
            *_window_specs(windows, tb, d),
            _const_spec((1, d)), _const_spec((n_exp, d)), _const_spec((n_exp, 1)), _const_spec((tb, tb)),
        ],
        out_specs=(
            pl.BlockSpec((tb * chunks, LANES), lambda i: (i, 0)),
            topi_spec, topw_spec, rank_spec,
            pl.BlockSpec((n_exp, LANES), lambda i: (0, 0)),
        ),
        out_shape=(
            jax.ShapeDtypeStruct((n * chunks, LANES), jnp.uint32),
            topi_shape, topw_shape, rank_shape,
            jax.ShapeDtypeStruct((n_exp, LANES), F32),
        ),
        scratch_shapes=[pltpu.VMEM((n_exp, LANES), F32)],
        compiler_params=pltpu.CompilerParams(
            dimension_semantics=("arbitrary",), vmem_limit_bytes=VMEM_LIMIT_BYTES),
        name="router",
    )(*x1_parts, norm_ffn.reshape(1, d), w_router_t, b_router.reshape(n_exp, 1), tri)


def _expert_kernel(te_ref, nu_ref, xs_ref, wgu_ref, bgu_ref, wdn_ref, bdn_ref, ys_ref, wgu_bf, wdn_bf,
                   *, tm, d, dff):
    i = pl.program_id(0)
    prev = te_ref[jnp.maximum(i - 1, 0)]

    @pl.when((i == 0) | (te_ref[i] != prev))
    def _():
        wgu_bf[...] = wgu_ref[0].astype(BF16)
        wdn_bf[...] = wdn_ref[0].astype(BF16)

    @pl.when(i < nu_ref[0])
    def _():
        x = _unpack_halves(_load_rows(xs_ref, tm, d // (2 * LANES))).astype(BF16)
        gu = _dot(x, wgu_bf[...]) + bgu_ref[0]
        gate = jnp.minimum(gu[:, :dff], SWIGLU_LIMIT)
        up = jnp.clip(gu[:, dff:], -SWIGLU_LIMIT, SWIGLU_LIMIT)
        hid = (up + 1.0) * gate * jax.nn.sigmoid(SWIGLU_ALPHA * gate)
        y = _dot(hid.astype(BF16), wdn_bf[...]) + bdn_ref[0]
        _store_rows(ys_ref, y, tm, d // LANES)


def _expert_call(tile_expert, n_used, xs, w_gu, b_gu, w_dn, b_dn, *, tm):
    n_exp, d, dff2 = w_gu.shape
    dff = dff2 // 2
    chunks = d // LANES
    xchunks = d // (2 * LANES)
    p_rows = xs.shape[0] // xchunks
    row_blk = lambda i, te, nu: (jnp.minimum(i, nu[0] - 1), 0)
    exp_blk = lambda i, te, nu: (te[i], 0, 0)
    return pl.pallas_call(
        functools.partial(_expert_kernel, tm=tm, d=d, dff=dff),
        grid_spec=pltpu.PrefetchScalarGridSpec(
            num_scalar_prefetch=2,
            grid=(p_rows // tm,),
            in_specs=[
                pl.BlockSpec((tm * xchunks, LANES), row_blk),
                pl.BlockSpec((1, d, dff2), exp_blk),
                pl.BlockSpec((1, 1, dff2), exp_blk),
                pl.BlockSpec((1, dff, d), exp_blk),
                pl.BlockSpec((1, 1, d), exp_blk),
            ],
            out_specs=pl.BlockSpec((tm * chunks, LANES), row_blk),
            scratch_shapes=[pltpu.VMEM((d, dff2), BF16), pltpu.VMEM((dff, d), BF16)],
        ),
        out_shape=jax.ShapeDtypeStruct((p_rows * chunks, LANES), F32),
        compiler_params=pltpu.CompilerParams(
            dimension_semantics=("arbitrary",), vmem_limit_bytes=VMEM_LIMIT_BYTES),
        name="experts",
    )(tile_expert, n_used, xs, w_gu, b_gu.reshape(n_exp, 1, dff2), w_dn, b_dn.reshape(n_exp, 1, d))


def _final_kernel(*refs, tb, d, in_windows, out_windows, n_alias):
    n_in, n_out = len(in_windows), len(out_windows)
    x1_refs = refs[:n_in]
    yk_ref, w_ref, nfin_ref = refs[n_in:n_in + 3]
    out_refs = refs[n_in + 3 + n_alias:n_in + 3 + n_alias + n_out]
    i = pl.program_id(0)
    w = w_ref[...]
    moe = _load_rows(yk_ref.at[0], tb, d // LANES) * w[:, 0:1]
    for k in range(1, TOP_K):
        moe = moe + _load_rows(yk_ref.at[k], tb, d // LANES) * w[:, k:k + 1]
    out = _rms_norm(_window_load(i, in_windows, x1_refs) + moe, nfin_ref[...])
    _window_store(i, out_windows, out_refs, out)


def _final_call(x1_parts, in_windows, yk, topw_t, norm_final, out_shapes, out_windows, out_init, *, tb):
    d = x1_parts[0].shape[1]
    n_tiles = sum(w.count for w in in_windows)
    chunks = d // LANES
    alias_in = [a for a in out_init if a is not None]
    n_fixed = len(x1_parts) + 3
    aliases, j = {}, 0
    for k, a in enumerate(out_init):
        if a is not None:
            aliases[n_fixed + j] = k
            j += 1
    return pl.pallas_call(
        functools.partial(_final_kernel, tb=tb, d=d, in_windows=tuple(in_windows),
                          out_windows=tuple(out_windows), n_alias=len(alias_in)),
        grid=(n_tiles,),
        in_specs=[
            *_window_specs(in_windows, tb, d),
            pl.BlockSpec((TOP_K, tb * chunks, LANES), lambda i: (0, i, 0)),
            pl.BlockSpec((tb, TOP_K), lambda i: (i, 0)),
            _const_spec((1, d)),
            *[pl.BlockSpec(memory_space=pl.ANY) for _ in alias_in],
        ],
        out_specs=_window_specs(out_windows, tb, d),
        out_shape=[jax.ShapeDtypeStruct(s, F32) for s in out_shapes],
        input_output_aliases=aliases,
        compiler_params=pltpu.CompilerParams(
            dimension_semantics=("arbitrary",), vmem_limit_bytes=VMEM_LIMIT_BYTES),
        name="final",
    )(*x1_parts, yk, topw_t, norm_final.reshape(1, d), *alias_in)


def _sc_step_rows(n):
    per_worker = n // SC_WORKERS
    assert per_worker * SC_WORKERS == n
    w = max(c for c in range(SUBLANES, SC_MAX_STEP_ROWS + 1, SUBLANES) if per_worker % c == 0)
    return w, per_worker // w


def _sc_mesh():
    return plsc.VectorSubcoreMesh(core_axis_name="c", subcore_axis_name="s",
                                  num_cores=SC_CORES, num_subcores=SC_SUBCORES)


def _dispatch_rows(x_tiles, pos_steps, p_rows):
    n, chunks, _ = x_tiles.shape
    w, steps = _sc_step_rows(n)

    @functools.partial(
        pl.kernel, mesh=_sc_mesh(),
        out_type=jax.ShapeDtypeStruct((p_rows, chunks, LANES), x_tiles.dtype),
        scratch_types=[pltpu.VMEM((w, chunks, LANES), x_tiles.dtype), pltpu.VMEM((TOP_K, w), jnp.int32)],
        name="dispatch",
    )
    def run(x_hbm, pos_hbm, out_hbm, buf, idx):
        wid = lax.axis_index("s") * SC_CORES + lax.axis_index("c")

        @pl.loop(0, steps)
        def _(i):
            step = wid * steps + i
            pltpu.sync_copy(x_hbm.at[pl.ds(pl.multiple_of(step * w, SUBLANES), w)], buf)
            pltpu.sync_copy(pos_hbm.at[step], idx)
            for k in range(TOP_K):
                pltpu.sync_copy(buf, out_hbm.at[idx.at[k]])

    return run(x_tiles, pos_steps)


def _combine_rows(y_tiles, pos_steps, n):
    _, chunks, _ = y_tiles.shape
    w, steps = _sc_step_rows(n)

    @functools.partial(
        pl.kernel, mesh=_sc_mesh(),
        out_type=jax.ShapeDtypeStruct((TOP_K, n, chunks, LANES), F32),
        scratch_types=[pltpu.VMEM((w, chunks, LANES), F32), pltpu.VMEM((TOP_K, w), jnp.int32)],
        name="combine",
    )
    def run(y_hbm, pos_hbm, out_hbm, buf, idx):
        wid = lax.axis_index("s") * SC_CORES + lax.axis_index("c")

        @pl.loop(0, steps)
        def _(i):
            step = wid * steps + i
            pltpu.sync_copy(pos_hbm.at[step], idx)
            for k in range(TOP_K):
                pltpu.sync_copy(y_hbm.at[idx.at[k]], buf)
                pltpu.sync_copy(buf, out_hbm.at[k, pl.ds(pl.multiple_of(step * w, SUBLANES), w)])

    return run(y_tiles, pos_steps)


def _block_diag(w, per_block):
    heads, n, _ = w.shape
    w4 = w.reshape(heads // per_block, per_block, n, n)
    eye = jnp.eye(per_block, dtype=w.dtype)
    bd = jnp.einsum('chij,hg->chigj', w4, eye)
    return bd.reshape(heads // per_block, per_block * n, per_block * n)


def _mixer_weights(norm_mix, w_in, b_gate, pool_w, pool_scale, conv_w, conv_b, lru_wa, lru_ba, lru_wi,
                   lru_bi, lru_lambda, w_br_pool, w_br_lru, w_out):
    row = lambda v: v.reshape(1, -1)
    head_dim = lru_wa.shape[-1]
    per_block = MXU_WIDTH // head_dim
    w_ai = jnp.concatenate([_block_diag(lru_wa, per_block), _block_diag(lru_wi, per_block)], axis=-1)
    return (row(norm_mix), w_in.astype(BF16), row(b_gate), pool_w.astype(BF16), row(pool_scale),
            conv_w, row(conv_b), w_ai.astype(BF16), row(lru_ba), row(lru_bi), row(lru_lambda),
            w_br_pool.astype(BF16), w_br_lru.astype(BF16), w_out.astype(BF16))


def _run_mixer(x, state_pool, state_conv, state_lru, weights, *, start_pos, bblk, tt, seq0=0):
    hp = jnp.pad(state_pool, ((0, 0), (POOL_PAD - POOL_HIST, 0), (0, 0)))
    hc = jnp.pad(state_conv, ((0, 0), (CONV_PAD - CONV_HIST, 0), (0, 0)))
    hl = state_lru[:, None, :]
    x1, npool, nconv, nlru = _mixer_call(x, hp, hc, hl, weights, start_pos=start_pos, bblk=bblk, tt=tt,
                                         seq0=seq0)
    return x1, npool, nconv, nlru[:, 0, :]


def _routing_plan(topi, rank, counts, *, tm, n_tiles):
    n_exp = counts.shape[0]
    tiles_per = (counts + tm - 1) // tm
    tiles_cum = jnp.cumsum(tiles_per)
    base = (tiles_cum - tiles_per) * tm
    experts = jnp.arange(n_exp, dtype=jnp.int32)[:, None, None]
    pos = rank + jnp.sum(jnp.where(topi[None] == experts, base[:, None, None], 0), axis=0)
    n_used = tiles_cum[-1]
    tile_ids = jnp.minimum(jnp.arange(n_tiles, dtype=jnp.int32), n_used - 1)
    tile_expert = jnp.sum((tiles_cum[None, :] <= tile_ids[:, None]).astype(jnp.int32), axis=1)
    return pos, tile_expert, n_used.reshape(1).astype(jnp.int32)


def _moe_rows(x1_parts, windows, norm_ffn, w_router_t, b_router, w_gu, b_gu, w_dn, b_dn):
    d = x1_parts[0].shape[1]
    chunks = d // LANES
    n_exp = w_router_t.shape[0]
    n = sum(w.count for w in windows) * TOKEN_TILE
    xn2, topi, topw, rank, cnt = _router_call(x1_parts, windows, norm_ffn, w_router_t, b_router, tb=TOKEN_TILE)
    n_tiles = (n * TOP_K + n_exp * (EXPERT_TILE - 1)) // EXPERT_TILE
    p_rows = n_tiles * EXPERT_TILE
    pos, tile_expert, n_used = _routing_plan(
        topi, rank, cnt[:, 0].astype(jnp.int32), tm=EXPERT_TILE, n_tiles=n_tiles)
    w, steps = _sc_step_rows(n)
    pos_steps = pos.reshape(TOP_K, SC_WORKERS * steps, w).transpose(1, 0, 2)
    xchunks = d // (2 * LANES)
    xs = _dispatch_rows(xn2.reshape(n, xchunks, LANES), pos_steps, p_rows)
    ys = _expert_call(tile_expert, n_used, xs.reshape(p_rows * xchunks, LANES), w_gu, b_gu, w_dn, b_dn,
                      tm=EXPERT_TILE)
    yk = _combine_rows(ys.reshape(p_rows, chunks, LANES), pos_steps, n)
    return yk.reshape(TOP_K, n * chunks, LANES), topw


def kernel(x_prompt, x_sample, state_pool, state_conv, state_lru, norm_mix, w_in, b_gate, pool_w, pool_scale,
           conv_w, conv_b, lru_wa, lru_ba, lru_wi, lru_bi, lru_lambda, w_br_pool, w_br_lru, w_out, norm_ffn,
           w_router, b_router, w_gu, b_gu, w_dn, b_dn, norm_final):
    bp, tp, d = x_prompt.shape
    bs, ts, _ = x_sample.shape
    n_p, n_s = bp * tp, bs * ts
    mw = _mixer_weights(norm_mix[0], w_in[0], b_gate[0], pool_w[0], pool_scale[0], conv_w[0], conv_b[0],
                        lru_wa[0], lru_ba[0], lru_wi[0], lru_bi[0], lru_lambda[0], w_br_pool[0], w_br_lru[0],
                        w_out[0])
    w_router_t = w_router[0].T.astype(BF16)
    zeros = lambda *shape: jnp.zeros(shape, x_prompt.dtype)

    assert sum(PROMPT_GROUP_SEQS) == bp and tp % TOKEN_TILE == 0 and n_s % TOKEN_TILE == 0
    seq_tiles = tp // TOKEN_TILE
    s_tiles = n_s // TOKEN_TILE
    x1_s, pool_s, conv_s, lru_s = _run_mixer(
        x_sample, state_pool[0], state_conv[0], state_lru[0], mw, start_pos=PAST_LEN, bblk=bs, tt=ts)
    y_p, y_s = None, None
    pools, convs, lrus = [], [], []
    seq0 = 0
    for g, seqs in enumerate(PROMPT_GROUP_SEQS):
        x1_g, pool_g, conv_g, lru_g = _run_mixer(
            x_prompt, zeros(seqs, POOL_HIST, d), zeros(seqs, CONV_HIST, d), zeros(seqs, d), mw,
            start_pos=0, bblk=1, tt=MIXER_TILE, seq0=seq0)
        pools.append(pool_g)
        convs.append(conv_g)
        lrus.append(lru_g)
        last = g == len(PROMPT_GROUP_SEQS) - 1
        g_tiles = seqs * seq_tiles
        parts, windows = [x1_g.reshape(seqs * tp, d)], [_Window(0, g_tiles)]
        out_shapes, out_windows, out_init = [(n_p, d)], [_Window(seq0 * seq_tiles, g_tiles)], [y_p]
        seq0 += seqs
        if last:
            parts.append(x1_s.reshape(n_s, d))
            windows.append(_Window(0, s_tiles))
            out_shapes.append((n_s, d))
            out_windows.append(_Window(0, s_tiles))
            out_init.append(None)
        yk, topw = _moe_rows(parts, windows, norm_ffn[0], w_router_t, b_router[0], w_gu[0], b_gu[0], w_dn[0],
                             b_dn[0])
        outs = _final_call(parts, windows, yk, topw.T, norm_final, out_shapes, out_windows, out_init,
                           tb=TOKEN_TILE)
        y_p = outs[0]
        if last:
            y_s = outs[1]

    cat = lambda xs: jnp.concatenate(xs, axis=0)[None]
    return (y_p.reshape(bp, tp, d), y_s.reshape(bs, ts, d), cat(pools), cat(convs), cat(lrus),
            pool_s[None], conv_s[None], lru_s[None])
```

```python
import functools
from typing import NamedTuple

import jax
import jax.numpy as jnp
from jax import lax
from jax.experimental import pallas as pl
from jax.experimental.pallas import tpu as pltpu
from jax.experimental.pallas import tpu_sc as plsc

BF16 = jnp.bfloat16
F32 = jnp.float32

POOL_WINDOWS = (2, 4, 8, 16)
POOL_HIST = max(POOL_WINDOWS) - 1
CONV_WIDTH = 4
CONV_HIST = CONV_WIDTH - 1
LRU_HEADS = 16
LRU_C = 8.0
N_EXPERTS = 32
TOP_K = 4
SWIGLU_LIMIT = 7.0
SWIGLU_ALPHA = 1.702
NORM_EPS = 1e-6

SUBLANES = 8
LANES = 128
MXU_WIDTH = 256
POOL_PAD = 16
CONV_PAD = 8
VMEM_LIMIT_BYTES = 56 * 1024 * 1024

PAST_LEN = 1024
MIXER_TILE = 256
TOKEN_TILE = 512
EXPERT_TILE = 256
PROMPT_GROUP_SEQS = (12, 4)

SC_CORES = 2
SC_SUBCORES = 16
SC_WORKERS = SC_CORES * SC_SUBCORES
SC_MAX_STEP_ROWS = 64


def _rms_norm(x, g):
    ms = jnp.mean(x * x, axis=-1, keepdims=True)
    return (x * lax.rsqrt(ms + NORM_EPS)) * g


def _dot(a, b):
    return jnp.dot(a, b, preferred_element_type=F32)


def _softplus(x):
    return jnp.maximum(x, 0.0) + jnp.log1p(jnp.exp(-jnp.abs(x)))


def _gelu_tanh(x):
    c = 0.7978845608028654
    return 0.5 * x * (1.0 + jnp.tanh(c * (x + 0.044715 * (x * x * x))))


def _lru_scan(a, b, h0):
    t_len, c = a.shape
    groups = t_len // SUBLANES
    a3 = a.reshape(groups, SUBLANES, c)
    b3 = b.reshape(groups, SUBLANES, c)
    sub = lax.broadcasted_iota(jnp.int32, (groups, SUBLANES, c), 1)
    for k in (1, 2, 4):
        a_prev = pltpu.roll(a3, k, axis=1)
        b_prev = pltpu.roll(b3, k, axis=1)
        valid = sub >= k
        b3 = jnp.where(valid, a3 * b_prev + b3, b3)
        a3 = jnp.where(valid, a3 * a_prev, a3)
    h = h0
    outs = []
    for g in range(groups):
        hg = a3[g] * h + b3[g]
        outs.append(hg)
        h = hg[SUBLANES - 1:SUBLANES, :]
    return jnp.concatenate(outs, axis=0), h


def _mixer_kernel(x_ref, hp_ref, hc_ref, hl_ref, nmix_ref, win_ref, bgate_ref, poolw_ref, pscale_ref,
                  convw_ref, convb_ref, wai_ref, ba_ref, bi_ref, lam_ref, wbrp_ref, wbrl_ref, wout_ref,
                  x1_ref, npool_ref, nconv_ref, nlru_ref,
                  pool_ext, conv_ext, h_carry, *, start_pos, bblk, tt, d):
    j = pl.program_id(1)

    @pl.when(j == 0)
    def _():
        pool_ext[:, 0:POOL_PAD, :] = hp_ref[...]
        conv_ext[:, 0:CONV_PAD, :] = hc_ref[...]
        h_carry[...] = hl_ref[...]

    rows = bblk * tt
    x = x_ref[...].reshape(rows, d)
    xnb = _rms_norm(x, nmix_ref[...]).astype(BF16)

    u_pool = _dot(xnb, win_ref[:, 0:d])
    pos1 = lax.broadcasted_iota(jnp.int32, (tt, 1), 0) + (start_pos + 1) + j * tt
    gd = d // len(POOL_WINDOWS)
    y_pool_rows = []
    for b in range(bblk):
        pool_ext[b, POOL_PAD:POOL_PAD + tt, :] = u_pool[b * tt:(b + 1) * tt, :]
        parts = []
        for g, w in enumerate(POOL_WINDOWS):
            sl = slice(g * gd, (g + 1) * gd)
            s = pool_ext[b, :, sl]
            width = 1
            while width < w:
                s = s + pltpu.roll(s, width, axis=0)
                width *= 2
            cur = s[POOL_PAD:POOL_PAD + tt, :]
            inv_cnt = 1.0 / jnp.minimum(pos1, w).astype(F32)
            pooled = cur * inv_cnt - u_pool[b * tt:(b + 1) * tt, sl]
            mixed = _dot(pooled.astype(BF16), poolw_ref[g])
            parts.append(mixed * pscale_ref[:, sl])
        y_pool_rows.append(jnp.concatenate(parts, axis=1))
        npool_ref[b] = pool_ext[b, POOL_PAD + tt - POOL_HIST:POOL_PAD + tt, :]
        pool_ext[b, 0:POOL_PAD, :] = pool_ext[b, tt:tt + POOL_PAD, :]
    y_pool = jnp.concatenate(y_pool_rows, axis=0) if bblk > 1 else y_pool_rows[0]
    br_pool = _dot(y_pool.astype(BF16), wbrp_ref[...])
    g_pool = jax.nn.sigmoid(_dot(xnb, win_ref[:, 3 * d:4 * d]) + bgate_ref[:, 0:d])
    acc = g_pool * br_pool

    u_x = _dot(xnb, win_ref[:, d:2 * d])
    xc_rows = []
    for b in range(bblk):
        conv_ext[b, CONV_PAD:CONV_PAD + tt, :] = u_x[b * tt:(b + 1) * tt, :]
        ce = conv_ext[b]
        y = convb_ref[...] + ce * convw_ref[CONV_WIDTH - 1:CONV_WIDTH, :]
        for k in range(1, CONV_WIDTH):
            y = y + pltpu.roll(ce, k, axis=0) * convw_ref[CONV_WIDTH - 1 - k:CONV_WIDTH - k, :]
        xc_rows.append(y[CONV_PAD:CONV_PAD + tt, :])
        nconv_ref[b] = conv_ext[b, CONV_PAD + tt - CONV_HIST:CONV_PAD + tt, :]
        conv_ext[b, 0:CONV_PAD, :] = conv_ext[b, tt:tt + CONV_PAD, :]
    xc = jnp.concatenate(xc_rows, axis=0) if bblk > 1 else xc_rows[0]

    n_chunks = d // MXU_WIDTH
    pre_a, pre_i = [], []
    for c in range(n_chunks):
        ai = _dot(xc[:, c * MXU_WIDTH:(c + 1) * MXU_WIDTH].astype(BF16), wai_ref[c])
        pre_a.append(ai[:, 0:MXU_WIDTH])
        pre_i.append(ai[:, MXU_WIDTH:2 * MXU_WIDTH])
    r_gate = jax.nn.sigmoid(jnp.concatenate(pre_a, axis=1) + ba_ref[...])
    i_gate = jax.nn.sigmoid(jnp.concatenate(pre_i, axis=1) + bi_ref[...])
    log_a = (-LRU_C) * r_gate * _softplus(-lam_ref[...])
    a = jnp.exp(log_a)
    mult = jnp.sqrt(-jnp.tanh(log_a) * (a * a + 1.0))
    bb = mult * (i_gate * xc)
    h_rows = []
    for b in range(bblk):
        h_b, h_last = _lru_scan(a[b * tt:(b + 1) * tt, :], bb[b * tt:(b + 1) * tt, :], h_carry[b])
        h_rows.append(h_b)
        h_carry[b] = h_last
        nlru_ref[b] = h_last
    h = jnp.concatenate(h_rows, axis=0) if bblk > 1 else h_rows[0]

    u_gate = _dot(xnb, win_ref[:, 2 * d:3 * d])
    y_lru = h * _gelu_tanh(u_gate)
    br_lru = _dot(y_lru.astype(BF16), wbrl_ref[...])
    g_lru = jax.nn.sigmoid(_dot(xnb, win_ref[:, 4 * d:5 * d]) + bgate_ref[:, d:2 * d])
    acc = acc + g_lru * br_lru

    out = _dot(acc.astype(BF16), wout_ref[...])
    x1_ref[...] = (x + out).reshape(bblk, tt, d)


def _const_spec(shape):
    nd = len(shape)
    return pl.BlockSpec(shape, lambda *_: (0,) * nd, pipeline_mode=pl.Buffered(1))


def _mixer_call(x, hist_pool, hist_conv, hist_lru, weights, *, start_pos, bblk, tt, seq0=0):
    _, t_len, d = x.shape
    bsz = hist_pool.shape[0]
    assert seq0 % bblk == 0
    grid = (bsz // bblk, t_len // tt)
    kern = functools.partial(_mixer_kernel, start_pos=start_pos, bblk=bblk, tt=tt, d=d)
    seq_spec = lambda rows: pl.BlockSpec((bblk, rows, d), lambda b, j: (b, 0, 0))
    in_specs = [
        pl.BlockSpec((bblk, tt, d), lambda b, j: (b + seq0 // bblk, j, 0)),
        seq_spec(POOL_PAD), seq_spec(CONV_PAD), seq_spec(1),
    ] + [_const_spec(w.shape) for w in weights]
    out_shape = (
        jax.ShapeDtypeStruct((bsz, t_len, d), F32),
        jax.ShapeDtypeStruct((bsz, POOL_HIST, d), F32),
        jax.ShapeDtypeStruct((bsz, CONV_HIST, d), F32),
        jax.ShapeDtypeStruct((bsz, 1, d), F32),
    )
    out_specs = (
        pl.BlockSpec((bblk, tt, d), lambda b, j: (b, j, 0)),
        seq_spec(POOL_HIST), seq_spec(CONV_HIST), seq_spec(1),
    )
    return pl.pallas_call(
        kern,
        grid=grid,
        in_specs=in_specs,
        out_specs=out_specs,
        out_shape=out_shape,
        scratch_shapes=[
            pltpu.VMEM((bblk, POOL_PAD + tt, d), F32),
            pltpu.VMEM((bblk, CONV_PAD + tt, d), F32),
            pltpu.VMEM((bblk, 1, d), F32),
        ],
        compiler_params=pltpu.CompilerParams(
            dimension_semantics=("arbitrary", "arbitrary"), vmem_limit_bytes=VMEM_LIMIT_BYTES),
        name="mixer",
    )(x, hist_pool, hist_conv, hist_lru, *weights)


def _load_rows(ref, rows, chunks):
    return jnp.concatenate([ref[pl.ds(s, rows, stride=chunks), :] for s in range(chunks)], axis=1)


def _store_rows(ref, val, rows, chunks):
    for s in range(chunks):
        ref[pl.ds(s, rows, stride=chunks), :] = val[:, s * LANES:(s + 1) * LANES]


def _pack_halves(x):
    half = x.shape[1] // 2
    bits = lax.bitcast_convert_type(x, jnp.uint32)
    return (bits[:, :half] >> 16) | (bits[:, half:] & jnp.uint32(0xFFFF0000))


def _unpack_halves(p):
    lo = lax.bitcast_convert_type(p << 16, F32)
    hi = lax.bitcast_convert_type(p & jnp.uint32(0xFFFF0000), F32)
    return jnp.concatenate([lo, hi], axis=1)


class _Window(NamedTuple):
    first: int
    count: int


def _window_specs(windows, tb, d):
    specs, start = [], 0
    for win in windows:
        specs.append(pl.BlockSpec(
            (tb, d), lambda i, win=win, start=start: (win.first + jnp.clip(i - start, 0, win.count - 1), 0)))
        start += win.count
    return specs


def _window_load(i, windows, refs):
    val, start = refs[-1][...], sum(w.count for w in windows[:-1])
    for win, ref in zip(reversed(windows[:-1]), reversed(refs[:-1])):
        val = jnp.where(i < start, ref[...], val)
        start -= win.count
    return val


def _window_store(i, windows, refs, val):
    start = 0
    for win, ref in zip(windows, refs):
        @pl.when((i >= start) & (i < start + win.count))
        def _(ref=ref):
            ref[...] = val
        start += win.count


def _router_kernel(*refs, tb, d, windows):
    x1_refs = refs[:len(windows)]
    (nffn_ref, wrt_ref, br_ref, tri_ref, xn_ref, topi_ref, topw_ref, rank_ref, cnt_ref, carry) = refs[len(windows):]
    i = pl.program_id(0)

    @pl.when(i == 0)
    def _():
        carry[...] = jnp.zeros_like(carry)

    xnb = _rms_norm(_window_load(i, windows, x1_refs), nffn_ref[...]).astype(BF16)
    xn_ref_chunks = d // (2 * LANES)
    _store_rows(xn_ref, _pack_halves(xnb.astype(F32)), tb, xn_ref_chunks)
    logits = lax.dot_general(wrt_ref[...], xnb, (((1,), (1,)), ((), ())),
                             preferred_element_type=F32) + br_ref[...]
    n_exp = logits.shape[0]
    iota_e = lax.broadcasted_iota(jnp.int32, (n_exp, tb), 0)
    work = logits
    vals, idxs, hots = [], [], []
    for _ in range(TOP_K):
        m = jnp.max(work, axis=0, keepdims=True)
        idx = jnp.min(jnp.where(work == m, iota_e, n_exp), axis=0, keepdims=True)
        hot = iota_e == idx
        vals.append(m)
        idxs.append(idx)
        hots.append(hot)
        work = jnp.where(hot, -jnp.inf, work)
    exps = [jnp.exp(v - vals[0]) for v in vals]
    denom = exps[0] + exps[1] + exps[2] + exps[3]
    topi_ref[...] = jnp.concatenate(idxs, axis=0)
    topw_ref[...] = jnp.concatenate([e / denom for e in exps], axis=0)

    sel = sum(jnp.where(h, 1.0, 0.0) for h in hots)
    before = _dot(sel.astype(BF16), tri_ref[...]) + carry[:, 0:1]
    ranks = [jnp.sum(jnp.where(h, before, 0.0), axis=0, keepdims=True) for h in hots]
    rank_ref[...] = jnp.concatenate(ranks, axis=0).astype(jnp.int32)
    carry[...] = carry[...] + jnp.sum(sel, axis=1, keepdims=True)
    cnt_ref[...] = carry[...]


def _router_call(x1_parts, windows, norm_ffn, w_router_t, b_router, *, tb):
    d = x1_parts[0].shape[1]
    n_tiles = sum(w.count for w in windows)
    n = n_tiles * tb
    chunks = d // (2 * LANES)
    n_exp = w_router_t.shape[0]
    tri = jnp.triu(jnp.ones((tb, tb), BF16), k=1)
    tok = lambda rows, dt: (pl.BlockSpec((rows, tb), lambda i: (0, i)), jax.ShapeDtypeStruct((rows, n), dt))
    (topi_spec, topi_shape), (topw_spec, topw_shape), (rank_spec, rank_shape) = (
        tok(TOP_K, jnp.int32), tok(TOP_K, F32), tok(TOP_K, jnp.int32))
    return pl.pallas_call(
        functools.partial(_router_kernel, tb=tb, d=d, windows=tuple(windows)),
        grid=(n_tiles,),
        in_specs=[
            *_window_specs(windows, tb, d),
            _const_spec((1, d)), _const_spec((n_exp, d)), _const_spec((n_exp, 1)), _const_spec((tb, tb)),
        ],
        out_specs=(
            pl.BlockSpec((tb * chunks, LANES), lambda i: (i, 0)),
            topi_spec, topw_spec, rank_spec,
            pl.BlockSpec((n_exp, LANES), lambda i: (0, 0)),
        ),
        out_shape=(
            jax.ShapeDtypeStruct((n * chunks, LANES), jnp.uint32),
            topi_shape, topw_shape, rank_shape,
            jax.ShapeDtypeStruct((n_exp, LANES), F32),
        ),
        scratch_shapes=[pltpu.VMEM((n_exp, LANES), F32)],
        compiler_params=pltpu.CompilerParams(
            dimension_semantics=("arbitrary",), vmem_limit_bytes=VMEM_LIMIT_BYTES),
        name="router",
    )(*x1_parts, norm_ffn.reshape(1, d), w_router_t, b_router.reshape(n_exp, 1), tri)


def _expert_kernel(te_ref, nu_ref, xs_ref, wgu_ref, bgu_ref, wdn_ref, bdn_ref, ys_ref, wgu_bf, wdn_bf,
                   *, tm, d, dff):
    i = pl.program_id(0)
    prev = te_ref[jnp.maximum(i - 1, 0)]

    @pl.when((i == 0) | (te_ref[i] != prev))
    def _():
        wgu_bf[...] = wgu_ref[0].astype(BF16)
        wdn_bf[...] = wdn_ref[0].astype(BF16)

    @pl.when(i < nu_ref[0])
    def _():
        x = _unpack_halves(_load_rows(xs_ref, tm, d // (2 * LANES))).astype(BF16)
        gu = _dot(x, wgu_bf[...]) + bgu_ref[0]
        gate = jnp.minimum(gu[:, :dff], SWIGLU_LIMIT)
        up = jnp.clip(gu[:, dff:], -SWIGLU_LIMIT, SWIGLU_LIMIT)
        hid = (up + 1.0) * gate * jax.nn.sigmoid(SWIGLU_ALPHA * gate)
        y = _dot(hid.astype(BF16), wdn_bf[...]) + bdn_ref[0]
        _store_rows(ys_ref, _pack_halves(y.astype(BF16).astype(F32)), tm, d // (2 * LANES))


def _expert_call(tile_expert, n_used, xs, w_gu, b_gu, w_dn, b_dn, *, tm):
    n_exp, d, dff2 = w_gu.shape
    dff = dff2 // 2
    chunks = d // LANES
    xchunks = d // (2 * LANES)
    p_rows = xs.shape[0] // xchunks
    row_blk = lambda i, te, nu: (jnp.minimum(i, nu[0] - 1), 0)
    exp_blk = lambda i, te, nu: (te[i], 0, 0)
    return pl.pallas_call(
        functools.partial(_expert_kernel, tm=tm, d=d, dff=dff),
        grid_spec=pltpu.PrefetchScalarGridSpec(
            num_scalar_prefetch=2,
            grid=(p_rows // tm,),
            in_specs=[
                pl.BlockSpec((tm * xchunks, LANES), row_blk),
                pl.BlockSpec((1, d, dff2), exp_blk),
                pl.BlockSpec((1, 1, dff2), exp_blk),
                pl.BlockSpec((1, dff, d), exp_blk),
                pl.BlockSpec((1, 1, d), exp_blk),
            ],
            out_specs=pl.BlockSpec((tm * xchunks, LANES), row_blk),
            scratch_shapes=[pltpu.VMEM((d, dff2), BF16), pltpu.VMEM((dff, d), BF16)],
        ),
        out_shape=jax.ShapeDtypeStruct((p_rows * xchunks, LANES), jnp.uint32),
        compiler_params=pltpu.CompilerParams(
            dimension_semantics=("arbitrary",), vmem_limit_bytes=VMEM_LIMIT_BYTES),
        name="experts",
    )(tile_expert, n_used, xs, w_gu, b_gu.reshape(n_exp, 1, dff2), w_dn, b_dn.reshape(n_exp, 1, d))


def _final_kernel(*refs, tb, d, in_windows, out_windows, n_alias):
    n_in, n_out = len(in_windows), len(out_windows)
    x1_refs = refs[:n_in]
    yk_ref, w_ref, nfin_ref = refs[n_in:n_in + 3]
    out_refs = refs[n_in + 3 + n_alias:n_in + 3 + n_alias + n_out]
    i = pl.program_id(0)
    w = w_ref[...]
    chunks = d // (2 * LANES)
    moe = _unpack_halves(_load_rows(yk_ref.at[0], tb, chunks)) * w[:, 0:1]
    for k in range(1, TOP_K):
        moe = moe + _unpack_halves(_load_rows(yk_ref.at[k], tb, chunks)) * w[:, k:k + 1]
    out = _rms_norm(_window_load(i, in_windows, x1_refs) + moe, nfin_ref[...])
    _window_store(i, out_windows, out_refs, out)


def _final_call(x1_parts, in_windows, yk, topw_t, norm_final, out_shapes, out_windows, out_init, *, tb):
    d = x1_parts[0].shape[1]
    n_tiles = sum(w.count for w in in_windows)
    chunks = d // (2 * LANES)
    alias_in = [a for a in out_init if a is not None]
    n_fixed = len(x1_parts) + 3
    aliases, j = {}, 0
    for k, a in enumerate(out_init):
        if a is not None:
            aliases[n_fixed + j] = k
            j += 1
    return pl.pallas_call(
        functools.partial(_final_kernel, tb=tb, d=d, in_windows=tuple(in_windows),
                          out_windows=tuple(out_windows), n_alias=len(alias_in)),
        grid=(n_tiles,),
        in_specs=[
            *_window_specs(in_windows, tb, d),
            pl.BlockSpec((TOP_K, tb * chunks, LANES), lambda i: (0, i, 0)),
            pl.BlockSpec((tb, TOP_K), lambda i: (i, 0)),
            _const_spec((1, d)),
            *[pl.BlockSpec(memory_space=pl.ANY) for _ in alias_in],
        ],
        out_specs=_window_specs(out_windows, tb, d),
        out_shape=[jax.ShapeDtypeStruct(s, F32) for s in out_shapes],
        input_output_aliases=aliases,
        compiler_params=pltpu.CompilerParams(
            dimension_semantics=("arbitrary",), vmem_limit_bytes=VMEM_LIMIT_BYTES),
        name="final",
    )(*x1_parts, yk, topw_t, norm_final.reshape(1, d), *alias_in)


def _sc_step_rows(n):
    per_worker = n // SC_WORKERS
    assert per_worker * SC_WORKERS == n
    w = max(c for c in range(SUBLANES, SC_MAX_STEP_ROWS + 1, SUBLANES) if per_worker % c == 0)
    return w, per_worker // w


def _sc_mesh():
    return plsc.VectorSubcoreMesh(core_axis_name="c", subcore_axis_name="s",
                                  num_cores=SC_CORES, num_subcores=SC_SUBCORES)


def _dispatch_rows(x_tiles, pos_steps, p_rows):
    n, chunks, _ = x_tiles.shape
    w, steps = _sc_step_rows(n)

    @functools.partial(
        pl.kernel, mesh=_sc_mesh(),
        out_type=jax.ShapeDtypeStruct((p_rows, chunks, LANES), x_tiles.dtype),
        scratch_types=[pltpu.VMEM((w, chunks, LANES), x_tiles.dtype), pltpu.VMEM((TOP_K, w), jnp.int32)],
        name="dispatch",
    )
    def run(x_hbm, pos_hbm, out_hbm, buf, idx):
        wid = lax.axis_index("s") * SC_CORES + lax.axis_index("c")

        @pl.loop(0, steps)
        def _(i):
            step = wid * steps + i
            pltpu.sync_copy(x_hbm.at[pl.ds(pl.multiple_of(step * w, SUBLANES), w)], buf)
            pltpu.sync_copy(pos_hbm.at[step], idx)
            for k in range(TOP_K):
                pltpu.sync_copy(buf, out_hbm.at[idx.at[k]])

    return run(x_tiles, pos_steps)


def _combine_rows(y_tiles, pos_steps, n):
    _, chunks, _ = y_tiles.shape
    w, steps = _sc_step_rows(n)

    @functools.partial(
        pl.kernel, mesh=_sc_mesh(),
        out_type=jax.ShapeDtypeStruct((TOP_K, n, chunks, LANES), y_tiles.dtype),
        scratch_types=[pltpu.VMEM((w, chunks, LANES), y_tiles.dtype), pltpu.VMEM((TOP_K, w), jnp.int32)],
        name="combine",
    )
    def run(y_hbm, pos_hbm, out_hbm, buf, idx):
        wid = lax.axis_index("s") * SC_CORES + lax.axis_index("c")

        @pl.loop(0, steps)
        def _(i):
            step = wid * steps + i
            pltpu.sync_copy(pos_hbm.at[step], idx)
            for k in range(TOP_K):
                pltpu.sync_copy(y_hbm.at[idx.at[k]], buf)
                pltpu.sync_copy(buf, out_hbm.at[k, pl.ds(pl.multiple_of(step * w, SUBLANES), w)])

    return run(y_tiles, pos_steps)


def _block_diag(w, per_block):
    heads, n, _ = w.shape
    w4 = w.reshape(heads // per_block, per_block, n, n)
    eye = jnp.eye(per_block, dtype=w.dtype)
    bd = jnp.einsum('chij,hg->chigj', w4, eye)
    return bd.reshape(heads // per_block, per_block * n, per_block * n)


def _mixer_weights(norm_mix, w_in, b_gate, pool_w, pool_scale, conv_w, conv_b, lru_wa, lru_ba, lru_wi,
                   lru_bi, lru_lambda, w_br_pool, w_br_lru, w_out):
    row = lambda v: v.reshape(1, -1)
    head_dim = lru_wa.shape[-1]
    per_block = MXU_WIDTH // head_dim
    w_ai = jnp.concatenate([_block_diag(lru_wa, per_block), _block_diag(lru_wi, per_block)], axis=-1)
    return (row(norm_mix), w_in.astype(BF16), row(b_gate), pool_w.astype(BF16), row(pool_scale),
            conv_w, row(conv_b), w_ai.astype(BF16), row(lru_ba), row(lru_bi), row(lru_lambda),
            w_br_pool.astype(BF16), w_br_lru.astype(BF16), w_out.astype(BF16))


def _run_mixer(x, state_pool, state_conv, state_lru, weights, *, start_pos, bblk, tt, seq0=0):
    hp = jnp.pad(state_pool, ((0, 0), (POOL_PAD - POOL_HIST, 0), (0, 0)))
    hc = jnp.pad(state_conv, ((0, 0), (CONV_PAD - CONV_HIST, 0), (0, 0)))
    hl = state_lru[:, None, :]
    x1, npool, nconv, nlru = _mixer_call(x, hp, hc, hl, weights, start_pos=start_pos, bblk=bblk, tt=tt,
                                         seq0=seq0)
    return x1, npool, nconv, nlru[:, 0, :]


def _routing_plan(topi, rank, counts, *, tm, n_tiles):
    n_exp = counts.shape[0]
    tiles_per = (counts + tm - 1) // tm
    tiles_cum = jnp.cumsum(tiles_per)
    base = (tiles_cum - tiles_per) * tm
    experts = jnp.arange(n_exp, dtype=jnp.int32)[:, None, None]
    pos = rank + jnp.sum(jnp.where(topi[None] == experts, base[:, None, None], 0), axis=0)
    n_used = tiles_cum[-1]
    tile_ids = jnp.minimum(jnp.arange(n_tiles, dtype=jnp.int32), n_used - 1)
    tile_expert = jnp.sum((tiles_cum[None, :] <= tile_ids[:, None]).astype(jnp.int32), axis=1)
    return pos, tile_expert, n_used.reshape(1).astype(jnp.int32)


def _moe_rows(x1_parts, windows, norm_ffn, w_router_t, b_router, w_gu, b_gu, w_dn, b_dn):
    d = x1_parts[0].shape[1]
    chunks = d // LANES
    n_exp = w_router_t.shape[0]
    n = sum(w.count for w in windows) * TOKEN_TILE
    xn2, topi, topw, rank, cnt = _router_call(x1_parts, windows, norm_ffn, w_router_t, b_router, tb=TOKEN_TILE)
    n_tiles = (n * TOP_K + n_exp * (EXPERT_TILE - 1)) // EXPERT_TILE
    p_rows = n_tiles * EXPERT_TILE
    pos, tile_expert, n_used = _routing_plan(
        topi, rank, cnt[:, 0].astype(jnp.int32), tm=EXPERT_TILE, n_tiles=n_tiles)
    w, steps = _sc_step_rows(n)
    pos_steps = pos.reshape(TOP_K, SC_WORKERS * steps, w).transpose(1, 0, 2)
    xchunks = d // (2 * LANES)
    xs = _dispatch_rows(xn2.reshape(n, xchunks, LANES), pos_steps, p_rows)
    ys = _expert_call(tile_expert, n_used, xs.reshape(p_rows * xchunks, LANES), w_gu, b_gu, w_dn, b_dn,
                      tm=EXPERT_TILE)
    yk = _combine_rows(ys.reshape(p_rows, xchunks, LANES), pos_steps, n)
    return yk.reshape(TOP_K, n * xchunks, LANES), topw


def kernel(x_prompt, x_sample, state_pool, state_conv, state_lru, norm_mix, w_in, b_gate, pool_w, pool_scale,
           conv_w, conv_b, lru_wa, lru_ba, lru_wi, lru_bi, lru_lambda, w_br_pool, w_br_lru, w_out, norm_ffn,
           w_router, b_router, w_gu, b_gu, w_dn, b_dn, norm_final):
    bp, tp, d = x_prompt.shape
    bs, ts, _ = x_sample.shape
    n_p, n_s = bp * tp, bs * ts
    mw = _mixer_weights(norm_mix[0], w_in[0], b_gate[0], pool_w[0], pool_scale[0], conv_w[0], conv_b[0],
                        lru_wa[0], lru_ba[0], lru_wi[0], lru_bi[0], lru_lambda[0], w_br_pool[0], w_br_lru[0],
                        w_out[0])
    w_router_t = w_router[0].T.astype(BF16)
    zeros = lambda *shape: jnp.zeros(shape, x_prompt.dtype)

    assert sum(PROMPT_GROUP_SEQS) == bp and tp % TOKEN_TILE == 0 and n_s % TOKEN_TILE == 0
    seq_tiles = tp // TOKEN_TILE
    s_tiles = n_s // TOKEN_TILE
    x1_s, pool_s, conv_s, lru_s = _run_mixer(
        x_sample, state_pool[0], state_conv[0], state_lru[0], mw, start_pos=PAST_LEN, bblk=bs, tt=ts)
    y_p, y_s = None, None
    pools, convs, lrus = [], [], []
    seq0 = 0
    for g, seqs in enumerate(PROMPT_GROUP_SEQS):
        x1_g, pool_g, conv_g, lru_g = _run_mixer(
            x_prompt, zeros(seqs, POOL_HIST, d), zeros(seqs, CONV_HIST, d), zeros(seqs, d), mw,
            start_pos=0, bblk=1, tt=MIXER_TILE, seq0=seq0)
        pools.append(pool_g)
        convs.append(conv_g)
        lrus.append(lru_g)
        last = g == len(PROMPT_GROUP_SEQS) - 1
        g_tiles = seqs * seq_tiles
        parts, windows = [x1_g.reshape(seqs * tp, d)], [_Window(0, g_tiles)]
        out_shapes, out_windows, out_init = [(n_p, d)], [_Window(seq0 * seq_tiles, g_tiles)], [y_p]
        seq0 += seqs
        if last:
            parts.append(x1_s.reshape(n_s, d))
            windows.append(_Window(0, s_tiles))
            out_shapes.append((n_s, d))
            out_windows.append(_Window(0, s_tiles))
            out_init.append(None)
        yk, topw = _moe_rows(parts, windows, norm_ffn[0], w_router_t, b_router[0], w_gu[0], b_gu[0], w_dn[0],
                             b_dn[0])
        outs = _final_call(parts, windows, yk, topw.T, norm_final, out_shapes, out_windows, out_init,
                           tb=TOKEN_TILE)
        y_p = outs[0]
        if last:
            y_s = outs[1]

    cat = lambda xs: jnp.concatenate(xs, axis=0)[None]
    return (y_p.reshape(bp, tp, d), y_s.reshape(bs, ts, d), cat(pools), cat(convs), cat(lrus),
            pool_s[None], conv_s[None], lru_s[None])
```

```python
import functools
from typing import NamedTuple

import jax
import jax.numpy as jnp
from jax import lax
from jax.experimental import pallas as pl
from jax.experimental.pallas import tpu as pltpu
from jax.experimental.pallas import tpu_sc as plsc

BF16 = jnp.bfloat16
F32 = jnp.float32

POOL_WINDOWS = (2, 4, 8, 16)
POOL_HIST = max(POOL_WINDOWS) - 1
CONV_WIDTH = 4
CONV_HIST = CONV_WIDTH - 1
LRU_HEADS = 16
LRU_C = 8.0
N_EXPERTS = 32
TOP_K = 4
SWIGLU_LIMIT = 7.0
SWIGLU_ALPHA = 1.702
NORM_EPS = 1e-6

SUBLANES = 8
LANES = 128
MXU_WIDTH = 256
POOL_PAD = 16
CONV_PAD = 8
VMEM_LIMIT_BYTES = 56 * 1024 * 1024

PAST_LEN = 1024
MIXER_TILE = 256
TOKEN_TILE = 512
EXPERT_TILE = 512
PROMPT_GROUP_SEQS = (12, 4)

SC_CORES = 2
SC_SUBCORES = 16
SC_WORKERS = SC_CORES * SC_SUBCORES
SC_MAX_STEP_ROWS = 64


def _rms_norm(x, g):
    ms = jnp.mean(x * x, axis=-1, keepdims=True)
    return (x * lax.rsqrt(ms + NORM_EPS)) * g


def _dot(a, b):
    return jnp.dot(a, b, preferred_element_type=F32)


def _softplus(x):
    return jnp.maximum(x, 0.0) + jnp.log1p(jnp.exp(-jnp.abs(x)))


def _gelu_tanh(x):
    c = 0.7978845608028654
    return 0.5 * x * (1.0 + jnp.tanh(c * (x + 0.044715 * (x * x * x))))


def _lru_scan(a, b, h0):
    t_len, c = a.shape
    groups = t_len // SUBLANES
    a3 = a.reshape(groups, SUBLANES, c)
    b3 = b.reshape(groups, SUBLANES, c)
    sub = lax.broadcasted_iota(jnp.int32, (groups, SUBLANES, c), 1)
    for k in (1, 2, 4):
        a_prev = pltpu.roll(a3, k, axis=1)
        b_prev = pltpu.roll(b3, k, axis=1)
        valid = sub >= k
        b3 = jnp.where(valid, a3 * b_prev + b3, b3)
        a3 = jnp.where(valid, a3 * a_prev, a3)
    h = h0
    outs = []
    for g in range(groups):
        hg = a3[g] * h + b3[g]
        outs.append(hg)
        h = hg[SUBLANES - 1:SUBLANES, :]
    return jnp.concatenate(outs, axis=0), h


def _mixer_kernel(x_ref, hp_ref, hc_ref, hl_ref, nmix_ref, win_ref, bgate_ref, poolw_ref, pscale_ref,
                  convw_ref, convb_ref, wai_ref, ba_ref, bi_ref, lam_ref, wbrp_ref, wbrl_ref, wout_ref,
                  x1_ref, npool_ref, nconv_ref, nlru_ref,
                  pool_ext, conv_ext, h_carry, *, start_pos, bblk, tt, d):
    j = pl.program_id(1)

    @pl.when(j == 0)
    def _():
        pool_ext[:, 0:POOL_PAD, :] = hp_ref[...]
        conv_ext[:, 0:CONV_PAD, :] = hc_ref[...]
        h_carry[...] = hl_ref[...]

    rows = bblk * tt
    x = x_ref[...].reshape(rows, d)
    xnb = _rms_norm(x, nmix_ref[...]).astype(BF16)

    u_pool = _dot(xnb, win_ref[:, 0:d])
    pos1 = lax.broadcasted_iota(jnp.int32, (tt, 1), 0) + (start_pos + 1) + j * tt
    gd = d // len(POOL_WINDOWS)
    y_pool_rows = []
    for b in range(bblk):
        pool_ext[b, POOL_PAD:POOL_PAD + tt, :] = u_pool[b * tt:(b + 1) * tt, :]
        parts = []
        for g, w in enumerate(POOL_WINDOWS):
            sl = slice(g * gd, (g + 1) * gd)
            s = pool_ext[b, :, sl]
            width = 1
            while width < w:
                s = s + pltpu.roll(s, width, axis=0)
                width *= 2
            cur = s[POOL_PAD:POOL_PAD + tt, :]
            inv_cnt = 1.0 / jnp.minimum(pos1, w).astype(F32)
            pooled = cur * inv_cnt - u_pool[b * tt:(b + 1) * tt, sl]
            mixed = _dot(pooled.astype(BF16), poolw_ref[g])
            parts.append(mixed * pscale_ref[:, sl])
        y_pool_rows.append(jnp.concatenate(parts, axis=1))
        npool_ref[b] = pool_ext[b, POOL_PAD + tt - POOL_HIST:POOL_PAD + tt, :]
        pool_ext[b, 0:POOL_PAD, :] = pool_ext[b, tt:tt + POOL_PAD, :]
    y_pool = jnp.concatenate(y_pool_rows, axis=0) if bblk > 1 else y_pool_rows[0]
    br_pool = _dot(y_pool.astype(BF16), wbrp_ref[...])
    g_pool = jax.nn.sigmoid(_dot(xnb, win_ref[:, 3 * d:4 * d]) + bgate_ref[:, 0:d])
    acc = g_pool * br_pool

    u_x = _dot(xnb, win_ref[:, d:2 * d])
    xc_rows = []
    for b in range(bblk):
        conv_ext[b, CONV_PAD:CONV_PAD + tt, :] = u_x[b * tt:(b + 1) * tt, :]
        ce = conv_ext[b]
        y = convb_ref[...] + ce * convw_ref[CONV_WIDTH - 1:CONV_WIDTH, :]
        for k in range(1, CONV_WIDTH):
            y = y + pltpu.roll(ce, k, axis=0) * convw_ref[CONV_WIDTH - 1 - k:CONV_WIDTH - k, :]
        xc_rows.append(y[CONV_PAD:CONV_PAD + tt, :])
        nconv_ref[b] = conv_ext[b, CONV_PAD + tt - CONV_HIST:CONV_PAD + tt, :]
        conv_ext[b, 0:CONV_PAD, :] = conv_ext[b, tt:tt + CONV_PAD, :]
    xc = jnp.concatenate(xc_rows, axis=0) if bblk > 1 else xc_rows[0]

    n_chunks = d // MXU_WIDTH
    pre_a, pre_i = [], []
    for c in range(n_chunks):
        ai = _dot(xc[:, c * MXU_WIDTH:(c + 1) * MXU_WIDTH].astype(BF16), wai_ref[c])
        pre_a.append(ai[:, 0:MXU_WIDTH])
        pre_i.append(ai[:, MXU_WIDTH:2 * MXU_WIDTH])
    r_gate = jax.nn.sigmoid(jnp.concatenate(pre_a, axis=1) + ba_ref[...])
    i_gate = jax.nn.sigmoid(jnp.concatenate(pre_i, axis=1) + bi_ref[...])
    log_a = (-LRU_C) * r_gate * _softplus(-lam_ref[...])
    a = jnp.exp(log_a)
    mult = jnp.sqrt(-jnp.tanh(log_a) * (a * a + 1.0))
    bb = mult * (i_gate * xc)
    h_rows = []
    for b in range(bblk):
        h_b, h_last = _lru_scan(a[b * tt:(b + 1) * tt, :], bb[b * tt:(b + 1) * tt, :], h_carry[b])
        h_rows.append(h_b)
        h_carry[b] = h_last
        nlru_ref[b] = h_last
    h = jnp.concatenate(h_rows, axis=0) if bblk > 1 else h_rows[0]

    u_gate = _dot(xnb, win_ref[:, 2 * d:3 * d])
    y_lru = h * _gelu_tanh(u_gate)
    br_lru = _dot(y_lru.astype(BF16), wbrl_ref[...])
    g_lru = jax.nn.sigmoid(_dot(xnb, win_ref[:, 4 * d:5 * d]) + bgate_ref[:, d:2 * d])
    acc = acc + g_lru * br_lru

    out = _dot(acc.astype(BF16), wout_ref[...])
    x1_ref[...] = (x + out).reshape(bblk, tt, d)


def _const_spec(shape):
    nd = len(shape)
    return pl.BlockSpec(shape, lambda *_: (0,) * nd, pipeline_mode=pl.Buffered(1))


def _mixer_call(x, hist_pool, hist_conv, hist_lru, weights, *, start_pos, bblk, tt, seq0=0):
    _, t_len, d = x.shape
    bsz = hist_pool.shape[0]
    assert seq0 % bblk == 0
    grid = (bsz // bblk, t_len // tt)
    kern = functools.partial(_mixer_kernel, start_pos=start_pos, bblk=bblk, tt=tt, d=d)
    seq_spec = lambda rows: pl.BlockSpec((bblk, rows, d), lambda b, j: (b, 0, 0))
    in_specs = [
        pl.BlockSpec((bblk, tt, d), lambda b, j: (b + seq0 // bblk, j, 0)),
        seq_spec(POOL_PAD), seq_spec(CONV_PAD), seq_spec(1),
    ] + [_const_spec(w.shape) for w in weights]
    out_shape = (
        jax.ShapeDtypeStruct((bsz, t_len, d), F32),
        jax.ShapeDtypeStruct((bsz, POOL_HIST, d), F32),
        jax.ShapeDtypeStruct((bsz, CONV_HIST, d), F32),
        jax.ShapeDtypeStruct((bsz, 1, d), F32),
    )
    out_specs = (
        pl.BlockSpec((bblk, tt, d), lambda b, j: (b, j, 0)),
        seq_spec(POOL_HIST), seq_spec(CONV_HIST), seq_spec(1),
    )
    return pl.pallas_call(
        kern,
        grid=grid,
        in_specs=in_specs,
        out_specs=out_specs,
        out_shape=out_shape,
        scratch_shapes=[
            pltpu.VMEM((bblk, POOL_PAD + tt, d), F32),
            pltpu.VMEM((bblk, CONV_PAD + tt, d), F32),
            pltpu.VMEM((bblk, 1, d), F32),
        ],
        compiler_params=pltpu.CompilerParams(
            dimension_semantics=("arbitrary", "arbitrary"), vmem_limit_bytes=VMEM_LIMIT_BYTES),
        name="mixer",
    )(x, hist_pool, hist_conv, hist_lru, *weights)


def _load_rows(ref, rows, chunks):
    return jnp.concatenate([ref[pl.ds(s, rows, stride=chunks), :] for s in range(chunks)], axis=1)


def _store_rows(ref, val, rows, chunks):
    for s in range(chunks):
        ref[pl.ds(s, rows, stride=chunks), :] = val[:, s * LANES:(s + 1) * LANES]


def _pack_halves(x):
    half = x.shape[1] // 2
    bits = lax.bitcast_convert_type(x, jnp.uint32)
    return (bits[:, :half] >> 16) | (bits[:, half:] & jnp.uint32(0xFFFF0000))


def _unpack_halves(p):
    lo = lax.bitcast_convert_type(p << 16, F32)
    hi = lax.bitcast_convert_type(p & jnp.uint32(0xFFFF0000), F32)
    return jnp.concatenate([lo, hi], axis=1)


class _Window(NamedTuple):
    first: int
    count: int


def _window_specs(windows, tb, d):
    specs, start = [], 0
    for win in windows:
        specs.append(pl.BlockSpec(
            (tb, d), lambda i, win=win, start=start: (win.first + jnp.clip(i - start, 0, win.count - 1), 0)))
        start += win.count
    return specs


def _window_load(i, windows, refs):
    val, start = refs[-1][...], sum(w.count for w in windows[:-1])
    for win, ref in zip(reversed(windows[:-1]), reversed(refs[:-1])):
        val = jnp.where(i < start, ref[...], val)
        start -= win.count
    return val


def _window_store(i, windows, refs, val):
    start = 0
    for win, ref in zip(windows, refs):
        @pl.when((i >= start) & (i < start + win.count))
        def _(ref=ref):
            ref[...] = val
        start += win.count


def _router_kernel(*refs, tb, d, windows):
    x1_refs = refs[:len(windows)]
    (nffn_ref, wrt_ref, br_ref, tri_ref, xn_ref, topi_ref, topw_ref, rank_ref, cnt_ref, carry) = refs[len(windows):]
    i = pl.program_id(0)

    @pl.when(i == 0)
    def _():
        carry[...] = jnp.zeros_like(carry)

    xnb = _rms_norm(_window_load(i, windows, x1_refs), nffn_ref[...]).astype(BF16)
    xn_ref_chunks = d // (2 * LANES)
    _store_rows(xn_ref, _pack_halves(xnb.astype(F32)), tb, xn_ref_chunks)
    logits = lax.dot_general(wrt_ref[...], xnb, (((1,), (1,)), ((), ())),
                             preferred_element_type=F32) + br_ref[...]
    n_exp = logits.shape[0]
    iota_e = lax.broadcasted_iota(jnp.int32, (n_exp, tb), 0)
    work = logits
    vals, idxs, hots = [], [], []
    for _ in range(TOP_K):
        m = jnp.max(work, axis=0, keepdims=True)
        idx = jnp.min(jnp.where(work == m, iota_e, n_exp), axis=0, keepdims=True)
        hot = iota_e == idx
        vals.append(m)
        idxs.append(idx)
        hots.append(hot)
        work = jnp.where(hot, -jnp.inf, work)
    exps = [jnp.exp(v - vals[0]) for v in vals]
    denom = exps[0] + exps[1] + exps[2] + exps[3]
    topi_ref[...] = jnp.concatenate(idxs, axis=0)
    topw_ref[...] = jnp.concatenate([e / denom for e in exps], axis=0)

    sel = sum(jnp.where(h, 1.0, 0.0) for h in hots)
    before = _dot(sel.astype(BF16), tri_ref[...]) + carry[:, 0:1]
    ranks = [jnp.sum(jnp.where(h, before, 0.0), axis=0, keepdims=True) for h in hots]
    rank_ref[...] = jnp.concatenate(ranks, axis=0).astype(jnp.int32)
    carry[...] = carry[...] + jnp.sum(sel, axis=1, keepdims=True)
    cnt_ref[...] = carry[...]


def _router_call(x1_parts, windows, norm_ffn, w_router_t, b_router, *, tb):
    d = x1_parts[0].shape[1]
    n_tiles = sum(w.count for w in windows)
    n = n_tiles * tb
    chunks = d // (2 * LANES)
    n_exp = w_router_t.shape[0]
    tri = jnp.triu(jnp.ones((tb, tb), BF16), k=1)
    tok = lambda rows, dt: (pl.BlockSpec((rows, tb), lambda i: (0, i)), jax.ShapeDtypeStruct((rows, n), dt))
    (topi_spec, topi_shape), (topw_spec, topw_shape), (rank_spec, rank_shape) = (
        tok(TOP_K, jnp.int32), tok(TOP_K, F32), tok(TOP_K, jnp.int32))
    return pl.pallas_call(
        functools.partial(_router_kernel, tb=tb, d=d, windows=tuple(windows)),
        grid=(n_tiles,),
        in_specs=[
            *_window_specs(windows, tb, d),
            _const_spec((1, d)), _const_spec((n_exp, d)), _const_spec((n_exp, 1)), _const_spec((tb, tb)),
        ],
        out_specs=(
            pl.BlockSpec((tb * chunks, LANES), lambda i: (i, 0)),
            topi_spec, topw_spec, rank_spec,
            pl.BlockSpec((n_exp, LANES), lambda i: (0, 0)),
        ),
        out_shape=(
            jax.ShapeDtypeStruct((n * chunks, LANES), jnp.uint32),
            topi_shape, topw_shape, rank_shape,
            jax.ShapeDtypeStruct((n_exp, LANES), F32),
        ),
        scratch_shapes=[pltpu.VMEM((n_exp, LANES), F32)],
        compiler_params=pltpu.CompilerParams(
            dimension_semantics=("arbitrary",), vmem_limit_bytes=VMEM_LIMIT_BYTES),
        name="router",
    )(*x1_parts, norm_ffn.reshape(1, d), w_router_t, b_router.reshape(n_exp, 1), tri)


def _expert_kernel(te_ref, tv_ref, nu_ref, xs_ref, wgu_ref, bgu_ref, wdn_ref, bdn_ref, ys_ref, wgu_bf, wdn_bf,
                   *, tm, d, dff):
    i = pl.program_id(0)
    prev = te_ref[jnp.maximum(i - 1, 0)]

    @pl.when((i == 0) | (te_ref[i] != prev))
    def _():
        wgu_bf[...] = wgu_ref[0].astype(BF16)
        wdn_bf[...] = wdn_ref[0].astype(BF16)

    def ffn(rows):
        x = _unpack_halves(_load_rows(xs_ref, rows, d // (2 * LANES))).astype(BF16)
        gu = _dot(x, wgu_bf[...]) + bgu_ref[0]
        gate = jnp.minimum(gu[:, :dff], SWIGLU_LIMIT)
        up = jnp.clip(gu[:, dff:], -SWIGLU_LIMIT, SWIGLU_LIMIT)
        hid = (up + 1.0) * gate * jax.nn.sigmoid(SWIGLU_ALPHA * gate)
        y = _dot(hid.astype(BF16), wdn_bf[...]) + bdn_ref[0]
        _store_rows(ys_ref, _pack_halves(y.astype(BF16).astype(F32)), rows, d // (2 * LANES))

    valid = tv_ref[i]
    pl.when(valid > tm // 2)(lambda: ffn(tm))
    pl.when((valid > 0) & (valid <= tm // 2))(lambda: ffn(tm // 2))


def _expert_call(tile_expert, tile_rows, n_used, xs, w_gu, b_gu, w_dn, b_dn, *, tm):
    n_exp, d, dff2 = w_gu.shape
    dff = dff2 // 2
    xchunks = d // (2 * LANES)
    p_rows = xs.shape[0] // xchunks
    row_blk = lambda i, te, tv, nu: (jnp.minimum(i, nu[0] - 1), 0)
    exp_blk = lambda i, te, tv, nu: (te[i], 0, 0)
    return pl.pallas_call(
        functools.partial(_expert_kernel, tm=tm, d=d, dff=dff),
        grid_spec=pltpu.PrefetchScalarGridSpec(
            num_scalar_prefetch=3,
            grid=(p_rows // tm,),
            in_specs=[
                pl.BlockSpec((tm * xchunks, LANES), row_blk),
                pl.BlockSpec((1, d, dff2), exp_blk),
                pl.BlockSpec((1, 1, dff2), exp_blk),
                pl.BlockSpec((1, dff, d), exp_blk),
                pl.BlockSpec((1, 1, d), exp_blk),
            ],
            out_specs=pl.BlockSpec((tm * xchunks, LANES), row_blk),
            scratch_shapes=[pltpu.VMEM((d, dff2), BF16), pltpu.VMEM((dff, d), BF16)],
        ),
        out_shape=jax.ShapeDtypeStruct((p_rows * xchunks, LANES), jnp.uint32),
        compiler_params=pltpu.CompilerParams(
            dimension_semantics=("arbitrary",), vmem_limit_bytes=VMEM_LIMIT_BYTES),
        name="experts",
    )(tile_expert, tile_rows, n_used, xs, w_gu, b_gu.reshape(n_exp, 1, dff2), w_dn, b_dn.reshape(n_exp, 1, d))


def _final_kernel(*refs, tb, d, in_windows, out_windows, n_alias):
    n_in, n_out = len(in_windows), len(out_windows)
    x1_refs = refs[:n_in]
    yk_ref, w_ref, nfin_ref = refs[n_in:n_in + 3]
    out_refs = refs[n_in + 3 + n_alias:n_in + 3 + n_alias + n_out]
    i = pl.program_id(0)
    w = w_ref[...]
    chunks = d // (2 * LANES)
    moe = _unpack_halves(_load_rows(yk_ref.at[0], tb, chunks)) * w[:, 0:1]
    for k in range(1, TOP_K):
        moe = moe + _unpack_halves(_load_rows(yk_ref.at[k], tb, chunks)) * w[:, k:k + 1]
    out = _rms_norm(_window_load(i, in_windows, x1_refs) + moe, nfin_ref[...])
    _window_store(i, out_windows, out_refs, out)


def _final_call(x1_parts, in_windows, yk, topw_t, norm_final, out_shapes, out_windows, out_init, *, tb):
    d = x1_parts[0].shape[1]
    n_tiles = sum(w.count for w in in_windows)
    chunks = d // (2 * LANES)
    alias_in = [a for a in out_init if a is not None]
    n_fixed = len(x1_parts) + 3
    aliases, j = {}, 0
    for k, a in enumerate(out_init):
        if a is not None:
            aliases[n_fixed + j] = k
            j += 1
    return pl.pallas_call(
        functools.partial(_final_kernel, tb=tb, d=d, in_windows=tuple(in_windows),
                          out_windows=tuple(out_windows), n_alias=len(alias_in)),
        grid=(n_tiles,),
        in_specs=[
            *_window_specs(in_windows, tb, d),
            pl.BlockSpec((TOP_K, tb * chunks, LANES), lambda i: (0, i, 0)),
            pl.BlockSpec((tb, TOP_K), lambda i: (i, 0)),
            _const_spec((1, d)),
            *[pl.BlockSpec(memory_space=pl.ANY) for _ in alias_in],
        ],
        out_specs=_window_specs(out_windows, tb, d),
        out_shape=[jax.ShapeDtypeStruct(s, F32) for s in out_shapes],
        input_output_aliases=aliases,
        compiler_params=pltpu.CompilerParams(
            dimension_semantics=("arbitrary",), vmem_limit_bytes=VMEM_LIMIT_BYTES),
        name="final",
    )(*x1_parts, yk, topw_t, norm_final.reshape(1, d), *alias_in)


def _sc_step_rows(n):
    per_worker = n // SC_WORKERS
    assert per_worker * SC_WORKERS == n
    w = max(c for c in range(SUBLANES, SC_MAX_STEP_ROWS + 1, SUBLANES) if per_worker % c == 0)
    return w, per_worker // w


def _sc_mesh():
    return plsc.VectorSubcoreMesh(core_axis_name="c", subcore_axis_name="s",
                                  num_cores=SC_CORES, num_subcores=SC_SUBCORES)


def _dispatch_rows(x_tiles, pos_steps, p_rows):
    n, chunks, _ = x_tiles.shape
    w, steps = _sc_step_rows(n)

    @functools.partial(
        pl.kernel, mesh=_sc_mesh(),
        out_type=jax.ShapeDtypeStruct((p_rows, chunks, LANES), x_tiles.dtype),
        scratch_types=[pltpu.VMEM((w, chunks, LANES), x_tiles.dtype), pltpu.VMEM((TOP_K, w), jnp.int32)],
        name="dispatch",
    )
    def run(x_hbm, pos_hbm, out_hbm, buf, idx):
        wid = lax.axis_index("s") * SC_CORES + lax.axis_index("c")

        @pl.loop(0, steps)
        def _(i):
            step = wid * steps + i
            pltpu.sync_copy(x_hbm.at[pl.ds(pl.multiple_of(step * w, SUBLANES), w)], buf)
            pltpu.sync_copy(pos_hbm.at[step], idx)
            for k in range(TOP_K):
                pltpu.sync_copy(buf, out_hbm.at[idx.at[k]])

    return run(x_tiles, pos_steps)


def _combine_rows(y_tiles, pos_steps, n):
    _, chunks, _ = y_tiles.shape
    w, steps = _sc_step_rows(n)

    @functools.partial(
        pl.kernel, mesh=_sc_mesh(),
        out_type=jax.ShapeDtypeStruct((TOP_K, n, chunks, LANES), y_tiles.dtype),
        scratch_types=[pltpu.VMEM((w, chunks, LANES), y_tiles.dtype), pltpu.VMEM((TOP_K, w), jnp.int32)],
        name="combine",
    )
    def run(y_hbm, pos_hbm, out_hbm, buf, idx):
        wid = lax.axis_index("s") * SC_CORES + lax.axis_index("c")

        @pl.loop(0, steps)
        def _(i):
            step = wid * steps + i
            pltpu.sync_copy(pos_hbm.at[step], idx)
            for k in range(TOP_K):
                pltpu.sync_copy(y_hbm.at[idx.at[k]], buf)
                pltpu.sync_copy(buf, out_hbm.at[k, pl.ds(pl.multiple_of(step * w, SUBLANES), w)])

    return run(y_tiles, pos_steps)


def _block_diag(w, per_block):
    heads, n, _ = w.shape
    w4 = w.reshape(heads // per_block, per_block, n, n)
    eye = jnp.eye(per_block, dtype=w.dtype)
    bd = jnp.einsum('chij,hg->chigj', w4, eye)
    return bd.reshape(heads // per_block, per_block * n, per_block * n)


def _mixer_weights(norm_mix, w_in, b_gate, pool_w, pool_scale, conv_w, conv_b, lru_wa, lru_ba, lru_wi,
                   lru_bi, lru_lambda, w_br_pool, w_br_lru, w_out):
    row = lambda v: v.reshape(1, -1)
    head_dim = lru_wa.shape[-1]
    per_block = MXU_WIDTH // head_dim
    w_ai = jnp.concatenate([_block_diag(lru_wa, per_block), _block_diag(lru_wi, per_block)], axis=-1)
    return (row(norm_mix), w_in.astype(BF16), row(b_gate), pool_w.astype(BF16), row(pool_scale),
            conv_w, row(conv_b), w_ai.astype(BF16), row(lru_ba), row(lru_bi), row(lru_lambda),
            w_br_pool.astype(BF16), w_br_lru.astype(BF16), w_out.astype(BF16))


def _run_mixer(x, state_pool, state_conv, state_lru, weights, *, start_pos, bblk, tt, seq0=0):
    hp = jnp.pad(state_pool, ((0, 0), (POOL_PAD - POOL_HIST, 0), (0, 0)))
    hc = jnp.pad(state_conv, ((0, 0), (CONV_PAD - CONV_HIST, 0), (0, 0)))
    hl = state_lru[:, None, :]
    x1, npool, nconv, nlru = _mixer_call(x, hp, hc, hl, weights, start_pos=start_pos, bblk=bblk, tt=tt,
                                         seq0=seq0)
    return x1, npool, nconv, nlru[:, 0, :]


def _routing_plan(topi, rank, counts, *, tm, n_tiles):
    n_exp = counts.shape[0]
    tiles_per = (counts + tm - 1) // tm
    tiles_cum = jnp.cumsum(tiles_per)
    tile_start = tiles_cum - tiles_per
    base = tile_start * tm
    experts = jnp.arange(n_exp, dtype=jnp.int32)[:, None, None]
    pos = rank + jnp.sum(jnp.where(topi[None] == experts, base[:, None, None], 0), axis=0)
    n_used = tiles_cum[-1]
    tile_ids = jnp.arange(n_tiles, dtype=jnp.int32)
    live = tile_ids < n_used
    owner = (tiles_cum[None, :] <= jnp.minimum(tile_ids, n_used - 1)[:, None])
    tile_expert = jnp.sum(owner.astype(jnp.int32), axis=1)
    is_owner = tile_expert[:, None] == jnp.arange(n_exp, dtype=jnp.int32)[None, :]
    rows_left = jnp.sum(jnp.where(is_owner, counts[None, :] - (tile_ids[:, None] - tile_start[None, :]) * tm, 0),
                        axis=1)
    tile_rows = jnp.where(live, jnp.clip(rows_left, 0, tm), 0).astype(jnp.int32)
    return pos, tile_expert, tile_rows, n_used.reshape(1).astype(jnp.int32)


def _moe_rows(x1_parts, windows, norm_ffn, w_router_t, b_router, w_gu, b_gu, w_dn, b_dn):
    d = x1_parts[0].shape[1]
    chunks = d // LANES
    n_exp = w_router_t.shape[0]
    n = sum(w.count for w in windows) * TOKEN_TILE
    xn2, topi, topw, rank, cnt = _router_call(x1_parts, windows, norm_ffn, w_router_t, b_router, tb=TOKEN_TILE)
    n_tiles = (n * TOP_K + n_exp * (EXPERT_TILE - 1)) // EXPERT_TILE
    p_rows = n_tiles * EXPERT_TILE
    pos, tile_expert, tile_rows, n_used = _routing_plan(
        topi, rank, cnt[:, 0].astype(jnp.int32), tm=EXPERT_TILE, n_tiles=n_tiles)
    w, steps = _sc_step_rows(n)
    pos_steps = pos.reshape(TOP_K, SC_WORKERS * steps, w).transpose(1, 0, 2)
    xchunks = d // (2 * LANES)
    xs = _dispatch_rows(xn2.reshape(n, xchunks, LANES), pos_steps, p_rows)
    ys = _expert_call(tile_expert, tile_rows, n_used, xs.reshape(p_rows * xchunks, LANES), w_gu, b_gu, w_dn,
                      b_dn, tm=EXPERT_TILE)
    yk = _combine_rows(ys.reshape(p_rows, xchunks, LANES), pos_steps, n)
    return yk.reshape(TOP_K, n * xchunks, LANES), topw


def kernel(x_prompt, x_sample, state_pool, state_conv, state_lru, norm_mix, w_in, b_gate, pool_w, pool_scale,
           conv_w, conv_b, lru_wa, lru_ba, lru_wi, lru_bi, lru_lambda, w_br_pool, w_br_lru, w_out, norm_ffn,
           w_router, b_router, w_gu, b_gu, w_dn, b_dn, norm_final):
    bp, tp, d = x_prompt.shape
    bs, ts, _ = x_sample.shape
    n_p, n_s = bp * tp, bs * ts
    mw = _mixer_weights(norm_mix[0], w_in[0], b_gate[0], pool_w[0], pool_scale[0], conv_w[0], conv_b[0],
                        lru_wa[0], lru_ba[0], lru_wi[0], lru_bi[0], lru_lambda[0], w_br_pool[0], w_br_lru[0],
                        w_out[0])
    w_router_t = w_router[0].T.astype(BF16)
    zeros = lambda *shape: jnp.zeros(shape, x_prompt.dtype)

    assert sum(PROMPT_GROUP_SEQS) == bp and tp % TOKEN_TILE == 0 and n_s % TOKEN_TILE == 0
    seq_tiles = tp // TOKEN_TILE
    s_tiles = n_s // TOKEN_TILE
    x1_s, pool_s, conv_s, lru_s = _run_mixer(
        x_sample, state_pool[0], state_conv[0], state_lru[0], mw, start_pos=PAST_LEN, bblk=bs, tt=ts)
    y_p, y_s = None, None
    pools, convs, lrus = [], [], []
    seq0 = 0
    for g, seqs in enumerate(PROMPT_GROUP_SEQS):
        x1_g, pool_g, conv_g, lru_g = _run_mixer(
            x_prompt, zeros(seqs, POOL_HIST, d), zeros(seqs, CONV_HIST, d), zeros(seqs, d), mw,
            start_pos=0, bblk=1, tt=MIXER_TILE, seq0=seq0)
        pools.append(pool_g)
        convs.append(conv_g)
        lrus.append(lru_g)
        last = g == len(PROMPT_GROUP_SEQS) - 1
        g_tiles = seqs * seq_tiles
        parts, windows = [x1_g.reshape(seqs * tp, d)], [_Window(0, g_tiles)]
        out_shapes, out_windows, out_init = [(n_p, d)], [_Window(seq0 * seq_tiles, g_tiles)], [y_p]
        seq0 += seqs
        if last:
            parts.append(x1_s.reshape(n_s, d))
            windows.append(_Window(0, s_tiles))
            out_shapes.append((n_s, d))
            out_windows.append(_Window(0, s_tiles))
            out_init.append(None)
        yk, topw = _moe_rows(parts, windows, norm_ffn[0], w_router_t, b_router[0], w_gu[0], b_gu[0], w_dn[0],
                             b_dn[0])
        outs = _final_call(parts, windows, yk, topw.T, norm_final, out_shapes, out_windows, out_init,
                           tb=TOKEN_TILE)
        y_p = outs[0]
        if last:
            y_s = outs[1]

    cat = lambda xs: jnp.concatenate(xs, axis=0)[None]
    return (y_p.reshape(bp, tp, d), y_s.reshape(bs, ts, d), cat(pools), cat(convs), cat(lrus),
            pool_s[None], conv_s[None], lru_s[None])
```

```python
import functools
from typing import NamedTuple

import jax
import jax.numpy as jnp
from jax import lax
from jax.experimental import pallas as pl
from jax.experimental.pallas import tpu as pltpu
from jax.experimental.pallas import tpu_sc as plsc

BF16 = jnp.bfloat16
F32 = jnp.float32

POOL_WINDOWS = (2, 4, 8, 16)
POOL_HIST = max(POOL_WINDOWS) - 1
CONV_WIDTH = 4
CONV_HIST = CONV_WIDTH - 1
LRU_HEADS = 16
LRU_C = 8.0
LRU_SEGMENTS = 8
N_EXPERTS = 32
TOP_K = 4
SWIGLU_LIMIT = 7.0
SWIGLU_ALPHA = 1.702
NORM_EPS = 1e-6

SUBLANES = 8
LANES = 128
MXU_WIDTH = 256
POOL_PAD = 16
CONV_PAD = 8
VMEM_LIMIT_BYTES = 56 * 1024 * 1024

PAST_LEN = 1024
MIXER_TILE = 256
TOKEN_TILE = 512
EXPERT_TILE = 512
PROMPT_GROUP_SEQS = (12, 4)

SC_CORES = 2
SC_SUBCORES = 16
SC_WORKERS = SC_CORES * SC_SUBCORES
SC_MAX_STEP_ROWS = 64


def _rms_norm(x, g):
    ms = jnp.mean(x * x, axis=-1, keepdims=True)
    return (x * lax.rsqrt(ms + NORM_EPS)) * g


def _dot(a, b):
    return jnp.dot(a, b, preferred_element_type=F32)


def _softplus(x):
    return jnp.maximum(x, 0.0) + jnp.log1p(jnp.exp(-jnp.abs(x)))


def _gelu_tanh(x):
    c = 0.7978845608028654
    return 0.5 * x * (1.0 + jnp.tanh(c * (x + 0.044715 * (x * x * x))))


def _lru_scan(a_ref, b_ref, h_ref, h0, frames, chunks):
    tile = lambda ref, t: ref[t * chunks:(t + 1) * chunks, :]
    seg = frames // LRU_SEGMENTS
    assert seg * LRU_SEGMENTS == frames
    prods = [tile(a_ref, s * seg) for s in range(LRU_SEGMENTS)]
    sums = [tile(b_ref, s * seg) for s in range(LRU_SEGMENTS)]
    for t in range(1, seg):
        for s in range(LRU_SEGMENTS):
            a_t = tile(a_ref, s * seg + t)
            sums[s] = a_t * sums[s] + tile(b_ref, s * seg + t)
            prods[s] = a_t * prods[s]
            b_ref[(s * seg + t) * chunks:(s * seg + t + 1) * chunks, :] = sums[s]
            a_ref[(s * seg + t) * chunks:(s * seg + t + 1) * chunks, :] = prods[s]
    h_in = [h0]
    for s in range(LRU_SEGMENTS):
        h_in.append(prods[s] * h_in[s] + sums[s])
    for s in range(LRU_SEGMENTS):
        for t in range(seg):
            f = s * seg + t
            h_ref[f * chunks:(f + 1) * chunks, :] = tile(a_ref, f) * h_in[s] + tile(b_ref, f)
    return h_in[LRU_SEGMENTS]


def _mixer_kernel(x_ref, hp_ref, hc_ref, hl_ref, nmix_ref, win_ref, bgate_ref, poolw_ref, pscale_ref,
                  convw_ref, convb_ref, wai_ref, ba_ref, bi_ref, lam_ref, wbrp_ref, wbrl_ref, wout_ref,
                  x1_ref, npool_ref, nconv_ref, nlru_ref,
                  pool_ext, conv_ext, h_carry, frames_a, frames_b, frames_h, *, start_pos, bblk, tt, d):
    j = pl.program_id(1)

    @pl.when(j == 0)
    def _():
        pool_ext[:, 0:POOL_PAD, :] = hp_ref[...]
        conv_ext[:, 0:CONV_PAD * (d // LANES), :] = hc_ref[...]
        h_carry[...] = hl_ref[...]

    rows = bblk * tt
    chunks = d // LANES
    cat_rows = lambda parts: jnp.concatenate(parts, axis=0) if len(parts) > 1 else parts[0]
    x = x_ref[...].reshape(rows, d)
    xnb = _rms_norm(x, nmix_ref[...]).astype(BF16)

    u_x = _dot(xnb, win_ref[:, d:2 * d])
    xc_rows = []
    for b in range(bblk):
        _store_rows(conv_ext.at[b, pl.ds(CONV_PAD * chunks, tt * chunks)], u_x[b * tt:(b + 1) * tt, :], tt, chunks)
        ce = conv_ext[b].reshape(CONV_PAD + tt, chunks, LANES)
        y = convb_ref[...] + ce[CONV_PAD:CONV_PAD + tt] * convw_ref[CONV_WIDTH - 1]
        for k in range(1, CONV_WIDTH):
            y = y + ce[CONV_PAD - k:CONV_PAD - k + tt] * convw_ref[CONV_WIDTH - 1 - k]
        frames_h[...] = y.reshape(tt * chunks, LANES)
        xc_rows.append(_load_rows(frames_h, tt, chunks))
        nconv_ref[b] = conv_ext[b, (CONV_PAD + tt - CONV_HIST) * chunks:(CONV_PAD + tt) * chunks, :]
        conv_ext[b, 0:CONV_PAD * chunks, :] = conv_ext[b, tt * chunks:(tt + CONV_PAD) * chunks, :]
    xc = cat_rows(xc_rows)

    n_chunks = d // MXU_WIDTH
    pre_a, pre_i = [], []
    for c in range(n_chunks):
        ai = _dot(xc[:, c * MXU_WIDTH:(c + 1) * MXU_WIDTH].astype(BF16), wai_ref[c])
        pre_a.append(ai[:, 0:MXU_WIDTH])
        pre_i.append(ai[:, MXU_WIDTH:2 * MXU_WIDTH])

    u_pool = _dot(xnb, win_ref[:, 0:d])
    pos1 = lax.broadcasted_iota(jnp.int32, (tt, 1), 0) + (start_pos + 1) + j * tt
    gd = d // len(POOL_WINDOWS)
    y_pool_rows = []
    for b in range(bblk):
        pool_ext[b, POOL_PAD:POOL_PAD + tt, :] = u_pool[b * tt:(b + 1) * tt, :]
        parts = []
        for g, w in enumerate(POOL_WINDOWS):
            sl = slice(g * gd, (g + 1) * gd)
            s = pool_ext[b, :, sl]
            width = 1
            while width < w:
                s = s + pltpu.roll(s, width, axis=0)
                width *= 2
            cur = s[POOL_PAD:POOL_PAD + tt, :]
            inv_cnt = 1.0 / jnp.minimum(pos1, w).astype(F32)
            pooled = cur * inv_cnt - u_pool[b * tt:(b + 1) * tt, sl]
            mixed = _dot(pooled.astype(BF16), poolw_ref[g])
            parts.append(mixed * pscale_ref[:, sl])
        y_pool_rows.append(jnp.concatenate(parts, axis=1))
        npool_ref[b] = pool_ext[b, POOL_PAD + tt - POOL_HIST:POOL_PAD + tt, :]
        pool_ext[b, 0:POOL_PAD, :] = pool_ext[b, tt:tt + POOL_PAD, :]
    y_pool = cat_rows(y_pool_rows)

    r_gate = jax.nn.sigmoid(jnp.concatenate(pre_a, axis=1) + ba_ref[...])
    i_gate = jax.nn.sigmoid(jnp.concatenate(pre_i, axis=1) + bi_ref[...])
    log_a = (-LRU_C) * r_gate * _softplus(-lam_ref[...])
    a = jnp.exp(log_a)
    mult = jnp.sqrt(-jnp.tanh(log_a) * (a * a + 1.0))
    bb = mult * (i_gate * xc)
    gl_pool = _dot(xnb, win_ref[:, 3 * d:4 * d])
    u_gate = _dot(xnb, win_ref[:, 2 * d:3 * d])
    h_rows = []
    for b in range(bblk):
        _store_rows(frames_a, a[b * tt:(b + 1) * tt, :], tt, chunks)
        _store_rows(frames_b, bb[b * tt:(b + 1) * tt, :], tt, chunks)
        h_last = _lru_scan(frames_a, frames_b, frames_h, h_carry[b], tt, chunks)
        h_carry[b] = h_last
        nlru_ref[b] = h_last
        h_rows.append(_load_rows(frames_h, tt, chunks))
    h = cat_rows(h_rows)
    br_pool = _dot(y_pool.astype(BF16), wbrp_ref[...])
    gl_lru = _dot(xnb, win_ref[:, 4 * d:5 * d])

    y_lru = h * _gelu_tanh(u_gate)
    br_lru = _dot(y_lru.astype(BF16), wbrl_ref[...])
    g_pool = jax.nn.sigmoid(gl_pool + bgate_ref[:, 0:d])
    g_lru = jax.nn.sigmoid(gl_lru + bgate_ref[:, d:2 * d])
    acc = g_pool * br_pool + g_lru * br_lru
    out = _dot(acc.astype(BF16), wout_ref[...])
    x1_ref[...] = (x + out).reshape(bblk, tt, d)


def _const_spec(shape):
    nd = len(shape)
    return pl.BlockSpec(shape, lambda *_: (0,) * nd, pipeline_mode=pl.Buffered(1))


def _mixer_call(x, hist_pool, hist_conv, hist_lru, weights, *, start_pos, bblk, tt, seq0=0):
    _, t_len, d = x.shape
    bsz = hist_pool.shape[0]
    assert seq0 % bblk == 0
    grid = (bsz // bblk, t_len // tt)
    kern = functools.partial(_mixer_kernel, start_pos=start_pos, bblk=bblk, tt=tt, d=d)
    chunks = d // LANES
    seq_spec = lambda rows: pl.BlockSpec((bblk, rows, d), lambda b, j: (b, 0, 0))
    frame_spec = lambda frames: pl.BlockSpec((bblk, frames * chunks, LANES), lambda b, j: (b, 0, 0))
    in_specs = [
        pl.BlockSpec((bblk, tt, d), lambda b, j: (b + seq0 // bblk, j, 0)),
        seq_spec(POOL_PAD), frame_spec(CONV_PAD), frame_spec(1),
    ] + [_const_spec(w.shape) for w in weights]
    out_shape = (
        jax.ShapeDtypeStruct((bsz, t_len, d), F32),
        jax.ShapeDtypeStruct((bsz, POOL_HIST, d), F32),
        jax.ShapeDtypeStruct((bsz, CONV_HIST * chunks, LANES), F32),
        jax.ShapeDtypeStruct((bsz, chunks, LANES), F32),
    )
    out_specs = (
        pl.BlockSpec((bblk, tt, d), lambda b, j: (b, j, 0)),
        seq_spec(POOL_HIST), frame_spec(CONV_HIST), frame_spec(1),
    )
    return pl.pallas_call(
        kern,
        grid=grid,
        in_specs=in_specs,
        out_specs=out_specs,
        out_shape=out_shape,
        scratch_shapes=[
            pltpu.VMEM((bblk, POOL_PAD + tt, d), F32),
            pltpu.VMEM((bblk, (CONV_PAD + tt) * chunks, LANES), F32),
            pltpu.VMEM((bblk, chunks, LANES), F32),
            pltpu.VMEM((tt * chunks, LANES), F32),
            pltpu.VMEM((tt * chunks, LANES), F32),
            pltpu.VMEM((tt * chunks, LANES), F32),
        ],
        compiler_params=pltpu.CompilerParams(
            dimension_semantics=("arbitrary", "arbitrary"), vmem_limit_bytes=VMEM_LIMIT_BYTES),
        name="mixer",
    )(x, hist_pool, hist_conv, hist_lru, *weights)


def _load_rows(ref, rows, chunks):
    return jnp.concatenate([ref[pl.ds(s, rows, stride=chunks), :] for s in range(chunks)], axis=1)


def _store_rows(ref, val, rows, chunks):
    for s in range(chunks):
        ref[pl.ds(s, rows, stride=chunks), :] = val[:, s * LANES:(s + 1) * LANES]


def _pack_halves(x):
    half = x.shape[1] // 2
    bits = lax.bitcast_convert_type(x, jnp.uint32)
    return (bits[:, :half] >> 16) | (bits[:, half:] & jnp.uint32(0xFFFF0000))


def _unpack_halves(p):
    lo = lax.bitcast_convert_type(p << 16, F32)
    hi = lax.bitcast_convert_type(p & jnp.uint32(0xFFFF0000), F32)
    return jnp.concatenate([lo, hi], axis=1)


class _Window(NamedTuple):
    first: int
    count: int


def _window_specs(windows, tb, d):
    specs, start = [], 0
    for win in windows:
        specs.append(pl.BlockSpec(
            (tb, d), lambda i, win=win, start=start: (win.first + jnp.clip(i - start, 0, win.count - 1), 0)))
        start += win.count
    return specs


def _window_load(i, windows, refs):
    val, start = refs[-1][...], sum(w.count for w in windows[:-1])
    for win, ref in zip(reversed(windows[:-1]), reversed(refs[:-1])):
        val = jnp.where(i < start, ref[...], val)
        start -= win.count
    return val


def _window_store(i, windows, refs, val):
    start = 0
    for win, ref in zip(windows, refs):
        @pl.when((i >= start) & (i < start + win.count))
        def _(ref=ref):
            ref[...] = val
        start += win.count


def _router_kernel(*refs, tb, d, windows):
    x1_refs = refs[:len(windows)]
    (nffn_ref, wrt_ref, br_ref, tri_ref, xn_ref, topi_ref, topw_ref, rank_ref, cnt_ref, carry) = refs[len(windows):]
    i = pl.program_id(0)

    @pl.when(i == 0)
    def _():
        carry[...] = jnp.zeros_like(carry)

    xnb = _rms_norm(_window_load(i, windows, x1_refs), nffn_ref[...]).astype(BF16)
    xn_ref_chunks = d // (2 * LANES)
    _store_rows(xn_ref, _pack_halves(xnb.astype(F32)), tb, xn_ref_chunks)
    logits = lax.dot_general(wrt_ref[...], xnb, (((1,), (1,)), ((), ())),
                             preferred_element_type=F32) + br_ref[...]
    n_exp = logits.shape[0]
    iota_e = lax.broadcasted_iota(jnp.int32, (n_exp, tb), 0)
    work = logits
    vals, idxs, hots = [], [], []
    for _ in range(TOP_K):
        m = jnp.max(work, axis=0, keepdims=True)
        idx = jnp.min(jnp.where(work == m, iota_e, n_exp), axis=0, keepdims=True)
        hot = iota_e == idx
        vals.append(m)
        idxs.append(idx)
        hots.append(hot)
        work = jnp.where(hot, -jnp.inf, work)
    exps = [jnp.exp(v - vals[0]) for v in vals]
    denom = exps[0] + exps[1] + exps[2] + exps[3]
    topi_ref[...] = jnp.concatenate(idxs, axis=0)
    topw_ref[...] = jnp.concatenate([e / denom for e in exps], axis=0)

    sel = sum(jnp.where(h, 1.0, 0.0) for h in hots)
    before = _dot(sel.astype(BF16), tri_ref[...]) + carry[:, 0:1]
    ranks = [jnp.sum(jnp.where(h, before, 0.0), axis=0, keepdims=True) for h in hots]
    rank_ref[...] = jnp.concatenate(ranks, axis=0).astype(jnp.int32)
    carry[...] = carry[...] + jnp.sum(sel, axis=1, keepdims=True)
    cnt_ref[...] = carry[...]


def _router_call(x1_parts, windows, norm_ffn, w_router_t, b_router, *, tb):
    d = x1_parts[0].shape[1]
    n_tiles = sum(w.count for w in windows)
    n = n_tiles * tb
    chunks = d // (2 * LANES)
    n_exp = w_router_t.shape[0]
    tri = jnp.triu(jnp.ones((tb, tb), BF16), k=1)
    tok = lambda rows, dt: (pl.BlockSpec((rows, tb), lambda i: (0, i)), jax.ShapeDtypeStruct((rows, n), dt))
    (topi_spec, topi_shape), (topw_spec, topw_shape), (rank_spec, rank_shape) = (
        tok(TOP_K, jnp.int32), tok(TOP_K, F32), tok(TOP_K, jnp.int32))
    return pl.pallas_call(
        functools.partial(_router_kernel, tb=tb, d=d, windows=tuple(windows)),
        grid=(n_tiles,),
        in_specs=[
            *_window_specs(windows, tb, d),
            _const_spec((1, d)), _const_spec((n_exp, d)), _const_spec((n_exp, 1)), _const_spec((tb, tb)),
        ],
        out_specs=(
            pl.BlockSpec((tb * chunks, LANES), lambda i: (i, 0)),
            topi_spec, topw_spec, rank_spec,
            pl.BlockSpec((n_exp, LANES), lambda i: (0, 0)),
        ),
        out_shape=(
            jax.ShapeDtypeStruct((n * chunks, LANES), jnp.uint32),
            topi_shape, topw_shape, rank_shape,
            jax.ShapeDtypeStruct((n_exp, LANES), F32),
        ),
        scratch_shapes=[pltpu.VMEM((n_exp, LANES), F32)],
        compiler_params=pltpu.CompilerParams(
            dimension_semantics=("arbitrary",), vmem_limit_bytes=VMEM_LIMIT_BYTES),
        name="router",
    )(*x1_parts, norm_ffn.reshape(1, d), w_router_t, b_router.reshape(n_exp, 1), tri)


def _expert_kernel(te_ref, tv_ref, nu_ref, xs_ref, wgu_ref, bgu_ref, wdn_ref, bdn_ref, ys_ref, wgu_bf, wdn_bf,
                   *, tm, d, dff):
    i = pl.program_id(0)
    prev = te_ref[jnp.maximum(i - 1, 0)]

    @pl.when((i == 0) | (te_ref[i] != prev))
    def _():
        wgu_bf[...] = wgu_ref[0].astype(BF16)
        wdn_bf[...] = wdn_ref[0].astype(BF16)

    def ffn(rows):
        x = _unpack_halves(_load_rows(xs_ref, rows, d // (2 * LANES))).astype(BF16)
        gu = _dot(x, wgu_bf[...]) + bgu_ref[0]
        gate = jnp.minimum(gu[:, :dff], SWIGLU_LIMIT)
        up = jnp.clip(gu[:, dff:], -SWIGLU_LIMIT, SWIGLU_LIMIT)
        hid = (up + 1.0) * gate * jax.nn.sigmoid(SWIGLU_ALPHA * gate)
        y = _dot(hid.astype(BF16), wdn_bf[...]) + bdn_ref[0]
        _store_rows(ys_ref, _pack_halves(y.astype(BF16).astype(F32)), rows, d // (2 * LANES))

    valid = tv_ref[i]
    pl.when(valid > tm // 2)(lambda: ffn(tm))
    pl.when((valid > 0) & (valid <= tm // 2))(lambda: ffn(tm // 2))


def _expert_call(tile_expert, tile_rows, n_used, xs, w_gu, b_gu, w_dn, b_dn, *, tm):
    n_exp, d, dff2 = w_gu.shape
    dff = dff2 // 2
    xchunks = d // (2 * LANES)
    p_rows = xs.shape[0] // xchunks
    row_blk = lambda i, te, tv, nu: (jnp.minimum(i, nu[0] - 1), 0)
    exp_blk = lambda i, te, tv, nu: (te[i], 0, 0)
    return pl.pallas_call(
        functools.partial(_expert_kernel, tm=tm, d=d, dff=dff),
        grid_spec=pltpu.PrefetchScalarGridSpec(
            num_scalar_prefetch=3,
            grid=(p_rows // tm,),
            in_specs=[
                pl.BlockSpec((tm * xchunks, LANES), row_blk),
                pl.BlockSpec((1, d, dff2), exp_blk),
                pl.BlockSpec((1, 1, dff2), exp_blk),
                pl.BlockSpec((1, dff, d), exp_blk),
                pl.BlockSpec((1, 1, d), exp_blk),
            ],
            out_specs=pl.BlockSpec((tm * xchunks, LANES), row_blk),
            scratch_shapes=[pltpu.VMEM((d, dff2), BF16), pltpu.VMEM((dff, d), BF16)],
        ),
        out_shape=jax.ShapeDtypeStruct((p_rows * xchunks, LANES), jnp.uint32),
        compiler_params=pltpu.CompilerParams(
            dimension_semantics=("arbitrary",), vmem_limit_bytes=VMEM_LIMIT_BYTES),
        name="experts",
    )(tile_expert, tile_rows, n_used, xs, w_gu, b_gu.reshape(n_exp, 1, dff2), w_dn, b_dn.reshape(n_exp, 1, d))


def _final_kernel(*refs, tb, d, in_windows, out_windows, n_alias):
    n_in, n_out = len(in_windows), len(out_windows)
    x1_refs = refs[:n_in]
    yk_ref, w_ref, nfin_ref = refs[n_in:n_in + 3]
    out_refs = refs[n_in + 3 + n_alias:n_in + 3 + n_alias + n_out]
    i = pl.program_id(0)
    w = w_ref[...]
    chunks = d // (2 * LANES)
    moe = _unpack_halves(_load_rows(yk_ref.at[0], tb, chunks)) * w[:, 0:1]
    for k in range(1, TOP_K):
        moe = moe + _unpack_halves(_load_rows(yk_ref.at[k], tb, chunks)) * w[:, k:k + 1]
    out = _rms_norm(_window_load(i, in_windows, x1_refs) + moe, nfin_ref[...])
    _window_store(i, out_windows, out_refs, out)


def _final_call(x1_parts, in_windows, yk, topw_t, norm_final, out_shapes, out_windows, out_init, *, tb):
    d = x1_parts[0].shape[1]
    n_tiles = sum(w.count for w in in_windows)
    chunks = d // (2 * LANES)
    alias_in = [a for a in out_init if a is not None]
    n_fixed = len(x1_parts) + 3
    aliases, j = {}, 0
    for k, a in enumerate(out_init):
        if a is not None:
            aliases[n_fixed + j] = k
            j += 1
    return pl.pallas_call(
        functools.partial(_final_kernel, tb=tb, d=d, in_windows=tuple(in_windows),
                          out_windows=tuple(out_windows), n_alias=len(alias_in)),
        grid=(n_tiles,),
        in_specs=[
            *_window_specs(in_windows, tb, d),
            pl.BlockSpec((TOP_K, tb * chunks, LANES), lambda i: (0, i, 0)),
            pl.BlockSpec((tb, TOP_K), lambda i: (i, 0)),
            _const_spec((1, d)),
            *[pl.BlockSpec(memory_space=pl.ANY) for _ in alias_in],
        ],
        out_specs=_window_specs(out_windows, tb, d),
        out_shape=[jax.ShapeDtypeStruct(s, F32) for s in out_shapes],
        input_output_aliases=aliases,
        compiler_params=pltpu.CompilerParams(
            dimension_semantics=("arbitrary",), vmem_limit_bytes=VMEM_LIMIT_BYTES),
        name="final",
    )(*x1_parts, yk, topw_t, norm_final.reshape(1, d), *alias_in)


def _sc_step_rows(n):
    per_worker = n // SC_WORKERS
    assert per_worker * SC_WORKERS == n
    w = max(c for c in range(SUBLANES, SC_MAX_STEP_ROWS + 1, SUBLANES) if per_worker % c == 0)
    return w, per_worker // w


def _sc_mesh():
    return plsc.VectorSubcoreMesh(core_axis_name="c", subcore_axis_name="s",
                                  num_cores=SC_CORES, num_subcores=SC_SUBCORES)


def _dispatch_rows(x_tiles, pos_steps, p_rows):
    n, chunks, _ = x_tiles.shape
    w, steps = _sc_step_rows(n)

    @functools.partial(
        pl.kernel, mesh=_sc_mesh(),
        out_type=jax.ShapeDtypeStruct((p_rows, chunks, LANES), x_tiles.dtype),
        scratch_types=[pltpu.VMEM((w, chunks, LANES), x_tiles.dtype), pltpu.VMEM((TOP_K, w), jnp.int32)],
        name="dispatch",
    )
    def run(x_hbm, pos_hbm, out_hbm, buf, idx):
        wid = lax.axis_index("s") * SC_CORES + lax.axis_index("c")

        @pl.loop(0, steps)
        def _(i):
            step = wid * steps + i
            pltpu.sync_copy(x_hbm.at[pl.ds(pl.multiple_of(step * w, SUBLANES), w)], buf)
            pltpu.sync_copy(pos_hbm.at[step], idx)
            for k in range(TOP_K):
                pltpu.sync_copy(buf, out_hbm.at[idx.at[k]])

    return run(x_tiles, pos_steps)


def _combine_rows(y_tiles, pos_steps, n):
    _, chunks, _ = y_tiles.shape
    w, steps = _sc_step_rows(n)

    @functools.partial(
        pl.kernel, mesh=_sc_mesh(),
        out_type=jax.ShapeDtypeStruct((TOP_K, n, chunks, LANES), y_tiles.dtype),
        scratch_types=[pltpu.VMEM((w, chunks, LANES), y_tiles.dtype), pltpu.VMEM((TOP_K, w), jnp.int32)],
        name="combine",
    )
    def run(y_hbm, pos_hbm, out_hbm, buf, idx):
        wid = lax.axis_index("s") * SC_CORES + lax.axis_index("c")

        @pl.loop(0, steps)
        def _(i):
            step = wid * steps + i
            pltpu.sync_copy(pos_hbm.at[step], idx)
            for k in range(TOP_K):
                pltpu.sync_copy(y_hbm.at[idx.at[k]], buf)
                pltpu.sync_copy(buf, out_hbm.at[k, pl.ds(pl.multiple_of(step * w, SUBLANES), w)])

    return run(y_tiles, pos_steps)


def _block_diag(w, per_block):
    heads, n, _ = w.shape
    w4 = w.reshape(heads // per_block, per_block, n, n)
    eye = jnp.eye(per_block, dtype=w.dtype)
    bd = jnp.einsum('chij,hg->chigj', w4, eye)
    return bd.reshape(heads // per_block, per_block * n, per_block * n)


def _mixer_weights(norm_mix, w_in, b_gate, pool_w, pool_scale, conv_w, conv_b, lru_wa, lru_ba, lru_wi,
                   lru_bi, lru_lambda, w_br_pool, w_br_lru, w_out):
    row = lambda v: v.reshape(1, -1)
    d = norm_mix.shape[0]
    head_dim = lru_wa.shape[-1]
    per_block = MXU_WIDTH // head_dim
    w_ai = jnp.concatenate([_block_diag(lru_wa, per_block), _block_diag(lru_wi, per_block)], axis=-1)
    return (row(norm_mix), w_in.astype(BF16), row(b_gate), pool_w.astype(BF16), row(pool_scale),
            conv_w.reshape(-1, d // LANES, LANES), conv_b.reshape(d // LANES, LANES), w_ai.astype(BF16),
            row(lru_ba), row(lru_bi), row(lru_lambda),
            w_br_pool.astype(BF16), w_br_lru.astype(BF16), w_out.astype(BF16))


def _run_mixer(x, state_pool, state_conv, state_lru, weights, *, start_pos, bblk, tt, seq0=0):
    bsz, d = state_lru.shape
    chunks = d // LANES
    hp = jnp.pad(state_pool, ((0, 0), (POOL_PAD - POOL_HIST, 0), (0, 0)))
    hc = jnp.pad(state_conv, ((0, 0), (CONV_PAD - CONV_HIST, 0), (0, 0))).reshape(bsz, CONV_PAD * chunks, LANES)
    hl = state_lru.reshape(bsz, chunks, LANES)
    x1, npool, nconv, nlru = _mixer_call(x, hp, hc, hl, weights, start_pos=start_pos, bblk=bblk, tt=tt,
                                         seq0=seq0)
    return x1, npool, nconv.reshape(bsz, CONV_HIST, d), nlru.reshape(bsz, d)


def _routing_plan(topi, rank, counts, *, tm, n_tiles):
    n_exp = counts.shape[0]
    tiles_per = (counts + tm - 1) // tm
    tiles_cum = jnp.cumsum(tiles_per)
    tile_start = tiles_cum - tiles_per
    base = tile_start * tm
    experts = jnp.arange(n_exp, dtype=jnp.int32)[:, None, None]
    pos = rank + jnp.sum(jnp.where(topi[None] == experts, base[:, None, None], 0), axis=0)
    n_used = tiles_cum[-1]
    tile_ids = jnp.arange(n_tiles, dtype=jnp.int32)
    live = tile_ids < n_used
    owner = (tiles_cum[None, :] <= jnp.minimum(tile_ids, n_used - 1)[:, None])
    tile_expert = jnp.sum(owner.astype(jnp.int32), axis=1)
    is_owner = tile_expert[:, None] == jnp.arange(n_exp, dtype=jnp.int32)[None, :]
    rows_left = jnp.sum(jnp.where(is_owner, counts[None, :] - (tile_ids[:, None] - tile_start[None, :]) * tm, 0),
                        axis=1)
    tile_rows = jnp.where(live, jnp.clip(rows_left, 0, tm), 0).astype(jnp.int32)
    return pos, tile_expert, tile_rows, n_used.reshape(1).astype(jnp.int32)


def _moe_rows(x1_parts, windows, norm_ffn, w_router_t, b_router, w_gu, b_gu, w_dn, b_dn):
    d = x1_parts[0].shape[1]
    chunks = d // LANES
    n_exp = w_router_t.shape[0]
    n = sum(w.count for w in windows) * TOKEN_TILE
    xn2, topi, topw, rank, cnt = _router_call(x1_parts, windows, norm_ffn, w_router_t, b_router, tb=TOKEN_TILE)
    n_tiles = (n * TOP_K + n_exp * (EXPERT_TILE - 1)) // EXPERT_TILE
    p_rows = n_tiles * EXPERT_TILE
    pos, tile_expert, tile_rows, n_used = _routing_plan(
        topi, rank, cnt[:, 0].astype(jnp.int32), tm=EXPERT_TILE, n_tiles=n_tiles)
    w, steps = _sc_step_rows(n)
    pos_steps = pos.reshape(TOP_K, SC_WORKERS * steps, w).transpose(1, 0, 2)
    xchunks = d // (2 * LANES)
    xs = _dispatch_rows(xn2.reshape(n, xchunks, LANES), pos_steps, p_rows)
    ys = _expert_call(tile_expert, tile_rows, n_used, xs.reshape(p_rows * xchunks, LANES), w_gu, b_gu, w_dn,
                      b_dn, tm=EXPERT_TILE)
    yk = _combine_rows(ys.reshape(p_rows, xchunks, LANES), pos_steps, n)
    return yk.reshape(TOP_K, n * xchunks, LANES), topw


def kernel(x_prompt, x_sample, state_pool, state_conv, state_lru, norm_mix, w_in, b_gate, pool_w, pool_scale,
           conv_w, conv_b, lru_wa, lru_ba, lru_wi, lru_bi, lru_lambda, w_br_pool, w_br_lru, w_out, norm_ffn,
           w_router, b_router, w_gu, b_gu, w_dn, b_dn, norm_final):
    bp, tp, d = x_prompt.shape
    bs, ts, _ = x_sample.shape
    n_p, n_s = bp * tp, bs * ts
    mw = _mixer_weights(norm_mix[0], w_in[0], b_gate[0], pool_w[0], pool_scale[0], conv_w[0], conv_b[0],
                        lru_wa[0], lru_ba[0], lru_wi[0], lru_bi[0], lru_lambda[0], w_br_pool[0], w_br_lru[0],
                        w_out[0])
    w_router_t = w_router[0].T.astype(BF16)
    zeros = lambda *shape: jnp.zeros(shape, x_prompt.dtype)

    assert sum(PROMPT_GROUP_SEQS) == bp and tp % TOKEN_TILE == 0 and n_s % TOKEN_TILE == 0
    seq_tiles = tp // TOKEN_TILE
    s_tiles = n_s // TOKEN_TILE
    x1_s, pool_s, conv_s, lru_s = _run_mixer(
        x_sample, state_pool[0], state_conv[0], state_lru[0], mw, start_pos=PAST_LEN, bblk=bs, tt=ts)
    y_p, y_s = None, None
    pools, convs, lrus = [], [], []
    seq0 = 0
    for g, seqs in enumerate(PROMPT_GROUP_SEQS):
        x1_g, pool_g, conv_g, lru_g = _run_mixer(
            x_prompt, zeros(seqs, POOL_HIST, d), zeros(seqs, CONV_HIST, d), zeros(seqs, d), mw,
            start_pos=0, bblk=1, tt=MIXER_TILE, seq0=seq0)
        pools.append(pool_g)
        convs.append(conv_g)
        lrus.append(lru_g)
        last = g == len(PROMPT_GROUP_SEQS) - 1
        g_tiles = seqs * seq_tiles
        parts, windows = [x1_g.reshape(seqs * tp, d)], [_Window(0, g_tiles)]
        out_shapes, out_windows, out_init = [(n_p, d)], [_Window(seq0 * seq_tiles, g_tiles)], [y_p]
        seq0 += seqs
        if last:
            parts.append(x1_s.reshape(n_s, d))
            windows.append(_Window(0, s_tiles))
            out_shapes.append((n_s, d))
            out_windows.append(_Window(0, s_tiles))
            out_init.append(None)
        yk, topw = _moe_rows(parts, windows, norm_ffn[0], w_router_t, b_router[0], w_gu[0], b_gu[0], w_dn[0],
                             b_dn[0])
        outs = _final_call(parts, windows, yk, topw.T, norm_final, out_shapes, out_windows, out_init,
                           tb=TOKEN_TILE)
        y_p = outs[0]
        if last:
            y_s = outs[1]

    cat = lambda xs: jnp.concatenate(xs, axis=0)[None]
    return (y_p.reshape(bp, tp, d), y_s.reshape(bs, ts, d), cat(pools), cat(convs), cat(lrus),
            pool_s[None], conv_s[None], lru_s[None])
```

```python
import functools
from typing import NamedTuple

import jax
import jax.numpy as jnp
from jax import lax
from jax.experimental import pallas as pl
from jax.experimental.pallas import tpu as pltpu
from jax.experimental.pallas import tpu_sc as plsc

BF16 = jnp.bfloat16
F32 = jnp.float32

POOL_WINDOWS = (2, 4, 8, 16)
POOL_HIST = max(POOL_WINDOWS) - 1
CONV_WIDTH = 4
CONV_HIST = CONV_WIDTH - 1
LRU_HEADS = 16
LRU_C = 8.0
LRU_SEGMENTS = 8
N_EXPERTS = 32
TOP_K = 4
SWIGLU_LIMIT = 7.0
SWIGLU_ALPHA = 1.702
NORM_EPS = 1e-6

SUBLANES = 8
LANES = 128
MXU_WIDTH = 256
POOL_PAD = 16
CONV_PAD = 8
VMEM_LIMIT_BYTES = 56 * 1024 * 1024

PAST_LEN = 1024
MIXER_TILE = 256
TOKEN_TILE = 512
EXPERT_TILE = 512
PROMPT_GROUP_SEQS = (12, 4)

SC_CORES = 2
SC_SUBCORES = 16
SC_WORKERS = SC_CORES * SC_SUBCORES
SC_MAX_STEP_ROWS = 64


def _rms_norm(x, g):
    ms = jnp.mean(x * x, axis=-1, keepdims=True)
    return (x * lax.rsqrt(ms + NORM_EPS)) * g


def _dot(a, b):
    return jnp.dot(a, b, preferred_element_type=F32)


def _softplus(x):
    return jnp.maximum(x, 0.0) + jnp.log1p(jnp.exp(-jnp.abs(x)))


def _gelu_tanh(x):
    c = 0.7978845608028654
    return 0.5 * x * (1.0 + jnp.tanh(c * (x + 0.044715 * (x * x * x))))


def _lru_scan(a_ref, b_ref, h_ref, h0, frames, chunks):
    tile = lambda ref, t: ref[t * chunks:(t + 1) * chunks, :]
    seg = frames // LRU_SEGMENTS
    assert seg * LRU_SEGMENTS == frames
    prods = [tile(a_ref, s * seg) for s in range(LRU_SEGMENTS)]
    sums = [tile(b_ref, s * seg) for s in range(LRU_SEGMENTS)]
    for t in range(1, seg):
        for s in range(LRU_SEGMENTS):
            a_t = tile(a_ref, s * seg + t)
            sums[s] = a_t * sums[s] + tile(b_ref, s * seg + t)
            prods[s] = a_t * prods[s]
            b_ref[(s * seg + t) * chunks:(s * seg + t + 1) * chunks, :] = sums[s]
            a_ref[(s * seg + t) * chunks:(s * seg + t + 1) * chunks, :] = prods[s]
    h_in = [h0]
    for s in range(LRU_SEGMENTS):
        h_in.append(prods[s] * h_in[s] + sums[s])
    for s in range(LRU_SEGMENTS):
        for t in range(seg):
            f = s * seg + t
            h_ref[f * chunks:(f + 1) * chunks, :] = tile(a_ref, f) * h_in[s] + tile(b_ref, f)
    return h_in[LRU_SEGMENTS]


def _mixer_kernel(x_ref, hp_ref, hc_ref, hl_ref, nmix_ref, win_ref, bgate_ref, poolw_ref, pscale_ref,
                  convw_ref, convb_ref, wai_ref, ba_ref, bi_ref, lam_ref, wbrp_ref, wbrl_ref, wout_ref,
                  x1_ref, npool_ref, nconv_ref, nlru_ref,
                  pool_ext, conv_ext, h_carry, frames_a, frames_b, frames_h, *, start_pos, bblk, tt, d):
    j = pl.program_id(1)

    @pl.when(j == 0)
    def _():
        pool_ext[:, 0:POOL_PAD, :] = hp_ref[...]
        conv_ext[:, 0:CONV_PAD * (d // LANES), :] = hc_ref[...]
        h_carry[...] = hl_ref[...]

    rows = bblk * tt
    chunks = d // LANES
    cat_rows = lambda parts: jnp.concatenate(parts, axis=0) if len(parts) > 1 else parts[0]
    x = x_ref[...].reshape(rows, d)
    xnb = _rms_norm(x, nmix_ref[...]).astype(BF16)

    u_x = _dot(xnb, win_ref[:, d:2 * d])
    xc_rows = []
    for b in range(bblk):
        _store_rows(conv_ext.at[b, pl.ds(CONV_PAD * chunks, tt * chunks)], u_x[b * tt:(b + 1) * tt, :], tt, chunks)
        ce = conv_ext[b].reshape(CONV_PAD + tt, chunks, LANES)
        y = convb_ref[...] + ce[CONV_PAD:CONV_PAD + tt] * convw_ref[CONV_WIDTH - 1]
        for k in range(1, CONV_WIDTH):
            y = y + ce[CONV_PAD - k:CONV_PAD - k + tt] * convw_ref[CONV_WIDTH - 1 - k]
        frames_h[...] = y.reshape(tt * chunks, LANES)
        xc_rows.append(_load_rows(frames_h, tt, chunks))
        nconv_ref[b] = conv_ext[b, (CONV_PAD + tt - CONV_HIST) * chunks:(CONV_PAD + tt) * chunks, :]
        conv_ext[b, 0:CONV_PAD * chunks, :] = conv_ext[b, tt * chunks:(tt + CONV_PAD) * chunks, :]
    xc = cat_rows(xc_rows)

    n_chunks = d // MXU_WIDTH
    pre_a, pre_i = [], []
    for c in range(n_chunks):
        ai = _dot(xc[:, c * MXU_WIDTH:(c + 1) * MXU_WIDTH].astype(BF16), wai_ref[c])
        pre_a.append(ai[:, 0:MXU_WIDTH])
        pre_i.append(ai[:, MXU_WIDTH:2 * MXU_WIDTH])

    u_pool = _dot(xnb, win_ref[:, 0:d])
    pos1 = lax.broadcasted_iota(jnp.int32, (tt, 1), 0) + (start_pos + 1) + j * tt
    gd = d // len(POOL_WINDOWS)
    y_pool_rows = []
    for b in range(bblk):
        pool_ext[b, POOL_PAD:POOL_PAD + tt, :] = u_pool[b * tt:(b + 1) * tt, :]
        parts = []
        for g, w in enumerate(POOL_WINDOWS):
            sl = slice(g * gd, (g + 1) * gd)
            s = pool_ext[b, :, sl]
            width = 1
            while width < w:
                s = s + pltpu.roll(s, width, axis=0)
                width *= 2
            cur = s[POOL_PAD:POOL_PAD + tt, :]
            inv_cnt = 1.0 / jnp.minimum(pos1, w).astype(F32)
            pooled = cur * inv_cnt - u_pool[b * tt:(b + 1) * tt, sl]
            mixed = _dot(pooled.astype(BF16), poolw_ref[g])
            parts.append(mixed * pscale_ref[:, sl])
        y_pool_rows.append(jnp.concatenate(parts, axis=1))
        npool_ref[b] = pool_ext[b, POOL_PAD + tt - POOL_HIST:POOL_PAD + tt, :]
        pool_ext[b, 0:POOL_PAD, :] = pool_ext[b, tt:tt + POOL_PAD, :]
    y_pool = cat_rows(y_pool_rows)

    r_gate = jax.nn.sigmoid(jnp.concatenate(pre_a, axis=1) + ba_ref[...])
    i_gate = jax.nn.sigmoid(jnp.concatenate(pre_i, axis=1) + bi_ref[...])
    log_a = (-LRU_C) * r_gate * _softplus(-lam_ref[...])
    a = jnp.exp(log_a)
    mult = jnp.sqrt(-jnp.tanh(log_a) * (a * a + 1.0))
    bb = mult * (i_gate * xc)
    gl_pool = _dot(xnb, win_ref[:, 3 * d:4 * d])
    u_gate = _dot(xnb, win_ref[:, 2 * d:3 * d])
    h_rows = []
    for b in range(bblk):
        _store_rows(frames_a, a[b * tt:(b + 1) * tt, :], tt, chunks)
        _store_rows(frames_b, bb[b * tt:(b + 1) * tt, :], tt, chunks)
        h_last = _lru_scan(frames_a, frames_b, frames_h, h_carry[b], tt, chunks)
        h_carry[b] = h_last
        nlru_ref[b] = h_last
        h_rows.append(_load_rows(frames_h, tt, chunks))
    h = cat_rows(h_rows)
    br_pool = _dot(y_pool.astype(BF16), wbrp_ref[...])
    gl_lru = _dot(xnb, win_ref[:, 4 * d:5 * d])

    y_lru = h * _gelu_tanh(u_gate)
    br_lru = _dot(y_lru.astype(BF16), wbrl_ref[...])
    g_pool = jax.nn.sigmoid(gl_pool + bgate_ref[:, 0:d])
    g_lru = jax.nn.sigmoid(gl_lru + bgate_ref[:, d:2 * d])
    acc = g_pool * br_pool + g_lru * br_lru
    out = _dot(acc.astype(BF16), wout_ref[...])
    x1_ref[...] = (x + out).reshape(bblk, tt, d)


def _const_spec(shape):
    nd = len(shape)
    return pl.BlockSpec(shape, lambda *_: (0,) * nd, pipeline_mode=pl.Buffered(1))


def _mixer_call(x, hist_pool, hist_conv, hist_lru, weights, *, start_pos, bblk, tt, seq0=0):
    _, t_len, d = x.shape
    bsz = hist_pool.shape[0]
    assert seq0 % bblk == 0
    grid = (bsz // bblk, t_len // tt)
    kern = functools.partial(_mixer_kernel, start_pos=start_pos, bblk=bblk, tt=tt, d=d)
    chunks = d // LANES
    seq_spec = lambda rows: pl.BlockSpec((bblk, rows, d), lambda b, j: (b, 0, 0))
    frame_spec = lambda frames: pl.BlockSpec((bblk, frames * chunks, LANES), lambda b, j: (b, 0, 0))
    in_specs = [
        pl.BlockSpec((bblk, tt, d), lambda b, j: (b + seq0 // bblk, j, 0)),
        seq_spec(POOL_PAD), frame_spec(CONV_PAD), frame_spec(1),
    ] + [_const_spec(w.shape) for w in weights]
    out_shape = (
        jax.ShapeDtypeStruct((bsz, t_len, d), F32),
        jax.ShapeDtypeStruct((bsz, POOL_HIST, d), F32),
        jax.ShapeDtypeStruct((bsz, CONV_HIST * chunks, LANES), F32),
        jax.ShapeDtypeStruct((bsz, chunks, LANES), F32),
    )
    out_specs = (
        pl.BlockSpec((bblk, tt, d), lambda b, j: (b, j, 0)),
        seq_spec(POOL_HIST), frame_spec(CONV_HIST), frame_spec(1),
    )
    return pl.pallas_call(
        kern,
        grid=grid,
        in_specs=in_specs,
        out_specs=out_specs,
        out_shape=out_shape,
        scratch_shapes=[
            pltpu.VMEM((bblk, POOL_PAD + tt, d), F32),
            pltpu.VMEM((bblk, (CONV_PAD + tt) * chunks, LANES), F32),
            pltpu.VMEM((bblk, chunks, LANES), F32),
            pltpu.VMEM((tt * chunks, LANES), F32),
            pltpu.VMEM((tt * chunks, LANES), F32),
            pltpu.VMEM((tt * chunks, LANES), F32),
        ],
        compiler_params=pltpu.CompilerParams(
            dimension_semantics=("arbitrary", "arbitrary"), vmem_limit_bytes=VMEM_LIMIT_BYTES),
        name="mixer",
    )(x, hist_pool, hist_conv, hist_lru, *weights)


def _load_rows(ref, rows, chunks):
    return jnp.concatenate([ref[pl.ds(s, rows, stride=chunks), :] for s in range(chunks)], axis=1)


def _store_rows(ref, val, rows, chunks):
    for s in range(chunks):
        ref[pl.ds(s, rows, stride=chunks), :] = val[:, s * LANES:(s + 1) * LANES]


def _pack_halves(x):
    half = x.shape[1] // 2
    bits = lax.bitcast_convert_type(x, jnp.uint32)
    return (bits[:, :half] >> 16) | (bits[:, half:] & jnp.uint32(0xFFFF0000))


def _unpack_halves(p):
    lo = lax.bitcast_convert_type(p << 16, F32)
    hi = lax.bitcast_convert_type(p & jnp.uint32(0xFFFF0000), F32)
    return jnp.concatenate([lo, hi], axis=1)


class _Window(NamedTuple):
    first: int
    count: int


def _window_specs(windows, tb, d):
    specs, start = [], 0
    for win in windows:
        specs.append(pl.BlockSpec(
            (tb, d), lambda i, win=win, start=start: (win.first + jnp.clip(i - start, 0, win.count - 1), 0)))
        start += win.count
    return specs


def _window_load(i, windows, refs):
    val, start = refs[-1][...], sum(w.count for w in windows[:-1])
    for win, ref in zip(reversed(windows[:-1]), reversed(refs[:-1])):
        val = jnp.where(i < start, ref[...], val)
        start -= win.count
    return val


def _window_store(i, windows, refs, val):
    start = 0
    for win, ref in zip(windows, refs):
        @pl.when((i >= start) & (i < start + win.count))
        def _(ref=ref):
            ref[...] = val
        start += win.count


def _router_kernel(*refs, tb, d, windows):
    x1_refs = refs[:len(windows)]
    (nffn_ref, wrt_ref, br_ref, tri_ref, xn_ref, topi_ref, topw_ref, rank_ref, cnt_ref, carry) = refs[len(windows):]
    i = pl.program_id(0)

    @pl.when(i == 0)
    def _():
        carry[...] = jnp.zeros_like(carry)

    xnb = _rms_norm(_window_load(i, windows, x1_refs), nffn_ref[...]).astype(BF16)
    xn_ref_chunks = d // (2 * LANES)
    _store_rows(xn_ref, _pack_halves(xnb.astype(F32)), tb, xn_ref_chunks)
    logits = lax.dot_general(wrt_ref[...], xnb, (((1,), (1,)), ((), ())),
                             preferred_element_type=F32) + br_ref[...]
    n_exp = logits.shape[0]
    iota_e = lax.broadcasted_iota(jnp.int32, (n_exp, tb), 0)
    work = logits
    vals, idxs, hots = [], [], []
    for _ in range(TOP_K):
        m = jnp.max(work, axis=0, keepdims=True)
        idx = jnp.min(jnp.where(work == m, iota_e, n_exp), axis=0, keepdims=True)
        hot = iota_e == idx
        vals.append(m)
        idxs.append(idx)
        hots.append(hot)
        work = jnp.where(hot, -jnp.inf, work)
    exps = [jnp.exp(v - vals[0]) for v in vals]
    denom = exps[0] + exps[1] + exps[2] + exps[3]
    topi_ref[...] = jnp.concatenate(idxs, axis=0)
    topw_ref[...] = jnp.concatenate([e / denom for e in exps], axis=0)

    sel = sum(jnp.where(h, 1.0, 0.0) for h in hots)
    before = _dot(sel.astype(BF16), tri_ref[...]) + carry[:, 0:1]
    ranks = [jnp.sum(jnp.where(h, before, 0.0), axis=0, keepdims=True) for h in hots]
    rank_ref[...] = jnp.concatenate(ranks, axis=0).astype(jnp.int32)
    carry[...] = carry[...] + jnp.sum(sel, axis=1, keepdims=True)
    cnt_ref[...] = carry[...]


def _router_call(x1_parts, windows, norm_ffn, w_router_t, b_router, *, tb):
    d = x1_parts[0].shape[1]
    n_tiles = sum(w.count for w in windows)
    n = n_tiles * tb
    chunks = d // (2 * LANES)
    n_exp = w_router_t.shape[0]
    tri = jnp.triu(jnp.ones((tb, tb), BF16), k=1)
    tok = lambda rows, dt: (pl.BlockSpec((rows, tb), lambda i: (0, i)), jax.ShapeDtypeStruct((rows, n), dt))
    (topi_spec, topi_shape), (topw_spec, topw_shape), (rank_spec, rank_shape) = (
        tok(TOP_K, jnp.int32), tok(TOP_K, F32), tok(TOP_K, jnp.int32))
    return pl.pallas_call(
        functools.partial(_router_kernel, tb=tb, d=d, windows=tuple(windows)),
        grid=(n_tiles,),
        in_specs=[
            *_window_specs(windows, tb, d),
            _const_spec((1, d)), _const_spec((n_exp, d)), _const_spec((n_exp, 1)), _const_spec((tb, tb)),
        ],
        out_specs=(
            pl.BlockSpec((tb * chunks, LANES), lambda i: (i, 0)),
            topi_spec, topw_spec, rank_spec,
            pl.BlockSpec((n_exp, LANES), lambda i: (0, 0)),
        ),
        out_shape=(
            jax.ShapeDtypeStruct((n * chunks, LANES), jnp.uint32),
            topi_shape, topw_shape, rank_shape,
            jax.ShapeDtypeStruct((n_exp, LANES), F32),
        ),
        scratch_shapes=[pltpu.VMEM((n_exp, LANES), F32)],
        compiler_params=pltpu.CompilerParams(
            dimension_semantics=("arbitrary",), vmem_limit_bytes=VMEM_LIMIT_BYTES),
        name="router",
    )(*x1_parts, norm_ffn.reshape(1, d), w_router_t, b_router.reshape(n_exp, 1), tri)


def _expert_kernel(te_ref, tv_ref, nu_ref, xs_ref, wgu_ref, bgu_ref, wdn_ref, bdn_ref, ys_ref, *bf_refs,
                   tm, d, dff, round_weights):
    del nu_ref
    i = pl.program_id(0)
    if round_weights:
        wgu_bf, wdn_bf = bf_refs
        prev = te_ref[jnp.maximum(i - 1, 0)]

        @pl.when((i == 0) | (te_ref[i] != prev))
        def _():
            wgu_bf[...] = wgu_ref[...].astype(BF16)
            wdn_bf[...] = wdn_ref[...].astype(BF16)
    else:
        wgu_bf, wdn_bf = wgu_ref, wdn_ref

    def ffn(rows):
        x = _unpack_halves(_load_rows(xs_ref, rows, d // (2 * LANES))).astype(BF16)
        gu = _dot(x, wgu_bf[0]) + bgu_ref[0]
        gate = jnp.minimum(gu[:, :dff], SWIGLU_LIMIT)
        up = jnp.clip(gu[:, dff:], -SWIGLU_LIMIT, SWIGLU_LIMIT)
        hid = (up + 1.0) * gate * jax.nn.sigmoid(SWIGLU_ALPHA * gate)
        y = _dot(hid.astype(BF16), wdn_bf[0]) + bdn_ref[0]
        _store_rows(ys_ref, _pack_halves(y.astype(BF16).astype(F32)), rows, d // (2 * LANES))

    valid = tv_ref[i]
    pl.when(valid > tm // 2)(lambda: ffn(tm))
    pl.when((valid > 0) & (valid <= tm // 2))(lambda: ffn(tm // 2))


def _expert_call(tile_expert, tile_rows, n_used, xs, w_gu, b_gu, w_dn, b_dn, *, tm):
    n_exp, d, dff2 = w_gu.shape
    dff = dff2 // 2
    xchunks = d // (2 * LANES)
    p_rows = xs.shape[0] // xchunks
    round_weights = w_gu.dtype != BF16
    row_blk = lambda i, te, tv, nu: (jnp.minimum(i, nu[0] - 1), 0)
    exp_blk = lambda i, te, tv, nu: (te[i], 0, 0)
    out_specs = [pl.BlockSpec((tm * xchunks, LANES), row_blk)]
    out_shape = [jax.ShapeDtypeStruct((p_rows * xchunks, LANES), jnp.uint32)]
    if round_weights:
        out_specs += [pl.BlockSpec((1, d, dff2), exp_blk), pl.BlockSpec((1, dff, d), exp_blk)]
        out_shape += [jax.ShapeDtypeStruct(w_gu.shape, BF16), jax.ShapeDtypeStruct(w_dn.shape, BF16)]
    return pl.pallas_call(
        functools.partial(_expert_kernel, tm=tm, d=d, dff=dff, round_weights=round_weights),
        grid_spec=pltpu.PrefetchScalarGridSpec(
            num_scalar_prefetch=3,
            grid=(p_rows // tm,),
            in_specs=[
                pl.BlockSpec((tm * xchunks, LANES), row_blk),
                pl.BlockSpec((1, d, dff2), exp_blk),
                pl.BlockSpec((1, 1, dff2), exp_blk),
                pl.BlockSpec((1, dff, d), exp_blk),
                pl.BlockSpec((1, 1, d), exp_blk),
            ],
            out_specs=out_specs,
        ),
        out_shape=out_shape,
        compiler_params=pltpu.CompilerParams(
            dimension_semantics=("arbitrary",), vmem_limit_bytes=VMEM_LIMIT_BYTES),
        name="experts",
    )(tile_expert, tile_rows, n_used, xs, w_gu, b_gu.reshape(n_exp, 1, dff2), w_dn, b_dn.reshape(n_exp, 1, d))


def _final_kernel(*refs, tb, d, in_windows, out_windows, n_alias):
    n_in, n_out = len(in_windows), len(out_windows)
    x1_refs = refs[:n_in]
    yk_ref, w_ref, nfin_ref = refs[n_in:n_in + 3]
    out_refs = refs[n_in + 3 + n_alias:n_in + 3 + n_alias + n_out]
    i = pl.program_id(0)
    w = w_ref[...]
    chunks = d // (2 * LANES)
    moe = _unpack_halves(_load_rows(yk_ref.at[0], tb, chunks)) * w[:, 0:1]
    for k in range(1, TOP_K):
        moe = moe + _unpack_halves(_load_rows(yk_ref.at[k], tb, chunks)) * w[:, k:k + 1]
    out = _rms_norm(_window_load(i, in_windows, x1_refs) + moe, nfin_ref[...])
    _window_store(i, out_windows, out_refs, out)


def _final_call(x1_parts, in_windows, yk, topw_t, norm_final, out_shapes, out_windows, out_init, *, tb):
    d = x1_parts[0].shape[1]
    n_tiles = sum(w.count for w in in_windows)
    chunks = d // (2 * LANES)
    alias_in = [a for a in out_init if a is not None]
    n_fixed = len(x1_parts) + 3
    aliases, j = {}, 0
    for k, a in enumerate(out_init):
        if a is not None:
            aliases[n_fixed + j] = k
            j += 1
    return pl.pallas_call(
        functools.partial(_final_kernel, tb=tb, d=d, in_windows=tuple(in_windows),
                          out_windows=tuple(out_windows), n_alias=len(alias_in)),
        grid=(n_tiles,),
        in_specs=[
            *_window_specs(in_windows, tb, d),
            pl.BlockSpec((TOP_K, tb * chunks, LANES), lambda i: (0, i, 0)),
            pl.BlockSpec((tb, TOP_K), lambda i: (i, 0)),
            _const_spec((1, d)),
            *[pl.BlockSpec(memory_space=pl.ANY) for _ in alias_in],
        ],
        out_specs=_window_specs(out_windows, tb, d),
        out_shape=[jax.ShapeDtypeStruct(s, F32) for s in out_shapes],
        input_output_aliases=aliases,
        compiler_params=pltpu.CompilerParams(
            dimension_semantics=("arbitrary",), vmem_limit_bytes=VMEM_LIMIT_BYTES),
        name="final",
    )(*x1_parts, yk, topw_t, norm_final.reshape(1, d), *alias_in)


def _sc_step_rows(n):
    per_worker = n // SC_WORKERS
    assert per_worker * SC_WORKERS == n
    w = max(c for c in range(SUBLANES, SC_MAX_STEP_ROWS + 1, SUBLANES) if per_worker % c == 0)
    return w, per_worker // w


def _sc_mesh():
    return plsc.VectorSubcoreMesh(core_axis_name="c", subcore_axis_name="s",
                                  num_cores=SC_CORES, num_subcores=SC_SUBCORES)


def _dispatch_rows(x_tiles, pos_steps, p_rows):
    n, chunks, _ = x_tiles.shape
    w, steps = _sc_step_rows(n)

    @functools.partial(
        pl.kernel, mesh=_sc_mesh(),
        out_type=jax.ShapeDtypeStruct((p_rows, chunks, LANES), x_tiles.dtype),
        scratch_types=[pltpu.VMEM((w, chunks, LANES), x_tiles.dtype), pltpu.VMEM((TOP_K, w), jnp.int32)],
        name="dispatch",
    )
    def run(x_hbm, pos_hbm, out_hbm, buf, idx):
        wid = lax.axis_index("s") * SC_CORES + lax.axis_index("c")

        @pl.loop(0, steps)
        def _(i):
            step = wid * steps + i
            pltpu.sync_copy(x_hbm.at[pl.ds(pl.multiple_of(step * w, SUBLANES), w)], buf)
            pltpu.sync_copy(pos_hbm.at[step], idx)
            for k in range(TOP_K):
                pltpu.sync_copy(buf, out_hbm.at[idx.at[k]])

    return run(x_tiles, pos_steps)


def _combine_rows(y_tiles, pos_steps, n):
    _, chunks, _ = y_tiles.shape
    w, steps = _sc_step_rows(n)

    @functools.partial(
        pl.kernel, mesh=_sc_mesh(),
        out_type=jax.ShapeDtypeStruct((TOP_K, n, chunks, LANES), y_tiles.dtype),
        scratch_types=[pltpu.VMEM((w, chunks, LANES), y_tiles.dtype), pltpu.VMEM((TOP_K, w), jnp.int32)],
        name="combine",
    )
    def run(y_hbm, pos_hbm, out_hbm, buf, idx):
        wid = lax.axis_index("s") * SC_CORES + lax.axis_index("c")

        @pl.loop(0, steps)
        def _(i):
            step = wid * steps + i
            pltpu.sync_copy(pos_hbm.at[step], idx)
            for k in range(TOP_K):
                pltpu.sync_copy(y_hbm.at[idx.at[k]], buf)
                pltpu.sync_copy(buf, out_hbm.at[k, pl.ds(pl.multiple_of(step * w, SUBLANES), w)])

    return run(y_tiles, pos_steps)


def _block_diag(w, per_block):
    heads, n, _ = w.shape
    w4 = w.reshape(heads // per_block, per_block, n, n)
    eye = jnp.eye(per_block, dtype=w.dtype)
    bd = jnp.einsum('chij,hg->chigj', w4, eye)
    return bd.reshape(heads // per_block, per_block * n, per_block * n)


def _mixer_weights(norm_mix, w_in, b_gate, pool_w, pool_scale, conv_w, conv_b, lru_wa, lru_ba, lru_wi,
                   lru_bi, lru_lambda, w_br_pool, w_br_lru, w_out):
    row = lambda v: v.reshape(1, -1)
    d = norm_mix.shape[0]
    head_dim = lru_wa.shape[-1]
    per_block = MXU_WIDTH // head_dim
    w_ai = jnp.concatenate([_block_diag(lru_wa, per_block), _block_diag(lru_wi, per_block)], axis=-1)
    return (row(norm_mix), w_in.astype(BF16), row(b_gate), pool_w.astype(BF16), row(pool_scale),
            conv_w.reshape(-1, d // LANES, LANES), conv_b.reshape(d // LANES, LANES), w_ai.astype(BF16),
            row(lru_ba), row(lru_bi), row(lru_lambda),
            w_br_pool.astype(BF16), w_br_lru.astype(BF16), w_out.astype(BF16))


def _run_mixer(x, state_pool, state_conv, state_lru, weights, *, start_pos, bblk, tt, seq0=0):
    bsz, d = state_lru.shape
    chunks = d // LANES
    hp = jnp.pad(state_pool, ((0, 0), (POOL_PAD - POOL_HIST, 0), (0, 0)))
    hc = jnp.pad(state_conv, ((0, 0), (CONV_PAD - CONV_HIST, 0), (0, 0))).reshape(bsz, CONV_PAD * chunks, LANES)
    hl = state_lru.reshape(bsz, chunks, LANES)
    x1, npool, nconv, nlru = _mixer_call(x, hp, hc, hl, weights, start_pos=start_pos, bblk=bblk, tt=tt,
                                         seq0=seq0)
    return x1, npool, nconv.reshape(bsz, CONV_HIST, d), nlru.reshape(bsz, d)


def _routing_plan(topi, rank, counts, *, tm, n_tiles, min_tiles):
    n_exp = counts.shape[0]
    tiles_per = jnp.maximum((counts + tm - 1) // tm, min_tiles)
    tiles_cum = jnp.cumsum(tiles_per)
    tile_start = tiles_cum - tiles_per
    base = tile_start * tm
    experts = jnp.arange(n_exp, dtype=jnp.int32)[:, None, None]
    pos = rank + jnp.sum(jnp.where(topi[None] == experts, base[:, None, None], 0), axis=0)
    n_used = tiles_cum[-1]
    tile_ids = jnp.arange(n_tiles, dtype=jnp.int32)
    live = tile_ids < n_used
    owner = (tiles_cum[None, :] <= jnp.minimum(tile_ids, n_used - 1)[:, None])
    tile_expert = jnp.sum(owner.astype(jnp.int32), axis=1)
    is_owner = tile_expert[:, None] == jnp.arange(n_exp, dtype=jnp.int32)[None, :]
    rows_left = jnp.sum(jnp.where(is_owner, counts[None, :] - (tile_ids[:, None] - tile_start[None, :]) * tm, 0),
                        axis=1)
    tile_rows = jnp.where(live, jnp.clip(rows_left, 0, tm), 0).astype(jnp.int32)
    return pos, tile_expert, tile_rows, n_used.reshape(1).astype(jnp.int32)


def _moe_rows(x1_parts, windows, norm_ffn, w_router_t, b_router, w_gu, b_gu, w_dn, b_dn):
    d = x1_parts[0].shape[1]
    chunks = d // LANES
    n_exp = w_router_t.shape[0]
    n = sum(w.count for w in windows) * TOKEN_TILE
    xn2, topi, topw, rank, cnt = _router_call(x1_parts, windows, norm_ffn, w_router_t, b_router, tb=TOKEN_TILE)
    min_tiles = 0 if w_gu.dtype == BF16 else 1
    n_tiles = (n * TOP_K + n_exp * (EXPERT_TILE - 1)) // EXPERT_TILE + n_exp * min_tiles
    p_rows = n_tiles * EXPERT_TILE
    pos, tile_expert, tile_rows, n_used = _routing_plan(
        topi, rank, cnt[:, 0].astype(jnp.int32), tm=EXPERT_TILE, n_tiles=n_tiles, min_tiles=min_tiles)
    w, steps = _sc_step_rows(n)
    pos_steps = pos.reshape(TOP_K, SC_WORKERS * steps, w).transpose(1, 0, 2)
    xchunks = d // (2 * LANES)
    xs = _dispatch_rows(xn2.reshape(n, xchunks, LANES), pos_steps, p_rows)
    ys, *w_bf = _expert_call(tile_expert, tile_rows, n_used, xs.reshape(p_rows * xchunks, LANES), w_gu, b_gu,
                             w_dn, b_dn, tm=EXPERT_TILE)
    yk = _combine_rows(ys.reshape(p_rows, xchunks, LANES), pos_steps, n)
    return yk.reshape(TOP_K, n * xchunks, LANES), topw, (w_bf if w_bf else (w_gu, w_dn))


def kernel(x_prompt, x_sample, state_pool, state_conv, state_lru, norm_mix, w_in, b_gate, pool_w, pool_scale,
           conv_w, conv_b, lru_wa, lru_ba, lru_wi, lru_bi, lru_lambda, w_br_pool, w_br_lru, w_out, norm_ffn,
           w_router, b_router, w_gu, b_gu, w_dn, b_dn, norm_final):
    bp, tp, d = x_prompt.shape
    bs, ts, _ = x_sample.shape
    n_p, n_s = bp * tp, bs * ts
    mw = _mixer_weights(norm_mix[0], w_in[0], b_gate[0], pool_w[0], pool_scale[0], conv_w[0], conv_b[0],
                        lru_wa[0], lru_ba[0], lru_wi[0], lru_bi[0], lru_lambda[0], w_br_pool[0], w_br_lru[0],
                        w_out[0])
    w_router_t = w_router[0].T.astype(BF16)
    zeros = lambda *shape: jnp.zeros(shape, x_prompt.dtype)

    assert sum(PROMPT_GROUP_SEQS) == bp and tp % TOKEN_TILE == 0 and n_s % TOKEN_TILE == 0
    seq_tiles = tp // TOKEN_TILE
    s_tiles = n_s // TOKEN_TILE
    x1_s, pool_s, conv_s, lru_s = _run_mixer(
        x_sample, state_pool[0], state_conv[0], state_lru[0], mw, start_pos=PAST_LEN, bblk=bs, tt=ts)
    y_p, y_s = None, None
    expert_w = (w_gu[0], w_dn[0])
    pools, convs, lrus = [], [], []
    seq0 = 0
    for g, seqs in enumerate(PROMPT_GROUP_SEQS):
        x1_g, pool_g, conv_g, lru_g = _run_mixer(
            x_prompt, zeros(seqs, POOL_HIST, d), zeros(seqs, CONV_HIST, d), zeros(seqs, d), mw,
            start_pos=0, bblk=1, tt=MIXER_TILE, seq0=seq0)
        pools.append(pool_g)
        convs.append(conv_g)
        lrus.append(lru_g)
        last = g == len(PROMPT_GROUP_SEQS) - 1
        g_tiles = seqs * seq_tiles
        parts, windows = [x1_g.reshape(seqs * tp, d)], [_Window(0, g_tiles)]
        out_shapes, out_windows, out_init = [(n_p, d)], [_Window(seq0 * seq_tiles, g_tiles)], [y_p]
        seq0 += seqs
        if last:
            parts.append(x1_s.reshape(n_s, d))
            windows.append(_Window(0, s_tiles))
            out_shapes.append((n_s, d))
            out_windows.append(_Window(0, s_tiles))
            out_init.append(None)
        yk, topw, expert_w = _moe_rows(parts, windows, norm_ffn[0], w_router_t, b_router[0], expert_w[0],
                                       b_gu[0], expert_w[1], b_dn[0])
        outs = _final_call(parts, windows, yk, topw.T, norm_final, out_shapes, out_windows, out_init,
                           tb=TOKEN_TILE)
        y_p = outs[0]
        if last:
            y_s = outs[1]

    cat = lambda xs: jnp.concatenate(xs, axis=0)[None]
    return (y_p.reshape(bp, tp, d), y_s.reshape(bs, ts, d), cat(pools), cat(convs), cat(lrus),
            pool_s[None], conv_s[None], lru_s[None])
```

```python
import functools
from typing import NamedTuple

import jax
import jax.numpy as jnp
from jax import lax
from jax.experimental import pallas as pl
from jax.experimental.pallas import tpu as pltpu
from jax.experimental.pallas import tpu_sc as plsc

BF16 = jnp.bfloat16
F32 = jnp.float32

POOL_WINDOWS = (2, 4, 8, 16)
POOL_HIST = max(POOL_WINDOWS) - 1
CONV_WIDTH = 4
CONV_HIST = CONV_WIDTH - 1
LRU_HEADS = 16
LRU_C = 8.0
LRU_SEGMENTS = 8
N_EXPERTS = 32
TOP_K = 4
SWIGLU_LIMIT = 7.0
SWIGLU_ALPHA = 1.702
NORM_EPS = 1e-6

SUBLANES = 8
LANES = 128
MXU_WIDTH = 256
POOL_PAD = 16
CONV_PAD = 8
VMEM_LIMIT_BYTES = 56 * 1024 * 1024

PAST_LEN = 1024
MIXER_TILE = 512
TOKEN_TILE = 512
EXPERT_TILE = 512
PROMPT_GROUP_SEQS = (12, 4)

SC_CORES = 2
SC_SUBCORES = 16
SC_WORKERS = SC_CORES * SC_SUBCORES
SC_MAX_STEP_ROWS = 64


def _rms_norm(x, g):
    ms = jnp.mean(x * x, axis=-1, keepdims=True)
    return (x * lax.rsqrt(ms + NORM_EPS)) * g


def _dot(a, b):
    return jnp.dot(a, b, preferred_element_type=F32)


def _softplus(x):
    return jnp.maximum(x, 0.0) + jnp.log1p(jnp.exp(-jnp.abs(x)))


def _gelu_tanh(x):
    c = 0.7978845608028654
    return 0.5 * x * (1.0 + jnp.tanh(c * (x + 0.044715 * (x * x * x))))


def _lru_scan(a_ref, b_ref, h_ref, h0, frames, chunks):
    tile = lambda ref, t: ref[t * chunks:(t + 1) * chunks, :]
    seg = frames // LRU_SEGMENTS
    assert seg * LRU_SEGMENTS == frames
    prods = [tile(a_ref, s * seg) for s in range(LRU_SEGMENTS)]
    sums = [tile(b_ref, s * seg) for s in range(LRU_SEGMENTS)]
    for t in range(1, seg):
        for s in range(LRU_SEGMENTS):
            a_t = tile(a_ref, s * seg + t)
            sums[s] = a_t * sums[s] + tile(b_ref, s * seg + t)
            prods[s] = a_t * prods[s]
            b_ref[(s * seg + t) * chunks:(s * seg + t + 1) * chunks, :] = sums[s]
            a_ref[(s * seg + t) * chunks:(s * seg + t + 1) * chunks, :] = prods[s]
    h_in = [h0]
    for s in range(LRU_SEGMENTS):
        h_in.append(prods[s] * h_in[s] + sums[s])
    for s in range(LRU_SEGMENTS):
        for t in range(seg):
            f = s * seg + t
            h_ref[f * chunks:(f + 1) * chunks, :] = tile(a_ref, f) * h_in[s] + tile(b_ref, f)
    return h_in[LRU_SEGMENTS]


def _mixer_kernel(x_ref, hp_ref, hc_ref, hl_ref, nmix_ref, win_ref, bgate_ref, poolw_ref, pscale_ref,
                  convw_ref, convb_ref, wai_ref, ba_ref, bi_ref, lam_ref, wbrp_ref, wbrl_ref, wout_ref,
                  x1_ref, npool_ref, nconv_ref, nlru_ref,
                  pool_ext, conv_ext, h_carry, frames_a, frames_b, frames_h, *, start_pos, bblk, tt, d):
    j = pl.program_id(1)

    @pl.when(j == 0)
    def _():
        pool_ext[:, 0:POOL_PAD, :] = hp_ref[...]
        conv_ext[:, 0:CONV_PAD * (d // LANES), :] = hc_ref[...]
        h_carry[...] = hl_ref[...]

    rows = bblk * tt
    chunks = d // LANES
    cat_rows = lambda parts: jnp.concatenate(parts, axis=0) if len(parts) > 1 else parts[0]
    x = x_ref[...].reshape(rows, d)
    xnb = _rms_norm(x, nmix_ref[...]).astype(BF16)

    u_x = _dot(xnb, win_ref[:, d:2 * d])
    xc_rows = []
    for b in range(bblk):
        _store_rows(conv_ext.at[b, pl.ds(CONV_PAD * chunks, tt * chunks)], u_x[b * tt:(b + 1) * tt, :], tt, chunks)
        ce = conv_ext[b].reshape(CONV_PAD + tt, chunks, LANES)
        y = convb_ref[...] + ce[CONV_PAD:CONV_PAD + tt] * convw_ref[CONV_WIDTH - 1]
        for k in range(1, CONV_WIDTH):
            y = y + ce[CONV_PAD - k:CONV_PAD - k + tt] * convw_ref[CONV_WIDTH - 1 - k]
        frames_h[...] = y.reshape(tt * chunks, LANES)
        xc_rows.append(_load_rows(frames_h, tt, chunks))
        nconv_ref[b] = conv_ext[b, (CONV_PAD + tt - CONV_HIST) * chunks:(CONV_PAD + tt) * chunks, :]
        conv_ext[b, 0:CONV_PAD * chunks, :] = conv_ext[b, tt * chunks:(tt + CONV_PAD) * chunks, :]
    xc = cat_rows(xc_rows)

    n_chunks = d // MXU_WIDTH
    pre_a, pre_i = [], []
    for c in range(n_chunks):
        ai = _dot(xc[:, c * MXU_WIDTH:(c + 1) * MXU_WIDTH].astype(BF16), wai_ref[c])
        pre_a.append(ai[:, 0:MXU_WIDTH])
        pre_i.append(ai[:, MXU_WIDTH:2 * MXU_WIDTH])

    u_pool = _dot(xnb, win_ref[:, 0:d])
    pos1 = lax.broadcasted_iota(jnp.int32, (tt, 1), 0) + (start_pos + 1) + j * tt
    gd = d // len(POOL_WINDOWS)
    y_pool_rows = []
    for b in range(bblk):
        pool_ext[b, POOL_PAD:POOL_PAD + tt, :] = u_pool[b * tt:(b + 1) * tt, :]
        parts = []
        for g, w in enumerate(POOL_WINDOWS):
            sl = slice(g * gd, (g + 1) * gd)
            s = pool_ext[b, :, sl]
            width = 1
            while width < w:
                s = s + pltpu.roll(s, width, axis=0)
                width *= 2
            cur = s[POOL_PAD:POOL_PAD + tt, :]
            inv_cnt = 1.0 / jnp.minimum(pos1, w).astype(F32)
            pooled = cur * inv_cnt - u_pool[b * tt:(b + 1) * tt, sl]
            mixed = _dot(pooled.astype(BF16), poolw_ref[g])
            parts.append(mixed * pscale_ref[:, sl])
        y_pool_rows.append(jnp.concatenate(parts, axis=1))
        npool_ref[b] = pool_ext[b, POOL_PAD + tt - POOL_HIST:POOL_PAD + tt, :]
        pool_ext[b, 0:POOL_PAD, :] = pool_ext[b, tt:tt + POOL_PAD, :]
    y_pool = cat_rows(y_pool_rows)

    r_gate = jax.nn.sigmoid(jnp.concatenate(pre_a, axis=1) + ba_ref[...])
    i_gate = jax.nn.sigmoid(jnp.concatenate(pre_i, axis=1) + bi_ref[...])
    log_a = (-LRU_C) * r_gate * _softplus(-lam_ref[...])
    a = jnp.exp(log_a)
    mult = jnp.sqrt(-jnp.tanh(log_a) * (a * a + 1.0))
    bb = mult * (i_gate * xc)
    gl_pool = _dot(xnb, win_ref[:, 3 * d:4 * d])
    u_gate = _dot(xnb, win_ref[:, 2 * d:3 * d])
    h_rows = []
    for b in range(bblk):
        _store_rows(frames_a, a[b * tt:(b + 1) * tt, :], tt, chunks)
        _store_rows(frames_b, bb[b * tt:(b + 1) * tt, :], tt, chunks)
        h_last = _lru_scan(frames_a, frames_b, frames_h, h_carry[b], tt, chunks)
        h_carry[b] = h_last
        nlru_ref[b] = h_last
        h_rows.append(_load_rows(frames_h, tt, chunks))
    h = cat_rows(h_rows)
    br_pool = _dot(y_pool.astype(BF16), wbrp_ref[...])
    gl_lru = _dot(xnb, win_ref[:, 4 * d:5 * d])

    y_lru = h * _gelu_tanh(u_gate)
    br_lru = _dot(y_lru.astype(BF16), wbrl_ref[...])
    g_pool = jax.nn.sigmoid(gl_pool + bgate_ref[:, 0:d])
    g_lru = jax.nn.sigmoid(gl_lru + bgate_ref[:, d:2 * d])
    acc = g_pool * br_pool + g_lru * br_lru
    out = _dot(acc.astype(BF16), wout_ref[...])
    x1_ref[...] = (x + out).reshape(bblk, tt, d)


def _const_spec(shape):
    nd = len(shape)
    return pl.BlockSpec(shape, lambda *_: (0,) * nd, pipeline_mode=pl.Buffered(1))


def _mixer_call(x, hist_pool, hist_conv, hist_lru, weights, *, start_pos, bblk, tt, seq0=0):
    _, t_len, d = x.shape
    bsz = hist_pool.shape[0]
    assert seq0 % bblk == 0
    grid = (bsz // bblk, t_len // tt)
    kern = functools.partial(_mixer_kernel, start_pos=start_pos, bblk=bblk, tt=tt, d=d)
    chunks = d // LANES
    seq_spec = lambda rows: pl.BlockSpec((bblk, rows, d), lambda b, j: (b, 0, 0))
    frame_spec = lambda frames: pl.BlockSpec((bblk, frames * chunks, LANES), lambda b, j: (b, 0, 0))
    in_specs = [
        pl.BlockSpec((bblk, tt, d), lambda b, j: (b + seq0 // bblk, j, 0)),
        seq_spec(POOL_PAD), frame_spec(CONV_PAD), frame_spec(1),
    ] + [_const_spec(w.shape) for w in weights]
    out_shape = (
        jax.ShapeDtypeStruct((bsz, t_len, d), F32),
        jax.ShapeDtypeStruct((bsz, POOL_HIST, d), F32),
        jax.ShapeDtypeStruct((bsz, CONV_HIST * chunks, LANES), F32),
        jax.ShapeDtypeStruct((bsz, chunks, LANES), F32),
    )
    out_specs = (
        pl.BlockSpec((bblk, tt, d), lambda b, j: (b, j, 0)),
        seq_spec(POOL_HIST), frame_spec(CONV_HIST), frame_spec(1),
    )
    return pl.pallas_call(
        kern,
        grid=grid,
        in_specs=in_specs,
        out_specs=out_specs,
        out_shape=out_shape,
        scratch_shapes=[
            pltpu.VMEM((bblk, POOL_PAD + tt, d), F32),
            pltpu.VMEM((bblk, (CONV_PAD + tt) * chunks, LANES), F32),
            pltpu.VMEM((bblk, chunks, LANES), F32),
            pltpu.VMEM((tt * chunks, LANES), F32),
            pltpu.VMEM((tt * chunks, LANES), F32),
            pltpu.VMEM((tt * chunks, LANES), F32),
        ],
        compiler_params=pltpu.CompilerParams(
            dimension_semantics=("arbitrary", "arbitrary"), vmem_limit_bytes=VMEM_LIMIT_BYTES),
        name="mixer",
    )(x, hist_pool, hist_conv, hist_lru, *weights)


def _load_rows(ref, rows, chunks):
    return jnp.concatenate([ref[pl.ds(s, rows, stride=chunks), :] for s in range(chunks)], axis=1)


def _store_rows(ref, val, rows, chunks):
    for s in range(chunks):
        ref[pl.ds(s, rows, stride=chunks), :] = val[:, s * LANES:(s + 1) * LANES]


def _pack_halves(x):
    half = x.shape[1] // 2
    bits = lax.bitcast_convert_type(x, jnp.uint32)
    return (bits[:, :half] >> 16) | (bits[:, half:] & jnp.uint32(0xFFFF0000))


def _unpack_halves(p):
    lo = lax.bitcast_convert_type(p << 16, F32)
    hi = lax.bitcast_convert_type(p & jnp.uint32(0xFFFF0000), F32)
    return jnp.concatenate([lo, hi], axis=1)


class _Window(NamedTuple):
    first: int
    count: int


def _window_specs(windows, tb, d):
    specs, start = [], 0
    for win in windows:
        specs.append(pl.BlockSpec(
            (tb, d), lambda i, win=win, start=start: (win.first + jnp.clip(i - start, 0, win.count - 1), 0)))
        start += win.count
    return specs


def _window_load(i, windows, refs):
    val, start = refs[-1][...], sum(w.count for w in windows[:-1])
    for win, ref in zip(reversed(windows[:-1]), reversed(refs[:-1])):
        val = jnp.where(i < start, ref[...], val)
        start -= win.count
    return val


def _window_store(i, windows, refs, val):
    start = 0
    for win, ref in zip(windows, refs):
        @pl.when((i >= start) & (i < start + win.count))
        def _(ref=ref):
            ref[...] = val
        start += win.count


def _router_kernel(*refs, tb, d, windows):
    x1_refs = refs[:len(windows)]
    (nffn_ref, wrt_ref, br_ref, tri_ref, xn_ref, topi_ref, topw_ref, rank_ref, cnt_ref, carry) = refs[len(windows):]
    i = pl.program_id(0)

    @pl.when(i == 0)
    def _():
        carry[...] = jnp.zeros_like(carry)

    xnb = _rms_norm(_window_load(i, windows, x1_refs), nffn_ref[...]).astype(BF16)
    xn_ref_chunks = d // (2 * LANES)
    _store_rows(xn_ref, _pack_halves(xnb.astype(F32)), tb, xn_ref_chunks)
    logits = lax.dot_general(wrt_ref[...], xnb, (((1,), (1,)), ((), ())),
                             preferred_element_type=F32) + br_ref[...]
    n_exp = logits.shape[0]
    iota_e = lax.broadcasted_iota(jnp.int32, (n_exp, tb), 0)
    work = logits
    vals, idxs, hots = [], [], []
    for _ in range(TOP_K):
        m = jnp.max(work, axis=0, keepdims=True)
        idx = jnp.min(jnp.where(work == m, iota_e, n_exp), axis=0, keepdims=True)
        hot = iota_e == idx
        vals.append(m)
        idxs.append(idx)
        hots.append(hot)
        work = jnp.where(hot, -jnp.inf, work)
    exps = [jnp.exp(v - vals[0]) for v in vals]
    denom = exps[0] + exps[1] + exps[2] + exps[3]
    topi_ref[...] = jnp.concatenate(idxs, axis=0)
    topw_ref[...] = jnp.concatenate([e / denom for e in exps], axis=0)

    sel = sum(jnp.where(h, 1.0, 0.0) for h in hots)
    before = _dot(sel.astype(BF16), tri_ref[...]) + carry[:, 0:1]
    ranks = [jnp.sum(jnp.where(h, before, 0.0), axis=0, keepdims=True) for h in hots]
    rank_ref[...] = jnp.concatenate(ranks, axis=0).astype(jnp.int32)
    carry[...] = carry[...] + jnp.sum(sel, axis=1, keepdims=True)
    cnt_ref[...] = carry[...]


def _router_call(x1_parts, windows, norm_ffn, w_router_t, b_router, *, tb):
    d = x1_parts[0].shape[1]
    n_tiles = sum(w.count for w in windows)
    n = n_tiles * tb
    chunks = d // (2 * LANES)
    n_exp = w_router_t.shape[0]
    tri = jnp.triu(jnp.ones((tb, tb), BF16), k=1)
    tok = lambda rows, dt: (pl.BlockSpec((rows, tb), lambda i: (0, i)), jax.ShapeDtypeStruct((rows, n), dt))
    (topi_spec, topi_shape), (topw_spec, topw_shape), (rank_spec, rank_shape) = (
        tok(TOP_K, jnp.int32), tok(TOP_K, F32), tok(TOP_K, jnp.int32))
    return pl.pallas_call(
        functools.partial(_router_kernel, tb=tb, d=d, windows=tuple(windows)),
        grid=(n_tiles,),
        in_specs=[
            *_window_specs(windows, tb, d),
            _const_spec((1, d)), _const_spec((n_exp, d)), _const_spec((n_exp, 1)), _const_spec((tb, tb)),
        ],
        out_specs=(
            pl.BlockSpec((tb * chunks, LANES), lambda i: (i, 0)),
            topi_spec, topw_spec, rank_spec,
            pl.BlockSpec((n_exp, LANES), lambda i: (0, 0)),
        ),
        out_shape=(
            jax.ShapeDtypeStruct((n * chunks, LANES), jnp.uint32),
            topi_shape, topw_shape, rank_shape,
            jax.ShapeDtypeStruct((n_exp, LANES), F32),
        ),
        scratch_shapes=[pltpu.VMEM((n_exp, LANES), F32)],
        compiler_params=pltpu.CompilerParams(
            dimension_semantics=("arbitrary",), vmem_limit_bytes=VMEM_LIMIT_BYTES),
        name="router",
    )(*x1_parts, norm_ffn.reshape(1, d), w_router_t, b_router.reshape(n_exp, 1), tri)


def _expert_kernel(te_ref, tv_ref, nu_ref, xs_ref, wgu_ref, bgu_ref, wdn_ref, bdn_ref, ys_ref, *bf_refs,
                   tm, d, dff, round_weights):
    del nu_ref
    i = pl.program_id(0)
    if round_weights:
        wgu_bf, wdn_bf = bf_refs
        prev = te_ref[jnp.maximum(i - 1, 0)]

        @pl.when((i == 0) | (te_ref[i] != prev))
        def _():
            wgu_bf[...] = wgu_ref[...].astype(BF16)
            wdn_bf[...] = wdn_ref[...].astype(BF16)
    else:
        wgu_bf, wdn_bf = wgu_ref, wdn_ref

    def ffn(rows):
        x = _unpack_halves(_load_rows(xs_ref, rows, d // (2 * LANES))).astype(BF16)
        gu = _dot(x, wgu_bf[0]) + bgu_ref[0]
        gate = jnp.minimum(gu[:, :dff], SWIGLU_LIMIT)
        up = jnp.clip(gu[:, dff:], -SWIGLU_LIMIT, SWIGLU_LIMIT)
        hid = (up + 1.0) * gate * jax.nn.sigmoid(SWIGLU_ALPHA * gate)
        y = _dot(hid.astype(BF16), wdn_bf[0]) + bdn_ref[0]
        _store_rows(ys_ref, _pack_halves(y.astype(BF16).astype(F32)), rows, d // (2 * LANES))

    valid = tv_ref[i]
    pl.when(valid > tm // 2)(lambda: ffn(tm))
    pl.when((valid > 0) & (valid <= tm // 2))(lambda: ffn(tm // 2))


def _expert_call(tile_expert, tile_rows, n_used, xs, w_gu, b_gu, w_dn, b_dn, *, tm):
    n_exp, d, dff2 = w_gu.shape
    dff = dff2 // 2
    xchunks = d // (2 * LANES)
    p_rows = xs.shape[0] // xchunks
    round_weights = w_gu.dtype != BF16
    row_blk = lambda i, te, tv, nu: (jnp.minimum(i, nu[0] - 1), 0)
    exp_blk = lambda i, te, tv, nu: (te[i], 0, 0)
    out_specs = [pl.BlockSpec((tm * xchunks, LANES), row_blk)]
    out_shape = [jax.ShapeDtypeStruct((p_rows * xchunks, LANES), jnp.uint32)]
    if round_weights:
        out_specs += [pl.BlockSpec((1, d, dff2), exp_blk), pl.BlockSpec((1, dff, d), exp_blk)]
        out_shape += [jax.ShapeDtypeStruct(w_gu.shape, BF16), jax.ShapeDtypeStruct(w_dn.shape, BF16)]
    return pl.pallas_call(
        functools.partial(_expert_kernel, tm=tm, d=d, dff=dff, round_weights=round_weights),
        grid_spec=pltpu.PrefetchScalarGridSpec(
            num_scalar_prefetch=3,
            grid=(p_rows // tm,),
            in_specs=[
                pl.BlockSpec((tm * xchunks, LANES), row_blk),
                pl.BlockSpec((1, d, dff2), exp_blk),
                pl.BlockSpec((1, 1, dff2), exp_blk),
                pl.BlockSpec((1, dff, d), exp_blk),
                pl.BlockSpec((1, 1, d), exp_blk),
            ],
            out_specs=out_specs,
        ),
        out_shape=out_shape,
        compiler_params=pltpu.CompilerParams(
            dimension_semantics=("arbitrary",), vmem_limit_bytes=VMEM_LIMIT_BYTES),
        name="experts",
    )(tile_expert, tile_rows, n_used, xs, w_gu, b_gu.reshape(n_exp, 1, dff2), w_dn, b_dn.reshape(n_exp, 1, d))


def _final_kernel(*refs, tb, d, in_windows, out_windows, n_alias):
    n_in, n_out = len(in_windows), len(out_windows)
    x1_refs = refs[:n_in]
    yk_ref, w_ref, nfin_ref = refs[n_in:n_in + 3]
    out_refs = refs[n_in + 3 + n_alias:n_in + 3 + n_alias + n_out]
    i = pl.program_id(0)
    w = w_ref[...]
    chunks = d // (2 * LANES)
    moe = _unpack_halves(_load_rows(yk_ref.at[0], tb, chunks)) * w[:, 0:1]
    for k in range(1, TOP_K):
        moe = moe + _unpack_halves(_load_rows(yk_ref.at[k], tb, chunks)) * w[:, k:k + 1]
    out = _rms_norm(_window_load(i, in_windows, x1_refs) + moe, nfin_ref[...])
    _window_store(i, out_windows, out_refs, out)


def _final_call(x1_parts, in_windows, yk, topw_t, norm_final, out_shapes, out_windows, out_init, *, tb):
    d = x1_parts[0].shape[1]
    n_tiles = sum(w.count for w in in_windows)
    chunks = d // (2 * LANES)
    alias_in = [a for a in out_init if a is not None]
    n_fixed = len(x1_parts) + 3
    aliases, j = {}, 0
    for k, a in enumerate(out_init):
        if a is not None:
            aliases[n_fixed + j] = k
            j += 1
    return pl.pallas_call(
        functools.partial(_final_kernel, tb=tb, d=d, in_windows=tuple(in_windows),
                          out_windows=tuple(out_windows), n_alias=len(alias_in)),
        grid=(n_tiles,),
        in_specs=[
            *_window_specs(in_windows, tb, d),
            pl.BlockSpec((TOP_K, tb * chunks, LANES), lambda i: (0, i, 0)),
            pl.BlockSpec((tb, TOP_K), lambda i: (i, 0)),
            _const_spec((1, d)),
            *[pl.BlockSpec(memory_space=pl.ANY) for _ in alias_in],
        ],
        out_specs=_window_specs(out_windows, tb, d),
        out_shape=[jax.ShapeDtypeStruct(s, F32) for s in out_shapes],
        input_output_aliases=aliases,
        compiler_params=pltpu.CompilerParams(
            dimension_semantics=("arbitrary",), vmem_limit_bytes=VMEM_LIMIT_BYTES),
        name="final",
    )(*x1_parts, yk, topw_t, norm_final.reshape(1, d), *alias_in)


def _sc_step_rows(n):
    per_worker = n // SC_WORKERS
    assert per_worker * SC_WORKERS == n
    w = max(c for c in range(SUBLANES, SC_MAX_STEP_ROWS + 1, SUBLANES) if per_worker % c == 0)
    return w, per_worker // w


def _sc_mesh():
    return plsc.VectorSubcoreMesh(core_axis_name="c", subcore_axis_name="s",
                                  num_cores=SC_CORES, num_subcores=SC_SUBCORES)


def _dispatch_rows(x_tiles, pos_steps, p_rows):
    n, chunks, _ = x_tiles.shape
    w, steps = _sc_step_rows(n)

    @functools.partial(
        pl.kernel, mesh=_sc_mesh(),
        out_type=jax.ShapeDtypeStruct((p_rows, chunks, LANES), x_tiles.dtype),
        scratch_types=[pltpu.VMEM((w, chunks, LANES), x_tiles.dtype), pltpu.VMEM((TOP_K, w), jnp.int32)],
        name="dispatch",
    )
    def run(x_hbm, pos_hbm, out_hbm, buf, idx):
        wid = lax.axis_index("s") * SC_CORES + lax.axis_index("c")

        @pl.loop(0, steps)
        def _(i):
            step = wid * steps + i
            pltpu.sync_copy(x_hbm.at[pl.ds(pl.multiple_of(step * w, SUBLANES), w)], buf)
            pltpu.sync_copy(pos_hbm.at[step], idx)
            for k in range(TOP_K):
                pltpu.sync_copy(buf, out_hbm.at[idx.at[k]])

    return run(x_tiles, pos_steps)


def _combine_rows(y_tiles, pos_steps, n):
    _, chunks, _ = y_tiles.shape
    w, steps = _sc_step_rows(n)

    @functools.partial(
        pl.kernel, mesh=_sc_mesh(),
        out_type=jax.ShapeDtypeStruct((TOP_K, n, chunks, LANES), y_tiles.dtype),
        scratch_types=[pltpu.VMEM((w, chunks, LANES), y_tiles.dtype), pltpu.VMEM((TOP_K, w), jnp.int32)],
        name="combine",
    )
    def run(y_hbm, pos_hbm, out_hbm, buf, idx):
        wid = lax.axis_index("s") * SC_CORES + lax.axis_index("c")

        @pl.loop(0, steps)
        def _(i):
            step = wid * steps + i
            pltpu.sync_copy(pos_hbm.at[step], idx)
            for k in range(TOP_K):
                pltpu.sync_copy(y_hbm.at[idx.at[k]], buf)
                pltpu.sync_copy(buf, out_hbm.at[k, pl.ds(pl.multiple_of(step * w, SUBLANES), w)])

    return run(y_tiles, pos_steps)


def _block_diag(w, per_block):
    heads, n, _ = w.shape
    w4 = w.reshape(heads // per_block, per_block, n, n)
    eye = jnp.eye(per_block, dtype=w.dtype)
    bd = jnp.einsum('chij,hg->chigj', w4, eye)
    return bd.reshape(heads // per_block, per_block * n, per_block * n)


def _mixer_weights(norm_mix, w_in, b_gate, pool_w, pool_scale, conv_w, conv_b, lru_wa, lru_ba, lru_wi,
                   lru_bi, lru_lambda, w_br_pool, w_br_lru, w_out):
    row = lambda v: v.reshape(1, -1)
    d = norm_mix.shape[0]
    head_dim = lru_wa.shape[-1]
    per_block = MXU_WIDTH // head_dim
    w_ai = jnp.concatenate([_block_diag(lru_wa, per_block), _block_diag(lru_wi, per_block)], axis=-1)
    return (row(norm_mix), w_in.astype(BF16), row(b_gate), pool_w.astype(BF16), row(pool_scale),
            conv_w.reshape(-1, d // LANES, LANES), conv_b.reshape(d // LANES, LANES), w_ai.astype(BF16),
            row(lru_ba), row(lru_bi), row(lru_lambda),
            w_br_pool.astype(BF16), w_br_lru.astype(BF16), w_out.astype(BF16))


def _run_mixer(x, state_pool, state_conv, state_lru, weights, *, start_pos, bblk, tt, seq0=0):
    bsz, d = state_lru.shape
    chunks = d // LANES
    hp = jnp.pad(state_pool, ((0, 0), (POOL_PAD - POOL_HIST, 0), (0, 0)))
    hc = jnp.pad(state_conv, ((0, 0), (CONV_PAD - CONV_HIST, 0), (0, 0))).reshape(bsz, CONV_PAD * chunks, LANES)
    hl = state_lru.reshape(bsz, chunks, LANES)
    x1, npool, nconv, nlru = _mixer_call(x, hp, hc, hl, weights, start_pos=start_pos, bblk=bblk, tt=tt,
                                         seq0=seq0)
    return x1, npool, nconv.reshape(bsz, CONV_HIST, d), nlru.reshape(bsz, d)


def _routing_plan(topi, rank, counts, *, tm, n_tiles, min_tiles):
    n_exp = counts.shape[0]
    tiles_per = jnp.maximum((counts + tm - 1) // tm, min_tiles)
    tiles_cum = jnp.cumsum(tiles_per)
    tile_start = tiles_cum - tiles_per
    base = tile_start * tm
    experts = jnp.arange(n_exp, dtype=jnp.int32)[:, None, None]
    pos = rank + jnp.sum(jnp.where(topi[None] == experts, base[:, None, None], 0), axis=0)
    n_used = tiles_cum[-1]
    tile_ids = jnp.arange(n_tiles, dtype=jnp.int32)
    live = tile_ids < n_used
    owner = (tiles_cum[None, :] <= jnp.minimum(tile_ids, n_used - 1)[:, None])
    tile_expert = jnp.sum(owner.astype(jnp.int32), axis=1)
    is_owner = tile_expert[:, None] == jnp.arange(n_exp, dtype=jnp.int32)[None, :]
    rows_left = jnp.sum(jnp.where(is_owner, counts[None, :] - (tile_ids[:, None] - tile_start[None, :]) * tm, 0),
                        axis=1)
    tile_rows = jnp.where(live, jnp.clip(rows_left, 0, tm), 0).astype(jnp.int32)
    return pos, tile_expert, tile_rows, n_used.reshape(1).astype(jnp.int32)


def _moe_rows(x1_parts, windows, norm_ffn, w_router_t, b_router, w_gu, b_gu, w_dn, b_dn):
    d = x1_parts[0].shape[1]
    chunks = d // LANES
    n_exp = w_router_t.shape[0]
    n = sum(w.count for w in windows) * TOKEN_TILE
    xn2, topi, topw, rank, cnt = _router_call(x1_parts, windows, norm_ffn, w_router_t, b_router, tb=TOKEN_TILE)
    min_tiles = 0 if w_gu.dtype == BF16 else 1
    n_tiles = (n * TOP_K + n_exp * (EXPERT_TILE - 1)) // EXPERT_TILE + n_exp * min_tiles
    p_rows = n_tiles * EXPERT_TILE
    pos, tile_expert, tile_rows, n_used = _routing_plan(
        topi, rank, cnt[:, 0].astype(jnp.int32), tm=EXPERT_TILE, n_tiles=n_tiles, min_tiles=min_tiles)
    w, steps = _sc_step_rows(n)
    pos_steps = pos.reshape(TOP_K, SC_WORKERS * steps, w).transpose(1, 0, 2)
    xchunks = d // (2 * LANES)
    xs = _dispatch_rows(xn2.reshape(n, xchunks, LANES), pos_steps, p_rows)
    ys, *w_bf = _expert_call(tile_expert, tile_rows, n_used, xs.reshape(p_rows * xchunks, LANES), w_gu, b_gu,
                             w_dn, b_dn, tm=EXPERT_TILE)
    yk = _combine_rows(ys.reshape(p_rows, xchunks, LANES), pos_steps, n)
    return yk.reshape(TOP_K, n * xchunks, LANES), topw, (w_bf if w_bf else (w_gu, w_dn))


def kernel(x_prompt, x_sample, state_pool, state_conv, state_lru, norm_mix, w_in, b_gate, pool_w, pool_scale,
           conv_w, conv_b, lru_wa, lru_ba, lru_wi, lru_bi, lru_lambda, w_br_pool, w_br_lru, w_out, norm_ffn,
           w_router, b_router, w_gu, b_gu, w_dn, b_dn, norm_final):
    bp, tp, d = x_prompt.shape
    bs, ts, _ = x_sample.shape
    n_p, n_s = bp * tp, bs * ts
    mw = _mixer_weights(norm_mix[0], w_in[0], b_gate[0], pool_w[0], pool_scale[0], conv_w[0], conv_b[0],
                        lru_wa[0], lru_ba[0], lru_wi[0], lru_bi[0], lru_lambda[0], w_br_pool[0], w_br_lru[0],
                        w_out[0])
    w_router_t = w_router[0].T.astype(BF16)
    zeros = lambda *shape: jnp.zeros(shape, x_prompt.dtype)

    assert sum(PROMPT_GROUP_SEQS) == bp and tp % TOKEN_TILE == 0 and n_s % TOKEN_TILE == 0
    seq_tiles = tp // TOKEN_TILE
    s_tiles = n_s // TOKEN_TILE
    x1_s, pool_s, conv_s, lru_s = _run_mixer(
        x_sample, state_pool[0], state_conv[0], state_lru[0], mw, start_pos=PAST_LEN, bblk=bs, tt=ts)
    y_p, y_s = None, None
    expert_w = (w_gu[0], w_dn[0])
    pools, convs, lrus = [], [], []
    seq0 = 0
    for g, seqs in enumerate(PROMPT_GROUP_SEQS):
        x1_g, pool_g, conv_g, lru_g = _run_mixer(
            x_prompt, zeros(seqs, POOL_HIST, d), zeros(seqs, CONV_HIST, d), zeros(seqs, d), mw,
            start_pos=0, bblk=1, tt=MIXER_TILE, seq0=seq0)
        pools.append(pool_g)
        convs.append(conv_g)
        lrus.append(lru_g)
        last = g == len(PROMPT_GROUP_SEQS) - 1
        g_tiles = seqs * seq_tiles
        parts, windows = [x1_g.reshape(seqs * tp, d)], [_Window(0, g_tiles)]
        out_shapes, out_windows, out_init = [(n_p, d)], [_Window(seq0 * seq_tiles, g_tiles)], [y_p]
        seq0 += seqs
        if last:
            parts.append(x1_s.reshape(n_s, d))
            windows.append(_Window(0, s_tiles))
            out_shapes.append((n_s, d))
            out_windows.append(_Window(0, s_tiles))
            out_init.append(None)
        yk, topw, expert_w = _moe_rows(parts, windows, norm_ffn[0], w_router_t, b_router[0], expert_w[0],
                                       b_gu[0], expert_w[1], b_dn[0])
        outs = _final_call(parts, windows, yk, topw.T, norm_final, out_shapes, out_windows, out_init,
                           tb=TOKEN_TILE)
        y_p = outs[0]
        if last:
            y_s = outs[1]

    cat = lambda xs: jnp.concatenate(xs, axis=0)[None]
    return (y_p.reshape(bp, tp, d), y_s.reshape(bs, ts, d), cat(pools), cat(convs), cat(lrus),
            pool_s[None], conv_s[None], lru_s[None])
```

```python
import functools
from typing import NamedTuple

import jax
import jax.numpy as jnp
from jax import lax
from jax.experimental import pallas as pl
from jax.experimental.pallas import tpu as pltpu
from jax.experimental.pallas import tpu_sc as plsc

BF16 = jnp.bfloat16
F32 = jnp.float32

POOL_WINDOWS = (2, 4, 8, 16)
POOL_HIST = max(POOL_WINDOWS) - 1
CONV_WIDTH = 4
CONV_HIST = CONV_WIDTH - 1
LRU_HEADS = 16
LRU_C = 8.0
LRU_SEGMENTS = 8
N_EXPERTS = 32
TOP_K = 4
SWIGLU_LIMIT = 7.0
SWIGLU_ALPHA = 1.702
NORM_EPS = 1e-6

SUBLANES = 8
LANES = 128
MXU_WIDTH = 256
POOL_PAD = 16
CONV_PAD = 8
VMEM_LIMIT_BYTES = 56 * 1024 * 1024

PAST_LEN = 1024
MIXER_TILE = 256
TOKEN_TILE = 512
EXPERT_TILE = 1024
EXPERT_SUBTILE = 512
PROMPT_GROUP_SEQS = (12, 4)

SC_CORES = 2
SC_SUBCORES = 16
SC_WORKERS = SC_CORES * SC_SUBCORES
SC_MAX_STEP_ROWS = 64


def _rms_norm(x, g):
    ms = jnp.mean(x * x, axis=-1, keepdims=True)
    return (x * lax.rsqrt(ms + NORM_EPS)) * g


def _dot(a, b):
    return jnp.dot(a, b, preferred_element_type=F32)


def _softplus(x):
    return jnp.maximum(x, 0.0) + jnp.log1p(jnp.exp(-jnp.abs(x)))


def _gelu_tanh(x):
    c = 0.7978845608028654
    return 0.5 * x * (1.0 + jnp.tanh(c * (x + 0.044715 * (x * x * x))))


def _lru_scan(a_ref, b_ref, h_ref, h0, frames, chunks):
    tile = lambda ref, t: ref[t * chunks:(t + 1) * chunks, :]
    seg = frames // LRU_SEGMENTS
    assert seg * LRU_SEGMENTS == frames
    prods = [tile(a_ref, s * seg) for s in range(LRU_SEGMENTS)]
    sums = [tile(b_ref, s * seg) for s in range(LRU_SEGMENTS)]
    for t in range(1, seg):
        for s in range(LRU_SEGMENTS):
            a_t = tile(a_ref, s * seg + t)
            sums[s] = a_t * sums[s] + tile(b_ref, s * seg + t)
            prods[s] = a_t * prods[s]
            b_ref[(s * seg + t) * chunks:(s * seg + t + 1) * chunks, :] = sums[s]
            a_ref[(s * seg + t) * chunks:(s * seg + t + 1) * chunks, :] = prods[s]
    h_in = [h0]
    for s in range(LRU_SEGMENTS):
        h_in.append(prods[s] * h_in[s] + sums[s])
    for s in range(LRU_SEGMENTS):
        for t in range(seg):
            f = s * seg + t
            h_ref[f * chunks:(f + 1) * chunks, :] = tile(a_ref, f) * h_in[s] + tile(b_ref, f)
    return h_in[LRU_SEGMENTS]


def _mixer_kernel(x_ref, hp_ref, hc_ref, hl_ref, nmix_ref, win_ref, bgate_ref, poolw_ref, pscale_ref,
                  convw_ref, convb_ref, wai_ref, ba_ref, bi_ref, lam_ref, wbrp_ref, wbrl_ref, wout_ref,
                  x1_ref, npool_ref, nconv_ref, nlru_ref,
                  pool_ext, conv_ext, h_carry, frames_a, frames_b, frames_h, *, start_pos, bblk, tt, d):
    j = pl.program_id(1)

    @pl.when(j == 0)
    def _():
        pool_ext[:, 0:POOL_PAD, :] = hp_ref[...]
        conv_ext[:, 0:CONV_PAD * (d // LANES), :] = hc_ref[...]
        h_carry[...] = hl_ref[...]

    rows = bblk * tt
    chunks = d // LANES
    cat_rows = lambda parts: jnp.concatenate(parts, axis=0) if len(parts) > 1 else parts[0]
    x = x_ref[...].reshape(rows, d)
    xnb = _rms_norm(x, nmix_ref[...]).astype(BF16)

    u_x = _dot(xnb, win_ref[:, d:2 * d])
    xc_rows = []
    for b in range(bblk):
        _store_rows(conv_ext.at[b, pl.ds(CONV_PAD * chunks, tt * chunks)], u_x[b * tt:(b + 1) * tt, :], tt, chunks)
        ce = conv_ext[b].reshape(CONV_PAD + tt, chunks, LANES)
        y = convb_ref[...] + ce[CONV_PAD:CONV_PAD + tt] * convw_ref[CONV_WIDTH - 1]
        for k in range(1, CONV_WIDTH):
            y = y + ce[CONV_PAD - k:CONV_PAD - k + tt] * convw_ref[CONV_WIDTH - 1 - k]
        frames_h[...] = y.reshape(tt * chunks, LANES)
        xc_rows.append(_load_rows(frames_h, tt, chunks))
        nconv_ref[b] = conv_ext[b, (CONV_PAD + tt - CONV_HIST) * chunks:(CONV_PAD + tt) * chunks, :]
        conv_ext[b, 0:CONV_PAD * chunks, :] = conv_ext[b, tt * chunks:(tt + CONV_PAD) * chunks, :]
    xc = cat_rows(xc_rows)

    n_chunks = d // MXU_WIDTH
    pre_a, pre_i = [], []
    for c in range(n_chunks):
        ai = _dot(xc[:, c * MXU_WIDTH:(c + 1) * MXU_WIDTH].astype(BF16), wai_ref[c])
        pre_a.append(ai[:, 0:MXU_WIDTH])
        pre_i.append(ai[:, MXU_WIDTH:2 * MXU_WIDTH])

    u_pool = _dot(xnb, win_ref[:, 0:d])
    pos1 = lax.broadcasted_iota(jnp.int32, (tt, 1), 0) + (start_pos + 1) + j * tt
    gd = d // len(POOL_WINDOWS)
    y_pool_rows = []
    for b in range(bblk):
        pool_ext[b, POOL_PAD:POOL_PAD + tt, :] = u_pool[b * tt:(b + 1) * tt, :]
        parts = []
        for g, w in enumerate(POOL_WINDOWS):
            sl = slice(g * gd, (g + 1) * gd)
            s = pool_ext[b, :, sl]
            width = 1
            while width < w:
                s = s + pltpu.roll(s, width, axis=0)
                width *= 2
            cur = s[POOL_PAD:POOL_PAD + tt, :]
            inv_cnt = 1.0 / jnp.minimum(pos1, w).astype(F32)
            pooled = cur * inv_cnt - u_pool[b * tt:(b + 1) * tt, sl]
            mixed = _dot(pooled.astype(BF16), poolw_ref[g])
            parts.append(mixed * pscale_ref[:, sl])
        y_pool_rows.append(jnp.concatenate(parts, axis=1))
        npool_ref[b] = pool_ext[b, POOL_PAD + tt - POOL_HIST:POOL_PAD + tt, :]
        pool_ext[b, 0:POOL_PAD, :] = pool_ext[b, tt:tt + POOL_PAD, :]
    y_pool = cat_rows(y_pool_rows)

    r_gate = jax.nn.sigmoid(jnp.concatenate(pre_a, axis=1) + ba_ref[...])
    i_gate = jax.nn.sigmoid(jnp.concatenate(pre_i, axis=1) + bi_ref[...])
    log_a = (-LRU_C) * r_gate * _softplus(-lam_ref[...])
    a = jnp.exp(log_a)
    mult = jnp.sqrt(-jnp.tanh(log_a) * (a * a + 1.0))
    bb = mult * (i_gate * xc)
    gl_pool = _dot(xnb, win_ref[:, 3 * d:4 * d])
    u_gate = _dot(xnb, win_ref[:, 2 * d:3 * d])
    h_rows = []
    for b in range(bblk):
        _store_rows(frames_a, a[b * tt:(b + 1) * tt, :], tt, chunks)
        _store_rows(frames_b, bb[b * tt:(b + 1) * tt, :], tt, chunks)
        h_last = _lru_scan(frames_a, frames_b, frames_h, h_carry[b], tt, chunks)
        h_carry[b] = h_last
        nlru_ref[b] = h_last
        h_rows.append(_load_rows(frames_h, tt, chunks))
    h = cat_rows(h_rows)
    br_pool = _dot(y_pool.astype(BF16), wbrp_ref[...])
    gl_lru = _dot(xnb, win_ref[:, 4 * d:5 * d])

    y_lru = h * _gelu_tanh(u_gate)
    br_lru = _dot(y_lru.astype(BF16), wbrl_ref[...])
    g_pool = jax.nn.sigmoid(gl_pool + bgate_ref[:, 0:d])
    g_lru = jax.nn.sigmoid(gl_lru + bgate_ref[:, d:2 * d])
    acc = g_pool * br_pool + g_lru * br_lru
    out = _dot(acc.astype(BF16), wout_ref[...])
    x1_ref[...] = (x + out).reshape(bblk, tt, d)


def _const_spec(shape):
    nd = len(shape)
    return pl.BlockSpec(shape, lambda *_: (0,) * nd, pipeline_mode=pl.Buffered(1))


def _mixer_call(x, hist_pool, hist_conv, hist_lru, weights, *, start_pos, bblk, tt, seq0=0):
    _, t_len, d = x.shape
    bsz = hist_pool.shape[0]
    assert seq0 % bblk == 0
    grid = (bsz // bblk, t_len // tt)
    kern = functools.partial(_mixer_kernel, start_pos=start_pos, bblk=bblk, tt=tt, d=d)
    chunks = d // LANES
    seq_spec = lambda rows: pl.BlockSpec((bblk, rows, d), lambda b, j: (b, 0, 0))
    frame_spec = lambda frames: pl.BlockSpec((bblk, frames * chunks, LANES), lambda b, j: (b, 0, 0))
    in_specs = [
        pl.BlockSpec((bblk, tt, d), lambda b, j: (b + seq0 // bblk, j, 0)),
        seq_spec(POOL_PAD), frame_spec(CONV_PAD), frame_spec(1),
    ] + [_const_spec(w.shape) for w in weights]
    out_shape = (
        jax.ShapeDtypeStruct((bsz, t_len, d), F32),
        jax.ShapeDtypeStruct((bsz, POOL_HIST, d), F32),
        jax.ShapeDtypeStruct((bsz, CONV_HIST * chunks, LANES), F32),
        jax.ShapeDtypeStruct((bsz, chunks, LANES), F32),
    )
    out_specs = (
        pl.BlockSpec((bblk, tt, d), lambda b, j: (b, j, 0)),
        seq_spec(POOL_HIST), frame_spec(CONV_HIST), frame_spec(1),
    )
    return pl.pallas_call(
        kern,
        grid=grid,
        in_specs=in_specs,
        out_specs=out_specs,
        out_shape=out_shape,
        scratch_shapes=[
            pltpu.VMEM((bblk, POOL_PAD + tt, d), F32),
            pltpu.VMEM((bblk, (CONV_PAD + tt) * chunks, LANES), F32),
            pltpu.VMEM((bblk, chunks, LANES), F32),
            pltpu.VMEM((tt * chunks, LANES), F32),
            pltpu.VMEM((tt * chunks, LANES), F32),
            pltpu.VMEM((tt * chunks, LANES), F32),
        ],
        compiler_params=pltpu.CompilerParams(
            dimension_semantics=("arbitrary", "arbitrary"), vmem_limit_bytes=VMEM_LIMIT_BYTES),
        name="mixer",
    )(x, hist_pool, hist_conv, hist_lru, *weights)


def _load_rows(ref, rows, chunks):
    return jnp.concatenate([ref[pl.ds(s, rows, stride=chunks), :] for s in range(chunks)], axis=1)


def _store_rows(ref, val, rows, chunks):
    for s in range(chunks):
        ref[pl.ds(s, rows, stride=chunks), :] = val[:, s * LANES:(s + 1) * LANES]


def _pack_halves(x):
    half = x.shape[1] // 2
    bits = lax.bitcast_convert_type(x, jnp.uint32)
    return (bits[:, :half] >> 16) | (bits[:, half:] & jnp.uint32(0xFFFF0000))


def _unpack_halves(p):
    lo = lax.bitcast_convert_type(p << 16, F32)
    hi = lax.bitcast_convert_type(p & jnp.uint32(0xFFFF0000), F32)
    return jnp.concatenate([lo, hi], axis=1)


class _Window(NamedTuple):
    first: int
    count: int


def _window_specs(windows, tb, d):
    specs, start = [], 0
    for win in windows:
        specs.append(pl.BlockSpec(
            (tb, d), lambda i, win=win, start=start: (win.first + jnp.clip(i - start, 0, win.count - 1), 0)))
        start += win.count
    return specs


def _window_load(i, windows, refs):
    val, start = refs[-1][...], sum(w.count for w in windows[:-1])
    for win, ref in zip(reversed(windows[:-1]), reversed(refs[:-1])):
        val = jnp.where(i < start, ref[...], val)
        start -= win.count
    return val


def _window_store(i, windows, refs, val):
    start = 0
    for win, ref in zip(windows, refs):
        @pl.when((i >= start) & (i < start + win.count))
        def _(ref=ref):
            ref[...] = val
        start += win.count


def _router_kernel(*refs, tb, d, windows):
    x1_refs = refs[:len(windows)]
    (nffn_ref, wrt_ref, br_ref, tri_ref, xn_ref, topi_ref, topw_ref, rank_ref, cnt_ref, carry) = refs[len(windows):]
    i = pl.program_id(0)

    @pl.when(i == 0)
    def _():
        carry[...] = jnp.zeros_like(carry)

    xnb = _rms_norm(_window_load(i, windows, x1_refs), nffn_ref[...]).astype(BF16)
    xn_ref_chunks = d // (2 * LANES)
    _store_rows(xn_ref, _pack_halves(xnb.astype(F32)), tb, xn_ref_chunks)
    logits = lax.dot_general(wrt_ref[...], xnb, (((1,), (1,)), ((), ())),
                             preferred_element_type=F32) + br_ref[...]
    n_exp = logits.shape[0]
    iota_e = lax.broadcasted_iota(jnp.int32, (n_exp, tb), 0)
    work = logits
    vals, idxs, hots = [], [], []
    for _ in range(TOP_K):
        m = jnp.max(work, axis=0, keepdims=True)
        idx = jnp.min(jnp.where(work == m, iota_e, n_exp), axis=0, keepdims=True)
        hot = iota_e == idx
        vals.append(m)
        idxs.append(idx)
        hots.append(hot)
        work = jnp.where(hot, -jnp.inf, work)
    exps = [jnp.exp(v - vals[0]) for v in vals]
    denom = exps[0] + exps[1] + exps[2] + exps[3]
    topi_ref[...] = jnp.concatenate(idxs, axis=0)
    topw_ref[...] = jnp.concatenate([e / denom for e in exps], axis=0)

    sel = sum(jnp.where(h, 1.0, 0.0) for h in hots)
    before = _dot(sel.astype(BF16), tri_ref[...]) + carry[:, 0:1]
    ranks = [jnp.sum(jnp.where(h, before, 0.0), axis=0, keepdims=True) for h in hots]
    rank_ref[...] = jnp.concatenate(ranks, axis=0).astype(jnp.int32)
    carry[...] = carry[...] + jnp.sum(sel, axis=1, keepdims=True)
    cnt_ref[...] = carry[...]


def _router_call(x1_parts, windows, norm_ffn, w_router_t, b_router, *, tb):
    d = x1_parts[0].shape[1]
    n_tiles = sum(w.count for w in windows)
    n = n_tiles * tb
    chunks = d // (2 * LANES)
    n_exp = w_router_t.shape[0]
    tri = jnp.triu(jnp.ones((tb, tb), BF16), k=1)
    tok = lambda rows, dt: (pl.BlockSpec((rows, tb), lambda i: (0, i)), jax.ShapeDtypeStruct((rows, n), dt))
    (topi_spec, topi_shape), (topw_spec, topw_shape), (rank_spec, rank_shape) = (
        tok(TOP_K, jnp.int32), tok(TOP_K, F32), tok(TOP_K, jnp.int32))
    return pl.pallas_call(
        functools.partial(_router_kernel, tb=tb, d=d, windows=tuple(windows)),
        grid=(n_tiles,),
        in_specs=[
            *_window_specs(windows, tb, d),
            _const_spec((1, d)), _const_spec((n_exp, d)), _const_spec((n_exp, 1)), _const_spec((tb, tb)),
        ],
        out_specs=(
            pl.BlockSpec((tb * chunks, LANES), lambda i: (i, 0)),
            topi_spec, topw_spec, rank_spec,
            pl.BlockSpec((n_exp, LANES), lambda i: (0, 0)),
        ),
        out_shape=(
            jax.ShapeDtypeStruct((n * chunks, LANES), jnp.uint32),
            topi_shape, topw_shape, rank_shape,
            jax.ShapeDtypeStruct((n_exp, LANES), F32),
        ),
        scratch_shapes=[pltpu.VMEM((n_exp, LANES), F32)],
        compiler_params=pltpu.CompilerParams(
            dimension_semantics=("arbitrary",), vmem_limit_bytes=VMEM_LIMIT_BYTES),
        name="router",
    )(*x1_parts, norm_ffn.reshape(1, d), w_router_t, b_router.reshape(n_exp, 1), tri)


def _expert_kernel(te_ref, tv_ref, nu_ref, xs_ref, wgu_ref, bgu_ref, wdn_ref, bdn_ref, ys_ref, *bf_refs,
                   tm, d, dff, round_weights):
    del nu_ref
    i = pl.program_id(0)
    if round_weights:
        wgu_bf, wdn_bf = bf_refs
        prev = te_ref[jnp.maximum(i - 1, 0)]

        @pl.when((i == 0) | (te_ref[i] != prev))
        def _():
            wgu_bf[...] = wgu_ref[...].astype(BF16)
            wdn_bf[...] = wdn_ref[...].astype(BF16)
    else:
        wgu_bf, wdn_bf = wgu_ref, wdn_ref

    xchunks = d // (2 * LANES)

    def ffn(row0, rows):
        window = pl.ds(row0 * xchunks, rows * xchunks)
        x = _unpack_halves(_load_rows(xs_ref.at[window], rows, xchunks)).astype(BF16)
        gu = _dot(x, wgu_bf[0]) + bgu_ref[0]
        gate = jnp.minimum(gu[:, :dff], SWIGLU_LIMIT)
        up = jnp.clip(gu[:, dff:], -SWIGLU_LIMIT, SWIGLU_LIMIT)
        hid = (up + 1.0) * gate * jax.nn.sigmoid(SWIGLU_ALPHA * gate)
        y = _dot(hid.astype(BF16), wdn_bf[0]) + bdn_ref[0]
        _store_rows(ys_ref.at[window], _pack_halves(y.astype(BF16).astype(F32)), rows, xchunks)

    sub = EXPERT_SUBTILE
    for s in range(tm // sub):
        valid = tv_ref[i] - s * sub
        pl.when(valid > sub // 2)(functools.partial(ffn, s * sub, sub))
        pl.when((valid > 0) & (valid <= sub // 2))(functools.partial(ffn, s * sub, sub // 2))


def _expert_call(tile_expert, tile_rows, n_used, xs, w_gu, b_gu, w_dn, b_dn, *, tm):
    n_exp, d, dff2 = w_gu.shape
    dff = dff2 // 2
    xchunks = d // (2 * LANES)
    p_rows = xs.shape[0] // xchunks
    round_weights = w_gu.dtype != BF16
    row_blk = lambda i, te, tv, nu: (jnp.minimum(i, nu[0] - 1), 0)
    exp_blk = lambda i, te, tv, nu: (te[i], 0, 0)
    out_specs = [pl.BlockSpec((tm * xchunks, LANES), row_blk)]
    out_shape = [jax.ShapeDtypeStruct((p_rows * xchunks, LANES), jnp.uint32)]
    if round_weights:
        out_specs += [pl.BlockSpec((1, d, dff2), exp_blk), pl.BlockSpec((1, dff, d), exp_blk)]
        out_shape += [jax.ShapeDtypeStruct(w_gu.shape, BF16), jax.ShapeDtypeStruct(w_dn.shape, BF16)]
    return pl.pallas_call(
        functools.partial(_expert_kernel, tm=tm, d=d, dff=dff, round_weights=round_weights),
        grid_spec=pltpu.PrefetchScalarGridSpec(
            num_scalar_prefetch=3,
            grid=(p_rows // tm,),
            in_specs=[
                pl.BlockSpec((tm * xchunks, LANES), row_blk),
                pl.BlockSpec((1, d, dff2), exp_blk),
                pl.BlockSpec((1, 1, dff2), exp_blk),
                pl.BlockSpec((1, dff, d), exp_blk),
                pl.BlockSpec((1, 1, d), exp_blk),
            ],
            out_specs=out_specs,
        ),
        out_shape=out_shape,
        compiler_params=pltpu.CompilerParams(
            dimension_semantics=("arbitrary",), vmem_limit_bytes=VMEM_LIMIT_BYTES),
        name="experts",
    )(tile_expert, tile_rows, n_used, xs, w_gu, b_gu.reshape(n_exp, 1, dff2), w_dn, b_dn.reshape(n_exp, 1, d))


def _final_kernel(*refs, tb, d, in_windows, out_windows, n_alias):
    n_in, n_out = len(in_windows), len(out_windows)
    x1_refs = refs[:n_in]
    yk_ref, w_ref, nfin_ref = refs[n_in:n_in + 3]
    out_refs = refs[n_in + 3 + n_alias:n_in + 3 + n_alias + n_out]
    i = pl.program_id(0)
    w = w_ref[...]
    chunks = d // (2 * LANES)
    moe = _unpack_halves(_load_rows(yk_ref.at[0], tb, chunks)) * w[:, 0:1]
    for k in range(1, TOP_K):
        moe = moe + _unpack_halves(_load_rows(yk_ref.at[k], tb, chunks)) * w[:, k:k + 1]
    out = _rms_norm(_window_load(i, in_windows, x1_refs) + moe, nfin_ref[...])
    _window_store(i, out_windows, out_refs, out)


def _final_call(x1_parts, in_windows, yk, topw_t, norm_final, out_shapes, out_windows, out_init, *, tb):
    d = x1_parts[0].shape[1]
    n_tiles = sum(w.count for w in in_windows)
    chunks = d // (2 * LANES)
    alias_in = [a for a in out_init if a is not None]
    n_fixed = len(x1_parts) + 3
    aliases, j = {}, 0
    for k, a in enumerate(out_init):
        if a is not None:
            aliases[n_fixed + j] = k
            j += 1
    return pl.pallas_call(
        functools.partial(_final_kernel, tb=tb, d=d, in_windows=tuple(in_windows),
                          out_windows=tuple(out_windows), n_alias=len(alias_in)),
        grid=(n_tiles,),
        in_specs=[
            *_window_specs(in_windows, tb, d),
            pl.BlockSpec((TOP_K, tb * chunks, LANES), lambda i: (0, i, 0)),
            pl.BlockSpec((tb, TOP_K), lambda i: (i, 0)),
            _const_spec((1, d)),
            *[pl.BlockSpec(memory_space=pl.ANY) for _ in alias_in],
        ],
        out_specs=_window_specs(out_windows, tb, d),
        out_shape=[jax.ShapeDtypeStruct(s, F32) for s in out_shapes],
        input_output_aliases=aliases,
        compiler_params=pltpu.CompilerParams(
            dimension_semantics=("arbitrary",), vmem_limit_bytes=VMEM_LIMIT_BYTES),
        name="final",
    )(*x1_parts, yk, topw_t, norm_final.reshape(1, d), *alias_in)


def _sc_step_rows(n):
    per_worker = n // SC_WORKERS
    assert per_worker * SC_WORKERS == n
    w = max(c for c in range(SUBLANES, SC_MAX_STEP_ROWS + 1, SUBLANES) if per_worker % c == 0)
    return w, per_worker // w


def _sc_mesh():
    return plsc.VectorSubcoreMesh(core_axis_name="c", subcore_axis_name="s",
                                  num_cores=SC_CORES, num_subcores=SC_SUBCORES)


def _dispatch_rows(x_tiles, pos_steps, p_rows):
    n, chunks, _ = x_tiles.shape
    w, steps = _sc_step_rows(n)

    @functools.partial(
        pl.kernel, mesh=_sc_mesh(),
        out_type=jax.ShapeDtypeStruct((p_rows, chunks, LANES), x_tiles.dtype),
        scratch_types=[pltpu.VMEM((w, chunks, LANES), x_tiles.dtype), pltpu.VMEM((TOP_K, w), jnp.int32)],
        name="dispatch",
    )
    def run(x_hbm, pos_hbm, out_hbm, buf, idx):
        wid = lax.axis_index("s") * SC_CORES + lax.axis_index("c")

        @pl.loop(0, steps)
        def _(i):
            step = wid * steps + i
            pltpu.sync_copy(x_hbm.at[pl.ds(pl.multiple_of(step * w, SUBLANES), w)], buf)
            pltpu.sync_copy(pos_hbm.at[step], idx)
            for k in range(TOP_K):
                pltpu.sync_copy(buf, out_hbm.at[idx.at[k]])

    return run(x_tiles, pos_steps)


def _combine_rows(y_tiles, pos_steps, n):
    _, chunks, _ = y_tiles.shape
    w, steps = _sc_step_rows(n)

    @functools.partial(
        pl.kernel, mesh=_sc_mesh(),
        out_type=jax.ShapeDtypeStruct((TOP_K, n, chunks, LANES), y_tiles.dtype),
        scratch_types=[pltpu.VMEM((w, chunks, LANES), y_tiles.dtype), pltpu.VMEM((TOP_K, w), jnp.int32)],
        name="combine",
    )
    def run(y_hbm, pos_hbm, out_hbm, buf, idx):
        wid = lax.axis_index("s") * SC_CORES + lax.axis_index("c")

        @pl.loop(0, steps)
        def _(i):
            step = wid * steps + i
            pltpu.sync_copy(pos_hbm.at[step], idx)
            for k in range(TOP_K):
                pltpu.sync_copy(y_hbm.at[idx.at[k]], buf)
                pltpu.sync_copy(buf, out_hbm.at[k, pl.ds(pl.multiple_of(step * w, SUBLANES), w)])

    return run(y_tiles, pos_steps)


def _block_diag(w, per_block):
    heads, n, _ = w.shape
    w4 = w.reshape(heads // per_block, per_block, n, n)
    eye = jnp.eye(per_block, dtype=w.dtype)
    bd = jnp.einsum('chij,hg->chigj', w4, eye)
    return bd.reshape(heads // per_block, per_block * n, per_block * n)


def _mixer_weights(norm_mix, w_in, b_gate, pool_w, pool_scale, conv_w, conv_b, lru_wa, lru_ba, lru_wi,
                   lru_bi, lru_lambda, w_br_pool, w_br_lru, w_out):
    row = lambda v: v.reshape(1, -1)
    d = norm_mix.shape[0]
    head_dim = lru_wa.shape[-1]
    per_block = MXU_WIDTH // head_dim
    w_ai = jnp.concatenate([_block_diag(lru_wa, per_block), _block_diag(lru_wi, per_block)], axis=-1)
    return (row(norm_mix), w_in.astype(BF16), row(b_gate), pool_w.astype(BF16), row(pool_scale),
            conv_w.reshape(-1, d // LANES, LANES), conv_b.reshape(d // LANES, LANES), w_ai.astype(BF16),
            row(lru_ba), row(lru_bi), row(lru_lambda),
            w_br_pool.astype(BF16), w_br_lru.astype(BF16), w_out.astype(BF16))


def _run_mixer(x, state_pool, state_conv, state_lru, weights, *, start_pos, bblk, tt, seq0=0):
    bsz, d = state_lru.shape
    chunks = d // LANES
    hp = jnp.pad(state_pool, ((0, 0), (POOL_PAD - POOL_HIST, 0), (0, 0)))
    hc = jnp.pad(state_conv, ((0, 0), (CONV_PAD - CONV_HIST, 0), (0, 0))).reshape(bsz, CONV_PAD * chunks, LANES)
    hl = state_lru.reshape(bsz, chunks, LANES)
    x1, npool, nconv, nlru = _mixer_call(x, hp, hc, hl, weights, start_pos=start_pos, bblk=bblk, tt=tt,
                                         seq0=seq0)
    return x1, npool, nconv.reshape(bsz, CONV_HIST, d), nlru.reshape(bsz, d)


def _routing_plan(topi, rank, counts, *, tm, n_tiles, min_tiles):
    n_exp = counts.shape[0]
    tiles_per = jnp.maximum((counts + tm - 1) // tm, min_tiles)
    tiles_cum = jnp.cumsum(tiles_per)
    tile_start = tiles_cum - tiles_per
    base = tile_start * tm
    experts = jnp.arange(n_exp, dtype=jnp.int32)[:, None, None]
    pos = rank + jnp.sum(jnp.where(topi[None] == experts, base[:, None, None], 0), axis=0)
    n_used = tiles_cum[-1]
    tile_ids = jnp.arange(n_tiles, dtype=jnp.int32)
    live = tile_ids < n_used
    owner = (tiles_cum[None, :] <= jnp.minimum(tile_ids, n_used - 1)[:, None])
    tile_expert = jnp.sum(owner.astype(jnp.int32), axis=1)
    is_owner = tile_expert[:, None] == jnp.arange(n_exp, dtype=jnp.int32)[None, :]
    rows_left = jnp.sum(jnp.where(is_owner, counts[None, :] - (tile_ids[:, None] - tile_start[None, :]) * tm, 0),
                        axis=1)
    tile_rows = jnp.where(live, jnp.clip(rows_left, 0, tm), 0).astype(jnp.int32)
    return pos, tile_expert, tile_rows, n_used.reshape(1).astype(jnp.int32)


def _moe_rows(x1_parts, windows, norm_ffn, w_router_t, b_router, w_gu, b_gu, w_dn, b_dn):
    d = x1_parts[0].shape[1]
    chunks = d // LANES
    n_exp = w_router_t.shape[0]
    n = sum(w.count for w in windows) * TOKEN_TILE
    xn2, topi, topw, rank, cnt = _router_call(x1_parts, windows, norm_ffn, w_router_t, b_router, tb=TOKEN_TILE)
    min_tiles = 0 if w_gu.dtype == BF16 else 1
    n_tiles = (n * TOP_K + n_exp * (EXPERT_TILE - 1)) // EXPERT_TILE + n_exp * min_tiles
    p_rows = n_tiles * EXPERT_TILE
    pos, tile_expert, tile_rows, n_used = _routing_plan(
        topi, rank, cnt[:, 0].astype(jnp.int32), tm=EXPERT_TILE, n_tiles=n_tiles, min_tiles=min_tiles)
    w, steps = _sc_step_rows(n)
    pos_steps = pos.reshape(TOP_K, SC_WORKERS * steps, w).transpose(1, 0, 2)
    xchunks = d // (2 * LANES)
    xs = _dispatch_rows(xn2.reshape(n, xchunks, LANES), pos_steps, p_rows)
    ys, *w_bf = _expert_call(tile_expert, tile_rows, n_used, xs.reshape(p_rows * xchunks, LANES), w_gu, b_gu,
                             w_dn, b_dn, tm=EXPERT_TILE)
    yk = _combine_rows(ys.reshape(p_rows, xchunks, LANES), pos_steps, n)
    return yk.reshape(TOP_K, n * xchunks, LANES), topw, (w_bf if w_bf else (w_gu, w_dn))


def kernel(x_prompt, x_sample, state_pool, state_conv, state_lru, norm_mix, w_in, b_gate, pool_w, pool_scale,
           conv_w, conv_b, lru_wa, lru_ba, lru_wi, lru_bi, lru_lambda, w_br_pool, w_br_lru, w_out, norm_ffn,
           w_router, b_router, w_gu, b_gu, w_dn, b_dn, norm_final):
    bp, tp, d = x_prompt.shape
    bs, ts, _ = x_sample.shape
    n_p, n_s = bp * tp, bs * ts
    mw = _mixer_weights(norm_mix[0], w_in[0], b_gate[0], pool_w[0], pool_scale[0], conv_w[0], conv_b[0],
                        lru_wa[0], lru_ba[0], lru_wi[0], lru_bi[0], lru_lambda[0], w_br_pool[0], w_br_lru[0],
                        w_out[0])
    w_router_t = w_router[0].T.astype(BF16)
    zeros = lambda *shape: jnp.zeros(shape, x_prompt.dtype)

    assert sum(PROMPT_GROUP_SEQS) == bp and tp % TOKEN_TILE == 0 and n_s % TOKEN_TILE == 0
    seq_tiles = tp // TOKEN_TILE
    s_tiles = n_s // TOKEN_TILE
    x1_s, pool_s, conv_s, lru_s = _run_mixer(
        x_sample, state_pool[0], state_conv[0], state_lru[0], mw, start_pos=PAST_LEN, bblk=bs, tt=ts)
    y_p, y_s = None, None
    expert_w = (w_gu[0], w_dn[0])
    pools, convs, lrus = [], [], []
    seq0 = 0
    for g, seqs in enumerate(PROMPT_GROUP_SEQS):
        x1_g, pool_g, conv_g, lru_g = _run_mixer(
            x_prompt, zeros(seqs, POOL_HIST, d), zeros(seqs, CONV_HIST, d), zeros(seqs, d), mw,
            start_pos=0, bblk=1, tt=MIXER_TILE, seq0=seq0)
        pools.append(pool_g)
        convs.append(conv_g)
        lrus.append(lru_g)
        last = g == len(PROMPT_GROUP_SEQS) - 1
        g_tiles = seqs * seq_tiles
        parts, windows = [x1_g.reshape(seqs * tp, d)], [_Window(0, g_tiles)]
        out_shapes, out_windows, out_init = [(n_p, d)], [_Window(seq0 * seq_tiles, g_tiles)], [y_p]
        seq0 += seqs
        if last:
            parts.append(x1_s.reshape(n_s, d))
            windows.append(_Window(0, s_tiles))
            out_shapes.append((n_s, d))
            out_windows.append(_Window(0, s_tiles))
            out_init.append(None)
        yk, topw, expert_w = _moe_rows(parts, windows, norm_ffn[0], w_router_t, b_router[0], expert_w[0],
                                       b_gu[0], expert_w[1], b_dn[0])
        outs = _final_call(parts, windows, yk, topw.T, norm_final, out_shapes, out_windows, out_init,
                           tb=TOKEN_TILE)
        y_p = outs[0]
        if last:
            y_s = outs[1]

    cat = lambda xs: jnp.concatenate(xs, axis=0)[None]
    return (y_p.reshape(bp, tp, d), y_s.reshape(bs, ts, d), cat(pools), cat(convs), cat(lrus),
            pool_s[None], conv_s[None], lru_s[None])
```

```python
import functools
from typing import NamedTuple

import jax
import jax.numpy as jnp
from jax import lax
from jax.experimental import pallas as pl
from jax.experimental.pallas import tpu as pltpu
from jax.experimental.pallas import tpu_sc as plsc

BF16 = jnp.bfloat16
F32 = jnp.float32

POOL_WINDOWS = (2, 4, 8, 16)
POOL_HIST = max(POOL_WINDOWS) - 1
CONV_WIDTH = 4
CONV_HIST = CONV_WIDTH - 1
LRU_HEADS = 16
LRU_C = 8.0
LRU_SEGMENTS = 8
N_EXPERTS = 32
TOP_K = 4
SWIGLU_LIMIT = 7.0
SWIGLU_ALPHA = 1.702
NORM_EPS = 1e-6

SUBLANES = 8
LANES = 128
MXU_WIDTH = 256
POOL_PAD = 16
CONV_PAD = 8
VMEM_LIMIT_BYTES = 56 * 1024 * 1024

PAST_LEN = 1024
MIXER_TILE = 256
TOKEN_TILE = 512
EXPERT_TILE = 1024
EXPERT_SUBTILE = 512
PROMPT_GROUP_SEQS = (8, 8)

SC_CORES = 2
SC_SUBCORES = 16
SC_WORKERS = SC_CORES * SC_SUBCORES
SC_MAX_STEP_ROWS = 64


def _rms_norm(x, g):
    ms = jnp.mean(x * x, axis=-1, keepdims=True)
    return (x * lax.rsqrt(ms + NORM_EPS)) * g


def _dot(a, b):
    return jnp.dot(a, b, preferred_element_type=F32)


def _softplus(x):
    return jnp.maximum(x, 0.0) + jnp.log1p(jnp.exp(-jnp.abs(x)))


def _gelu_tanh(x):
    c = 0.7978845608028654
    return 0.5 * x * (1.0 + jnp.tanh(c * (x + 0.044715 * (x * x * x))))


def _lru_scan(a_ref, b_ref, h_ref, h0, frames, chunks):
    tile = lambda ref, t: ref[t * chunks:(t + 1) * chunks, :]
    seg = frames // LRU_SEGMENTS
    assert seg * LRU_SEGMENTS == frames
    prods = [tile(a_ref, s * seg) for s in range(LRU_SEGMENTS)]
    sums = [tile(b_ref, s * seg) for s in range(LRU_SEGMENTS)]
    for t in range(1, seg):
        for s in range(LRU_SEGMENTS):
            a_t = tile(a_ref, s * seg + t)
            sums[s] = a_t * sums[s] + tile(b_ref, s * seg + t)
            prods[s] = a_t * prods[s]
            b_ref[(s * seg + t) * chunks:(s * seg + t + 1) * chunks, :] = sums[s]
            a_ref[(s * seg + t) * chunks:(s * seg + t + 1) * chunks, :] = prods[s]
    h_in = [h0]
    for s in range(LRU_SEGMENTS):
        h_in.append(prods[s] * h_in[s] + sums[s])
    for s in range(LRU_SEGMENTS):
        for t in range(seg):
            f = s * seg + t
            h_ref[f * chunks:(f + 1) * chunks, :] = tile(a_ref, f) * h_in[s] + tile(b_ref, f)
    return h_in[LRU_SEGMENTS]


def _mixer_kernel(x_ref, hp_ref, hc_ref, hl_ref, nmix_ref, win_ref, bgate_ref, poolw_ref, pscale_ref,
                  convw_ref, convb_ref, wai_ref, ba_ref, bi_ref, lam_ref, wbrp_ref, wbrl_ref, wout_ref,
                  x1_ref, npool_ref, nconv_ref, nlru_ref,
                  pool_ext, conv_ext, h_carry, frames_a, frames_b, frames_h, *, start_pos, bblk, tt, d):
    j = pl.program_id(1)

    @pl.when(j == 0)
    def _():
        pool_ext[:, 0:POOL_PAD, :] = hp_ref[...]
        conv_ext[:, 0:CONV_PAD * (d // LANES), :] = hc_ref[...]
        h_carry[...] = hl_ref[...]

    rows = bblk * tt
    chunks = d // LANES
    cat_rows = lambda parts: jnp.concatenate(parts, axis=0) if len(parts) > 1 else parts[0]
    x = x_ref[...].reshape(rows, d)
    xnb = _rms_norm(x, nmix_ref[...]).astype(BF16)

    u_x = _dot(xnb, win_ref[:, d:2 * d])
    xc_rows = []
    for b in range(bblk):
        _store_rows(conv_ext.at[b, pl.ds(CONV_PAD * chunks, tt * chunks)], u_x[b * tt:(b + 1) * tt, :], tt, chunks)
        ce = conv_ext[b].reshape(CONV_PAD + tt, chunks, LANES)
        y = convb_ref[...] + ce[CONV_PAD:CONV_PAD + tt] * convw_ref[CONV_WIDTH - 1]
        for k in range(1, CONV_WIDTH):
            y = y + ce[CONV_PAD - k:CONV_PAD - k + tt] * convw_ref[CONV_WIDTH - 1 - k]
        frames_h[...] = y.reshape(tt * chunks, LANES)
        xc_rows.append(_load_rows(frames_h, tt, chunks))
        nconv_ref[b] = conv_ext[b, (CONV_PAD + tt - CONV_HIST) * chunks:(CONV_PAD + tt) * chunks, :]
        conv_ext[b, 0:CONV_PAD * chunks, :] = conv_ext[b, tt * chunks:(tt + CONV_PAD) * chunks, :]
    xc = cat_rows(xc_rows)

    n_chunks = d // MXU_WIDTH
    pre_a, pre_i = [], []
    for c in range(n_chunks):
        ai = _dot(xc[:, c * MXU_WIDTH:(c + 1) * MXU_WIDTH].astype(BF16), wai_ref[c])
        pre_a.append(ai[:, 0:MXU_WIDTH])
        pre_i.append(ai[:, MXU_WIDTH:2 * MXU_WIDTH])

    u_pool = _dot(xnb, win_ref[:, 0:d])
    pos1 = lax.broadcasted_iota(jnp.int32, (tt, 1), 0) + (start_pos + 1) + j * tt
    gd = d // len(POOL_WINDOWS)
    y_pool_rows = []
    for b in range(bblk):
        pool_ext[b, POOL_PAD:POOL_PAD + tt, :] = u_pool[b * tt:(b + 1) * tt, :]
        parts = []
        for g, w in enumerate(POOL_WINDOWS):
            sl = slice(g * gd, (g + 1) * gd)
            s = pool_ext[b, :, sl]
            width = 1
            while width < w:
                s = s + pltpu.roll(s, width, axis=0)
                width *= 2
            cur = s[POOL_PAD:POOL_PAD + tt, :]
            inv_cnt = 1.0 / jnp.minimum(pos1, w).astype(F32)
            pooled = cur * inv_cnt - u_pool[b * tt:(b + 1) * tt, sl]
            mixed = _dot(pooled.astype(BF16), poolw_ref[g])
            parts.append(mixed * pscale_ref[:, sl])
        y_pool_rows.append(jnp.concatenate(parts, axis=1))
        npool_ref[b] = pool_ext[b, POOL_PAD + tt - POOL_HIST:POOL_PAD + tt, :]
        pool_ext[b, 0:POOL_PAD, :] = pool_ext[b, tt:tt + POOL_PAD, :]
    y_pool = cat_rows(y_pool_rows)

    r_gate = jax.nn.sigmoid(jnp.concatenate(pre_a, axis=1) + ba_ref[...])
    i_gate = jax.nn.sigmoid(jnp.concatenate(pre_i, axis=1) + bi_ref[...])
    log_a = (-LRU_C) * r_gate * _softplus(-lam_ref[...])
    a = jnp.exp(log_a)
    mult = jnp.sqrt(-jnp.tanh(log_a) * (a * a + 1.0))
    bb = mult * (i_gate * xc)
    gl_pool = _dot(xnb, win_ref[:, 3 * d:4 * d])
    u_gate = _dot(xnb, win_ref[:, 2 * d:3 * d])
    h_rows = []
    for b in range(bblk):
        _store_rows(frames_a, a[b * tt:(b + 1) * tt, :], tt, chunks)
        _store_rows(frames_b, bb[b * tt:(b + 1) * tt, :], tt, chunks)
        h_last = _lru_scan(frames_a, frames_b, frames_h, h_carry[b], tt, chunks)
        h_carry[b] = h_last
        nlru_ref[b] = h_last
        h_rows.append(_load_rows(frames_h, tt, chunks))
    h = cat_rows(h_rows)
    br_pool = _dot(y_pool.astype(BF16), wbrp_ref[...])
    gl_lru = _dot(xnb, win_ref[:, 4 * d:5 * d])

    y_lru = h * _gelu_tanh(u_gate)
    br_lru = _dot(y_lru.astype(BF16), wbrl_ref[...])
    g_pool = jax.nn.sigmoid(gl_pool + bgate_ref[:, 0:d])
    g_lru = jax.nn.sigmoid(gl_lru + bgate_ref[:, d:2 * d])
    acc = g_pool * br_pool + g_lru * br_lru
    out = _dot(acc.astype(BF16), wout_ref[...])
    x1_ref[...] = (x + out).reshape(bblk, tt, d)


def _const_spec(shape):
    nd = len(shape)
    return pl.BlockSpec(shape, lambda *_: (0,) * nd, pipeline_mode=pl.Buffered(1))


def _mixer_call(x, hist_pool, hist_conv, hist_lru, weights, *, start_pos, bblk, tt, seq0=0):
    _, t_len, d = x.shape
    bsz = hist_pool.shape[0]
    assert seq0 % bblk == 0
    grid = (bsz // bblk, t_len // tt)
    kern = functools.partial(_mixer_kernel, start_pos=start_pos, bblk=bblk, tt=tt, d=d)
    chunks = d // LANES
    seq_spec = lambda rows: pl.BlockSpec((bblk, rows, d), lambda b, j: (b, 0, 0))
    frame_spec = lambda frames: pl.BlockSpec((bblk, frames * chunks, LANES), lambda b, j: (b, 0, 0))
    in_specs = [
        pl.BlockSpec((bblk, tt, d), lambda b, j: (b + seq0 // bblk, j, 0)),
        seq_spec(POOL_PAD), frame_spec(CONV_PAD), frame_spec(1),
    ] + [_const_spec(w.shape) for w in weights]
    out_shape = (
        jax.ShapeDtypeStruct((bsz, t_len, d), F32),
        jax.ShapeDtypeStruct((bsz, POOL_HIST, d), F32),
        jax.ShapeDtypeStruct((bsz, CONV_HIST * chunks, LANES), F32),
        jax.ShapeDtypeStruct((bsz, chunks, LANES), F32),
    )
    out_specs = (
        pl.BlockSpec((bblk, tt, d), lambda b, j: (b, j, 0)),
        seq_spec(POOL_HIST), frame_spec(CONV_HIST), frame_spec(1),
    )
    return pl.pallas_call(
        kern,
        grid=grid,
        in_specs=in_specs,
        out_specs=out_specs,
        out_shape=out_shape,
        scratch_shapes=[
            pltpu.VMEM((bblk, POOL_PAD + tt, d), F32),
            pltpu.VMEM((bblk, (CONV_PAD + tt) * chunks, LANES), F32),
            pltpu.VMEM((bblk, chunks, LANES), F32),
            pltpu.VMEM((tt * chunks, LANES), F32),
            pltpu.VMEM((tt * chunks, LANES), F32),
            pltpu.VMEM((tt * chunks, LANES), F32),
        ],
        compiler_params=pltpu.CompilerParams(
            dimension_semantics=("arbitrary", "arbitrary"), vmem_limit_bytes=VMEM_LIMIT_BYTES),
        name="mixer",
    )(x, hist_pool, hist_conv, hist_lru, *weights)


def _load_rows(ref, rows, chunks):
    return jnp.concatenate([ref[pl.ds(s, rows, stride=chunks), :] for s in range(chunks)], axis=1)


def _store_rows(ref, val, rows, chunks):
    for s in range(chunks):
        ref[pl.ds(s, rows, stride=chunks), :] = val[:, s * LANES:(s + 1) * LANES]


def _pack_halves(x):
    half = x.shape[1] // 2
    bits = lax.bitcast_convert_type(x, jnp.uint32)
    return (bits[:, :half] >> 16) | (bits[:, half:] & jnp.uint32(0xFFFF0000))


def _unpack_halves(p):
    lo = lax.bitcast_convert_type(p << 16, F32)
    hi = lax.bitcast_convert_type(p & jnp.uint32(0xFFFF0000), F32)
    return jnp.concatenate([lo, hi], axis=1)


class _Window(NamedTuple):
    first: int
    count: int


def _window_specs(windows, tb, d):
    specs, start = [], 0
    for win in windows:
        specs.append(pl.BlockSpec(
            (tb, d), lambda i, win=win, start=start: (win.first + jnp.clip(i - start, 0, win.count - 1), 0)))
        start += win.count
    return specs


def _window_load(i, windows, refs):
    val, start = refs[-1][...], sum(w.count for w in windows[:-1])
    for win, ref in zip(reversed(windows[:-1]), reversed(refs[:-1])):
        val = jnp.where(i < start, ref[...], val)
        start -= win.count
    return val


def _window_store(i, windows, refs, val):
    start = 0
    for win, ref in zip(windows, refs):
        @pl.when((i >= start) & (i < start + win.count))
        def _(ref=ref):
            ref[...] = val
        start += win.count


def _router_kernel(*refs, tb, d, windows):
    x1_refs = refs[:len(windows)]
    (nffn_ref, wrt_ref, br_ref, tri_ref, xn_ref, topi_ref, topw_ref, rank_ref, cnt_ref, carry) = refs[len(windows):]
    i = pl.program_id(0)

    @pl.when(i == 0)
    def _():
        carry[...] = jnp.zeros_like(carry)

    xnb = _rms_norm(_window_load(i, windows, x1_refs), nffn_ref[...]).astype(BF16)
    xn_ref_chunks = d // (2 * LANES)
    _store_rows(xn_ref, _pack_halves(xnb.astype(F32)), tb, xn_ref_chunks)
    logits = lax.dot_general(wrt_ref[...], xnb, (((1,), (1,)), ((), ())),
                             preferred_element_type=F32) + br_ref[...]
    n_exp = logits.shape[0]
    iota_e = lax.broadcasted_iota(jnp.int32, (n_exp, tb), 0)
    work = logits
    vals, idxs, hots = [], [], []
    for _ in range(TOP_K):
        m = jnp.max(work, axis=0, keepdims=True)
        idx = jnp.min(jnp.where(work == m, iota_e, n_exp), axis=0, keepdims=True)
        hot = iota_e == idx
        vals.append(m)
        idxs.append(idx)
        hots.append(hot)
        work = jnp.where(hot, -jnp.inf, work)
    exps = [jnp.exp(v - vals[0]) for v in vals]
    denom = exps[0] + exps[1] + exps[2] + exps[3]
    topi_ref[...] = jnp.concatenate(idxs, axis=0)
    topw_ref[...] = jnp.concatenate([e / denom for e in exps], axis=0)

    sel = sum(jnp.where(h, 1.0, 0.0) for h in hots)
    before = _dot(sel.astype(BF16), tri_ref[...]) + carry[:, 0:1]
    ranks = [jnp.sum(jnp.where(h, before, 0.0), axis=0, keepdims=True) for h in hots]
    rank_ref[...] = jnp.concatenate(ranks, axis=0).astype(jnp.int32)
    carry[...] = carry[...] + jnp.sum(sel, axis=1, keepdims=True)
    cnt_ref[...] = carry[...]


def _router_call(x1_parts, windows, norm_ffn, w_router_t, b_router, *, tb):
    d = x1_parts[0].shape[1]
    n_tiles = sum(w.count for w in windows)
    n = n_tiles * tb
    chunks = d // (2 * LANES)
    n_exp = w_router_t.shape[0]
    tri = jnp.triu(jnp.ones((tb, tb), BF16), k=1)
    tok = lambda rows, dt: (pl.BlockSpec((rows, tb), lambda i: (0, i)), jax.ShapeDtypeStruct((rows, n), dt))
    (topi_spec, topi_shape), (topw_spec, topw_shape), (rank_spec, rank_shape) = (
        tok(TOP_K, jnp.int32), tok(TOP_K, F32), tok(TOP_K, jnp.int32))
    return pl.pallas_call(
        functools.partial(_router_kernel, tb=tb, d=d, windows=tuple(windows)),
        grid=(n_tiles,),
        in_specs=[
            *_window_specs(windows, tb, d),
            _const_spec((1, d)), _const_spec((n_exp, d)), _const_spec((n_exp, 1)), _const_spec((tb, tb)),
        ],
        out_specs=(
            pl.BlockSpec((tb * chunks, LANES), lambda i: (i, 0)),
            topi_spec, topw_spec, rank_spec,
            pl.BlockSpec((n_exp, LANES), lambda i: (0, 0)),
        ),
        out_shape=(
            jax.ShapeDtypeStruct((n * chunks, LANES), jnp.uint32),
            topi_shape, topw_shape, rank_shape,
            jax.ShapeDtypeStruct((n_exp, LANES), F32),
        ),
        scratch_shapes=[pltpu.VMEM((n_exp, LANES), F32)],
        compiler_params=pltpu.CompilerParams(
            dimension_semantics=("arbitrary",), vmem_limit_bytes=VMEM_LIMIT_BYTES),
        name="router",
    )(*x1_parts, norm_ffn.reshape(1, d), w_router_t, b_router.reshape(n_exp, 1), tri)


def _expert_kernel(te_ref, tv_ref, nu_ref, xs_ref, wgu_ref, bgu_ref, wdn_ref, bdn_ref, ys_ref, *bf_refs,
                   tm, d, dff, round_weights):
    del nu_ref
    i = pl.program_id(0)
    if round_weights:
        wgu_bf, wdn_bf = bf_refs
        prev = te_ref[jnp.maximum(i - 1, 0)]

        @pl.when((i == 0) | (te_ref[i] != prev))
        def _():
            wgu_bf[...] = wgu_ref[...].astype(BF16)
            wdn_bf[...] = wdn_ref[...].astype(BF16)
    else:
        wgu_bf, wdn_bf = wgu_ref, wdn_ref

    xchunks = d // (2 * LANES)

    def ffn(row0, rows):
        window = pl.ds(row0 * xchunks, rows * xchunks)
        x = _unpack_halves(_load_rows(xs_ref.at[window], rows, xchunks)).astype(BF16)
        gu = _dot(x, wgu_bf[0]) + bgu_ref[0]
        gate = jnp.minimum(gu[:, :dff], SWIGLU_LIMIT)
        up = jnp.clip(gu[:, dff:], -SWIGLU_LIMIT, SWIGLU_LIMIT)
        hid = (up + 1.0) * gate * jax.nn.sigmoid(SWIGLU_ALPHA * gate)
        y = _dot(hid.astype(BF16), wdn_bf[0]) + bdn_ref[0]
        _store_rows(ys_ref.at[window], _pack_halves(y.astype(BF16).astype(F32)), rows, xchunks)

    sub = EXPERT_SUBTILE
    for s in range(tm // sub):
        valid = tv_ref[i] - s * sub
        pl.when(valid > sub // 2)(functools.partial(ffn, s * sub, sub))
        pl.when((valid > 0) & (valid <= sub // 2))(functools.partial(ffn, s * sub, sub // 2))


def _expert_call(tile_expert, tile_rows, n_used, xs, w_gu, b_gu, w_dn, b_dn, *, tm):
    n_exp, d, dff2 = w_gu.shape
    dff = dff2 // 2
    xchunks = d // (2 * LANES)
    p_rows = xs.shape[0] // xchunks
    round_weights = w_gu.dtype != BF16
    row_blk = lambda i, te, tv, nu: (jnp.minimum(i, nu[0] - 1), 0)
    exp_blk = lambda i, te, tv, nu: (te[i], 0, 0)
    out_specs = [pl.BlockSpec((tm * xchunks, LANES), row_blk)]
    out_shape = [jax.ShapeDtypeStruct((p_rows * xchunks, LANES), jnp.uint32)]
    if round_weights:
        out_specs += [pl.BlockSpec((1, d, dff2), exp_blk), pl.BlockSpec((1, dff, d), exp_blk)]
        out_shape += [jax.ShapeDtypeStruct(w_gu.shape, BF16), jax.ShapeDtypeStruct(w_dn.shape, BF16)]
    return pl.pallas_call(
        functools.partial(_expert_kernel, tm=tm, d=d, dff=dff, round_weights=round_weights),
        grid_spec=pltpu.PrefetchScalarGridSpec(
            num_scalar_prefetch=3,
            grid=(p_rows // tm,),
            in_specs=[
                pl.BlockSpec((tm * xchunks, LANES), row_blk),
                pl.BlockSpec((1, d, dff2), exp_blk),
                pl.BlockSpec((1, 1, dff2), exp_blk),
                pl.BlockSpec((1, dff, d), exp_blk),
                pl.BlockSpec((1, 1, d), exp_blk),
            ],
            out_specs=out_specs,
        ),
        out_shape=out_shape,
        compiler_params=pltpu.CompilerParams(
            dimension_semantics=("arbitrary",), vmem_limit_bytes=VMEM_LIMIT_BYTES),
        name="experts",
    )(tile_expert, tile_rows, n_used, xs, w_gu, b_gu.reshape(n_exp, 1, dff2), w_dn, b_dn.reshape(n_exp, 1, d))


def _final_kernel(*refs, tb, d, in_windows, out_windows, n_alias):
    n_in, n_out = len(in_windows), len(out_windows)
    x1_refs = refs[:n_in]
    yk_ref, w_ref, nfin_ref = refs[n_in:n_in + 3]
    out_refs = refs[n_in + 3 + n_alias:n_in + 3 + n_alias + n_out]
    i = pl.program_id(0)
    w = w_ref[...]
    chunks = d // (2 * LANES)
    moe = _unpack_halves(_load_rows(yk_ref.at[0], tb, chunks)) * w[:, 0:1]
    for k in range(1, TOP_K):
        moe = moe + _unpack_halves(_load_rows(yk_ref.at[k], tb, chunks)) * w[:, k:k + 1]
    out = _rms_norm(_window_load(i, in_windows, x1_refs) + moe, nfin_ref[...])
    _window_store(i, out_windows, out_refs, out)


def _final_call(x1_parts, in_windows, yk, topw_t, norm_final, out_shapes, out_windows, out_init, *, tb):
    d = x1_parts[0].shape[1]
    n_tiles = sum(w.count for w in in_windows)
    chunks = d // (2 * LANES)
    alias_in = [a for a in out_init if a is not None]
    n_fixed = len(x1_parts) + 3
    aliases, j = {}, 0
    for k, a in enumerate(out_init):
        if a is not None:
            aliases[n_fixed + j] = k
            j += 1
    return pl.pallas_call(
        functools.partial(_final_kernel, tb=tb, d=d, in_windows=tuple(in_windows),
                          out_windows=tuple(out_windows), n_alias=len(alias_in)),
        grid=(n_tiles,),
        in_specs=[
            *_window_specs(in_windows, tb, d),
            pl.BlockSpec((TOP_K, tb * chunks, LANES), lambda i: (0, i, 0)),
            pl.BlockSpec((tb, TOP_K), lambda i: (i, 0)),
            _const_spec((1, d)),
            *[pl.BlockSpec(memory_space=pl.ANY) for _ in alias_in],
        ],
        out_specs=_window_specs(out_windows, tb, d),
        out_shape=[jax.ShapeDtypeStruct(s, F32) for s in out_shapes],
        input_output_aliases=aliases,
        compiler_params=pltpu.CompilerParams(
            dimension_semantics=("arbitrary",), vmem_limit_bytes=VMEM_LIMIT_BYTES),
        name="final",
    )(*x1_parts, yk, topw_t, norm_final.reshape(1, d), *alias_in)


def _sc_step_rows(n):
    per_worker = n // SC_WORKERS
    assert per_worker * SC_WORKERS == n
    w = max(c for c in range(SUBLANES, SC_MAX_STEP_ROWS + 1, SUBLANES) if per_worker % c == 0)
    return w, per_worker // w


def _sc_mesh():
    return plsc.VectorSubcoreMesh(core_axis_name="c", subcore_axis_name="s",
                                  num_cores=SC_CORES, num_subcores=SC_SUBCORES)


def _dispatch_rows(x_tiles, pos_steps, p_rows):
    n, chunks, _ = x_tiles.shape
    w, steps = _sc_step_rows(n)

    @functools.partial(
        pl.kernel, mesh=_sc_mesh(),
        out_type=jax.ShapeDtypeStruct((p_rows, chunks, LANES), x_tiles.dtype),
        scratch_types=[pltpu.VMEM((w, chunks, LANES), x_tiles.dtype), pltpu.VMEM((TOP_K, w), jnp.int32)],
        name="dispatch",
    )
    def run(x_hbm, pos_hbm, out_hbm, buf, idx):
        wid = lax.axis_index("s") * SC_CORES + lax.axis_index("c")

        @pl.loop(0, steps)
        def _(i):
            step = wid * steps + i
            pltpu.sync_copy(x_hbm.at[pl.ds(pl.multiple_of(step * w, SUBLANES), w)], buf)
            pltpu.sync_copy(pos_hbm.at[step], idx)
            for k in range(TOP_K):
                pltpu.sync_copy(buf, out_hbm.at[idx.at[k]])

    return run(x_tiles, pos_steps)


def _combine_rows(y_tiles, pos_steps, n):
    _, chunks, _ = y_tiles.shape
    w, steps = _sc_step_rows(n)

    @functools.partial(
        pl.kernel, mesh=_sc_mesh(),
        out_type=jax.ShapeDtypeStruct((TOP_K, n, chunks, LANES), y_tiles.dtype),
        scratch_types=[pltpu.VMEM((w, chunks, LANES), y_tiles.dtype), pltpu.VMEM((TOP_K, w), jnp.int32)],
        name="combine",
    )
    def run(y_hbm, pos_hbm, out_hbm, buf, idx):
        wid = lax.axis_index("s") * SC_CORES + lax.axis_index("c")

        @pl.loop(0, steps)
        def _(i):
            step = wid * steps + i
            pltpu.sync_copy(pos_hbm.at[step], idx)
            for k in range(TOP_K):
                pltpu.sync_copy(y_hbm.at[idx.at[k]], buf)
                pltpu.sync_copy(buf, out_hbm.at[k, pl.ds(pl.multiple_of(step * w, SUBLANES), w)])

    return run(y_tiles, pos_steps)


def _block_diag(w, per_block):
    heads, n, _ = w.shape
    w4 = w.reshape(heads // per_block, per_block, n, n)
    eye = jnp.eye(per_block, dtype=w.dtype)
    bd = jnp.einsum('chij,hg->chigj', w4, eye)
    return bd.reshape(heads // per_block, per_block * n, per_block * n)


def _mixer_weights(norm_mix, w_in, b_gate, pool_w, pool_scale, conv_w, conv_b, lru_wa, lru_ba, lru_wi,
                   lru_bi, lru_lambda, w_br_pool, w_br_lru, w_out):
    row = lambda v: v.reshape(1, -1)
    d = norm_mix.shape[0]
    head_dim = lru_wa.shape[-1]
    per_block = MXU_WIDTH // head_dim
    w_ai = jnp.concatenate([_block_diag(lru_wa, per_block), _block_diag(lru_wi, per_block)], axis=-1)
    return (row(norm_mix), w_in.astype(BF16), row(b_gate), pool_w.astype(BF16), row(pool_scale),
            conv_w.reshape(-1, d // LANES, LANES), conv_b.reshape(d // LANES, LANES), w_ai.astype(BF16),
            row(lru_ba), row(lru_bi), row(lru_lambda),
            w_br_pool.astype(BF16), w_br_lru.astype(BF16), w_out.astype(BF16))


def _run_mixer(x, state_pool, state_conv, state_lru, weights, *, start_pos, bblk, tt, seq0=0):
    bsz, d = state_lru.shape
    chunks = d // LANES
    hp = jnp.pad(state_pool, ((0, 0), (POOL_PAD - POOL_HIST, 0), (0, 0)))
    hc = jnp.pad(state_conv, ((0, 0), (CONV_PAD - CONV_HIST, 0), (0, 0))).reshape(bsz, CONV_PAD * chunks, LANES)
    hl = state_lru.reshape(bsz, chunks, LANES)
    x1, npool, nconv, nlru = _mixer_call(x, hp, hc, hl, weights, start_pos=start_pos, bblk=bblk, tt=tt,
                                         seq0=seq0)
    return x1, npool, nconv.reshape(bsz, CONV_HIST, d), nlru.reshape(bsz, d)


def _routing_plan(topi, rank, counts, *, tm, n_tiles, min_tiles):
    n_exp = counts.shape[0]
    tiles_per = jnp.maximum((counts + tm - 1) // tm, min_tiles)
    tiles_cum = jnp.cumsum(tiles_per)
    tile_start = tiles_cum - tiles_per
    base = tile_start * tm
    experts = jnp.arange(n_exp, dtype=jnp.int32)[:, None, None]
    pos = rank + jnp.sum(jnp.where(topi[None] == experts, base[:, None, None], 0), axis=0)
    n_used = tiles_cum[-1]
    tile_ids = jnp.arange(n_tiles, dtype=jnp.int32)
    live = tile_ids < n_used
    owner = (tiles_cum[None, :] <= jnp.minimum(tile_ids, n_used - 1)[:, None])
    tile_expert = jnp.sum(owner.astype(jnp.int32), axis=1)
    is_owner = tile_expert[:, None] == jnp.arange(n_exp, dtype=jnp.int32)[None, :]
    rows_left = jnp.sum(jnp.where(is_owner, counts[None, :] - (tile_ids[:, None] - tile_start[None, :]) * tm, 0),
                        axis=1)
    tile_rows = jnp.where(live, jnp.clip(rows_left, 0, tm), 0).astype(jnp.int32)
    return pos, tile_expert, tile_rows, n_used.reshape(1).astype(jnp.int32)


def _moe_rows(x1_parts, windows, norm_ffn, w_router_t, b_router, w_gu, b_gu, w_dn, b_dn):
    d = x1_parts[0].shape[1]
    chunks = d // LANES
    n_exp = w_router_t.shape[0]
    n = sum(w.count for w in windows) * TOKEN_TILE
    xn2, topi, topw, rank, cnt = _router_call(x1_parts, windows, norm_ffn, w_router_t, b_router, tb=TOKEN_TILE)
    min_tiles = 0 if w_gu.dtype == BF16 else 1
    n_tiles = (n * TOP_K + n_exp * (EXPERT_TILE - 1)) // EXPERT_TILE + n_exp * min_tiles
    p_rows = n_tiles * EXPERT_TILE
    pos, tile_expert, tile_rows, n_used = _routing_plan(
        topi, rank, cnt[:, 0].astype(jnp.int32), tm=EXPERT_TILE, n_tiles=n_tiles, min_tiles=min_tiles)
    w, steps = _sc_step_rows(n)
    pos_steps = pos.reshape(TOP_K, SC_WORKERS * steps, w).transpose(1, 0, 2)
    xchunks = d // (2 * LANES)
    xs = _dispatch_rows(xn2.reshape(n, xchunks, LANES), pos_steps, p_rows)
    ys, *w_bf = _expert_call(tile_expert, tile_rows, n_used, xs.reshape(p_rows * xchunks, LANES), w_gu, b_gu,
                             w_dn, b_dn, tm=EXPERT_TILE)
    yk = _combine_rows(ys.reshape(p_rows, xchunks, LANES), pos_steps, n)
    return yk.reshape(TOP_K, n * xchunks, LANES), topw, (w_bf if w_bf else (w_gu, w_dn))


def kernel(x_prompt, x_sample, state_pool, state_conv, state_lru, norm_mix, w_in, b_gate, pool_w, pool_scale,
           conv_w, conv_b, lru_wa, lru_ba, lru_wi, lru_bi, lru_lambda, w_br_pool, w_br_lru, w_out, norm_ffn,
           w_router, b_router, w_gu, b_gu, w_dn, b_dn, norm_final):
    bp, tp, d = x_prompt.shape
    bs, ts, _ = x_sample.shape
    n_p, n_s = bp * tp, bs * ts
    mw = _mixer_weights(norm_mix[0], w_in[0], b_gate[0], pool_w[0], pool_scale[0], conv_w[0], conv_b[0],
                        lru_wa[0], lru_ba[0], lru_wi[0], lru_bi[0], lru_lambda[0], w_br_pool[0], w_br_lru[0],
                        w_out[0])
    w_router_t = w_router[0].T.astype(BF16)
    zeros = lambda *shape: jnp.zeros(shape, x_prompt.dtype)

    assert sum(PROMPT_GROUP_SEQS) == bp and tp % TOKEN_TILE == 0 and n_s % TOKEN_TILE == 0
    seq_tiles = tp // TOKEN_TILE
    s_tiles = n_s // TOKEN_TILE
    x1_s, pool_s, conv_s, lru_s = _run_mixer(
        x_sample, state_pool[0], state_conv[0], state_lru[0], mw, start_pos=PAST_LEN, bblk=bs, tt=ts)
    y_p, y_s = None, None
    expert_w = (w_gu[0], w_dn[0])
    pools, convs, lrus = [], [], []
    seq0 = 0
    for g, seqs in enumerate(PROMPT_GROUP_SEQS):
        x1_g, pool_g, conv_g, lru_g = _run_mixer(
            x_prompt, zeros(seqs, POOL_HIST, d), zeros(seqs, CONV_HIST, d), zeros(seqs, d), mw,
            start_pos=0, bblk=1, tt=MIXER_TILE, seq0=seq0)
        pools.append(pool_g)
        convs.append(conv_g)
        lrus.append(lru_g)
        last = g == len(PROMPT_GROUP_SEQS) - 1
        g_tiles = seqs * seq_tiles
        parts, windows = [x1_g.reshape(seqs * tp, d)], [_Window(0, g_tiles)]
        out_shapes, out_windows, out_init = [(n_p, d)], [_Window(seq0 * seq_tiles, g_tiles)], [y_p]
        seq0 += seqs
        if last:
            parts.append(x1_s.reshape(n_s, d))
            windows.append(_Window(0, s_tiles))
            out_shapes.append((n_s, d))
            out_windows.append(_Window(0, s_tiles))
            out_init.append(None)
        yk, topw, expert_w = _moe_rows(parts, windows, norm_ffn[0], w_router_t, b_router[0], expert_w[0],
                                       b_gu[0], expert_w[1], b_dn[0])
        outs = _final_call(parts, windows, yk, topw.T, norm_final, out_shapes, out_windows, out_init,
                           tb=TOKEN_TILE)
        y_p = outs[0]
        if last:
            y_s = outs[1]

    cat = lambda xs: jnp.concatenate(xs, axis=0)[None]
    return (y_p.reshape(bp, tp, d), y_s.reshape(bs, ts, d), cat(pools), cat(convs), cat(lrus),
            pool_s[None], conv_s[None], lru_s[None])
```

```python
import functools
from typing import NamedTuple

import jax
import jax.numpy as jnp
from jax import lax
from jax.experimental import pallas as pl
from jax.experimental.pallas import tpu as pltpu
from jax.experimental.pallas import tpu_sc as plsc

BF16 = jnp.bfloat16
F32 = jnp.float32

POOL_WINDOWS = (2, 4, 8, 16)
POOL_HIST = max(POOL_WINDOWS) - 1
CONV_WIDTH = 4
CONV_HIST = CONV_WIDTH - 1
LRU_C = 8.0
LRU_SEGMENTS = 8
TOP_K = 4
SWIGLU_LIMIT = 7.0
SWIGLU_ALPHA = 1.702
NORM_EPS = 1e-6

SUBLANES = 8
LANES = 128
MXU_WIDTH = 256
POOL_PAD = -(-POOL_HIST // SUBLANES) * SUBLANES
CONV_PAD = -(-CONV_HIST // SUBLANES) * SUBLANES
V7X_VMEM_BYTES = 64 * 1024 * 1024
VMEM_LIMIT_BYTES = V7X_VMEM_BYTES * 7 // 8

PAST_LEN = 1024
MIXER_TILE = 256
TOKEN_TILE = 512
EXPERT_TILE = 1024
EXPERT_SUBTILE = 512
PROMPT_GROUP_SEQS = (8, 8)

SC_CORES = 2
SC_SUBCORES = 16
SC_WORKERS = SC_CORES * SC_SUBCORES
SC_MAX_STEP_ROWS = 64


def _rms_norm(x, g):
    ms = jnp.mean(x * x, axis=-1, keepdims=True)
    return (x * lax.rsqrt(ms + NORM_EPS)) * g


def _dot(a, b):
    return jnp.dot(a, b, preferred_element_type=F32)


def _softplus(x):
    return jnp.maximum(x, 0.0) + jnp.log1p(jnp.exp(-jnp.abs(x)))


def _gelu_tanh(x):
    c = 0.7978845608028654
    return 0.5 * x * (1.0 + jnp.tanh(c * (x + 0.044715 * (x * x * x))))


def _lru_scan(a_ref, b_ref, h_ref, h0, frames, chunks):
    tile = lambda ref, t: ref[t * chunks:(t + 1) * chunks, :]
    seg = frames // LRU_SEGMENTS
    assert seg * LRU_SEGMENTS == frames
    prods = [tile(a_ref, s * seg) for s in range(LRU_SEGMENTS)]
    sums = [tile(b_ref, s * seg) for s in range(LRU_SEGMENTS)]
    for t in range(1, seg):
        for s in range(LRU_SEGMENTS):
            a_t = tile(a_ref, s * seg + t)
            sums[s] = a_t * sums[s] + tile(b_ref, s * seg + t)
            prods[s] = a_t * prods[s]
            b_ref[(s * seg + t) * chunks:(s * seg + t + 1) * chunks, :] = sums[s]
            a_ref[(s * seg + t) * chunks:(s * seg + t + 1) * chunks, :] = prods[s]
    h_in = [h0]
    for s in range(LRU_SEGMENTS):
        h_in.append(prods[s] * h_in[s] + sums[s])
    for s in range(LRU_SEGMENTS):
        for t in range(seg):
            f = s * seg + t
            h_ref[f * chunks:(f + 1) * chunks, :] = tile(a_ref, f) * h_in[s] + tile(b_ref, f)
    return h_in[LRU_SEGMENTS]


def _mixer_kernel(x_ref, hp_ref, hc_ref, hl_ref, nmix_ref, win_ref, bgate_ref, poolw_ref, pscale_ref,
                  convw_ref, convb_ref, wai_ref, ba_ref, bi_ref, lam_ref, wbrp_ref, wbrl_ref, wout_ref,
                  x1_ref, npool_ref, nconv_ref, nlru_ref,
                  pool_ext, conv_ext, h_carry, frames_a, frames_b, frames_h, *, start_pos, bblk, tt, d):
    j = pl.program_id(1)

    @pl.when(j == 0)
    def _():
        pool_ext[:, 0:POOL_PAD, :] = hp_ref[...]
        conv_ext[:, 0:CONV_PAD * (d // LANES), :] = hc_ref[...]
        h_carry[...] = hl_ref[...]

    rows = bblk * tt
    chunks = d // LANES
    cat_rows = lambda parts: jnp.concatenate(parts, axis=0) if len(parts) > 1 else parts[0]
    x = x_ref[...].reshape(rows, d)
    xnb = _rms_norm(x, nmix_ref[...]).astype(BF16)

    u_x = _dot(xnb, win_ref[:, d:2 * d])
    xc_rows = []
    for b in range(bblk):
        _store_rows(conv_ext.at[b, pl.ds(CONV_PAD * chunks, tt * chunks)], u_x[b * tt:(b + 1) * tt, :], tt, chunks)
        ce = conv_ext[b].reshape(CONV_PAD + tt, chunks, LANES)
        y = convb_ref[...] + ce[CONV_PAD:CONV_PAD + tt] * convw_ref[CONV_WIDTH - 1]
        for k in range(1, CONV_WIDTH):
            y = y + ce[CONV_PAD - k:CONV_PAD - k + tt] * convw_ref[CONV_WIDTH - 1 - k]
        frames_h[...] = y.reshape(tt * chunks, LANES)
        xc_rows.append(_load_rows(frames_h, tt, chunks))
        nconv_ref[b] = conv_ext[b, (CONV_PAD + tt - CONV_HIST) * chunks:(CONV_PAD + tt) * chunks, :]
        conv_ext[b, 0:CONV_PAD * chunks, :] = conv_ext[b, tt * chunks:(tt + CONV_PAD) * chunks, :]
    xc = cat_rows(xc_rows)

    n_chunks = d // MXU_WIDTH
    pre_a, pre_i = [], []
    for c in range(n_chunks):
        ai = _dot(xc[:, c * MXU_WIDTH:(c + 1) * MXU_WIDTH].astype(BF16), wai_ref[c])
        pre_a.append(ai[:, 0:MXU_WIDTH])
        pre_i.append(ai[:, MXU_WIDTH:2 * MXU_WIDTH])

    u_pool = _dot(xnb, win_ref[:, 0:d])
    pos1 = lax.broadcasted_iota(jnp.int32, (tt, 1), 0) + (start_pos + 1) + j * tt
    gd = d // len(POOL_WINDOWS)
    y_pool_rows = []
    for b in range(bblk):
        pool_ext[b, POOL_PAD:POOL_PAD + tt, :] = u_pool[b * tt:(b + 1) * tt, :]
        parts = []
        for g, w in enumerate(POOL_WINDOWS):
            sl = slice(g * gd, (g + 1) * gd)
            s = pool_ext[b, :, sl]
            width = 1
            while width < w:
                s = s + pltpu.roll(s, width, axis=0)
                width *= 2
            cur = s[POOL_PAD:POOL_PAD + tt, :]
            inv_cnt = 1.0 / jnp.minimum(pos1, w).astype(F32)
            pooled = cur * inv_cnt - u_pool[b * tt:(b + 1) * tt, sl]
            mixed = _dot(pooled.astype(BF16), poolw_ref[g])
            parts.append(mixed * pscale_ref[:, sl])
        y_pool_rows.append(jnp.concatenate(parts, axis=1))
        npool_ref[b] = pool_ext[b, POOL_PAD + tt - POOL_HIST:POOL_PAD + tt, :]
        pool_ext[b, 0:POOL_PAD, :] = pool_ext[b, tt:tt + POOL_PAD, :]
    y_pool = cat_rows(y_pool_rows)

    r_gate = jax.nn.sigmoid(jnp.concatenate(pre_a, axis=1) + ba_ref[...])
    i_gate = jax.nn.sigmoid(jnp.concatenate(pre_i, axis=1) + bi_ref[...])
    log_a = (-LRU_C) * r_gate * _softplus(-lam_ref[...])
    a = jnp.exp(log_a)
    mult = jnp.sqrt(-jnp.tanh(log_a) * (a * a + 1.0))
    bb = mult * (i_gate * xc)
    gl_pool = _dot(xnb, win_ref[:, 3 * d:4 * d])
    u_gate = _dot(xnb, win_ref[:, 2 * d:3 * d])
    h_rows = []
    for b in range(bblk):
        _store_rows(frames_a, a[b * tt:(b + 1) * tt, :], tt, chunks)
        _store_rows(frames_b, bb[b * tt:(b + 1) * tt, :], tt, chunks)
        h_last = _lru_scan(frames_a, frames_b, frames_h, h_carry[b], tt, chunks)
        h_carry[b] = h_last
        nlru_ref[b] = h_last
        h_rows.append(_load_rows(frames_h, tt, chunks))
    h = cat_rows(h_rows)
    br_pool = _dot(y_pool.astype(BF16), wbrp_ref[...])
    gl_lru = _dot(xnb, win_ref[:, 4 * d:5 * d])

    y_lru = h * _gelu_tanh(u_gate)
    br_lru = _dot(y_lru.astype(BF16), wbrl_ref[...])
    g_pool = jax.nn.sigmoid(gl_pool + bgate_ref[:, 0:d])
    g_lru = jax.nn.sigmoid(gl_lru + bgate_ref[:, d:2 * d])
    acc = g_pool * br_pool + g_lru * br_lru
    out = _dot(acc.astype(BF16), wout_ref[...])
    x1_ref[...] = (x + out).reshape(bblk, tt, d)


def _const_spec(shape):
    nd = len(shape)
    return pl.BlockSpec(shape, lambda *_: (0,) * nd, pipeline_mode=pl.Buffered(1))


def _mixer_call(x, hist_pool, hist_conv, hist_lru, weights, *, start_pos, bblk, tt, seq0=0):
    _, t_len, d = x.shape
    bsz = hist_pool.shape[0]
    assert seq0 % bblk == 0
    grid = (bsz // bblk, t_len // tt)
    kern = functools.partial(_mixer_kernel, start_pos=start_pos, bblk=bblk, tt=tt, d=d)
    chunks = d // LANES
    seq_spec = lambda rows: pl.BlockSpec((bblk, rows, d), lambda b, j: (b, 0, 0))
    frame_spec = lambda frames: pl.BlockSpec((bblk, frames * chunks, LANES), lambda b, j: (b, 0, 0))
    in_specs = [
        pl.BlockSpec((bblk, tt, d), lambda b, j: (b + seq0 // bblk, j, 0)),
        seq_spec(POOL_PAD), frame_spec(CONV_PAD), frame_spec(1),
    ] + [_const_spec(w.shape) for w in weights]
    out_shape = (
        jax.ShapeDtypeStruct((bsz, t_len, d), F32),
        jax.ShapeDtypeStruct((bsz, POOL_HIST, d), F32),
        jax.ShapeDtypeStruct((bsz, CONV_HIST * chunks, LANES), F32),
        jax.ShapeDtypeStruct((bsz, chunks, LANES), F32),
    )
    out_specs = (
        pl.BlockSpec((bblk, tt, d), lambda b, j: (b, j, 0)),
        seq_spec(POOL_HIST), frame_spec(CONV_HIST), frame_spec(1),
    )
    return pl.pallas_call(
        kern,
        grid=grid,
        in_specs=in_specs,
        out_specs=out_specs,
        out_shape=out_shape,
        scratch_shapes=[
            pltpu.VMEM((bblk, POOL_PAD + tt, d), F32),
            pltpu.VMEM((bblk, (CONV_PAD + tt) * chunks, LANES), F32),
            pltpu.VMEM((bblk, chunks, LANES), F32),
            pltpu.VMEM((tt * chunks, LANES), F32),
            pltpu.VMEM((tt * chunks, LANES), F32),
            pltpu.VMEM((tt * chunks, LANES), F32),
        ],
        compiler_params=pltpu.CompilerParams(
            dimension_semantics=("arbitrary", "arbitrary"), vmem_limit_bytes=VMEM_LIMIT_BYTES),
        name="mixer",
    )(x, hist_pool, hist_conv, hist_lru, *weights)


def _load_rows(ref, rows, chunks):
    return jnp.concatenate([ref[pl.ds(s, rows, stride=chunks), :] for s in range(chunks)], axis=1)


def _store_rows(ref, val, rows, chunks):
    for s in range(chunks):
        ref[pl.ds(s, rows, stride=chunks), :] = val[:, s * LANES:(s + 1) * LANES]


def _pack_halves(x):
    half = x.shape[1] // 2
    bits = lax.bitcast_convert_type(x, jnp.uint32)
    return (bits[:, :half] >> 16) | (bits[:, half:] & jnp.uint32(0xFFFF0000))


def _unpack_halves(p):
    lo = lax.bitcast_convert_type(p << 16, F32)
    hi = lax.bitcast_convert_type(p & jnp.uint32(0xFFFF0000), F32)
    return jnp.concatenate([lo, hi], axis=1)


class _Window(NamedTuple):
    first: int
    count: int


def _window_specs(windows, tb, d):
    specs, start = [], 0
    for win in windows:
        specs.append(pl.BlockSpec(
            (tb, d), lambda i, win=win, start=start: (win.first + jnp.clip(i - start, 0, win.count - 1), 0)))
        start += win.count
    return specs


def _window_load(i, windows, refs):
    val, start = refs[-1][...], sum(w.count for w in windows[:-1])
    for win, ref in zip(reversed(windows[:-1]), reversed(refs[:-1])):
        val = jnp.where(i < start, ref[...], val)
        start -= win.count
    return val


def _window_store(i, windows, refs, val):
    start = 0
    for win, ref in zip(windows, refs):
        @pl.when((i >= start) & (i < start + win.count))
        def _(ref=ref):
            ref[...] = val
        start += win.count


def _router_kernel(*refs, tb, d, windows):
    x1_refs = refs[:len(windows)]
    (nffn_ref, wrt_ref, br_ref, tri_ref, xn_ref, topi_ref, topw_ref, rank_ref, cnt_ref, carry) = refs[len(windows):]
    i = pl.program_id(0)

    @pl.when(i == 0)
    def _():
        carry[...] = jnp.zeros_like(carry)

    xnb = _rms_norm(_window_load(i, windows, x1_refs), nffn_ref[...]).astype(BF16)
    xn_ref_chunks = d // (2 * LANES)
    _store_rows(xn_ref, _pack_halves(xnb.astype(F32)), tb, xn_ref_chunks)
    logits = lax.dot_general(wrt_ref[...], xnb, (((1,), (1,)), ((), ())),
                             preferred_element_type=F32) + br_ref[...]
    n_exp = logits.shape[0]
    iota_e = lax.broadcasted_iota(jnp.int32, (n_exp, tb), 0)
    work = logits
    vals, idxs, hots = [], [], []
    for _ in range(TOP_K):
        m = jnp.max(work, axis=0, keepdims=True)
        idx = jnp.min(jnp.where(work == m, iota_e, n_exp), axis=0, keepdims=True)
        hot = iota_e == idx
        vals.append(m)
        idxs.append(idx)
        hots.append(hot)
        work = jnp.where(hot, -jnp.inf, work)
    exps = [jnp.exp(v - vals[0]) for v in vals]
    denom = exps[0] + exps[1] + exps[2] + exps[3]
    topi_ref[...] = jnp.concatenate(idxs, axis=0)
    topw_ref[...] = jnp.concatenate([e / denom for e in exps], axis=0)

    sel = sum(jnp.where(h, 1.0, 0.0) for h in hots)
    before = _dot(sel.astype(BF16), tri_ref[...]) + carry[:, 0:1]
    ranks = [jnp.sum(jnp.where(h, before, 0.0), axis=0, keepdims=True) for h in hots]
    rank_ref[...] = jnp.concatenate(ranks, axis=0).astype(jnp.int32)
    carry[...] = carry[...] + jnp.sum(sel, axis=1, keepdims=True)
    cnt_ref[...] = carry[...]


def _router_call(x1_parts, windows, norm_ffn, w_router_t, b_router, *, tb):
    d = x1_parts[0].shape[1]
    n_tiles = sum(w.count for w in windows)
    n = n_tiles * tb
    chunks = d // (2 * LANES)
    n_exp = w_router_t.shape[0]
    tri = jnp.triu(jnp.ones((tb, tb), BF16), k=1)
    tok = lambda rows, dt: (pl.BlockSpec((rows, tb), lambda i: (0, i)), jax.ShapeDtypeStruct((rows, n), dt))
    (topi_spec, topi_shape), (topw_spec, topw_shape), (rank_spec, rank_shape) = (
        tok(TOP_K, jnp.int32), tok(TOP_K, F32), tok(TOP_K, jnp.int32))
    return pl.pallas_call(
        functools.partial(_router_kernel, tb=tb, d=d, windows=tuple(windows)),
        grid=(n_tiles,),
        in_specs=[
            *_window_specs(windows, tb, d),
            _const_spec((1, d)), _const_spec((n_exp, d)), _const_spec((n_exp, 1)), _const_spec((tb, tb)),
        ],
        out_specs=(
            pl.BlockSpec((tb * chunks, LANES), lambda i: (i, 0)),
            topi_spec, topw_spec, rank_spec,
            pl.BlockSpec((n_exp, LANES), lambda i: (0, 0)),
        ),
        out_shape=(
            jax.ShapeDtypeStruct((n * chunks, LANES), jnp.uint32),
            topi_shape, topw_shape, rank_shape,
            jax.ShapeDtypeStruct((n_exp, LANES), F32),
        ),
        scratch_shapes=[pltpu.VMEM((n_exp, LANES), F32)],
        compiler_params=pltpu.CompilerParams(
            dimension_semantics=("arbitrary",), vmem_limit_bytes=VMEM_LIMIT_BYTES),
        name="router",
    )(*x1_parts, norm_ffn.reshape(1, d), w_router_t, b_router.reshape(n_exp, 1), tri)


def _expert_kernel(te_ref, tv_ref, nu_ref, xs_ref, wgu_ref, bgu_ref, wdn_ref, bdn_ref, ys_ref, *bf_refs,
                   tm, d, dff, round_weights):
    del nu_ref
    i = pl.program_id(0)
    if round_weights:
        wgu_bf, wdn_bf = bf_refs
        prev = te_ref[jnp.maximum(i - 1, 0)]

        @pl.when((i == 0) | (te_ref[i] != prev))
        def _():
            wgu_bf[...] = wgu_ref[...].astype(BF16)
            wdn_bf[...] = wdn_ref[...].astype(BF16)
    else:
        wgu_bf, wdn_bf = wgu_ref, wdn_ref

    xchunks = d // (2 * LANES)

    def ffn(row0, rows):
        window = pl.ds(row0 * xchunks, rows * xchunks)
        x = _unpack_halves(_load_rows(xs_ref.at[window], rows, xchunks)).astype(BF16)
        gu = _dot(x, wgu_bf[0]) + bgu_ref[0]
        gate = jnp.minimum(gu[:, :dff], SWIGLU_LIMIT)
        up = jnp.clip(gu[:, dff:], -SWIGLU_LIMIT, SWIGLU_LIMIT)
        hid = (up + 1.0) * gate * jax.nn.sigmoid(SWIGLU_ALPHA * gate)
        y = _dot(hid.astype(BF16), wdn_bf[0]) + bdn_ref[0]
        _store_rows(ys_ref.at[window], _pack_halves(y.astype(BF16).astype(F32)), rows, xchunks)

    sub = EXPERT_SUBTILE
    for s in range(tm // sub):
        valid = tv_ref[i] - s * sub
        pl.when(valid > sub // 2)(functools.partial(ffn, s * sub, sub))
        pl.when((valid > 0) & (valid <= sub // 2))(functools.partial(ffn, s * sub, sub // 2))


def _expert_call(tile_expert, tile_rows, n_used, xs, w_gu, b_gu, w_dn, b_dn, *, tm):
    n_exp, d, dff2 = w_gu.shape
    dff = dff2 // 2
    xchunks = d // (2 * LANES)
    p_rows = xs.shape[0] // xchunks
    round_weights = w_gu.dtype != BF16
    row_blk = lambda i, te, tv, nu: (jnp.minimum(i, nu[0] - 1), 0)
    exp_blk = lambda i, te, tv, nu: (te[i], 0, 0)
    out_specs = [pl.BlockSpec((tm * xchunks, LANES), row_blk)]
    out_shape = [jax.ShapeDtypeStruct((p_rows * xchunks, LANES), jnp.uint32)]
    if round_weights:
        out_specs += [pl.BlockSpec((1, d, dff2), exp_blk), pl.BlockSpec((1, dff, d), exp_blk)]
        out_shape += [jax.ShapeDtypeStruct(w_gu.shape, BF16), jax.ShapeDtypeStruct(w_dn.shape, BF16)]
    return pl.pallas_call(
        functools.partial(_expert_kernel, tm=tm, d=d, dff=dff, round_weights=round_weights),
        grid_spec=pltpu.PrefetchScalarGridSpec(
            num_scalar_prefetch=3,
            grid=(p_rows // tm,),
            in_specs=[
                pl.BlockSpec((tm * xchunks, LANES), row_blk),
                pl.BlockSpec((1, d, dff2), exp_blk),
                pl.BlockSpec((1, 1, dff2), exp_blk),
                pl.BlockSpec((1, dff, d), exp_blk),
                pl.BlockSpec((1, 1, d), exp_blk),
            ],
            out_specs=out_specs,
        ),
        out_shape=out_shape,
        compiler_params=pltpu.CompilerParams(
            dimension_semantics=("arbitrary",), vmem_limit_bytes=VMEM_LIMIT_BYTES),
        name="experts",
    )(tile_expert, tile_rows, n_used, xs, w_gu, b_gu.reshape(n_exp, 1, dff2), w_dn, b_dn.reshape(n_exp, 1, d))


def _final_kernel(*refs, tb, d, in_windows, out_windows, n_alias):
    n_in, n_out = len(in_windows), len(out_windows)
    x1_refs = refs[:n_in]
    yk_ref, w_ref, nfin_ref = refs[n_in:n_in + 3]
    out_refs = refs[n_in + 3 + n_alias:n_in + 3 + n_alias + n_out]
    i = pl.program_id(0)
    w = w_ref[...]
    chunks = d // (2 * LANES)
    moe = _unpack_halves(_load_rows(yk_ref.at[0], tb, chunks)) * w[:, 0:1]
    for k in range(1, TOP_K):
        moe = moe + _unpack_halves(_load_rows(yk_ref.at[k], tb, chunks)) * w[:, k:k + 1]
    out = _rms_norm(_window_load(i, in_windows, x1_refs) + moe, nfin_ref[...])
    _window_store(i, out_windows, out_refs, out)


def _final_call(x1_parts, in_windows, yk, topw_t, norm_final, out_shapes, out_windows, out_init, *, tb):
    d = x1_parts[0].shape[1]
    n_tiles = sum(w.count for w in in_windows)
    chunks = d // (2 * LANES)
    alias_in = [a for a in out_init if a is not None]
    n_fixed = len(x1_parts) + 3
    aliases, j = {}, 0
    for k, a in enumerate(out_init):
        if a is not None:
            aliases[n_fixed + j] = k
            j += 1
    return pl.pallas_call(
        functools.partial(_final_kernel, tb=tb, d=d, in_windows=tuple(in_windows),
                          out_windows=tuple(out_windows), n_alias=len(alias_in)),
        grid=(n_tiles,),
        in_specs=[
            *_window_specs(in_windows, tb, d),
            pl.BlockSpec((TOP_K, tb * chunks, LANES), lambda i: (0, i, 0)),
            pl.BlockSpec((tb, TOP_K), lambda i: (i, 0)),
            _const_spec((1, d)),
            *[pl.BlockSpec(memory_space=pl.ANY) for _ in alias_in],
        ],
        out_specs=_window_specs(out_windows, tb, d),
        out_shape=[jax.ShapeDtypeStruct(s, F32) for s in out_shapes],
        input_output_aliases=aliases,
        compiler_params=pltpu.CompilerParams(
            dimension_semantics=("arbitrary",), vmem_limit_bytes=VMEM_LIMIT_BYTES),
        name="final",
    )(*x1_parts, yk, topw_t, norm_final.reshape(1, d), *alias_in)


def _sc_step_rows(n):
    per_worker = n // SC_WORKERS
    assert per_worker * SC_WORKERS == n
    w = max(c for c in range(SUBLANES, SC_MAX_STEP_ROWS + 1, SUBLANES) if per_worker % c == 0)
    return w, per_worker // w


def _sc_mesh():
    return plsc.VectorSubcoreMesh(core_axis_name="c", subcore_axis_name="s",
                                  num_cores=SC_CORES, num_subcores=SC_SUBCORES)


def _dispatch_rows(x_tiles, pos_steps, p_rows):
    n, chunks, _ = x_tiles.shape
    w, steps = _sc_step_rows(n)

    @functools.partial(
        pl.kernel, mesh=_sc_mesh(),
        out_type=jax.ShapeDtypeStruct((p_rows, chunks, LANES), x_tiles.dtype),
        scratch_types=[pltpu.VMEM((w, chunks, LANES), x_tiles.dtype), pltpu.VMEM((TOP_K, w), jnp.int32)],
        name="dispatch",
    )
    def run(x_hbm, pos_hbm, out_hbm, buf, idx):
        wid = lax.axis_index("s") * SC_CORES + lax.axis_index("c")

        @pl.loop(0, steps)
        def _(i):
            step = wid * steps + i
            pltpu.sync_copy(x_hbm.at[pl.ds(pl.multiple_of(step * w, SUBLANES), w)], buf)
            pltpu.sync_copy(pos_hbm.at[step], idx)
            for k in range(TOP_K):
                pltpu.sync_copy(buf, out_hbm.at[idx.at[k]])

    return run(x_tiles, pos_steps)


def _combine_rows(y_tiles, pos_steps, n):
    _, chunks, _ = y_tiles.shape
    w, steps = _sc_step_rows(n)

    @functools.partial(
        pl.kernel, mesh=_sc_mesh(),
        out_type=jax.ShapeDtypeStruct((TOP_K, n, chunks, LANES), y_tiles.dtype),
        scratch_types=[pltpu.VMEM((w, chunks, LANES), y_tiles.dtype), pltpu.VMEM((TOP_K, w), jnp.int32)],
        name="combine",
    )
    def run(y_hbm, pos_hbm, out_hbm, buf, idx):
        wid = lax.axis_index("s") * SC_CORES + lax.axis_index("c")

        @pl.loop(0, steps)
        def _(i):
            step = wid * steps + i
            pltpu.sync_copy(pos_hbm.at[step], idx)
            for k in range(TOP_K):
                pltpu.sync_copy(y_hbm.at[idx.at[k]], buf)
                pltpu.sync_copy(buf, out_hbm.at[k, pl.ds(pl.multiple_of(step * w, SUBLANES), w)])

    return run(y_tiles, pos_steps)


def _block_diag(w, per_block):
    heads, n, _ = w.shape
    w4 = w.reshape(heads // per_block, per_block, n, n)
    eye = jnp.eye(per_block, dtype=w.dtype)
    bd = jnp.einsum('chij,hg->chigj', w4, eye)
    return bd.reshape(heads // per_block, per_block * n, per_block * n)


def _mixer_weights(norm_mix, w_in, b_gate, pool_w, pool_scale, conv_w, conv_b, lru_wa, lru_ba, lru_wi,
                   lru_bi, lru_lambda, w_br_pool, w_br_lru, w_out):
    row = lambda v: v.reshape(1, -1)
    d = norm_mix.shape[0]
    head_dim = lru_wa.shape[-1]
    per_block = MXU_WIDTH // head_dim
    w_ai = jnp.concatenate([_block_diag(lru_wa, per_block), _block_diag(lru_wi, per_block)], axis=-1)
    return (row(norm_mix), w_in.astype(BF16), row(b_gate), pool_w.astype(BF16), row(pool_scale),
            conv_w.reshape(-1, d // LANES, LANES), conv_b.reshape(d // LANES, LANES), w_ai.astype(BF16),
            row(lru_ba), row(lru_bi), row(lru_lambda),
            w_br_pool.astype(BF16), w_br_lru.astype(BF16), w_out.astype(BF16))


def _run_mixer(x, state_pool, state_conv, state_lru, weights, *, start_pos, bblk, tt, seq0=0):
    bsz, d = state_lru.shape
    chunks = d // LANES
    hp = jnp.pad(state_pool, ((0, 0), (POOL_PAD - POOL_HIST, 0), (0, 0)))
    hc = jnp.pad(state_conv, ((0, 0), (CONV_PAD - CONV_HIST, 0), (0, 0))).reshape(bsz, CONV_PAD * chunks, LANES)
    hl = state_lru.reshape(bsz, chunks, LANES)
    x1, npool, nconv, nlru = _mixer_call(x, hp, hc, hl, weights, start_pos=start_pos, bblk=bblk, tt=tt,
                                         seq0=seq0)
    return x1, npool, nconv.reshape(bsz, CONV_HIST, d), nlru.reshape(bsz, d)


def _routing_plan(topi, rank, counts, *, tm, n_tiles, min_tiles):
    n_exp = counts.shape[0]
    tiles_per = jnp.maximum((counts + tm - 1) // tm, min_tiles)
    tiles_cum = jnp.cumsum(tiles_per)
    tile_start = tiles_cum - tiles_per
    base = tile_start * tm
    experts = jnp.arange(n_exp, dtype=jnp.int32)[:, None, None]
    pos = rank + jnp.sum(jnp.where(topi[None] == experts, base[:, None, None], 0), axis=0)
    n_used = tiles_cum[-1]
    tile_ids = jnp.arange(n_tiles, dtype=jnp.int32)
    live = tile_ids < n_used
    owner = (tiles_cum[None, :] <= jnp.minimum(tile_ids, n_used - 1)[:, None])
    tile_expert = jnp.sum(owner.astype(jnp.int32), axis=1)
    is_owner = tile_expert[:, None] == jnp.arange(n_exp, dtype=jnp.int32)[None, :]
    rows_left = jnp.sum(jnp.where(is_owner, counts[None, :] - (tile_ids[:, None] - tile_start[None, :]) * tm, 0),
                        axis=1)
    tile_rows = jnp.where(live, jnp.clip(rows_left, 0, tm), 0).astype(jnp.int32)
    return pos, tile_expert, tile_rows, n_used.reshape(1).astype(jnp.int32)


def _moe_rows(x1_parts, windows, norm_ffn, w_router_t, b_router, w_gu, b_gu, w_dn, b_dn):
    d = x1_parts[0].shape[1]
    chunks = d // LANES
    n_exp = w_router_t.shape[0]
    n = sum(w.count for w in windows) * TOKEN_TILE
    xn2, topi, topw, rank, cnt = _router_call(x1_parts, windows, norm_ffn, w_router_t, b_router, tb=TOKEN_TILE)
    min_tiles = 0 if w_gu.dtype == BF16 else 1
    n_tiles = (n * TOP_K + n_exp * (EXPERT_TILE - 1)) // EXPERT_TILE + n_exp * min_tiles
    p_rows = n_tiles * EXPERT_TILE
    pos, tile_expert, tile_rows, n_used = _routing_plan(
        topi, rank, cnt[:, 0].astype(jnp.int32), tm=EXPERT_TILE, n_tiles=n_tiles, min_tiles=min_tiles)
    w, steps = _sc_step_rows(n)
    pos_steps = pos.reshape(TOP_K, SC_WORKERS * steps, w).transpose(1, 0, 2)
    xchunks = d // (2 * LANES)
    xs = _dispatch_rows(xn2.reshape(n, xchunks, LANES), pos_steps, p_rows)
    ys, *w_bf = _expert_call(tile_expert, tile_rows, n_used, xs.reshape(p_rows * xchunks, LANES), w_gu, b_gu,
                             w_dn, b_dn, tm=EXPERT_TILE)
    yk = _combine_rows(ys.reshape(p_rows, xchunks, LANES), pos_steps, n)
    return yk.reshape(TOP_K, n * xchunks, LANES), topw, (w_bf if w_bf else (w_gu, w_dn))


def kernel(x_prompt, x_sample, state_pool, state_conv, state_lru, norm_mix, w_in, b_gate, pool_w, pool_scale,
           conv_w, conv_b, lru_wa, lru_ba, lru_wi, lru_bi, lru_lambda, w_br_pool, w_br_lru, w_out, norm_ffn,
           w_router, b_router, w_gu, b_gu, w_dn, b_dn, norm_final):
    bp, tp, d = x_prompt.shape
    bs, ts, _ = x_sample.shape
    n_p, n_s = bp * tp, bs * ts
    mw = _mixer_weights(norm_mix[0], w_in[0], b_gate[0], pool_w[0], pool_scale[0], conv_w[0], conv_b[0],
                        lru_wa[0], lru_ba[0], lru_wi[0], lru_bi[0], lru_lambda[0], w_br_pool[0], w_br_lru[0],
                        w_out[0])
    w_router_t = w_router[0].T.astype(BF16)
    zeros = lambda *shape: jnp.zeros(shape, x_prompt.dtype)

    assert sum(PROMPT_GROUP_SEQS) == bp and tp % TOKEN_TILE == 0 and n_s % TOKEN_TILE == 0
    seq_tiles = tp // TOKEN_TILE
    s_tiles = n_s // TOKEN_TILE
    x1_s, pool_s, conv_s, lru_s = _run_mixer(
        x_sample, state_pool[0], state_conv[0], state_lru[0], mw, start_pos=PAST_LEN, bblk=bs, tt=ts)
    y_p, y_s = None, None
    expert_w = (w_gu[0], w_dn[0])
    pools, convs, lrus = [], [], []
    seq0 = 0
    for g, seqs in enumerate(PROMPT_GROUP_SEQS):
        x1_g, pool_g, conv_g, lru_g = _run_mixer(
            x_prompt, zeros(seqs, POOL_HIST, d), zeros(seqs, CONV_HIST, d), zeros(seqs, d), mw,
            start_pos=0, bblk=1, tt=MIXER_TILE, seq0=seq0)
        pools.append(pool_g)
        convs.append(conv_g)
        lrus.append(lru_g)
        last = g == len(PROMPT_GROUP_SEQS) - 1
        g_tiles = seqs * seq_tiles
        parts, windows = [x1_g.reshape(seqs * tp, d)], [_Window(0, g_tiles)]
        out_shapes, out_windows, out_init = [(n_p, d)], [_Window(seq0 * seq_tiles, g_tiles)], [y_p]
        seq0 += seqs
        if last:
            parts.append(x1_s.reshape(n_s, d))
            windows.append(_Window(0, s_tiles))
            out_shapes.append((n_s, d))
            out_windows.append(_Window(0, s_tiles))
            out_init.append(None)
        yk, topw, expert_w = _moe_rows(parts, windows, norm_ffn[0], w_router_t, b_router[0], expert_w[0],
                                       b_gu[0], expert_w[1], b_dn[0])
        outs = _final_call(parts, windows, yk, topw.T, norm_final, out_shapes, out_windows, out_init,
                           tb=TOKEN_TILE)
        y_p = outs[0]
        if last:
            y_s = outs[1]

    cat = lambda xs: jnp.concatenate(xs, axis=0)[None]
    return (y_p.reshape(bp, tp, d), y_s.reshape(bs, ts, d), cat(pools), cat(convs), cat(lrus),
            pool_s[None], conv_s[None], lru_s[None])
```

```python
import functools
from typing import NamedTuple

import jax
import jax.numpy as jnp
from jax import lax
from jax.experimental import pallas as pl
from jax.experimental.pallas import tpu as pltpu
from jax.experimental.pallas import tpu_sc as plsc

BF16 = jnp.bfloat16
F32 = jnp.float32

POOL_WINDOWS = (2, 4, 8, 16)
POOL_HIST = max(POOL_WINDOWS) - 1
CONV_WIDTH = 4
CONV_HIST = CONV_WIDTH - 1
LRU_C = 8.0
LRU_SEGMENTS = 8
TOP_K = 4
SWIGLU_LIMIT = 7.0
SWIGLU_ALPHA = 1.702
NORM_EPS = 1e-6

SUBLANES = 8
LANES = 128
MXU_WIDTH = 256
POOL_PAD = -(-POOL_HIST // SUBLANES) * SUBLANES
CONV_PAD = -(-CONV_HIST // SUBLANES) * SUBLANES
V7X_VMEM_BYTES = 64 * 1024 * 1024
VMEM_LIMIT_BYTES = V7X_VMEM_BYTES * 7 // 8

PAST_LEN = 1024
MIXER_TILE = 256
TOKEN_TILE = 512
EXPERT_TILE = 1024
EXPERT_SUBTILE = 512
PROMPT_GROUP_SEQS = (8, 8)

SC_CORES = 2
SC_SUBCORES = 16
SC_WORKERS = SC_CORES * SC_SUBCORES
SC_MAX_STEP_ROWS = 64


def _rms_norm(x, g):
    ms = jnp.mean(x * x, axis=-1, keepdims=True)
    return (x * lax.rsqrt(ms + NORM_EPS)) * g


def _dot(a, b):
    return jnp.dot(a, b, preferred_element_type=F32)


def _softplus(x):
    return jnp.maximum(x, 0.0) + jnp.log1p(jnp.exp(-jnp.abs(x)))


def _gelu_tanh(x):
    c = 0.7978845608028654
    return 0.5 * x * (1.0 + jnp.tanh(c * (x + 0.044715 * (x * x * x))))


def _lru_scan(a_ref, b_ref, h_ref, h0, frames, chunks):
    tile = lambda ref, t: ref[t * chunks:(t + 1) * chunks, :]
    seg = frames // LRU_SEGMENTS
    assert seg * LRU_SEGMENTS == frames
    prods = [tile(a_ref, s * seg) for s in range(LRU_SEGMENTS)]
    sums = [tile(b_ref, s * seg) for s in range(LRU_SEGMENTS)]
    for t in range(1, seg):
        for s in range(LRU_SEGMENTS):
            a_t = tile(a_ref, s * seg + t)
            sums[s] = a_t * sums[s] + tile(b_ref, s * seg + t)
            prods[s] = a_t * prods[s]
            b_ref[(s * seg + t) * chunks:(s * seg + t + 1) * chunks, :] = sums[s]
            a_ref[(s * seg + t) * chunks:(s * seg + t + 1) * chunks, :] = prods[s]
    h_in = [h0]
    for s in range(LRU_SEGMENTS):
        h_in.append(prods[s] * h_in[s] + sums[s])
    for s in range(LRU_SEGMENTS):
        for t in range(seg):
            f = s * seg + t
            h_ref[f * chunks:(f + 1) * chunks, :] = tile(a_ref, f) * h_in[s] + tile(b_ref, f)
    return h_in[LRU_SEGMENTS]


def _mixer_kernel(x_ref, hp_ref, hc_ref, hl_ref, nmix_ref, win_ref, bgate_ref, poolw_ref, pscale_ref,
                  convw_ref, convb_ref, wai_ref, ba_ref, bi_ref, lam_ref, wbrp_ref, wbrl_ref, wout_ref,
                  x1_ref, npool_ref, nconv_ref, nlru_ref,
                  pool_ext, conv_ext, h_carry, frames_a, frames_b, frames_h, *, start_pos, bblk, tt, d):
    j = pl.program_id(1)

    @pl.when(j == 0)
    def _():
        pool_ext[:, 0:POOL_PAD, :] = hp_ref[...]
        conv_ext[:, 0:CONV_PAD * (d // LANES), :] = hc_ref[...]
        h_carry[...] = hl_ref[...]

    rows = bblk * tt
    chunks = d // LANES
    cat_rows = lambda parts: jnp.concatenate(parts, axis=0) if len(parts) > 1 else parts[0]
    x = x_ref[...].reshape(rows, d)
    xnb = _rms_norm(x, nmix_ref[...]).astype(BF16)

    u_x = _dot(xnb, win_ref[:, d:2 * d])
    xc_rows = []
    for b in range(bblk):
        _store_rows(conv_ext.at[b, pl.ds(CONV_PAD * chunks, tt * chunks)], u_x[b * tt:(b + 1) * tt, :], tt, chunks)
        ce = conv_ext[b].reshape(CONV_PAD + tt, chunks, LANES)
        y = convb_ref[...] + ce[CONV_PAD:CONV_PAD + tt] * convw_ref[CONV_WIDTH - 1]
        for k in range(1, CONV_WIDTH):
            y = y + ce[CONV_PAD - k:CONV_PAD - k + tt] * convw_ref[CONV_WIDTH - 1 - k]
        frames_h[...] = y.reshape(tt * chunks, LANES)
        xc_rows.append(_load_rows(frames_h, tt, chunks))
        nconv_ref[b] = conv_ext[b, (CONV_PAD + tt - CONV_HIST) * chunks:(CONV_PAD + tt) * chunks, :]
        conv_ext[b, 0:CONV_PAD * chunks, :] = conv_ext[b, tt * chunks:(tt + CONV_PAD) * chunks, :]
    xc = cat_rows(xc_rows)

    n_chunks = d // MXU_WIDTH
    pre_a, pre_i = [], []
    for c in range(n_chunks):
        ai = _dot(xc[:, c * MXU_WIDTH:(c + 1) * MXU_WIDTH].astype(BF16), wai_ref[c])
        pre_a.append(ai[:, 0:MXU_WIDTH])
        pre_i.append(ai[:, MXU_WIDTH:2 * MXU_WIDTH])

    u_pool = _dot(xnb, win_ref[:, 0:d])
    pos1 = lax.broadcasted_iota(jnp.int32, (tt, 1), 0) + (start_pos + 1) + j * tt
    gd = d // len(POOL_WINDOWS)
    y_pool_rows = []
    for b in range(bblk):
        pool_ext[b, POOL_PAD:POOL_PAD + tt, :] = u_pool[b * tt:(b + 1) * tt, :]
        parts = []
        for g, w in enumerate(POOL_WINDOWS):
            sl = slice(g * gd, (g + 1) * gd)
            s = pool_ext[b, :, sl]
            width = 1
            while width < w:
                s = s + pltpu.roll(s, width, axis=0)
                width *= 2
            cur = s[POOL_PAD:POOL_PAD + tt, :]
            inv_cnt = 1.0 / jnp.minimum(pos1, w).astype(F32)
            pooled = cur * inv_cnt - u_pool[b * tt:(b + 1) * tt, sl]
            mixed = _dot(pooled.astype(BF16), poolw_ref[g])
            parts.append(mixed * pscale_ref[:, sl])
        y_pool_rows.append(jnp.concatenate(parts, axis=1))
        npool_ref[b] = pool_ext[b, POOL_PAD + tt - POOL_HIST:POOL_PAD + tt, :]
        pool_ext[b, 0:POOL_PAD, :] = pool_ext[b, tt:tt + POOL_PAD, :]
    y_pool = cat_rows(y_pool_rows)

    r_gate = jax.nn.sigmoid(jnp.concatenate(pre_a, axis=1) + ba_ref[...])
    i_gate = jax.nn.sigmoid(jnp.concatenate(pre_i, axis=1) + bi_ref[...])
    log_a = (-LRU_C) * r_gate * _softplus(-lam_ref[...])
    a = jnp.exp(log_a)
    mult = jnp.sqrt(-jnp.tanh(log_a) * (a * a + 1.0))
    bb = mult * (i_gate * xc)
    gl_pool = _dot(xnb, win_ref[:, 3 * d:4 * d])
    u_gate = _dot(xnb, win_ref[:, 2 * d:3 * d])
    h_rows = []
    for b in range(bblk):
        _store_rows(frames_a, a[b * tt:(b + 1) * tt, :], tt, chunks)
        _store_rows(frames_b, bb[b * tt:(b + 1) * tt, :], tt, chunks)
        h_last = _lru_scan(frames_a, frames_b, frames_h, h_carry[b], tt, chunks)
        h_carry[b] = h_last
        nlru_ref[b] = h_last
        h_rows.append(_load_rows(frames_h, tt, chunks))
    h = cat_rows(h_rows)
    br_pool = _dot(y_pool.astype(BF16), wbrp_ref[...])
    gl_lru = _dot(xnb, win_ref[:, 4 * d:5 * d])

    y_lru = h * _gelu_tanh(u_gate)
    br_lru = _dot(y_lru.astype(BF16), wbrl_ref[...])
    g_pool = jax.nn.sigmoid(gl_pool + bgate_ref[:, 0:d])
    g_lru = jax.nn.sigmoid(gl_lru + bgate_ref[:, d:2 * d])
    acc = g_pool * br_pool + g_lru * br_lru
    out = _dot(acc.astype(BF16), wout_ref[...])
    x1_ref[...] = (x + out).reshape(bblk, tt, d)


def _const_spec(shape):
    nd = len(shape)
    return pl.BlockSpec(shape, lambda *_: (0,) * nd, pipeline_mode=pl.Buffered(1))


def _mixer_call(x, hist_pool, hist_conv, hist_lru, weights, *, start_pos, bblk, tt, seq0=0):
    _, t_len, d = x.shape
    bsz = hist_pool.shape[0]
    assert seq0 % bblk == 0
    grid = (bsz // bblk, t_len // tt)
    kern = functools.partial(_mixer_kernel, start_pos=start_pos, bblk=bblk, tt=tt, d=d)
    chunks = d // LANES
    seq_spec = lambda rows: pl.BlockSpec((bblk, rows, d), lambda b, j: (b, 0, 0))
    frame_spec = lambda frames: pl.BlockSpec((bblk, frames * chunks, LANES), lambda b, j: (b, 0, 0))
    in_specs = [
        pl.BlockSpec((bblk, tt, d), lambda b, j: (b + seq0 // bblk, j, 0)),
        seq_spec(POOL_PAD), frame_spec(CONV_PAD), frame_spec(1),
    ] + [_const_spec(w.shape) for w in weights]
    out_shape = (
        jax.ShapeDtypeStruct((bsz, t_len, d), F32),
        jax.ShapeDtypeStruct((bsz, POOL_HIST, d), F32),
        jax.ShapeDtypeStruct((bsz, CONV_HIST * chunks, LANES), F32),
        jax.ShapeDtypeStruct((bsz, chunks, LANES), F32),
    )
    out_specs = (
        pl.BlockSpec((bblk, tt, d), lambda b, j: (b, j, 0)),
        seq_spec(POOL_HIST), frame_spec(CONV_HIST), frame_spec(1),
    )
    return pl.pallas_call(
        kern,
        grid=grid,
        in_specs=in_specs,
        out_specs=out_specs,
        out_shape=out_shape,
        scratch_shapes=[
            pltpu.VMEM((bblk, POOL_PAD + tt, d), F32),
            pltpu.VMEM((bblk, (CONV_PAD + tt) * chunks, LANES), F32),
            pltpu.VMEM((bblk, chunks, LANES), F32),
            pltpu.VMEM((tt * chunks, LANES), F32),
            pltpu.VMEM((tt * chunks, LANES), F32),
            pltpu.VMEM((tt * chunks, LANES), F32),
        ],
        compiler_params=pltpu.CompilerParams(
            dimension_semantics=("arbitrary", "arbitrary"), vmem_limit_bytes=VMEM_LIMIT_BYTES),
        name="mixer",
    )(x, hist_pool, hist_conv, hist_lru, *weights)


def _load_rows(ref, rows, chunks):
    return jnp.concatenate([ref[pl.ds(s, rows, stride=chunks), :] for s in range(chunks)], axis=1)


def _store_rows(ref, val, rows, chunks):
    for s in range(chunks):
        ref[pl.ds(s, rows, stride=chunks), :] = val[:, s * LANES:(s + 1) * LANES]


def _pack_halves(x):
    half = x.shape[1] // 2
    bits = lax.bitcast_convert_type(x, jnp.uint32)
    return (bits[:, :half] >> 16) | (bits[:, half:] & jnp.uint32(0xFFFF0000))


def _unpack_halves(p):
    lo = lax.bitcast_convert_type(p << 16, F32)
    hi = lax.bitcast_convert_type(p & jnp.uint32(0xFFFF0000), F32)
    return jnp.concatenate([lo, hi], axis=1)


class _Window(NamedTuple):
    first: int
    count: int


def _window_specs(windows, tb, d):
    specs, start = [], 0
    for win in windows:
        specs.append(pl.BlockSpec(
            (tb, d), lambda i, win=win, start=start: (win.first + jnp.clip(i - start, 0, win.count - 1), 0)))
        start += win.count
    return specs


def _window_load(i, windows, refs):
    val, start = refs[-1][...], sum(w.count for w in windows[:-1])
    for win, ref in zip(reversed(windows[:-1]), reversed(refs[:-1])):
        val = jnp.where(i < start, ref[...], val)
        start -= win.count
    return val


def _window_store(i, windows, refs, val):
    start = 0
    for win, ref in zip(windows, refs):
        @pl.when((i >= start) & (i < start + win.count))
        def _(ref=ref):
            ref[...] = val
        start += win.count


def _router_kernel(*refs, tb, d, windows):
    x1_refs = refs[:len(windows)]
    (nffn_ref, wrt_ref, br_ref, tri_ref, xn_ref, topi_ref, topw_ref, rank_ref, cnt_ref, carry) = refs[len(windows):]
    i = pl.program_id(0)

    @pl.when(i == 0)
    def _():
        carry[...] = jnp.zeros_like(carry)

    xnb = _rms_norm(_window_load(i, windows, x1_refs), nffn_ref[...]).astype(BF16)
    xn_ref_chunks = d // (2 * LANES)
    _store_rows(xn_ref, _pack_halves(xnb.astype(F32)), tb, xn_ref_chunks)
    logits = lax.dot_general(wrt_ref[...], xnb, (((1,), (1,)), ((), ())),
                             preferred_element_type=F32) + br_ref[...]
    n_exp = logits.shape[0]
    iota_e = lax.broadcasted_iota(jnp.int32, (n_exp, tb), 0)
    work = logits
    vals, idxs, hots = [], [], []
    for _ in range(TOP_K):
        m = jnp.max(work, axis=0, keepdims=True)
        idx = jnp.min(jnp.where(work == m, iota_e, n_exp), axis=0, keepdims=True)
        hot = iota_e == idx
        vals.append(m)
        idxs.append(idx)
        hots.append(hot)
        work = jnp.where(hot, -jnp.inf, work)
    exps = [jnp.exp(v - vals[0]) for v in vals]
    denom = exps[0] + exps[1] + exps[2] + exps[3]
    topi_ref[...] = jnp.concatenate(idxs, axis=0)
    topw_ref[...] = jnp.concatenate([e / denom for e in exps], axis=0)

    sel = sum(jnp.where(h, 1.0, 0.0) for h in hots)
    before = _dot(sel.astype(BF16), tri_ref[...]) + carry[:, 0:1]
    ranks = [jnp.sum(jnp.where(h, before, 0.0), axis=0, keepdims=True) for h in hots]
    rank_ref[...] = jnp.concatenate(ranks, axis=0).astype(jnp.int32)
    carry[...] = carry[...] + jnp.sum(sel, axis=1, keepdims=True)
    cnt_ref[...] = carry[...]


def _router_call(x1_parts, windows, norm_ffn, w_router_t, b_router, *, tb):
    d = x1_parts[0].shape[1]
    n_tiles = sum(w.count for w in windows)
    n = n_tiles * tb
    chunks = d // (2 * LANES)
    n_exp = w_router_t.shape[0]
    tri = jnp.triu(jnp.ones((tb, tb), BF16), k=1)
    tok = lambda rows, dt: (pl.BlockSpec((rows, tb), lambda i: (0, i)), jax.ShapeDtypeStruct((rows, n), dt))
    (topi_spec, topi_shape), (topw_spec, topw_shape), (rank_spec, rank_shape) = (
        tok(TOP_K, jnp.int32), tok(TOP_K, F32), tok(TOP_K, jnp.int32))
    return pl.pallas_call(
        functools.partial(_router_kernel, tb=tb, d=d, windows=tuple(windows)),
        grid=(n_tiles,),
        in_specs=[
            *_window_specs(windows, tb, d),
            _const_spec((1, d)), _const_spec((n_exp, d)), _const_spec((n_exp, 1)), _const_spec((tb, tb)),
        ],
        out_specs=(
            pl.BlockSpec((tb * chunks, LANES), lambda i: (i, 0)),
            topi_spec, topw_spec, rank_spec,
            pl.BlockSpec((n_exp, LANES), lambda i: (0, 0)),
        ),
        out_shape=(
            jax.ShapeDtypeStruct((n * chunks, LANES), jnp.uint32),
            topi_shape, topw_shape, rank_shape,
            jax.ShapeDtypeStruct((n_exp, LANES), F32),
        ),
        scratch_shapes=[pltpu.VMEM((n_exp, LANES), F32)],
        compiler_params=pltpu.CompilerParams(
            dimension_semantics=("arbitrary",), vmem_limit_bytes=VMEM_LIMIT_BYTES),
        name="router",
    )(*x1_parts, norm_ffn.reshape(1, d), w_router_t, b_router.reshape(n_exp, 1), tri)


def _expert_kernel(te_ref, tv_ref, nu_ref, xs_ref, wgu_ref, bgu_ref, wdn_ref, bdn_ref, ys_ref, *bf_refs,
                   tm, d, dff, round_weights):
    del nu_ref
    i = pl.program_id(0)
    if round_weights:
        wgu_bf, wdn_bf = bf_refs
        prev = te_ref[jnp.maximum(i - 1, 0)]

        @pl.when((i == 0) | (te_ref[i] != prev))
        def _():
            wgu_bf[...] = wgu_ref[...].astype(BF16)
            wdn_bf[...] = wdn_ref[...].astype(BF16)
    else:
        wgu_bf, wdn_bf = wgu_ref, wdn_ref

    xchunks = d // (2 * LANES)

    def ffn(row0, rows):
        window = pl.ds(row0 * xchunks, rows * xchunks)
        x = _unpack_halves(_load_rows(xs_ref.at[window], rows, xchunks)).astype(BF16)
        gu = _dot(x, wgu_bf[0]) + bgu_ref[0]
        gate = jnp.minimum(gu[:, :dff], SWIGLU_LIMIT)
        up = jnp.clip(gu[:, dff:], -SWIGLU_LIMIT, SWIGLU_LIMIT)
        hid = (up + 1.0) * gate * jax.nn.sigmoid(SWIGLU_ALPHA * gate)
        y = _dot(hid.astype(BF16), wdn_bf[0]) + bdn_ref[0]
        _store_rows(ys_ref.at[window], _pack_halves(y.astype(BF16).astype(F32)), rows, xchunks)

    sub = EXPERT_SUBTILE
    quarter = sub // 4
    for s in range(tm // sub):
        valid = tv_ref[i] - s * sub
        for q in range(1, 5):
            lo = (q - 1) * quarter
            in_band = (valid > lo) if q == 4 else (valid > lo) & (valid <= q * quarter)
            pl.when(in_band)(functools.partial(ffn, s * sub, q * quarter))


def _expert_call(tile_expert, tile_rows, n_used, xs, w_gu, b_gu, w_dn, b_dn, *, tm):
    n_exp, d, dff2 = w_gu.shape
    dff = dff2 // 2
    xchunks = d // (2 * LANES)
    p_rows = xs.shape[0] // xchunks
    round_weights = w_gu.dtype != BF16
    row_blk = lambda i, te, tv, nu: (jnp.minimum(i, nu[0] - 1), 0)
    exp_blk = lambda i, te, tv, nu: (te[i], 0, 0)
    out_specs = [pl.BlockSpec((tm * xchunks, LANES), row_blk)]
    out_shape = [jax.ShapeDtypeStruct((p_rows * xchunks, LANES), jnp.uint32)]
    if round_weights:
        out_specs += [pl.BlockSpec((1, d, dff2), exp_blk), pl.BlockSpec((1, dff, d), exp_blk)]
        out_shape += [jax.ShapeDtypeStruct(w_gu.shape, BF16), jax.ShapeDtypeStruct(w_dn.shape, BF16)]
    return pl.pallas_call(
        functools.partial(_expert_kernel, tm=tm, d=d, dff=dff, round_weights=round_weights),
        grid_spec=pltpu.PrefetchScalarGridSpec(
            num_scalar_prefetch=3,
            grid=(p_rows // tm,),
            in_specs=[
                pl.BlockSpec((tm * xchunks, LANES), row_blk),
                pl.BlockSpec((1, d, dff2), exp_blk),
                pl.BlockSpec((1, 1, dff2), exp_blk),
                pl.BlockSpec((1, dff, d), exp_blk),
                pl.BlockSpec((1, 1, d), exp_blk),
            ],
            out_specs=out_specs,
        ),
        out_shape=out_shape,
        compiler_params=pltpu.CompilerParams(
            dimension_semantics=("arbitrary",), vmem_limit_bytes=VMEM_LIMIT_BYTES),
        name="experts",
    )(tile_expert, tile_rows, n_used, xs, w_gu, b_gu.reshape(n_exp, 1, dff2), w_dn, b_dn.reshape(n_exp, 1, d))


def _final_kernel(*refs, tb, d, in_windows, out_windows, n_alias):
    n_in, n_out = len(in_windows), len(out_windows)
    x1_refs = refs[:n_in]
    yk_ref, w_ref, nfin_ref = refs[n_in:n_in + 3]
    out_refs = refs[n_in + 3 + n_alias:n_in + 3 + n_alias + n_out]
    i = pl.program_id(0)
    w = w_ref[...]
    chunks = d // (2 * LANES)
    moe = _unpack_halves(_load_rows(yk_ref.at[0], tb, chunks)) * w[:, 0:1]
    for k in range(1, TOP_K):
        moe = moe + _unpack_halves(_load_rows(yk_ref.at[k], tb, chunks)) * w[:, k:k + 1]
    out = _rms_norm(_window_load(i, in_windows, x1_refs) + moe, nfin_ref[...])
    _window_store(i, out_windows, out_refs, out)


def _final_call(x1_parts, in_windows, yk, topw_t, norm_final, out_shapes, out_windows, out_init, *, tb):
    d = x1_parts[0].shape[1]
    n_tiles = sum(w.count for w in in_windows)
    chunks = d // (2 * LANES)
    alias_in = [a for a in out_init if a is not None]
    n_fixed = len(x1_parts) + 3
    aliases, j = {}, 0
    for k, a in enumerate(out_init):
        if a is not None:
            aliases[n_fixed + j] = k
            j += 1
    return pl.pallas_call(
        functools.partial(_final_kernel, tb=tb, d=d, in_windows=tuple(in_windows),
                          out_windows=tuple(out_windows), n_alias=len(alias_in)),
        grid=(n_tiles,),
        in_specs=[
            *_window_specs(in_windows, tb, d),
            pl.BlockSpec((TOP_K, tb * chunks, LANES), lambda i: (0, i, 0)),
            pl.BlockSpec((tb, TOP_K), lambda i: (i, 0)),
            _const_spec((1, d)),
            *[pl.BlockSpec(memory_space=pl.ANY) for _ in alias_in],
        ],
        out_specs=_window_specs(out_windows, tb, d),
        out_shape=[jax.ShapeDtypeStruct(s, F32) for s in out_shapes],
        input_output_aliases=aliases,
        compiler_params=pltpu.CompilerParams(
            dimension_semantics=("arbitrary",), vmem_limit_bytes=VMEM_LIMIT_BYTES),
        name="final",
    )(*x1_parts, yk, topw_t, norm_final.reshape(1, d), *alias_in)


def _sc_step_rows(n):
    per_worker = n // SC_WORKERS
    assert per_worker * SC_WORKERS == n
    w = max(c for c in range(SUBLANES, SC_MAX_STEP_ROWS + 1, SUBLANES) if per_worker % c == 0)
    return w, per_worker // w


def _sc_mesh():
    return plsc.VectorSubcoreMesh(core_axis_name="c", subcore_axis_name="s",
                                  num_cores=SC_CORES, num_subcores=SC_SUBCORES)


def _dispatch_rows(x_tiles, pos_steps, p_rows):
    n, chunks, _ = x_tiles.shape
    w, steps = _sc_step_rows(n)

    @functools.partial(
        pl.kernel, mesh=_sc_mesh(),
        out_type=jax.ShapeDtypeStruct((p_rows, chunks, LANES), x_tiles.dtype),
        scratch_types=[pltpu.VMEM((w, chunks, LANES), x_tiles.dtype), pltpu.VMEM((TOP_K, w), jnp.int32)],
        name="dispatch",
    )
    def run(x_hbm, pos_hbm, out_hbm, buf, idx):
        wid = lax.axis_index("s") * SC_CORES + lax.axis_index("c")

        @pl.loop(0, steps)
        def _(i):
            step = wid * steps + i
            pltpu.sync_copy(x_hbm.at[pl.ds(pl.multiple_of(step * w, SUBLANES), w)], buf)
            pltpu.sync_copy(pos_hbm.at[step], idx)
            for k in range(TOP_K):
                pltpu.sync_copy(buf, out_hbm.at[idx.at[k]])

    return run(x_tiles, pos_steps)


def _combine_rows(y_tiles, pos_steps, n):
    _, chunks, _ = y_tiles.shape
    w, steps = _sc_step_rows(n)

    @functools.partial(
        pl.kernel, mesh=_sc_mesh(),
        out_type=jax.ShapeDtypeStruct((TOP_K, n, chunks, LANES), y_tiles.dtype),
        scratch_types=[pltpu.VMEM((w, chunks, LANES), y_tiles.dtype), pltpu.VMEM((TOP_K, w), jnp.int32)],
        name="combine",
    )
    def run(y_hbm, pos_hbm, out_hbm, buf, idx):
        wid = lax.axis_index("s") * SC_CORES + lax.axis_index("c")

        @pl.loop(0, steps)
        def _(i):
            step = wid * steps + i
            pltpu.sync_copy(pos_hbm.at[step], idx)
            for k in range(TOP_K):
                pltpu.sync_copy(y_hbm.at[idx.at[k]], buf)
                pltpu.sync_copy(buf, out_hbm.at[k, pl.ds(pl.multiple_of(step * w, SUBLANES), w)])

    return run(y_tiles, pos_steps)


def _block_diag(w, per_block):
    heads, n, _ = w.shape
    w4 = w.reshape(heads // per_block, per_block, n, n)
    eye = jnp.eye(per_block, dtype=w.dtype)
    bd = jnp.einsum('chij,hg->chigj', w4, eye)
    return bd.reshape(heads // per_block, per_block * n, per_block * n)


def _mixer_weights(norm_mix, w_in, b_gate, pool_w, pool_scale, conv_w, conv_b, lru_wa, lru_ba, lru_wi,
                   lru_bi, lru_lambda, w_br_pool, w_br_lru, w_out):
    row = lambda v: v.reshape(1, -1)
    d = norm_mix.shape[0]
    head_dim = lru_wa.shape[-1]
    per_block = MXU_WIDTH // head_dim
    w_ai = jnp.concatenate([_block_diag(lru_wa, per_block), _block_diag(lru_wi, per_block)], axis=-1)
    return (row(norm_mix), w_in.astype(BF16), row(b_gate), pool_w.astype(BF16), row(pool_scale),
            conv_w.reshape(-1, d // LANES, LANES), conv_b.reshape(d // LANES, LANES), w_ai.astype(BF16),
            row(lru_ba), row(lru_bi), row(lru_lambda),
            w_br_pool.astype(BF16), w_br_lru.astype(BF16), w_out.astype(BF16))


def _run_mixer(x, state_pool, state_conv, state_lru, weights, *, start_pos, bblk, tt, seq0=0):
    bsz, d = state_lru.shape
    chunks = d // LANES
    hp = jnp.pad(state_pool, ((0, 0), (POOL_PAD - POOL_HIST, 0), (0, 0)))
    hc = jnp.pad(state_conv, ((0, 0), (CONV_PAD - CONV_HIST, 0), (0, 0))).reshape(bsz, CONV_PAD * chunks, LANES)
    hl = state_lru.reshape(bsz, chunks, LANES)
    x1, npool, nconv, nlru = _mixer_call(x, hp, hc, hl, weights, start_pos=start_pos, bblk=bblk, tt=tt,
                                         seq0=seq0)
    return x1, npool, nconv.reshape(bsz, CONV_HIST, d), nlru.reshape(bsz, d)


def _routing_plan(topi, rank, counts, *, tm, n_tiles, min_tiles):
    n_exp = counts.shape[0]
    tiles_per = jnp.maximum((counts + tm - 1) // tm, min_tiles)
    tiles_cum = jnp.cumsum(tiles_per)
    tile_start = tiles_cum - tiles_per
    base = tile_start * tm
    experts = jnp.arange(n_exp, dtype=jnp.int32)[:, None, None]
    pos = rank + jnp.sum(jnp.where(topi[None] == experts, base[:, None, None], 0), axis=0)
    n_used = tiles_cum[-1]
    tile_ids = jnp.arange(n_tiles, dtype=jnp.int32)
    live = tile_ids < n_used
    owner = (tiles_cum[None, :] <= jnp.minimum(tile_ids, n_used - 1)[:, None])
    tile_expert = jnp.sum(owner.astype(jnp.int32), axis=1)
    is_owner = tile_expert[:, None] == jnp.arange(n_exp, dtype=jnp.int32)[None, :]
    rows_left = jnp.sum(jnp.where(is_owner, counts[None, :] - (tile_ids[:, None] - tile_start[None, :]) * tm, 0),
                        axis=1)
    tile_rows = jnp.where(live, jnp.clip(rows_left, 0, tm), 0).astype(jnp.int32)
    return pos, tile_expert, tile_rows, n_used.reshape(1).astype(jnp.int32)


def _moe_rows(x1_parts, windows, norm_ffn, w_router_t, b_router, w_gu, b_gu, w_dn, b_dn):
    d = x1_parts[0].shape[1]
    chunks = d // LANES
    n_exp = w_router_t.shape[0]
    n = sum(w.count for w in windows) * TOKEN_TILE
    xn2, topi, topw, rank, cnt = _router_call(x1_parts, windows, norm_ffn, w_router_t, b_router, tb=TOKEN_TILE)
    min_tiles = 0 if w_gu.dtype == BF16 else 1
    n_tiles = (n * TOP_K + n_exp * (EXPERT_TILE - 1)) // EXPERT_TILE + n_exp * min_tiles
    p_rows = n_tiles * EXPERT_TILE
    pos, tile_expert, tile_rows, n_used = _routing_plan(
        topi, rank, cnt[:, 0].astype(jnp.int32), tm=EXPERT_TILE, n_tiles=n_tiles, min_tiles=min_tiles)
    w, steps = _sc_step_rows(n)
    pos_steps = pos.reshape(TOP_K, SC_WORKERS * steps, w).transpose(1, 0, 2)
    xchunks = d // (2 * LANES)
    xs = _dispatch_rows(xn2.reshape(n, xchunks, LANES), pos_steps, p_rows)
    ys, *w_bf = _expert_call(tile_expert, tile_rows, n_used, xs.reshape(p_rows * xchunks, LANES), w_gu, b_gu,
                             w_dn, b_dn, tm=EXPERT_TILE)
    yk = _combine_rows(ys.reshape(p_rows, xchunks, LANES), pos_steps, n)
    return yk.reshape(TOP_K, n * xchunks, LANES), topw, (w_bf if w_bf else (w_gu, w_dn))


def kernel(x_prompt, x_sample, state_pool, state_conv, state_lru, norm_mix, w_in, b_gate, pool_w, pool_scale,
           conv_w, conv_b, lru_wa, lru_ba, lru_wi, lru_bi, lru_lambda, w_br_pool, w_br_lru, w_out, norm_ffn,
           w_router, b_router, w_gu, b_gu, w_dn, b_dn, norm_final):
    bp, tp, d = x_prompt.shape
    bs, ts, _ = x_sample.shape
    n_p, n_s = bp * tp, bs * ts
    mw = _mixer_weights(norm_mix[0], w_in[0], b_gate[0], pool_w[0], pool_scale[0], conv_w[0], conv_b[0],
                        lru_wa[0], lru_ba[0], lru_wi[0], lru_bi[0], lru_lambda[0], w_br_pool[0], w_br_lru[0],
                        w_out[0])
    w_router_t = w_router[0].T.astype(BF16)
    zeros = lambda *shape: jnp.zeros(shape, x_prompt.dtype)

    assert sum(PROMPT_GROUP_SEQS) == bp and tp % TOKEN_TILE == 0 and n_s % TOKEN_TILE == 0
    seq_tiles = tp // TOKEN_TILE
    s_tiles = n_s // TOKEN_TILE
    x1_s, pool_s, conv_s, lru_s = _run_mixer(
        x_sample, state_pool[0], state_conv[0], state_lru[0], mw, start_pos=PAST_LEN, bblk=bs, tt=ts)
    y_p, y_s = None, None
    expert_w = (w_gu[0], w_dn[0])
    pools, convs, lrus = [], [], []
    seq0 = 0
    for g, seqs in enumerate(PROMPT_GROUP_SEQS):
        x1_g, pool_g, conv_g, lru_g = _run_mixer(
            x_prompt, zeros(seqs, POOL_HIST, d), zeros(seqs, CONV_HIST, d), zeros(seqs, d), mw,
            start_pos=0, bblk=1, tt=MIXER_TILE, seq0=seq0)
        pools.append(pool_g)
        convs.append(conv_g)
        lrus.append(lru_g)
        last = g == len(PROMPT_GROUP_SEQS) - 1
        g_tiles = seqs * seq_tiles
        parts, windows = [x1_g.reshape(seqs * tp, d)], [_Window(0, g_tiles)]
        out_shapes, out_windows, out_init = [(n_p, d)], [_Window(seq0 * seq_tiles, g_tiles)], [y_p]
        seq0 += seqs
        if last:
            parts.append(x1_s.reshape(n_s, d))
            windows.append(_Window(0, s_tiles))
            out_shapes.append((n_s, d))
            out_windows.append(_Window(0, s_tiles))
            out_init.append(None)
        yk, topw, expert_w = _moe_rows(parts, windows, norm_ffn[0], w_router_t, b_router[0], expert_w[0],
                                       b_gu[0], expert_w[1], b_dn[0])
        outs = _final_call(parts, windows, yk, topw.T, norm_final, out_shapes, out_windows, out_init,
                           tb=TOKEN_TILE)
        y_p = outs[0]
        if last:
            y_s = outs[1]

    cat = lambda xs: jnp.concatenate(xs, axis=0)[None]
    return (y_p.reshape(bp, tp, d), y_s.reshape(bs, ts, d), cat(pools), cat(convs), cat(lrus),
            pool_s[None], conv_s[None], lru_s[None])
```

```python
import functools
from typing import NamedTuple

import jax
import jax.numpy as jnp
from jax import lax
from jax.experimental import pallas as pl
from jax.experimental.pallas import tpu as pltpu
from jax.experimental.pallas import tpu_sc as plsc

BF16 = jnp.bfloat16
F32 = jnp.float32

POOL_WINDOWS = (2, 4, 8, 16)
POOL_HIST = max(POOL_WINDOWS) - 1
CONV_WIDTH = 4
CONV_HIST = CONV_WIDTH - 1
LRU_C = 8.0
LRU_SEGMENTS = 8
TOP_K = 4
SWIGLU_LIMIT = 7.0
SWIGLU_ALPHA = 1.702
NORM_EPS = 1e-6

SUBLANES = 8
LANES = 128
MXU_WIDTH = 256
POOL_PAD = -(-POOL_HIST // SUBLANES) * SUBLANES
CONV_PAD = -(-CONV_HIST // SUBLANES) * SUBLANES
V7X_VMEM_BYTES = 64 * 1024 * 1024
VMEM_LIMIT_BYTES = V7X_VMEM_BYTES * 7 // 8

PAST_LEN = 1024
MIXER_TILE = 256
TOKEN_TILE = 512
EXPERT_TILE = 1024
EXPERT_SUBTILE = 512
PROMPT_GROUP_SEQS = (10, 6)

SC_CORES = 2
SC_SUBCORES = 16
SC_WORKERS = SC_CORES * SC_SUBCORES
SC_MAX_STEP_ROWS = 64


def _rms_norm(x, g):
    ms = jnp.mean(x * x, axis=-1, keepdims=True)
    return (x * lax.rsqrt(ms + NORM_EPS)) * g


def _dot(a, b):
    return jnp.dot(a, b, preferred_element_type=F32)


def _softplus(x):
    return jnp.maximum(x, 0.0) + jnp.log1p(jnp.exp(-jnp.abs(x)))


def _gelu_tanh(x):
    c = 0.7978845608028654
    return 0.5 * x * (1.0 + jnp.tanh(c * (x + 0.044715 * (x * x * x))))


def _lru_scan(a_ref, b_ref, h_ref, h0, frames, chunks):
    tile = lambda ref, t: ref[t * chunks:(t + 1) * chunks, :]
    seg = frames // LRU_SEGMENTS
    assert seg * LRU_SEGMENTS == frames
    prods = [tile(a_ref, s * seg) for s in range(LRU_SEGMENTS)]
    sums = [tile(b_ref, s * seg) for s in range(LRU_SEGMENTS)]
    for t in range(1, seg):
        for s in range(LRU_SEGMENTS):
            a_t = tile(a_ref, s * seg + t)
            sums[s] = a_t * sums[s] + tile(b_ref, s * seg + t)
            prods[s] = a_t * prods[s]
            b_ref[(s * seg + t) * chunks:(s * seg + t + 1) * chunks, :] = sums[s]
            a_ref[(s * seg + t) * chunks:(s * seg + t + 1) * chunks, :] = prods[s]
    h_in = [h0]
    for s in range(LRU_SEGMENTS):
        h_in.append(prods[s] * h_in[s] + sums[s])
    for s in range(LRU_SEGMENTS):
        for t in range(seg):
            f = s * seg + t
            h_ref[f * chunks:(f + 1) * chunks, :] = tile(a_ref, f) * h_in[s] + tile(b_ref, f)
    return h_in[LRU_SEGMENTS]


def _mixer_kernel(x_ref, hp_ref, hc_ref, hl_ref, nmix_ref, win_ref, bgate_ref, poolw_ref, pscale_ref,
                  convw_ref, convb_ref, wai_ref, ba_ref, bi_ref, lam_ref, wbrp_ref, wbrl_ref, wout_ref,
                  x1_ref, npool_ref, nconv_ref, nlru_ref,
                  pool_ext, conv_ext, h_carry, frames_a, frames_b, frames_h, *, start_pos, bblk, tt, d):
    j = pl.program_id(1)

    @pl.when(j == 0)
    def _():
        pool_ext[:, 0:POOL_PAD, :] = hp_ref[...]
        conv_ext[:, 0:CONV_PAD * (d // LANES), :] = hc_ref[...]
        h_carry[...] = hl_ref[...]

    rows = bblk * tt
    chunks = d // LANES
    cat_rows = lambda parts: jnp.concatenate(parts, axis=0) if len(parts) > 1 else parts[0]
    x = x_ref[...].reshape(rows, d)
    xnb = _rms_norm(x, nmix_ref[...]).astype(BF16)

    u_x = _dot(xnb, win_ref[:, d:2 * d])
    xc_rows = []
    for b in range(bblk):
        _store_rows(conv_ext.at[b, pl.ds(CONV_PAD * chunks, tt * chunks)], u_x[b * tt:(b + 1) * tt, :], tt, chunks)
        ce = conv_ext[b].reshape(CONV_PAD + tt, chunks, LANES)
        y = convb_ref[...] + ce[CONV_PAD:CONV_PAD + tt] * convw_ref[CONV_WIDTH - 1]
        for k in range(1, CONV_WIDTH):
            y = y + ce[CONV_PAD - k:CONV_PAD - k + tt] * convw_ref[CONV_WIDTH - 1 - k]
        frames_h[...] = y.reshape(tt * chunks, LANES)
        xc_rows.append(_load_rows(frames_h, tt, chunks))
        nconv_ref[b] = conv_ext[b, (CONV_PAD + tt - CONV_HIST) * chunks:(CONV_PAD + tt) * chunks, :]
        conv_ext[b, 0:CONV_PAD * chunks, :] = conv_ext[b, tt * chunks:(tt + CONV_PAD) * chunks, :]
    xc = cat_rows(xc_rows)

    n_chunks = d // MXU_WIDTH
    pre_a, pre_i = [], []
    for c in range(n_chunks):
        ai = _dot(xc[:, c * MXU_WIDTH:(c + 1) * MXU_WIDTH].astype(BF16), wai_ref[c])
        pre_a.append(ai[:, 0:MXU_WIDTH])
        pre_i.append(ai[:, MXU_WIDTH:2 * MXU_WIDTH])

    u_pool = _dot(xnb, win_ref[:, 0:d])
    pos1 = lax.broadcasted_iota(jnp.int32, (tt, 1), 0) + (start_pos + 1) + j * tt
    gd = d // len(POOL_WINDOWS)
    y_pool_rows = []
    for b in range(bblk):
        pool_ext[b, POOL_PAD:POOL_PAD + tt, :] = u_pool[b * tt:(b + 1) * tt, :]
        parts = []
        for g, w in enumerate(POOL_WINDOWS):
            sl = slice(g * gd, (g + 1) * gd)
            s = pool_ext[b, :, sl]
            width = 1
            while width < w:
                s = s + pltpu.roll(s, width, axis=0)
                width *= 2
            cur = s[POOL_PAD:POOL_PAD + tt, :]
            inv_cnt = 1.0 / jnp.minimum(pos1, w).astype(F32)
            pooled = cur * inv_cnt - u_pool[b * tt:(b + 1) * tt, sl]
            mixed = _dot(pooled.astype(BF16), poolw_ref[g])
            parts.append(mixed * pscale_ref[:, sl])
        y_pool_rows.append(jnp.concatenate(parts, axis=1))
        npool_ref[b] = pool_ext[b, POOL_PAD + tt - POOL_HIST:POOL_PAD + tt, :]
        pool_ext[b, 0:POOL_PAD, :] = pool_ext[b, tt:tt + POOL_PAD, :]
    y_pool = cat_rows(y_pool_rows)

    r_gate = jax.nn.sigmoid(jnp.concatenate(pre_a, axis=1) + ba_ref[...])
    i_gate = jax.nn.sigmoid(jnp.concatenate(pre_i, axis=1) + bi_ref[...])
    log_a = (-LRU_C) * r_gate * _softplus(-lam_ref[...])
    a = jnp.exp(log_a)
    mult = jnp.sqrt(-jnp.tanh(log_a) * (a * a + 1.0))
    bb = mult * (i_gate * xc)
    gl_pool = _dot(xnb, win_ref[:, 3 * d:4 * d])
    u_gate = _dot(xnb, win_ref[:, 2 * d:3 * d])
    h_rows = []
    for b in range(bblk):
        _store_rows(frames_a, a[b * tt:(b + 1) * tt, :], tt, chunks)
        _store_rows(frames_b, bb[b * tt:(b + 1) * tt, :], tt, chunks)
        h_last = _lru_scan(frames_a, frames_b, frames_h, h_carry[b], tt, chunks)
        h_carry[b] = h_last
        nlru_ref[b] = h_last
        h_rows.append(_load_rows(frames_h, tt, chunks))
    h = cat_rows(h_rows)
    br_pool = _dot(y_pool.astype(BF16), wbrp_ref[...])
    gl_lru = _dot(xnb, win_ref[:, 4 * d:5 * d])

    y_lru = h * _gelu_tanh(u_gate)
    br_lru = _dot(y_lru.astype(BF16), wbrl_ref[...])
    g_pool = jax.nn.sigmoid(gl_pool + bgate_ref[:, 0:d])
    g_lru = jax.nn.sigmoid(gl_lru + bgate_ref[:, d:2 * d])
    acc = g_pool * br_pool + g_lru * br_lru
    out = _dot(acc.astype(BF16), wout_ref[...])
    x1_ref[...] = (x + out).reshape(bblk, tt, d)


def _const_spec(shape):
    nd = len(shape)
    return pl.BlockSpec(shape, lambda *_: (0,) * nd, pipeline_mode=pl.Buffered(1))


def _mixer_call(x, hist_pool, hist_conv, hist_lru, weights, *, start_pos, bblk, tt, seq0=0):
    _, t_len, d = x.shape
    bsz = hist_pool.shape[0]
    assert seq0 % bblk == 0
    grid = (bsz // bblk, t_len // tt)
    kern = functools.partial(_mixer_kernel, start_pos=start_pos, bblk=bblk, tt=tt, d=d)
    chunks = d // LANES
    seq_spec = lambda rows: pl.BlockSpec((bblk, rows, d), lambda b, j: (b, 0, 0))
    frame_spec = lambda frames: pl.BlockSpec((bblk, frames * chunks, LANES), lambda b, j: (b, 0, 0))
    in_specs = [
        pl.BlockSpec((bblk, tt, d), lambda b, j: (b + seq0 // bblk, j, 0)),
        seq_spec(POOL_PAD), frame_spec(CONV_PAD), frame_spec(1),
    ] + [_const_spec(w.shape) for w in weights]
    out_shape = (
        jax.ShapeDtypeStruct((bsz, t_len, d), F32),
        jax.ShapeDtypeStruct((bsz, POOL_HIST, d), F32),
        jax.ShapeDtypeStruct((bsz, CONV_HIST * chunks, LANES), F32),
        jax.ShapeDtypeStruct((bsz, chunks, LANES), F32),
    )
    out_specs = (
        pl.BlockSpec((bblk, tt, d), lambda b, j: (b, j, 0)),
        seq_spec(POOL_HIST), frame_spec(CONV_HIST), frame_spec(1),
    )
    return pl.pallas_call(
        kern,
        grid=grid,
        in_specs=in_specs,
        out_specs=out_specs,
        out_shape=out_shape,
        scratch_shapes=[
            pltpu.VMEM((bblk, POOL_PAD + tt, d), F32),
            pltpu.VMEM((bblk, (CONV_PAD + tt) * chunks, LANES), F32),
            pltpu.VMEM((bblk, chunks, LANES), F32),
            pltpu.VMEM((tt * chunks, LANES), F32),
            pltpu.VMEM((tt * chunks, LANES), F32),
            pltpu.VMEM((tt * chunks, LANES), F32),
        ],
        compiler_params=pltpu.CompilerParams(
            dimension_semantics=("arbitrary", "arbitrary"), vmem_limit_bytes=VMEM_LIMIT_BYTES),
        name="mixer",
    )(x, hist_pool, hist_conv, hist_lru, *weights)


def _load_rows(ref, rows, chunks):
    return jnp.concatenate([ref[pl.ds(s, rows, stride=chunks), :] for s in range(chunks)], axis=1)


def _store_rows(ref, val, rows, chunks):
    for s in range(chunks):
        ref[pl.ds(s, rows, stride=chunks), :] = val[:, s * LANES:(s + 1) * LANES]


def _pack_halves(x):
    half = x.shape[1] // 2
    bits = lax.bitcast_convert_type(x, jnp.uint32)
    return (bits[:, :half] >> 16) | (bits[:, half:] & jnp.uint32(0xFFFF0000))


def _unpack_halves(p):
    lo = lax.bitcast_convert_type(p << 16, F32)
    hi = lax.bitcast_convert_type(p & jnp.uint32(0xFFFF0000), F32)
    return jnp.concatenate([lo, hi], axis=1)


class _Window(NamedTuple):
    first: int
    count: int


def _window_specs(windows, tb, d):
    specs, start = [], 0
    for win in windows:
        specs.append(pl.BlockSpec(
            (tb, d), lambda i, win=win, start=start: (win.first + jnp.clip(i - start, 0, win.count - 1), 0)))
        start += win.count
    return specs


def _window_load(i, windows, refs):
    val, start = refs[-1][...], sum(w.count for w in windows[:-1])
    for win, ref in zip(reversed(windows[:-1]), reversed(refs[:-1])):
        val = jnp.where(i < start, ref[...], val)
        start -= win.count
    return val


def _window_store(i, windows, refs, val):
    start = 0
    for win, ref in zip(windows, refs):
        @pl.when((i >= start) & (i < start + win.count))
        def _(ref=ref):
            ref[...] = val
        start += win.count


def _router_kernel(*refs, tb, d, windows):
    x1_refs = refs[:len(windows)]
    (nffn_ref, wrt_ref, br_ref, tri_ref, xn_ref, topi_ref, topw_ref, rank_ref, cnt_ref, carry) = refs[len(windows):]
    i = pl.program_id(0)

    @pl.when(i == 0)
    def _():
        carry[...] = jnp.zeros_like(carry)

    xnb = _rms_norm(_window_load(i, windows, x1_refs), nffn_ref[...]).astype(BF16)
    xn_ref_chunks = d // (2 * LANES)
    _store_rows(xn_ref, _pack_halves(xnb.astype(F32)), tb, xn_ref_chunks)
    logits = lax.dot_general(wrt_ref[...], xnb, (((1,), (1,)), ((), ())),
                             preferred_element_type=F32) + br_ref[...]
    n_exp = logits.shape[0]
    iota_e = lax.broadcasted_iota(jnp.int32, (n_exp, tb), 0)
    work = logits
    vals, idxs, hots = [], [], []
    for _ in range(TOP_K):
        m = jnp.max(work, axis=0, keepdims=True)
        idx = jnp.min(jnp.where(work == m, iota_e, n_exp), axis=0, keepdims=True)
        hot = iota_e == idx
        vals.append(m)
        idxs.append(idx)
        hots.append(hot)
        work = jnp.where(hot, -jnp.inf, work)
    exps = [jnp.exp(v - vals[0]) for v in vals]
    denom = exps[0] + exps[1] + exps[2] + exps[3]
    topi_ref[...] = jnp.concatenate(idxs, axis=0)
    topw_ref[...] = jnp.concatenate([e / denom for e in exps], axis=0)

    sel = sum(jnp.where(h, 1.0, 0.0) for h in hots)
    before = _dot(sel.astype(BF16), tri_ref[...]) + carry[:, 0:1]
    ranks = [jnp.sum(jnp.where(h, before, 0.0), axis=0, keepdims=True) for h in hots]
    rank_ref[...] = jnp.concatenate(ranks, axis=0).astype(jnp.int32)
    carry[...] = carry[...] + jnp.sum(sel, axis=1, keepdims=True)
    cnt_ref[...] = carry[...]


def _router_call(x1_parts, windows, norm_ffn, w_router_t, b_router, *, tb):
    d = x1_parts[0].shape[1]
    n_tiles = sum(w.count for w in windows)
    n = n_tiles * tb
    chunks = d // (2 * LANES)
    n_exp = w_router_t.shape[0]
    tri = jnp.triu(jnp.ones((tb, tb), BF16), k=1)
    tok = lambda rows, dt: (pl.BlockSpec((rows, tb), lambda i: (0, i)), jax.ShapeDtypeStruct((rows, n), dt))
    (topi_spec, topi_shape), (topw_spec, topw_shape), (rank_spec, rank_shape) = (
        tok(TOP_K, jnp.int32), tok(TOP_K, F32), tok(TOP_K, jnp.int32))
    return pl.pallas_call(
        functools.partial(_router_kernel, tb=tb, d=d, windows=tuple(windows)),
        grid=(n_tiles,),
        in_specs=[
            *_window_specs(windows, tb, d),
            _const_spec((1, d)), _const_spec((n_exp, d)), _const_spec((n_exp, 1)), _const_spec((tb, tb)),
        ],
        out_specs=(
            pl.BlockSpec((tb * chunks, LANES), lambda i: (i, 0)),
            topi_spec, topw_spec, rank_spec,
            pl.BlockSpec((n_exp, LANES), lambda i: (0, 0)),
        ),
        out_shape=(
            jax.ShapeDtypeStruct((n * chunks, LANES), jnp.uint32),
            topi_shape, topw_shape, rank_shape,
            jax.ShapeDtypeStruct((n_exp, LANES), F32),
        ),
        scratch_shapes=[pltpu.VMEM((n_exp, LANES), F32)],
        compiler_params=pltpu.CompilerParams(
            dimension_semantics=("arbitrary",), vmem_limit_bytes=VMEM_LIMIT_BYTES),
        name="router",
    )(*x1_parts, norm_ffn.reshape(1, d), w_router_t, b_router.reshape(n_exp, 1), tri)


def _expert_kernel(te_ref, tv_ref, nu_ref, xs_ref, wgu_ref, bgu_ref, wdn_ref, bdn_ref, ys_ref, *bf_refs,
                   tm, d, dff, round_weights):
    del nu_ref
    i = pl.program_id(0)
    if round_weights:
        wgu_bf, wdn_bf = bf_refs
        prev = te_ref[jnp.maximum(i - 1, 0)]

        @pl.when((i == 0) | (te_ref[i] != prev))
        def _():
            wgu_bf[...] = wgu_ref[...].astype(BF16)
            wdn_bf[...] = wdn_ref[...].astype(BF16)
    else:
        wgu_bf, wdn_bf = wgu_ref, wdn_ref

    xchunks = d // (2 * LANES)

    def ffn(row0, rows):
        window = pl.ds(row0 * xchunks, rows * xchunks)
        x = _unpack_halves(_load_rows(xs_ref.at[window], rows, xchunks)).astype(BF16)
        gu = _dot(x, wgu_bf[0]) + bgu_ref[0]
        gate = jnp.minimum(gu[:, :dff], SWIGLU_LIMIT)
        up = jnp.clip(gu[:, dff:], -SWIGLU_LIMIT, SWIGLU_LIMIT)
        hid = (up + 1.0) * gate * jax.nn.sigmoid(SWIGLU_ALPHA * gate)
        y = _dot(hid.astype(BF16), wdn_bf[0]) + bdn_ref[0]
        _store_rows(ys_ref.at[window], _pack_halves(y.astype(BF16).astype(F32)), rows, xchunks)

    sub = EXPERT_SUBTILE
    for s in range(tm // sub):
        valid = tv_ref[i] - s * sub
        pl.when(valid > sub // 2)(functools.partial(ffn, s * sub, sub))
        pl.when((valid > 0) & (valid <= sub // 2))(functools.partial(ffn, s * sub, sub // 2))


def _expert_call(tile_expert, tile_rows, n_used, xs, w_gu, b_gu, w_dn, b_dn, *, tm):
    n_exp, d, dff2 = w_gu.shape
    dff = dff2 // 2
    xchunks = d // (2 * LANES)
    p_rows = xs.shape[0] // xchunks
    round_weights = w_gu.dtype != BF16
    row_blk = lambda i, te, tv, nu: (jnp.minimum(i, nu[0] - 1), 0)
    exp_blk = lambda i, te, tv, nu: (te[i], 0, 0)
    out_specs = [pl.BlockSpec((tm * xchunks, LANES), row_blk)]
    out_shape = [jax.ShapeDtypeStruct((p_rows * xchunks, LANES), jnp.uint32)]
    if round_weights:
        out_specs += [pl.BlockSpec((1, d, dff2), exp_blk), pl.BlockSpec((1, dff, d), exp_blk)]
        out_shape += [jax.ShapeDtypeStruct(w_gu.shape, BF16), jax.ShapeDtypeStruct(w_dn.shape, BF16)]
    return pl.pallas_call(
        functools.partial(_expert_kernel, tm=tm, d=d, dff=dff, round_weights=round_weights),
        grid_spec=pltpu.PrefetchScalarGridSpec(
            num_scalar_prefetch=3,
            grid=(p_rows // tm,),
            in_specs=[
                pl.BlockSpec((tm * xchunks, LANES), row_blk),
                pl.BlockSpec((1, d, dff2), exp_blk),
                pl.BlockSpec((1, 1, dff2), exp_blk),
                pl.BlockSpec((1, dff, d), exp_blk),
                pl.BlockSpec((1, 1, d), exp_blk),
            ],
            out_specs=out_specs,
        ),
        out_shape=out_shape,
        compiler_params=pltpu.CompilerParams(
            dimension_semantics=("arbitrary",), vmem_limit_bytes=VMEM_LIMIT_BYTES),
        name="experts",
    )(tile_expert, tile_rows, n_used, xs, w_gu, b_gu.reshape(n_exp, 1, dff2), w_dn, b_dn.reshape(n_exp, 1, d))


def _final_kernel(*refs, tb, d, in_windows, out_windows, n_alias):
    n_in, n_out = len(in_windows), len(out_windows)
    x1_refs = refs[:n_in]
    yk_ref, w_ref, nfin_ref = refs[n_in:n_in + 3]
    out_refs = refs[n_in + 3 + n_alias:n_in + 3 + n_alias + n_out]
    i = pl.program_id(0)
    w = w_ref[...]
    chunks = d // (2 * LANES)
    moe = _unpack_halves(_load_rows(yk_ref.at[0], tb, chunks)) * w[:, 0:1]
    for k in range(1, TOP_K):
        moe = moe + _unpack_halves(_load_rows(yk_ref.at[k], tb, chunks)) * w[:, k:k + 1]
    out = _rms_norm(_window_load(i, in_windows, x1_refs) + moe, nfin_ref[...])
    _window_store(i, out_windows, out_refs, out)


def _final_call(x1_parts, in_windows, yk, topw_t, norm_final, out_shapes, out_windows, out_init, *, tb):
    d = x1_parts[0].shape[1]
    n_tiles = sum(w.count for w in in_windows)
    chunks = d // (2 * LANES)
    alias_in = [a for a in out_init if a is not None]
    n_fixed = len(x1_parts) + 3
    aliases, j = {}, 0
    for k, a in enumerate(out_init):
        if a is not None:
            aliases[n_fixed + j] = k
            j += 1
    return pl.pallas_call(
        functools.partial(_final_kernel, tb=tb, d=d, in_windows=tuple(in_windows),
                          out_windows=tuple(out_windows), n_alias=len(alias_in)),
        grid=(n_tiles,),
        in_specs=[
            *_window_specs(in_windows, tb, d),
            pl.BlockSpec((TOP_K, tb * chunks, LANES), lambda i: (0, i, 0)),
            pl.BlockSpec((tb, TOP_K), lambda i: (i, 0)),
            _const_spec((1, d)),
            *[pl.BlockSpec(memory_space=pl.ANY) for _ in alias_in],
        ],
        out_specs=_window_specs(out_windows, tb, d),
        out_shape=[jax.ShapeDtypeStruct(s, F32) for s in out_shapes],
        input_output_aliases=aliases,
        compiler_params=pltpu.CompilerParams(
            dimension_semantics=("arbitrary",), vmem_limit_bytes=VMEM_LIMIT_BYTES),
        name="final",
    )(*x1_parts, yk, topw_t, norm_final.reshape(1, d), *alias_in)


def _sc_step_rows(n):
    per_worker = n // SC_WORKERS
    assert per_worker * SC_WORKERS == n
    w = max(c for c in range(SUBLANES, SC_MAX_STEP_ROWS + 1, SUBLANES) if per_worker % c == 0)
    return w, per_worker // w


def _sc_mesh():
    return plsc.VectorSubcoreMesh(core_axis_name="c", subcore_axis_name="s",
                                  num_cores=SC_CORES, num_subcores=SC_SUBCORES)


def _dispatch_rows(x_tiles, pos_steps, p_rows):
    n, chunks, _ = x_tiles.shape
    w, steps = _sc_step_rows(n)

    @functools.partial(
        pl.kernel, mesh=_sc_mesh(),
        out_type=jax.ShapeDtypeStruct((p_rows, chunks, LANES), x_tiles.dtype),
        scratch_types=[pltpu.VMEM((w, chunks, LANES), x_tiles.dtype), pltpu.VMEM((TOP_K, w), jnp.int32)],
        name="dispatch",
    )
    def run(x_hbm, pos_hbm, out_hbm, buf, idx):
        wid = lax.axis_index("s") * SC_CORES + lax.axis_index("c")

        @pl.loop(0, steps)
        def _(i):
            step = wid * steps + i
            pltpu.sync_copy(x_hbm.at[pl.ds(pl.multiple_of(step * w, SUBLANES), w)], buf)
            pltpu.sync_copy(pos_hbm.at[step], idx)
            for k in range(TOP_K):
                pltpu.sync_copy(buf, out_hbm.at[idx.at[k]])

    return run(x_tiles, pos_steps)


def _combine_rows(y_tiles, pos_steps, n):
    _, chunks, _ = y_tiles.shape
    w, steps = _sc_step_rows(n)

    @functools.partial(
        pl.kernel, mesh=_sc_mesh(),
        out_type=jax.ShapeDtypeStruct((TOP_K, n, chunks, LANES), y_tiles.dtype),
        scratch_types=[pltpu.VMEM((w, chunks, LANES), y_tiles.dtype), pltpu.VMEM((TOP_K, w), jnp.int32)],
        name="combine",
    )
    def run(y_hbm, pos_hbm, out_hbm, buf, idx):
        wid = lax.axis_index("s") * SC_CORES + lax.axis_index("c")

        @pl.loop(0, steps)
        def _(i):
            step = wid * steps + i
            pltpu.sync_copy(pos_hbm.at[step], idx)
            for k in range(TOP_K):
                pltpu.sync_copy(y_hbm.at[idx.at[k]], buf)
                pltpu.sync_copy(buf, out_hbm.at[k, pl.ds(pl.multiple_of(step * w, SUBLANES), w)])

    return run(y_tiles, pos_steps)


def _block_diag(w, per_block):
    heads, n, _ = w.shape
    w4 = w.reshape(heads // per_block, per_block, n, n)
    eye = jnp.eye(per_block, dtype=w.dtype)
    bd = jnp.einsum('chij,hg->chigj', w4, eye)
    return bd.reshape(heads // per_block, per_block * n, per_block * n)


def _mixer_weights(norm_mix, w_in, b_gate, pool_w, pool_scale, conv_w, conv_b, lru_wa, lru_ba, lru_wi,
                   lru_bi, lru_lambda, w_br_pool, w_br_lru, w_out):
    row = lambda v: v.reshape(1, -1)
    d = norm_mix.shape[0]
    head_dim = lru_wa.shape[-1]
    per_block = MXU_WIDTH // head_dim
    w_ai = jnp.concatenate([_block_diag(lru_wa, per_block), _block_diag(lru_wi, per_block)], axis=-1)
    return (row(norm_mix), w_in.astype(BF16), row(b_gate), pool_w.astype(BF16), row(pool_scale),
            conv_w.reshape(-1, d // LANES, LANES), conv_b.reshape(d // LANES, LANES), w_ai.astype(BF16),
            row(lru_ba), row(lru_bi), row(lru_lambda),
            w_br_pool.astype(BF16), w_br_lru.astype(BF16), w_out.astype(BF16))


def _run_mixer(x, state_pool, state_conv, state_lru, weights, *, start_pos, bblk, tt, seq0=0):
    bsz, d = state_lru.shape
    chunks = d // LANES
    hp = jnp.pad(state_pool, ((0, 0), (POOL_PAD - POOL_HIST, 0), (0, 0)))
    hc = jnp.pad(state_conv, ((0, 0), (CONV_PAD - CONV_HIST, 0), (0, 0))).reshape(bsz, CONV_PAD * chunks, LANES)
    hl = state_lru.reshape(bsz, chunks, LANES)
    x1, npool, nconv, nlru = _mixer_call(x, hp, hc, hl, weights, start_pos=start_pos, bblk=bblk, tt=tt,
                                         seq0=seq0)
    return x1, npool, nconv.reshape(bsz, CONV_HIST, d), nlru.reshape(bsz, d)


def _routing_plan(topi, rank, counts, *, tm, n_tiles, min_tiles):
    n_exp = counts.shape[0]
    tiles_per = jnp.maximum((counts + tm - 1) // tm, min_tiles)
    tiles_cum = jnp.cumsum(tiles_per)
    tile_start = tiles_cum - tiles_per
    base = tile_start * tm
    experts = jnp.arange(n_exp, dtype=jnp.int32)[:, None, None]
    pos = rank + jnp.sum(jnp.where(topi[None] == experts, base[:, None, None], 0), axis=0)
    n_used = tiles_cum[-1]
    tile_ids = jnp.arange(n_tiles, dtype=jnp.int32)
    live = tile_ids < n_used
    owner = (tiles_cum[None, :] <= jnp.minimum(tile_ids, n_used - 1)[:, None])
    tile_expert = jnp.sum(owner.astype(jnp.int32), axis=1)
    is_owner = tile_expert[:, None] == jnp.arange(n_exp, dtype=jnp.int32)[None, :]
    rows_left = jnp.sum(jnp.where(is_owner, counts[None, :] - (tile_ids[:, None] - tile_start[None, :]) * tm, 0),
                        axis=1)
    tile_rows = jnp.where(live, jnp.clip(rows_left, 0, tm), 0).astype(jnp.int32)
    return pos, tile_expert, tile_rows, n_used.reshape(1).astype(jnp.int32)


def _moe_rows(x1_parts, windows, norm_ffn, w_router_t, b_router, w_gu, b_gu, w_dn, b_dn):
    d = x1_parts[0].shape[1]
    chunks = d // LANES
    n_exp = w_router_t.shape[0]
    n = sum(w.count for w in windows) * TOKEN_TILE
    xn2, topi, topw, rank, cnt = _router_call(x1_parts, windows, norm_ffn, w_router_t, b_router, tb=TOKEN_TILE)
    min_tiles = 0 if w_gu.dtype == BF16 else 1
    n_tiles = (n * TOP_K + n_exp * (EXPERT_TILE - 1)) // EXPERT_TILE + n_exp * min_tiles
    p_rows = n_tiles * EXPERT_TILE
    pos, tile_expert, tile_rows, n_used = _routing_plan(
        topi, rank, cnt[:, 0].astype(jnp.int32), tm=EXPERT_TILE, n_tiles=n_tiles, min_tiles=min_tiles)
    w, steps = _sc_step_rows(n)
    pos_steps = pos.reshape(TOP_K, SC_WORKERS * steps, w).transpose(1, 0, 2)
    xchunks = d // (2 * LANES)
    xs = _dispatch_rows(xn2.reshape(n, xchunks, LANES), pos_steps, p_rows)
    ys, *w_bf = _expert_call(tile_expert, tile_rows, n_used, xs.reshape(p_rows * xchunks, LANES), w_gu, b_gu,
                             w_dn, b_dn, tm=EXPERT_TILE)
    yk = _combine_rows(ys.reshape(p_rows, xchunks, LANES), pos_steps, n)
    return yk.reshape(TOP_K, n * xchunks, LANES), topw, (w_bf if w_bf else (w_gu, w_dn))


def kernel(x_prompt, x_sample, state_pool, state_conv, state_lru, norm_mix, w_in, b_gate, pool_w, pool_scale,
           conv_w, conv_b, lru_wa, lru_ba, lru_wi, lru_bi, lru_lambda, w_br_pool, w_br_lru, w_out, norm_ffn,
           w_router, b_router, w_gu, b_gu, w_dn, b_dn, norm_final):
    bp, tp, d = x_prompt.shape
    bs, ts, _ = x_sample.shape
    n_p, n_s = bp * tp, bs * ts
    mw = _mixer_weights(norm_mix[0], w_in[0], b_gate[0], pool_w[0], pool_scale[0], conv_w[0], conv_b[0],
                        lru_wa[0], lru_ba[0], lru_wi[0], lru_bi[0], lru_lambda[0], w_br_pool[0], w_br_lru[0],
                        w_out[0])
    w_router_t = w_router[0].T.astype(BF16)
    zeros = lambda *shape: jnp.zeros(shape, x_prompt.dtype)

    assert sum(PROMPT_GROUP_SEQS) == bp and tp % TOKEN_TILE == 0 and n_s % TOKEN_TILE == 0
    seq_tiles = tp // TOKEN_TILE
    s_tiles = n_s // TOKEN_TILE
    x1_s, pool_s, conv_s, lru_s = _run_mixer(
        x_sample, state_pool[0], state_conv[0], state_lru[0], mw, start_pos=PAST_LEN, bblk=bs, tt=ts)
    y_p, y_s = None, None
    expert_w = (w_gu[0], w_dn[0])
    pools, convs, lrus = [], [], []
    seq0 = 0
    for g, seqs in enumerate(PROMPT_GROUP_SEQS):
        x1_g, pool_g, conv_g, lru_g = _run_mixer(
            x_prompt, zeros(seqs, POOL_HIST, d), zeros(seqs, CONV_HIST, d), zeros(seqs, d), mw,
            start_pos=0, bblk=1, tt=MIXER_TILE, seq0=seq0)
        pools.append(pool_g)
        convs.append(conv_g)
        lrus.append(lru_g)
        last = g == len(PROMPT_GROUP_SEQS) - 1
        g_tiles = seqs * seq_tiles
        parts, windows = [x1_g.reshape(seqs * tp, d)], [_Window(0, g_tiles)]
        out_shapes, out_windows, out_init = [(n_p, d)], [_Window(seq0 * seq_tiles, g_tiles)], [y_p]
        seq0 += seqs
        if last:
            parts.append(x1_s.reshape(n_s, d))
            windows.append(_Window(0, s_tiles))
            out_shapes.append((n_s, d))
            out_windows.append(_Window(0, s_tiles))
            out_init.append(None)
        yk, topw, expert_w = _moe_rows(parts, windows, norm_ffn[0], w_router_t, b_router[0], expert_w[0],
                                       b_gu[0], expert_w[1], b_dn[0])
        outs = _final_call(parts, windows, yk, topw.T, norm_final, out_shapes, out_windows, out_init,
                           tb=TOKEN_TILE)
        y_p = outs[0]
        if last:
            y_s = outs[1]

    cat = lambda xs: jnp.concatenate(xs, axis=0)[None]
    return (y_p.reshape(bp, tp, d), y_s.reshape(bs, ts, d), cat(pools), cat(convs), cat(lrus),
            pool_s[None], conv_s[None], lru_s[None])
```

```python
import functools
from typing import NamedTuple

import jax
import jax.numpy as jnp
from jax import lax
from jax.experimental import pallas as pl
from jax.experimental.pallas import tpu as pltpu
from jax.experimental.pallas import tpu_sc as plsc

BF16 = jnp.bfloat16
F32 = jnp.float32

POOL_WINDOWS = (2, 4, 8, 16)
POOL_HIST = max(POOL_WINDOWS) - 1
CONV_WIDTH = 4
CONV_HIST = CONV_WIDTH - 1
LRU_C = 8.0
LRU_SEGMENTS = 8
TOP_K = 4
SWIGLU_LIMIT = 7.0
SWIGLU_ALPHA = 1.702
NORM_EPS = 1e-6

SUBLANES = 8
LANES = 128
MXU_WIDTH = 256
POOL_PAD = -(-POOL_HIST // SUBLANES) * SUBLANES
CONV_PAD = -(-CONV_HIST // SUBLANES) * SUBLANES
V7X_VMEM_BYTES = 64 * 1024 * 1024
VMEM_LIMIT_BYTES = V7X_VMEM_BYTES * 7 // 8

PAST_LEN = 1024
MIXER_TILE = 256
TOKEN_TILE = 512
EXPERT_TILE = 1024
EXPERT_SUBTILE = 512
PROMPT_GROUP_SEQS = (10, 6)

SC_CORES = 2
SC_SUBCORES = 16
SC_WORKERS = SC_CORES * SC_SUBCORES
SC_MAX_STEP_ROWS = 128


def _rms_norm(x, g):
    ms = jnp.mean(x * x, axis=-1, keepdims=True)
    return (x * lax.rsqrt(ms + NORM_EPS)) * g


def _dot(a, b):
    return jnp.dot(a, b, preferred_element_type=F32)


def _softplus(x):
    return jnp.maximum(x, 0.0) + jnp.log1p(jnp.exp(-jnp.abs(x)))


def _gelu_tanh(x):
    c = 0.7978845608028654
    return 0.5 * x * (1.0 + jnp.tanh(c * (x + 0.044715 * (x * x * x))))


def _lru_scan(a_ref, b_ref, h_ref, h0, frames, chunks):
    tile = lambda ref, t: ref[t * chunks:(t + 1) * chunks, :]
    seg = frames // LRU_SEGMENTS
    assert seg * LRU_SEGMENTS == frames
    prods = [tile(a_ref, s * seg) for s in range(LRU_SEGMENTS)]
    sums = [tile(b_ref, s * seg) for s in range(LRU_SEGMENTS)]
    for t in range(1, seg):
        for s in range(LRU_SEGMENTS):
            a_t = tile(a_ref, s * seg + t)
            sums[s] = a_t * sums[s] + tile(b_ref, s * seg + t)
            prods[s] = a_t * prods[s]
            b_ref[(s * seg + t) * chunks:(s * seg + t + 1) * chunks, :] = sums[s]
            a_ref[(s * seg + t) * chunks:(s * seg + t + 1) * chunks, :] = prods[s]
    h_in = [h0]
    for s in range(LRU_SEGMENTS):
        h_in.append(prods[s] * h_in[s] + sums[s])
    for s in range(LRU_SEGMENTS):
        for t in range(seg):
            f = s * seg + t
            h_ref[f * chunks:(f + 1) * chunks, :] = tile(a_ref, f) * h_in[s] + tile(b_ref, f)
    return h_in[LRU_SEGMENTS]


def _mixer_kernel(x_ref, hp_ref, hc_ref, hl_ref, nmix_ref, win_ref, bgate_ref, poolw_ref, pscale_ref,
                  convw_ref, convb_ref, wai_ref, ba_ref, bi_ref, lam_ref, wbrp_ref, wbrl_ref, wout_ref,
                  x1_ref, npool_ref, nconv_ref, nlru_ref,
                  pool_ext, conv_ext, h_carry, frames_a, frames_b, frames_h, *, start_pos, bblk, tt, d):
    j = pl.program_id(1)

    @pl.when(j == 0)
    def _():
        pool_ext[:, 0:POOL_PAD, :] = hp_ref[...]
        conv_ext[:, 0:CONV_PAD * (d // LANES), :] = hc_ref[...]
        h_carry[...] = hl_ref[...]

    rows = bblk * tt
    chunks = d // LANES
    cat_rows = lambda parts: jnp.concatenate(parts, axis=0) if len(parts) > 1 else parts[0]
    x = x_ref[...].reshape(rows, d)
    xnb = _rms_norm(x, nmix_ref[...]).astype(BF16)

    u_x = _dot(xnb, win_ref[:, d:2 * d])
    xc_rows = []
    for b in range(bblk):
        _store_rows(conv_ext.at[b, pl.ds(CONV_PAD * chunks, tt * chunks)], u_x[b * tt:(b + 1) * tt, :], tt, chunks)
        ce = conv_ext[b].reshape(CONV_PAD + tt, chunks, LANES)
        y = convb_ref[...] + ce[CONV_PAD:CONV_PAD + tt] * convw_ref[CONV_WIDTH - 1]
        for k in range(1, CONV_WIDTH):
            y = y + ce[CONV_PAD - k:CONV_PAD - k + tt] * convw_ref[CONV_WIDTH - 1 - k]
        frames_h[...] = y.reshape(tt * chunks, LANES)
        xc_rows.append(_load_rows(frames_h, tt, chunks))
        nconv_ref[b] = conv_ext[b, (CONV_PAD + tt - CONV_HIST) * chunks:(CONV_PAD + tt) * chunks, :]
        conv_ext[b, 0:CONV_PAD * chunks, :] = conv_ext[b, tt * chunks:(tt + CONV_PAD) * chunks, :]
    xc = cat_rows(xc_rows)

    n_chunks = d // MXU_WIDTH
    pre_a, pre_i = [], []
    for c in range(n_chunks):
        ai = _dot(xc[:, c * MXU_WIDTH:(c + 1) * MXU_WIDTH].astype(BF16), wai_ref[c])
        pre_a.append(ai[:, 0:MXU_WIDTH])
        pre_i.append(ai[:, MXU_WIDTH:2 * MXU_WIDTH])

    u_pool = _dot(xnb, win_ref[:, 0:d])
    pos1 = lax.broadcasted_iota(jnp.int32, (tt, 1), 0) + (start_pos + 1) + j * tt
    gd = d // len(POOL_WINDOWS)
    y_pool_rows = []
    for b in range(bblk):
        pool_ext[b, POOL_PAD:POOL_PAD + tt, :] = u_pool[b * tt:(b + 1) * tt, :]
        parts = []
        for g, w in enumerate(POOL_WINDOWS):
            sl = slice(g * gd, (g + 1) * gd)
            s = pool_ext[b, :, sl]
            width = 1
            while width < w:
                s = s + pltpu.roll(s, width, axis=0)
                width *= 2
            cur = s[POOL_PAD:POOL_PAD + tt, :]
            inv_cnt = 1.0 / jnp.minimum(pos1, w).astype(F32)
            pooled = cur * inv_cnt - u_pool[b * tt:(b + 1) * tt, sl]
            mixed = _dot(pooled.astype(BF16), poolw_ref[g])
            parts.append(mixed * pscale_ref[:, sl])
        y_pool_rows.append(jnp.concatenate(parts, axis=1))
        npool_ref[b] = pool_ext[b, POOL_PAD + tt - POOL_HIST:POOL_PAD + tt, :]
        pool_ext[b, 0:POOL_PAD, :] = pool_ext[b, tt:tt + POOL_PAD, :]
    y_pool = cat_rows(y_pool_rows)

    r_gate = jax.nn.sigmoid(jnp.concatenate(pre_a, axis=1) + ba_ref[...])
    i_gate = jax.nn.sigmoid(jnp.concatenate(pre_i, axis=1) + bi_ref[...])
    log_a = (-LRU_C) * r_gate * _softplus(-lam_ref[...])
    a = jnp.exp(log_a)
    mult = jnp.sqrt(-jnp.tanh(log_a) * (a * a + 1.0))
    bb = mult * (i_gate * xc)
    gl_pool = _dot(xnb, win_ref[:, 3 * d:4 * d])
    u_gate = _dot(xnb, win_ref[:, 2 * d:3 * d])
    h_rows = []
    for b in range(bblk):
        _store_rows(frames_a, a[b * tt:(b + 1) * tt, :], tt, chunks)
        _store_rows(frames_b, bb[b * tt:(b + 1) * tt, :], tt, chunks)
        h_last = _lru_scan(frames_a, frames_b, frames_h, h_carry[b], tt, chunks)
        h_carry[b] = h_last
        nlru_ref[b] = h_last
        h_rows.append(_load_rows(frames_h, tt, chunks))
    h = cat_rows(h_rows)
    br_pool = _dot(y_pool.astype(BF16), wbrp_ref[...])
    gl_lru = _dot(xnb, win_ref[:, 4 * d:5 * d])

    y_lru = h * _gelu_tanh(u_gate)
    br_lru = _dot(y_lru.astype(BF16), wbrl_ref[...])
    g_pool = jax.nn.sigmoid(gl_pool + bgate_ref[:, 0:d])
    g_lru = jax.nn.sigmoid(gl_lru + bgate_ref[:, d:2 * d])
    acc = g_pool * br_pool + g_lru * br_lru
    out = _dot(acc.astype(BF16), wout_ref[...])
    x1_ref[...] = (x + out).reshape(bblk, tt, d)


def _const_spec(shape):
    nd = len(shape)
    return pl.BlockSpec(shape, lambda *_: (0,) * nd, pipeline_mode=pl.Buffered(1))


def _mixer_call(x, hist_pool, hist_conv, hist_lru, weights, *, start_pos, bblk, tt, seq0=0):
    _, t_len, d = x.shape
    bsz = hist_pool.shape[0]
    assert seq0 % bblk == 0
    grid = (bsz // bblk, t_len // tt)
    kern = functools.partial(_mixer_kernel, start_pos=start_pos, bblk=bblk, tt=tt, d=d)
    chunks = d // LANES
    seq_spec = lambda rows: pl.BlockSpec((bblk, rows, d), lambda b, j: (b, 0, 0))
    frame_spec = lambda frames: pl.BlockSpec((bblk, frames * chunks, LANES), lambda b, j: (b, 0, 0))
    in_specs = [
        pl.BlockSpec((bblk, tt, d), lambda b, j: (b + seq0 // bblk, j, 0)),
        seq_spec(POOL_PAD), frame_spec(CONV_PAD), frame_spec(1),
    ] + [_const_spec(w.shape) for w in weights]
    out_shape = (
        jax.ShapeDtypeStruct((bsz, t_len, d), F32),
        jax.ShapeDtypeStruct((bsz, POOL_HIST, d), F32),
        jax.ShapeDtypeStruct((bsz, CONV_HIST * chunks, LANES), F32),
        jax.ShapeDtypeStruct((bsz, chunks, LANES), F32),
    )
    out_specs = (
        pl.BlockSpec((bblk, tt, d), lambda b, j: (b, j, 0)),
        seq_spec(POOL_HIST), frame_spec(CONV_HIST), frame_spec(1),
    )
    return pl.pallas_call(
        kern,
        grid=grid,
        in_specs=in_specs,
        out_specs=out_specs,
        out_shape=out_shape,
        scratch_shapes=[
            pltpu.VMEM((bblk, POOL_PAD + tt, d), F32),
            pltpu.VMEM((bblk, (CONV_PAD + tt) * chunks, LANES), F32),
            pltpu.VMEM((bblk, chunks, LANES), F32),
            pltpu.VMEM((tt * chunks, LANES), F32),
            pltpu.VMEM((tt * chunks, LANES), F32),
            pltpu.VMEM((tt * chunks, LANES), F32),
        ],
        compiler_params=pltpu.CompilerParams(
            dimension_semantics=("arbitrary", "arbitrary"), vmem_limit_bytes=VMEM_LIMIT_BYTES),
        name="mixer",
    )(x, hist_pool, hist_conv, hist_lru, *weights)


def _load_rows(ref, rows, chunks):
    return jnp.concatenate([ref[pl.ds(s, rows, stride=chunks), :] for s in range(chunks)], axis=1)


def _store_rows(ref, val, rows, chunks):
    for s in range(chunks):
        ref[pl.ds(s, rows, stride=chunks), :] = val[:, s * LANES:(s + 1) * LANES]


def _pack_halves(x):
    half = x.shape[1] // 2
    bits = lax.bitcast_convert_type(x, jnp.uint32)
    return (bits[:, :half] >> 16) | (bits[:, half:] & jnp.uint32(0xFFFF0000))


def _unpack_halves(p):
    lo = lax.bitcast_convert_type(p << 16, F32)
    hi = lax.bitcast_convert_type(p & jnp.uint32(0xFFFF0000), F32)
    return jnp.concatenate([lo, hi], axis=1)


class _Window(NamedTuple):
    first: int
    count: int


def _window_specs(windows, tb, d):
    specs, start = [], 0
    for win in windows:
        specs.append(pl.BlockSpec(
            (tb, d), lambda i, win=win, start=start: (win.first + jnp.clip(i - start, 0, win.count - 1), 0)))
        start += win.count
    return specs


def _window_load(i, windows, refs):
    val, start = refs[-1][...], sum(w.count for w in windows[:-1])
    for win, ref in zip(reversed(windows[:-1]), reversed(refs[:-1])):
        val = jnp.where(i < start, ref[...], val)
        start -= win.count
    return val


def _window_store(i, windows, refs, val):
    start = 0
    for win, ref in zip(windows, refs):
        @pl.when((i >= start) & (i < start + win.count))
        def _(ref=ref):
            ref[...] = val
        start += win.count


def _router_kernel(*refs, tb, d, windows):
    x1_refs = refs[:len(windows)]
    (nffn_ref, wrt_ref, br_ref, tri_ref, xn_ref, topi_ref, topw_ref, rank_ref, cnt_ref, carry) = refs[len(windows):]
    i = pl.program_id(0)

    @pl.when(i == 0)
    def _():
        carry[...] = jnp.zeros_like(carry)

    xnb = _rms_norm(_window_load(i, windows, x1_refs), nffn_ref[...]).astype(BF16)
    xn_ref_chunks = d // (2 * LANES)
    _store_rows(xn_ref, _pack_halves(xnb.astype(F32)), tb, xn_ref_chunks)
    logits = lax.dot_general(wrt_ref[...], xnb, (((1,), (1,)), ((), ())),
                             preferred_element_type=F32) + br_ref[...]
    n_exp = logits.shape[0]
    iota_e = lax.broadcasted_iota(jnp.int32, (n_exp, tb), 0)
    work = logits
    vals, idxs, hots = [], [], []
    for _ in range(TOP_K):
        m = jnp.max(work, axis=0, keepdims=True)
        idx = jnp.min(jnp.where(work == m, iota_e, n_exp), axis=0, keepdims=True)
        hot = iota_e == idx
        vals.append(m)
        idxs.append(idx)
        hots.append(hot)
        work = jnp.where(hot, -jnp.inf, work)
    exps = [jnp.exp(v - vals[0]) for v in vals]
    denom = exps[0] + exps[1] + exps[2] + exps[3]
    topi_ref[...] = jnp.concatenate(idxs, axis=0)
    topw_ref[...] = jnp.concatenate([e / denom for e in exps], axis=0)

    sel = sum(jnp.where(h, 1.0, 0.0) for h in hots)
    before = _dot(sel.astype(BF16), tri_ref[...]) + carry[:, 0:1]
    ranks = [jnp.sum(jnp.where(h, before, 0.0), axis=0, keepdims=True) for h in hots]
    rank_ref[...] = jnp.concatenate(ranks, axis=0).astype(jnp.int32)
    carry[...] = carry[...] + jnp.sum(sel, axis=1, keepdims=True)
    cnt_ref[...] = carry[...]


def _router_call(x1_parts, windows, norm_ffn, w_router_t, b_router, *, tb):
    d = x1_parts[0].shape[1]
    n_tiles = sum(w.count for w in windows)
    n = n_tiles * tb
    chunks = d // (2 * LANES)
    n_exp = w_router_t.shape[0]
    tri = jnp.triu(jnp.ones((tb, tb), BF16), k=1)
    tok = lambda rows, dt: (pl.BlockSpec((rows, tb), lambda i: (0, i)), jax.ShapeDtypeStruct((rows, n), dt))
    (topi_spec, topi_shape), (topw_spec, topw_shape), (rank_spec, rank_shape) = (
        tok(TOP_K, jnp.int32), tok(TOP_K, F32), tok(TOP_K, jnp.int32))
    return pl.pallas_call(
        functools.partial(_router_kernel, tb=tb, d=d, windows=tuple(windows)),
        grid=(n_tiles,),
        in_specs=[
            *_window_specs(windows, tb, d),
            _const_spec((1, d)), _const_spec((n_exp, d)), _const_spec((n_exp, 1)), _const_spec((tb, tb)),
        ],
        out_specs=(
            pl.BlockSpec((tb * chunks, LANES), lambda i: (i, 0)),
            topi_spec, topw_spec, rank_spec,
            pl.BlockSpec((n_exp, LANES), lambda i: (0, 0)),
        ),
        out_shape=(
            jax.ShapeDtypeStruct((n * chunks, LANES), jnp.uint32),
            topi_shape, topw_shape, rank_shape,
            jax.ShapeDtypeStruct((n_exp, LANES), F32),
        ),
        scratch_shapes=[pltpu.VMEM((n_exp, LANES), F32)],
        compiler_params=pltpu.CompilerParams(
            dimension_semantics=("arbitrary",), vmem_limit_bytes=VMEM_LIMIT_BYTES),
        name="router",
    )(*x1_parts, norm_ffn.reshape(1, d), w_router_t, b_router.reshape(n_exp, 1), tri)


def _expert_kernel(te_ref, tv_ref, nu_ref, xs_ref, wgu_ref, bgu_ref, wdn_ref, bdn_ref, ys_ref, *bf_refs,
                   tm, d, dff, round_weights):
    del nu_ref
    i = pl.program_id(0)
    if round_weights:
        wgu_bf, wdn_bf = bf_refs
        prev = te_ref[jnp.maximum(i - 1, 0)]

        @pl.when((i == 0) | (te_ref[i] != prev))
        def _():
            wgu_bf[...] = wgu_ref[...].astype(BF16)
            wdn_bf[...] = wdn_ref[...].astype(BF16)
    else:
        wgu_bf, wdn_bf = wgu_ref, wdn_ref

    xchunks = d // (2 * LANES)

    def ffn(row0, rows):
        window = pl.ds(row0 * xchunks, rows * xchunks)
        x = _unpack_halves(_load_rows(xs_ref.at[window], rows, xchunks)).astype(BF16)
        gu = _dot(x, wgu_bf[0]) + bgu_ref[0]
        gate = jnp.minimum(gu[:, :dff], SWIGLU_LIMIT)
        up = jnp.clip(gu[:, dff:], -SWIGLU_LIMIT, SWIGLU_LIMIT)
        hid = (up + 1.0) * gate * jax.nn.sigmoid(SWIGLU_ALPHA * gate)
        y = _dot(hid.astype(BF16), wdn_bf[0]) + bdn_ref[0]
        _store_rows(ys_ref.at[window], _pack_halves(y.astype(BF16).astype(F32)), rows, xchunks)

    sub = EXPERT_SUBTILE
    for s in range(tm // sub):
        valid = tv_ref[i] - s * sub
        pl.when(valid > sub // 2)(functools.partial(ffn, s * sub, sub))
        pl.when((valid > 0) & (valid <= sub // 2))(functools.partial(ffn, s * sub, sub // 2))


def _expert_call(tile_expert, tile_rows, n_used, xs, w_gu, b_gu, w_dn, b_dn, *, tm):
    n_exp, d, dff2 = w_gu.shape
    dff = dff2 // 2
    xchunks = d // (2 * LANES)
    p_rows = xs.shape[0] // xchunks
    round_weights = w_gu.dtype != BF16
    row_blk = lambda i, te, tv, nu: (jnp.minimum(i, nu[0] - 1), 0)
    exp_blk = lambda i, te, tv, nu: (te[i], 0, 0)
    out_specs = [pl.BlockSpec((tm * xchunks, LANES), row_blk)]
    out_shape = [jax.ShapeDtypeStruct((p_rows * xchunks, LANES), jnp.uint32)]
    if round_weights:
        out_specs += [pl.BlockSpec((1, d, dff2), exp_blk), pl.BlockSpec((1, dff, d), exp_blk)]
        out_shape += [jax.ShapeDtypeStruct(w_gu.shape, BF16), jax.ShapeDtypeStruct(w_dn.shape, BF16)]
    return pl.pallas_call(
        functools.partial(_expert_kernel, tm=tm, d=d, dff=dff, round_weights=round_weights),
        grid_spec=pltpu.PrefetchScalarGridSpec(
            num_scalar_prefetch=3,
            grid=(p_rows // tm,),
            in_specs=[
                pl.BlockSpec((tm * xchunks, LANES), row_blk),
                pl.BlockSpec((1, d, dff2), exp_blk),
                pl.BlockSpec((1, 1, dff2), exp_blk),
                pl.BlockSpec((1, dff, d), exp_blk),
                pl.BlockSpec((1, 1, d), exp_blk),
            ],
            out_specs=out_specs,
        ),
        out_shape=out_shape,
        compiler_params=pltpu.CompilerParams(
            dimension_semantics=("arbitrary",), vmem_limit_bytes=VMEM_LIMIT_BYTES),
        name="experts",
    )(tile_expert, tile_rows, n_used, xs, w_gu, b_gu.reshape(n_exp, 1, dff2), w_dn, b_dn.reshape(n_exp, 1, d))


def _final_kernel(*refs, tb, d, in_windows, out_windows, n_alias):
    n_in, n_out = len(in_windows), len(out_windows)
    x1_refs = refs[:n_in]
    yk_ref, w_ref, nfin_ref = refs[n_in:n_in + 3]
    out_refs = refs[n_in + 3 + n_alias:n_in + 3 + n_alias + n_out]
    i = pl.program_id(0)
    w = w_ref[...]
    chunks = d // (2 * LANES)
    moe = _unpack_halves(_load_rows(yk_ref.at[0], tb, chunks)) * w[:, 0:1]
    for k in range(1, TOP_K):
        moe = moe + _unpack_halves(_load_rows(yk_ref.at[k], tb, chunks)) * w[:, k:k + 1]
    out = _rms_norm(_window_load(i, in_windows, x1_refs) + moe, nfin_ref[...])
    _window_store(i, out_windows, out_refs, out)


def _final_call(x1_parts, in_windows, yk, topw_t, norm_final, out_shapes, out_windows, out_init, *, tb):
    d = x1_parts[0].shape[1]
    n_tiles = sum(w.count for w in in_windows)
    chunks = d // (2 * LANES)
    alias_in = [a for a in out_init if a is not None]
    n_fixed = len(x1_parts) + 3
    aliases, j = {}, 0
    for k, a in enumerate(out_init):
        if a is not None:
            aliases[n_fixed + j] = k
            j += 1
    return pl.pallas_call(
        functools.partial(_final_kernel, tb=tb, d=d, in_windows=tuple(in_windows),
                          out_windows=tuple(out_windows), n_alias=len(alias_in)),
        grid=(n_tiles,),
        in_specs=[
            *_window_specs(in_windows, tb, d),
            pl.BlockSpec((TOP_K, tb * chunks, LANES), lambda i: (0, i, 0)),
            pl.BlockSpec((tb, TOP_K), lambda i: (i, 0)),
            _const_spec((1, d)),
            *[pl.BlockSpec(memory_space=pl.ANY) for _ in alias_in],
        ],
        out_specs=_window_specs(out_windows, tb, d),
        out_shape=[jax.ShapeDtypeStruct(s, F32) for s in out_shapes],
        input_output_aliases=aliases,
        compiler_params=pltpu.CompilerParams(
            dimension_semantics=("arbitrary",), vmem_limit_bytes=VMEM_LIMIT_BYTES),
        name="final",
    )(*x1_parts, yk, topw_t, norm_final.reshape(1, d), *alias_in)


def _sc_step_rows(n):
    per_worker = n // SC_WORKERS
    assert per_worker * SC_WORKERS == n
    w = max(c for c in range(SUBLANES, SC_MAX_STEP_ROWS + 1, SUBLANES) if per_worker % c == 0)
    return w, per_worker // w


def _sc_mesh():
    return plsc.VectorSubcoreMesh(core_axis_name="c", subcore_axis_name="s",
                                  num_cores=SC_CORES, num_subcores=SC_SUBCORES)


def _dispatch_rows(x_tiles, pos_steps, p_rows):
    n, chunks, _ = x_tiles.shape
    w, steps = _sc_step_rows(n)

    @functools.partial(
        pl.kernel, mesh=_sc_mesh(),
        out_type=jax.ShapeDtypeStruct((p_rows, chunks, LANES), x_tiles.dtype),
        scratch_types=[pltpu.VMEM((w, chunks, LANES), x_tiles.dtype), pltpu.VMEM((TOP_K, w), jnp.int32)],
        name="dispatch",
    )
    def run(x_hbm, pos_hbm, out_hbm, buf, idx):
        wid = lax.axis_index("s") * SC_CORES + lax.axis_index("c")

        @pl.loop(0, steps)
        def _(i):
            step = wid * steps + i
            pltpu.sync_copy(x_hbm.at[pl.ds(pl.multiple_of(step * w, SUBLANES), w)], buf)
            pltpu.sync_copy(pos_hbm.at[step], idx)
            for k in range(TOP_K):
                pltpu.sync_copy(buf, out_hbm.at[idx.at[k]])

    return run(x_tiles, pos_steps)


def _combine_rows(y_tiles, pos_steps, n):
    _, chunks, _ = y_tiles.shape
    w, steps = _sc_step_rows(n)

    @functools.partial(
        pl.kernel, mesh=_sc_mesh(),
        out_type=jax.ShapeDtypeStruct((TOP_K, n, chunks, LANES), y_tiles.dtype),
        scratch_types=[pltpu.VMEM((w, chunks, LANES), y_tiles.dtype), pltpu.VMEM((TOP_K, w), jnp.int32)],
        name="combine",
    )
    def run(y_hbm, pos_hbm, out_hbm, buf, idx):
        wid = lax.axis_index("s") * SC_CORES + lax.axis_index("c")

        @pl.loop(0, steps)
        def _(i):
            step = wid * steps + i
            pltpu.sync_copy(pos_hbm.at[step], idx)
            for k in range(TOP_K):
                pltpu.sync_copy(y_hbm.at[idx.at[k]], buf)
                pltpu.sync_copy(buf, out_hbm.at[k, pl.ds(pl.multiple_of(step * w, SUBLANES), w)])

    return run(y_tiles, pos_steps)


def _block_diag(w, per_block):
    heads, n, _ = w.shape
    w4 = w.reshape(heads // per_block, per_block, n, n)
    eye = jnp.eye(per_block, dtype=w.dtype)
    bd = jnp.einsum('chij,hg->chigj', w4, eye)
    return bd.reshape(heads // per_block, per_block * n, per_block * n)


def _mixer_weights(norm_mix, w_in, b_gate, pool_w, pool_scale, conv_w, conv_b, lru_wa, lru_ba, lru_wi,
                   lru_bi, lru_lambda, w_br_pool, w_br_lru, w_out):
    row = lambda v: v.reshape(1, -1)
    d = norm_mix.shape[0]
    head_dim = lru_wa.shape[-1]
    per_block = MXU_WIDTH // head_dim
    w_ai = jnp.concatenate([_block_diag(lru_wa, per_block), _block_diag(lru_wi, per_block)], axis=-1)
    return (row(norm_mix), w_in.astype(BF16), row(b_gate), pool_w.astype(BF16), row(pool_scale),
            conv_w.reshape(-1, d // LANES, LANES), conv_b.reshape(d // LANES, LANES), w_ai.astype(BF16),
            row(lru_ba), row(lru_bi), row(lru_lambda),
            w_br_pool.astype(BF16), w_br_lru.astype(BF16), w_out.astype(BF16))


def _run_mixer(x, state_pool, state_conv, state_lru, weights, *, start_pos, bblk, tt, seq0=0):
    bsz, d = state_lru.shape
    chunks = d // LANES
    hp = jnp.pad(state_pool, ((0, 0), (POOL_PAD - POOL_HIST, 0), (0, 0)))
    hc = jnp.pad(state_conv, ((0, 0), (CONV_PAD - CONV_HIST, 0), (0, 0))).reshape(bsz, CONV_PAD * chunks, LANES)
    hl = state_lru.reshape(bsz, chunks, LANES)
    x1, npool, nconv, nlru = _mixer_call(x, hp, hc, hl, weights, start_pos=start_pos, bblk=bblk, tt=tt,
                                         seq0=seq0)
    return x1, npool, nconv.reshape(bsz, CONV_HIST, d), nlru.reshape(bsz, d)


def _routing_plan(topi, rank, counts, *, tm, n_tiles, min_tiles):
    n_exp = counts.shape[0]
    tiles_per = jnp.maximum((counts + tm - 1) // tm, min_tiles)
    tiles_cum = jnp.cumsum(tiles_per)
    tile_start = tiles_cum - tiles_per
    base = tile_start * tm
    experts = jnp.arange(n_exp, dtype=jnp.int32)[:, None, None]
    pos = rank + jnp.sum(jnp.where(topi[None] == experts, base[:, None, None], 0), axis=0)
    n_used = tiles_cum[-1]
    tile_ids = jnp.arange(n_tiles, dtype=jnp.int32)
    live = tile_ids < n_used
    owner = (tiles_cum[None, :] <= jnp.minimum(tile_ids, n_used - 1)[:, None])
    tile_expert = jnp.sum(owner.astype(jnp.int32), axis=1)
    is_owner = tile_expert[:, None] == jnp.arange(n_exp, dtype=jnp.int32)[None, :]
    rows_left = jnp.sum(jnp.where(is_owner, counts[None, :] - (tile_ids[:, None] - tile_start[None, :]) * tm, 0),
                        axis=1)
    tile_rows = jnp.where(live, jnp.clip(rows_left, 0, tm), 0).astype(jnp.int32)
    return pos, tile_expert, tile_rows, n_used.reshape(1).astype(jnp.int32)


def _moe_rows(x1_parts, windows, norm_ffn, w_router_t, b_router, w_gu, b_gu, w_dn, b_dn):
    d = x1_parts[0].shape[1]
    chunks = d // LANES
    n_exp = w_router_t.shape[0]
    n = sum(w.count for w in windows) * TOKEN_TILE
    xn2, topi, topw, rank, cnt = _router_call(x1_parts, windows, norm_ffn, w_router_t, b_router, tb=TOKEN_TILE)
    min_tiles = 0 if w_gu.dtype == BF16 else 1
    n_tiles = (n * TOP_K + n_exp * (EXPERT_TILE - 1)) // EXPERT_TILE + n_exp * min_tiles
    p_rows = n_tiles * EXPERT_TILE
    pos, tile_expert, tile_rows, n_used = _routing_plan(
        topi, rank, cnt[:, 0].astype(jnp.int32), tm=EXPERT_TILE, n_tiles=n_tiles, min_tiles=min_tiles)
    w, steps = _sc_step_rows(n)
    pos_steps = pos.reshape(TOP_K, SC_WORKERS * steps, w).transpose(1, 0, 2)
    xchunks = d // (2 * LANES)
    xs = _dispatch_rows(xn2.reshape(n, xchunks, LANES), pos_steps, p_rows)
    ys, *w_bf = _expert_call(tile_expert, tile_rows, n_used, xs.reshape(p_rows * xchunks, LANES), w_gu, b_gu,
                             w_dn, b_dn, tm=EXPERT_TILE)
    yk = _combine_rows(ys.reshape(p_rows, xchunks, LANES), pos_steps, n)
    return yk.reshape(TOP_K, n * xchunks, LANES), topw, (w_bf if w_bf else (w_gu, w_dn))


def kernel(x_prompt, x_sample, state_pool, state_conv, state_lru, norm_mix, w_in, b_gate, pool_w, pool_scale,
           conv_w, conv_b, lru_wa, lru_ba, lru_wi, lru_bi, lru_lambda, w_br_pool, w_br_lru, w_out, norm_ffn,
           w_router, b_router, w_gu, b_gu, w_dn, b_dn, norm_final):
    bp, tp, d = x_prompt.shape
    bs, ts, _ = x_sample.shape
    n_p, n_s = bp * tp, bs * ts
    mw = _mixer_weights(norm_mix[0], w_in[0], b_gate[0], pool_w[0], pool_scale[0], conv_w[0], conv_b[0],
                        lru_wa[0], lru_ba[0], lru_wi[0], lru_bi[0], lru_lambda[0], w_br_pool[0], w_br_lru[0],
                        w_out[0])
    w_router_t = w_router[0].T.astype(BF16)
    zeros = lambda *shape: jnp.zeros(shape, x_prompt.dtype)

    assert sum(PROMPT_GROUP_SEQS) == bp and tp % TOKEN_TILE == 0 and n_s % TOKEN_TILE == 0
    seq_tiles = tp // TOKEN_TILE
    s_tiles = n_s // TOKEN_TILE
    x1_s, pool_s, conv_s, lru_s = _run_mixer(
        x_sample, state_pool[0], state_conv[0], state_lru[0], mw, start_pos=PAST_LEN, bblk=bs, tt=ts)
    y_p, y_s = None, None
    expert_w = (w_gu[0], w_dn[0])
    pools, convs, lrus = [], [], []
    seq0 = 0
    for g, seqs in enumerate(PROMPT_GROUP_SEQS):
        x1_g, pool_g, conv_g, lru_g = _run_mixer(
            x_prompt, zeros(seqs, POOL_HIST, d), zeros(seqs, CONV_HIST, d), zeros(seqs, d), mw,
            start_pos=0, bblk=1, tt=MIXER_TILE, seq0=seq0)
        pools.append(pool_g)
        convs.append(conv_g)
        lrus.append(lru_g)
        last = g == len(PROMPT_GROUP_SEQS) - 1
        g_tiles = seqs * seq_tiles
        parts, windows = [x1_g.reshape(seqs * tp, d)], [_Window(0, g_tiles)]
        out_shapes, out_windows, out_init = [(n_p, d)], [_Window(seq0 * seq_tiles, g_tiles)], [y_p]
        seq0 += seqs
        if last:
            parts.append(x1_s.reshape(n_s, d))
            windows.append(_Window(0, s_tiles))
            out_shapes.append((n_s, d))
            out_windows.append(_Window(0, s_tiles))
            out_init.append(None)
        yk, topw, expert_w = _moe_rows(parts, windows, norm_ffn[0], w_router_t, b_router[0], expert_w[0],
                                       b_gu[0], expert_w[1], b_dn[0])
        outs = _final_call(parts, windows, yk, topw.T, norm_final, out_shapes, out_windows, out_init,
                           tb=TOKEN_TILE)
        y_p = outs[0]
        if last:
            y_s = outs[1]

    cat = lambda xs: jnp.concatenate(xs, axis=0)[None]
    return (y_p.reshape(bp, tp, d), y_s.reshape(bs, ts, d), cat(pools), cat(convs), cat(lrus),
            pool_s[None], conv_s[None], lru_s[None])
```

```python
import functools
from typing import NamedTuple

import jax
import jax.numpy as jnp
from jax import lax
from jax.experimental import pallas as pl
from jax.experimental.pallas import tpu as pltpu
from jax.experimental.pallas import tpu_sc as plsc

BF16 = jnp.bfloat16
F32 = jnp.float32

POOL_WINDOWS = (2, 4, 8, 16)
POOL_HIST = max(POOL_WINDOWS) - 1
CONV_WIDTH = 4
CONV_HIST = CONV_WIDTH - 1
LRU_C = 8.0
GATE_PIECES = 4
LRU_SEGMENTS = 8
TOP_K = 4
SWIGLU_LIMIT = 7.0
SWIGLU_ALPHA = 1.702
NORM_EPS = 1e-6

SUBLANES = 8
LANES = 128
MXU_WIDTH = 256
POOL_PAD = -(-POOL_HIST // SUBLANES) * SUBLANES
CONV_PAD = -(-CONV_HIST // SUBLANES) * SUBLANES
V7X_VMEM_BYTES = 64 * 1024 * 1024
VMEM_LIMIT_BYTES = V7X_VMEM_BYTES * 7 // 8

PAST_LEN = 1024
MIXER_TILE = 256
TOKEN_TILE = 512
EXPERT_TILE = 1024
EXPERT_SUBTILE = 512
PROMPT_GROUP_SEQS = (10, 6)

SC_CORES = 2
SC_SUBCORES = 16
SC_WORKERS = SC_CORES * SC_SUBCORES
SC_MAX_STEP_ROWS = 128


def _rms_norm(x, g):
    ms = jnp.mean(x * x, axis=-1, keepdims=True)
    return (x * lax.rsqrt(ms + NORM_EPS)) * g


def _dot(a, b):
    return jnp.dot(a, b, preferred_element_type=F32)


def _softplus(x):
    return jnp.maximum(x, 0.0) + jnp.log1p(jnp.exp(-jnp.abs(x)))


def _gelu_tanh(x):
    c = 0.7978845608028654
    return 0.5 * x * (1.0 + jnp.tanh(c * (x + 0.044715 * (x * x * x))))


def _lru_scan(a_ref, b_ref, h_ref, h0, frames, chunks):
    tile = lambda ref, t: ref[t * chunks:(t + 1) * chunks, :]
    seg = frames // LRU_SEGMENTS
    assert seg * LRU_SEGMENTS == frames
    prods = [tile(a_ref, s * seg) for s in range(LRU_SEGMENTS)]
    sums = [tile(b_ref, s * seg) for s in range(LRU_SEGMENTS)]
    for t in range(1, seg):
        for s in range(LRU_SEGMENTS):
            a_t = tile(a_ref, s * seg + t)
            sums[s] = a_t * sums[s] + tile(b_ref, s * seg + t)
            prods[s] = a_t * prods[s]
            b_ref[(s * seg + t) * chunks:(s * seg + t + 1) * chunks, :] = sums[s]
            a_ref[(s * seg + t) * chunks:(s * seg + t + 1) * chunks, :] = prods[s]
    h_in = [h0]
    for s in range(LRU_SEGMENTS):
        h_in.append(prods[s] * h_in[s] + sums[s])
    for s in range(LRU_SEGMENTS):
        for t in range(seg):
            f = s * seg + t
            h_ref[f * chunks:(f + 1) * chunks, :] = tile(a_ref, f) * h_in[s] + tile(b_ref, f)
    return h_in[LRU_SEGMENTS]


def _mixer_kernel(x_ref, hp_ref, hc_ref, hl_ref, nmix_ref, win_ref, bgate_ref, poolw_ref, pscale_ref,
                  convw_ref, convb_ref, wai_ref, ba_ref, bi_ref, lam_ref, wbrp_ref, wbrl_ref, wout_ref,
                  x1_ref, npool_ref, nconv_ref, nlru_ref,
                  pool_ext, conv_ext, h_carry, frames_a, frames_b, frames_h, *, start_pos, bblk, tt, d):
    j = pl.program_id(1)

    @pl.when(j == 0)
    def _():
        pool_ext[:, 0:POOL_PAD, :] = hp_ref[...]
        conv_ext[:, 0:CONV_PAD * (d // LANES), :] = hc_ref[...]
        h_carry[...] = hl_ref[...]

    rows = bblk * tt
    chunks = d // LANES
    cat_rows = lambda parts: jnp.concatenate(parts, axis=0) if len(parts) > 1 else parts[0]
    x = x_ref[...].reshape(rows, d)
    xnb = _rms_norm(x, nmix_ref[...]).astype(BF16)

    u_x = _dot(xnb, win_ref[:, d:2 * d])
    xc_rows = []
    for b in range(bblk):
        _store_rows(conv_ext.at[b, pl.ds(CONV_PAD * chunks, tt * chunks)], u_x[b * tt:(b + 1) * tt, :], tt, chunks)
        ce = conv_ext[b].reshape(CONV_PAD + tt, chunks, LANES)
        y = convb_ref[...] + ce[CONV_PAD:CONV_PAD + tt] * convw_ref[CONV_WIDTH - 1]
        for k in range(1, CONV_WIDTH):
            y = y + ce[CONV_PAD - k:CONV_PAD - k + tt] * convw_ref[CONV_WIDTH - 1 - k]
        frames_h[...] = y.reshape(tt * chunks, LANES)
        xc_rows.append(_load_rows(frames_h, tt, chunks))
        nconv_ref[b] = conv_ext[b, (CONV_PAD + tt - CONV_HIST) * chunks:(CONV_PAD + tt) * chunks, :]
        conv_ext[b, 0:CONV_PAD * chunks, :] = conv_ext[b, tt * chunks:(tt + CONV_PAD) * chunks, :]
    xc = cat_rows(xc_rows)

    n_chunks = d // MXU_WIDTH
    pre_a, pre_i = [], []
    for c in range(n_chunks):
        ai = _dot(xc[:, c * MXU_WIDTH:(c + 1) * MXU_WIDTH].astype(BF16), wai_ref[c])
        pre_a.append(ai[:, 0:MXU_WIDTH])
        pre_i.append(ai[:, MXU_WIDTH:2 * MXU_WIDTH])

    u_pool = _dot(xnb, win_ref[:, 0:d])
    pos1 = lax.broadcasted_iota(jnp.int32, (tt, 1), 0) + (start_pos + 1) + j * tt
    gd = d // len(POOL_WINDOWS)
    y_pool_rows = []
    for b in range(bblk):
        pool_ext[b, POOL_PAD:POOL_PAD + tt, :] = u_pool[b * tt:(b + 1) * tt, :]
        parts = []
        for g, w in enumerate(POOL_WINDOWS):
            sl = slice(g * gd, (g + 1) * gd)
            s = pool_ext[b, :, sl]
            width = 1
            while width < w:
                s = s + pltpu.roll(s, width, axis=0)
                width *= 2
            cur = s[POOL_PAD:POOL_PAD + tt, :]
            inv_cnt = 1.0 / jnp.minimum(pos1, w).astype(F32)
            pooled = cur * inv_cnt - u_pool[b * tt:(b + 1) * tt, sl]
            mixed = _dot(pooled.astype(BF16), poolw_ref[g])
            parts.append(mixed * pscale_ref[:, sl])
        y_pool_rows.append(jnp.concatenate(parts, axis=1))
        npool_ref[b] = pool_ext[b, POOL_PAD + tt - POOL_HIST:POOL_PAD + tt, :]
        pool_ext[b, 0:POOL_PAD, :] = pool_ext[b, tt:tt + POOL_PAD, :]
    y_pool = cat_rows(y_pool_rows)

    pre_a = jnp.concatenate(pre_a, axis=1) + ba_ref[...]
    pre_i = jnp.concatenate(pre_i, axis=1) + bi_ref[...]
    neg_c_softplus = (-LRU_C) * _softplus(-lam_ref[...])
    rp, cp = rows // GATE_PIECES, d // GATE_PIECES
    a_parts, bb_parts, glp_parts, ug_parts = [], [], [], []
    for p in range(GATE_PIECES):
        rs = slice(p * rp, (p + 1) * rp)
        log_a = jax.nn.sigmoid(pre_a[rs]) * neg_c_softplus
        a_p = jnp.exp(log_a)
        mult = jnp.sqrt(-jnp.tanh(log_a) * (a_p * a_p + 1.0))
        a_parts.append(a_p)
        bb_parts.append(mult * (jax.nn.sigmoid(pre_i[rs]) * xc[rs]))
        glp_parts.append(_dot(xnb, win_ref[:, 3 * d + p * cp:3 * d + (p + 1) * cp]))
        ug_parts.append(_dot(xnb, win_ref[:, 2 * d + p * cp:2 * d + (p + 1) * cp]))
    a = jnp.concatenate(a_parts, axis=0)
    bb = jnp.concatenate(bb_parts, axis=0)
    gl_pool = jnp.concatenate(glp_parts, axis=1)
    u_gate = jnp.concatenate(ug_parts, axis=1)
    h_rows = []
    for b in range(bblk):
        _store_rows(frames_a, a[b * tt:(b + 1) * tt, :], tt, chunks)
        _store_rows(frames_b, bb[b * tt:(b + 1) * tt, :], tt, chunks)
        h_last = _lru_scan(frames_a, frames_b, frames_h, h_carry[b], tt, chunks)
        h_carry[b] = h_last
        nlru_ref[b] = h_last
        h_rows.append(_load_rows(frames_h, tt, chunks))
    h = cat_rows(h_rows)
    br_pool = _dot(y_pool.astype(BF16), wbrp_ref[...])
    gl_lru = _dot(xnb, win_ref[:, 4 * d:5 * d])

    y_lru = h * _gelu_tanh(u_gate)
    br_lru = _dot(y_lru.astype(BF16), wbrl_ref[...])
    g_pool = jax.nn.sigmoid(gl_pool + bgate_ref[:, 0:d])
    g_lru = jax.nn.sigmoid(gl_lru + bgate_ref[:, d:2 * d])
    acc = g_pool * br_pool + g_lru * br_lru
    out = _dot(acc.astype(BF16), wout_ref[...])
    x1_ref[...] = (x + out).reshape(bblk, tt, d)


def _const_spec(shape):
    nd = len(shape)
    return pl.BlockSpec(shape, lambda *_: (0,) * nd, pipeline_mode=pl.Buffered(1))


def _mixer_call(x, hist_pool, hist_conv, hist_lru, weights, *, start_pos, bblk, tt, seq0=0):
    _, t_len, d = x.shape
    bsz = hist_pool.shape[0]
    assert seq0 % bblk == 0
    grid = (bsz // bblk, t_len // tt)
    kern = functools.partial(_mixer_kernel, start_pos=start_pos, bblk=bblk, tt=tt, d=d)
    chunks = d // LANES
    seq_spec = lambda rows: pl.BlockSpec((bblk, rows, d), lambda b, j: (b, 0, 0))
    frame_spec = lambda frames: pl.BlockSpec((bblk, frames * chunks, LANES), lambda b, j: (b, 0, 0))
    in_specs = [
        pl.BlockSpec((bblk, tt, d), lambda b, j: (b + seq0 // bblk, j, 0)),
        seq_spec(POOL_PAD), frame_spec(CONV_PAD), frame_spec(1),
    ] + [_const_spec(w.shape) for w in weights]
    out_shape = (
        jax.ShapeDtypeStruct((bsz, t_len, d), F32),
        jax.ShapeDtypeStruct((bsz, POOL_HIST, d), F32),
        jax.ShapeDtypeStruct((bsz, CONV_HIST * chunks, LANES), F32),
        jax.ShapeDtypeStruct((bsz, chunks, LANES), F32),
    )
    out_specs = (
        pl.BlockSpec((bblk, tt, d), lambda b, j: (b, j, 0)),
        seq_spec(POOL_HIST), frame_spec(CONV_HIST), frame_spec(1),
    )
    return pl.pallas_call(
        kern,
        grid=grid,
        in_specs=in_specs,
        out_specs=out_specs,
        out_shape=out_shape,
        scratch_shapes=[
            pltpu.VMEM((bblk, POOL_PAD + tt, d), F32),
            pltpu.VMEM((bblk, (CONV_PAD + tt) * chunks, LANES), F32),
            pltpu.VMEM((bblk, chunks, LANES), F32),
            pltpu.VMEM((tt * chunks, LANES), F32),
            pltpu.VMEM((tt * chunks, LANES), F32),
            pltpu.VMEM((tt * chunks, LANES), F32),
        ],
        compiler_params=pltpu.CompilerParams(
            dimension_semantics=("arbitrary", "arbitrary"), vmem_limit_bytes=VMEM_LIMIT_BYTES),
        name="mixer",
    )(x, hist_pool, hist_conv, hist_lru, *weights)


def _load_rows(ref, rows, chunks):
    return jnp.concatenate([ref[pl.ds(s, rows, stride=chunks), :] for s in range(chunks)], axis=1)


def _store_rows(ref, val, rows, chunks):
    for s in range(chunks):
        ref[pl.ds(s, rows, stride=chunks), :] = val[:, s * LANES:(s + 1) * LANES]


def _pack_halves(x):
    half = x.shape[1] // 2
    bits = lax.bitcast_convert_type(x, jnp.uint32)
    return (bits[:, :half] >> 16) | (bits[:, half:] & jnp.uint32(0xFFFF0000))


def _unpack_halves(p):
    lo = lax.bitcast_convert_type(p << 16, F32)
    hi = lax.bitcast_convert_type(p & jnp.uint32(0xFFFF0000), F32)
    return jnp.concatenate([lo, hi], axis=1)


class _Window(NamedTuple):
    first: int
    count: int


def _window_specs(windows, tb, d):
    specs, start = [], 0
    for win in windows:
        specs.append(pl.BlockSpec(
            (tb, d), lambda i, win=win, start=start: (win.first + jnp.clip(i - start, 0, win.count - 1), 0)))
        start += win.count
    return specs


def _window_load(i, windows, refs):
    val, start = refs[-1][...], sum(w.count for w in windows[:-1])
    for win, ref in zip(reversed(windows[:-1]), reversed(refs[:-1])):
        val = jnp.where(i < start, ref[...], val)
        start -= win.count
    return val


def _window_store(i, windows, refs, val):
    start = 0
    for win, ref in zip(windows, refs):
        @pl.when((i >= start) & (i < start + win.count))
        def _(ref=ref):
            ref[...] = val
        start += win.count


def _router_kernel(*refs, tb, d, windows):
    x1_refs = refs[:len(windows)]
    (nffn_ref, wrt_ref, br_ref, tri_ref, xn_ref, topi_ref, topw_ref, rank_ref, cnt_ref, carry) = refs[len(windows):]
    i = pl.program_id(0)

    @pl.when(i == 0)
    def _():
        carry[...] = jnp.zeros_like(carry)

    xnb = _rms_norm(_window_load(i, windows, x1_refs), nffn_ref[...]).astype(BF16)
    xn_ref_chunks = d // (2 * LANES)
    _store_rows(xn_ref, _pack_halves(xnb.astype(F32)), tb, xn_ref_chunks)
    logits = lax.dot_general(wrt_ref[...], xnb, (((1,), (1,)), ((), ())),
                             preferred_element_type=F32) + br_ref[...]
    n_exp = logits.shape[0]
    iota_e = lax.broadcasted_iota(jnp.int32, (n_exp, tb), 0)
    work = logits
    vals, idxs, hots = [], [], []
    for _ in range(TOP_K):
        m = jnp.max(work, axis=0, keepdims=True)
        idx = jnp.min(jnp.where(work == m, iota_e, n_exp), axis=0, keepdims=True)
        hot = iota_e == idx
        vals.append(m)
        idxs.append(idx)
        hots.append(hot)
        work = jnp.where(hot, -jnp.inf, work)
    exps = [jnp.exp(v - vals[0]) for v in vals]
    denom = exps[0] + exps[1] + exps[2] + exps[3]
    topi_ref[...] = jnp.concatenate(idxs, axis=0)
    topw_ref[...] = jnp.concatenate([e / denom for e in exps], axis=0)

    sel = sum(jnp.where(h, 1.0, 0.0) for h in hots)
    before = _dot(sel.astype(BF16), tri_ref[...]) + carry[:, 0:1]
    ranks = [jnp.sum(jnp.where(h, before, 0.0), axis=0, keepdims=True) for h in hots]
    rank_ref[...] = jnp.concatenate(ranks, axis=0).astype(jnp.int32)
    carry[...] = carry[...] + jnp.sum(sel, axis=1, keepdims=True)
    cnt_ref[...] = carry[...]


def _router_call(x1_parts, windows, norm_ffn, w_router_t, b_router, *, tb):
    d = x1_parts[0].shape[1]
    n_tiles = sum(w.count for w in windows)
    n = n_tiles * tb
    chunks = d // (2 * LANES)
    n_exp = w_router_t.shape[0]
    tri = jnp.triu(jnp.ones((tb, tb), BF16), k=1)
    tok = lambda rows, dt: (pl.BlockSpec((rows, tb), lambda i: (0, i)), jax.ShapeDtypeStruct((rows, n), dt))
    (topi_spec, topi_shape), (topw_spec, topw_shape), (rank_spec, rank_shape) = (
        tok(TOP_K, jnp.int32), tok(TOP_K, F32), tok(TOP_K, jnp.int32))
    return pl.pallas_call(
        functools.partial(_router_kernel, tb=tb, d=d, windows=tuple(windows)),
        grid=(n_tiles,),
        in_specs=[
            *_window_specs(windows, tb, d),
            _const_spec((1, d)), _const_spec((n_exp, d)), _const_spec((n_exp, 1)), _const_spec((tb, tb)),
        ],
        out_specs=(
            pl.BlockSpec((tb * chunks, LANES), lambda i: (i, 0)),
            topi_spec, topw_spec, rank_spec,
            pl.BlockSpec((n_exp, LANES), lambda i: (0, 0)),
        ),
        out_shape=(
            jax.ShapeDtypeStruct((n * chunks, LANES), jnp.uint32),
            topi_shape, topw_shape, rank_shape,
            jax.ShapeDtypeStruct((n_exp, LANES), F32),
        ),
        scratch_shapes=[pltpu.VMEM((n_exp, LANES), F32)],
        compiler_params=pltpu.CompilerParams(
            dimension_semantics=("arbitrary",), vmem_limit_bytes=VMEM_LIMIT_BYTES),
        name="router",
    )(*x1_parts, norm_ffn.reshape(1, d), w_router_t, b_router.reshape(n_exp, 1), tri)


def _expert_kernel(te_ref, tv_ref, nu_ref, xs_ref, wgu_ref, bgu_ref, wdn_ref, bdn_ref, ys_ref, *bf_refs,
                   tm, d, dff, round_weights):
    del nu_ref
    i = pl.program_id(0)
    if round_weights:
        wgu_bf, wdn_bf = bf_refs
        prev = te_ref[jnp.maximum(i - 1, 0)]

        @pl.when((i == 0) | (te_ref[i] != prev))
        def _():
            wgu_bf[...] = wgu_ref[...].astype(BF16)
            wdn_bf[...] = wdn_ref[...].astype(BF16)
    else:
        wgu_bf, wdn_bf = wgu_ref, wdn_ref

    xchunks = d // (2 * LANES)

    def ffn(row0, rows):
        window = pl.ds(row0 * xchunks, rows * xchunks)
        x = _unpack_halves(_load_rows(xs_ref.at[window], rows, xchunks)).astype(BF16)
        gu = _dot(x, wgu_bf[0]) + bgu_ref[0]
        gate = jnp.minimum(gu[:, :dff], SWIGLU_LIMIT)
        up = jnp.clip(gu[:, dff:], -SWIGLU_LIMIT, SWIGLU_LIMIT)
        hid = (up + 1.0) * gate * jax.nn.sigmoid(SWIGLU_ALPHA * gate)
        y = _dot(hid.astype(BF16), wdn_bf[0]) + bdn_ref[0]
        _store_rows(ys_ref.at[window], _pack_halves(y.astype(BF16).astype(F32)), rows, xchunks)

    sub = EXPERT_SUBTILE
    for s in range(tm // sub):
        valid = tv_ref[i] - s * sub
        pl.when(valid > sub // 2)(functools.partial(ffn, s * sub, sub))
        pl.when((valid > 0) & (valid <= sub // 2))(functools.partial(ffn, s * sub, sub // 2))


def _expert_call(tile_expert, tile_rows, n_used, xs, w_gu, b_gu, w_dn, b_dn, *, tm):
    n_exp, d, dff2 = w_gu.shape
    dff = dff2 // 2
    xchunks = d // (2 * LANES)
    p_rows = xs.shape[0] // xchunks
    round_weights = w_gu.dtype != BF16
    row_blk = lambda i, te, tv, nu: (jnp.minimum(i, nu[0] - 1), 0)
    exp_blk = lambda i, te, tv, nu: (te[i], 0, 0)
    out_specs = [pl.BlockSpec((tm * xchunks, LANES), row_blk)]
    out_shape = [jax.ShapeDtypeStruct((p_rows * xchunks, LANES), jnp.uint32)]
    if round_weights:
        out_specs += [pl.BlockSpec((1, d, dff2), exp_blk), pl.BlockSpec((1, dff, d), exp_blk)]
        out_shape += [jax.ShapeDtypeStruct(w_gu.shape, BF16), jax.ShapeDtypeStruct(w_dn.shape, BF16)]
    return pl.pallas_call(
        functools.partial(_expert_kernel, tm=tm, d=d, dff=dff, round_weights=round_weights),
        grid_spec=pltpu.PrefetchScalarGridSpec(
            num_scalar_prefetch=3,
            grid=(p_rows // tm,),
            in_specs=[
                pl.BlockSpec((tm * xchunks, LANES), row_blk),
                pl.BlockSpec((1, d, dff2), exp_blk),
                pl.BlockSpec((1, 1, dff2), exp_blk),
                pl.BlockSpec((1, dff, d), exp_blk),
                pl.BlockSpec((1, 1, d), exp_blk),
            ],
            out_specs=out_specs,
        ),
        out_shape=out_shape,
        compiler_params=pltpu.CompilerParams(
            dimension_semantics=("arbitrary",), vmem_limit_bytes=VMEM_LIMIT_BYTES),
        name="experts",
    )(tile_expert, tile_rows, n_used, xs, w_gu, b_gu.reshape(n_exp, 1, dff2), w_dn, b_dn.reshape(n_exp, 1, d))


def _final_kernel(*refs, tb, d, in_windows, out_windows, n_alias):
    n_in, n_out = len(in_windows), len(out_windows)
    x1_refs = refs[:n_in]
    yk_ref, w_ref, nfin_ref = refs[n_in:n_in + 3]
    out_refs = refs[n_in + 3 + n_alias:n_in + 3 + n_alias + n_out]
    i = pl.program_id(0)
    w = w_ref[...]
    chunks = d // (2 * LANES)
    moe = _unpack_halves(_load_rows(yk_ref.at[0], tb, chunks)) * w[:, 0:1]
    for k in range(1, TOP_K):
        moe = moe + _unpack_halves(_load_rows(yk_ref.at[k], tb, chunks)) * w[:, k:k + 1]
    out = _rms_norm(_window_load(i, in_windows, x1_refs) + moe, nfin_ref[...])
    _window_store(i, out_windows, out_refs, out)


def _final_call(x1_parts, in_windows, yk, topw_t, norm_final, out_shapes, out_windows, out_init, *, tb):
    d = x1_parts[0].shape[1]
    n_tiles = sum(w.count for w in in_windows)
    chunks = d // (2 * LANES)
    alias_in = [a for a in out_init if a is not None]
    n_fixed = len(x1_parts) + 3
    aliases, j = {}, 0
    for k, a in enumerate(out_init):
        if a is not None:
            aliases[n_fixed + j] = k
            j += 1
    return pl.pallas_call(
        functools.partial(_final_kernel, tb=tb, d=d, in_windows=tuple(in_windows),
                          out_windows=tuple(out_windows), n_alias=len(alias_in)),
        grid=(n_tiles,),
        in_specs=[
            *_window_specs(in_windows, tb, d),
            pl.BlockSpec((TOP_K, tb * chunks, LANES), lambda i: (0, i, 0)),
            pl.BlockSpec((tb, TOP_K), lambda i: (i, 0)),
            _const_spec((1, d)),
            *[pl.BlockSpec(memory_space=pl.ANY) for _ in alias_in],
        ],
        out_specs=_window_specs(out_windows, tb, d),
        out_shape=[jax.ShapeDtypeStruct(s, F32) for s in out_shapes],
        input_output_aliases=aliases,
        compiler_params=pltpu.CompilerParams(
            dimension_semantics=("arbitrary",), vmem_limit_bytes=VMEM_LIMIT_BYTES),
        name="final",
    )(*x1_parts, yk, topw_t, norm_final.reshape(1, d), *alias_in)


def _sc_step_rows(n):
    per_worker = n // SC_WORKERS
    assert per_worker * SC_WORKERS == n
    w = max(c for c in range(SUBLANES, SC_MAX_STEP_ROWS + 1, SUBLANES) if per_worker % c == 0)
    return w, per_worker // w


def _sc_mesh():
    return plsc.VectorSubcoreMesh(core_axis_name="c", subcore_axis_name="s",
                                  num_cores=SC_CORES, num_subcores=SC_SUBCORES)


def _dispatch_rows(x_tiles, pos_steps, p_rows):
    n, chunks, _ = x_tiles.shape
    w, steps = _sc_step_rows(n)

    @functools.partial(
        pl.kernel, mesh=_sc_mesh(),
        out_type=jax.ShapeDtypeStruct((p_rows, chunks, LANES), x_tiles.dtype),
        scratch_types=[pltpu.VMEM((w, chunks, LANES), x_tiles.dtype), pltpu.VMEM((TOP_K, w), jnp.int32)],
        name="dispatch",
    )
    def run(x_hbm, pos_hbm, out_hbm, buf, idx):
        wid = lax.axis_index("s") * SC_CORES + lax.axis_index("c")

        @pl.loop(0, steps)
        def _(i):
            step = wid * steps + i
            pltpu.sync_copy(x_hbm.at[pl.ds(pl.multiple_of(step * w, SUBLANES), w)], buf)
            pltpu.sync_copy(pos_hbm.at[step], idx)
            for k in range(TOP_K):
                pltpu.sync_copy(buf, out_hbm.at[idx.at[k]])

    return run(x_tiles, pos_steps)


def _combine_rows(y_tiles, pos_steps, n):
    _, chunks, _ = y_tiles.shape
    w, steps = _sc_step_rows(n)

    @functools.partial(
        pl.kernel, mesh=_sc_mesh(),
        out_type=jax.ShapeDtypeStruct((TOP_K, n, chunks, LANES), y_tiles.dtype),
        scratch_types=[pltpu.VMEM((w, chunks, LANES), y_tiles.dtype), pltpu.VMEM((TOP_K, w), jnp.int32)],
        name="combine",
    )
    def run(y_hbm, pos_hbm, out_hbm, buf, idx):
        wid = lax.axis_index("s") * SC_CORES + lax.axis_index("c")

        @pl.loop(0, steps)
        def _(i):
            step = wid * steps + i
            pltpu.sync_copy(pos_hbm.at[step], idx)
            for k in range(TOP_K):
                pltpu.sync_copy(y_hbm.at[idx.at[k]], buf)
                pltpu.sync_copy(buf, out_hbm.at[k, pl.ds(pl.multiple_of(step * w, SUBLANES), w)])

    return run(y_tiles, pos_steps)


def _block_diag(w, per_block):
    heads, n, _ = w.shape
    w4 = w.reshape(heads // per_block, per_block, n, n)
    eye = jnp.eye(per_block, dtype=w.dtype)
    bd = jnp.einsum('chij,hg->chigj', w4, eye)
    return bd.reshape(heads // per_block, per_block * n, per_block * n)


def _mixer_weights(norm_mix, w_in, b_gate, pool_w, pool_scale, conv_w, conv_b, lru_wa, lru_ba, lru_wi,
                   lru_bi, lru_lambda, w_br_pool, w_br_lru, w_out):
    row = lambda v: v.reshape(1, -1)
    d = norm_mix.shape[0]
    head_dim = lru_wa.shape[-1]
    per_block = MXU_WIDTH // head_dim
    w_ai = jnp.concatenate([_block_diag(lru_wa, per_block), _block_diag(lru_wi, per_block)], axis=-1)
    return (row(norm_mix), w_in.astype(BF16), row(b_gate), pool_w.astype(BF16), row(pool_scale),
            conv_w.reshape(-1, d // LANES, LANES), conv_b.reshape(d // LANES, LANES), w_ai.astype(BF16),
            row(lru_ba), row(lru_bi), row(lru_lambda),
            w_br_pool.astype(BF16), w_br_lru.astype(BF16), w_out.astype(BF16))


def _run_mixer(x, state_pool, state_conv, state_lru, weights, *, start_pos, bblk, tt, seq0=0):
    bsz, d = state_lru.shape
    chunks = d // LANES
    hp = jnp.pad(state_pool, ((0, 0), (POOL_PAD - POOL_HIST, 0), (0, 0)))
    hc = jnp.pad(state_conv, ((0, 0), (CONV_PAD - CONV_HIST, 0), (0, 0))).reshape(bsz, CONV_PAD * chunks, LANES)
    hl = state_lru.reshape(bsz, chunks, LANES)
    x1, npool, nconv, nlru = _mixer_call(x, hp, hc, hl, weights, start_pos=start_pos, bblk=bblk, tt=tt,
                                         seq0=seq0)
    return x1, npool, nconv.reshape(bsz, CONV_HIST, d), nlru.reshape(bsz, d)


def _routing_plan(topi, rank, counts, *, tm, n_tiles, min_tiles):
    n_exp = counts.shape[0]
    tiles_per = jnp.maximum((counts + tm - 1) // tm, min_tiles)
    tiles_cum = jnp.cumsum(tiles_per)
    tile_start = tiles_cum - tiles_per
    base = tile_start * tm
    experts = jnp.arange(n_exp, dtype=jnp.int32)[:, None, None]
    pos = rank + jnp.sum(jnp.where(topi[None] == experts, base[:, None, None], 0), axis=0)
    n_used = tiles_cum[-1]
    tile_ids = jnp.arange(n_tiles, dtype=jnp.int32)
    live = tile_ids < n_used
    owner = (tiles_cum[None, :] <= jnp.minimum(tile_ids, n_used - 1)[:, None])
    tile_expert = jnp.sum(owner.astype(jnp.int32), axis=1)
    is_owner = tile_expert[:, None] == jnp.arange(n_exp, dtype=jnp.int32)[None, :]
    rows_left = jnp.sum(jnp.where(is_owner, counts[None, :] - (tile_ids[:, None] - tile_start[None, :]) * tm, 0),
                        axis=1)
    tile_rows = jnp.where(live, jnp.clip(rows_left, 0, tm), 0).astype(jnp.int32)
    return pos, tile_expert, tile_rows, n_used.reshape(1).astype(jnp.int32)


def _moe_rows(x1_parts, windows, norm_ffn, w_router_t, b_router, w_gu, b_gu, w_dn, b_dn):
    d = x1_parts[0].shape[1]
    chunks = d // LANES
    n_exp = w_router_t.shape[0]
    n = sum(w.count for w in windows) * TOKEN_TILE
    xn2, topi, topw, rank, cnt = _router_call(x1_parts, windows, norm_ffn, w_router_t, b_router, tb=TOKEN_TILE)
    min_tiles = 0 if w_gu.dtype == BF16 else 1
    n_tiles = (n * TOP_K + n_exp * (EXPERT_TILE - 1)) // EXPERT_TILE + n_exp * min_tiles
    p_rows = n_tiles * EXPERT_TILE
    pos, tile_expert, tile_rows, n_used = _routing_plan(
        topi, rank, cnt[:, 0].astype(jnp.int32), tm=EXPERT_TILE, n_tiles=n_tiles, min_tiles=min_tiles)
    w, steps = _sc_step_rows(n)
    pos_steps = pos.reshape(TOP_K, SC_WORKERS * steps, w).transpose(1, 0, 2)
    xchunks = d // (2 * LANES)
    xs = _dispatch_rows(xn2.reshape(n, xchunks, LANES), pos_steps, p_rows)
    ys, *w_bf = _expert_call(tile_expert, tile_rows, n_used, xs.reshape(p_rows * xchunks, LANES), w_gu, b_gu,
                             w_dn, b_dn, tm=EXPERT_TILE)
    yk = _combine_rows(ys.reshape(p_rows, xchunks, LANES), pos_steps, n)
    return yk.reshape(TOP_K, n * xchunks, LANES), topw, (w_bf if w_bf else (w_gu, w_dn))


def kernel(x_prompt, x_sample, state_pool, state_conv, state_lru, norm_mix, w_in, b_gate, pool_w, pool_scale,
           conv_w, conv_b, lru_wa, lru_ba, lru_wi, lru_bi, lru_lambda, w_br_pool, w_br_lru, w_out, norm_ffn,
           w_router, b_router, w_gu, b_gu, w_dn, b_dn, norm_final):
    bp, tp, d = x_prompt.shape
    bs, ts, _ = x_sample.shape
    n_p, n_s = bp * tp, bs * ts
    mw = _mixer_weights(norm_mix[0], w_in[0], b_gate[0], pool_w[0], pool_scale[0], conv_w[0], conv_b[0],
                        lru_wa[0], lru_ba[0], lru_wi[0], lru_bi[0], lru_lambda[0], w_br_pool[0], w_br_lru[0],
                        w_out[0])
    w_router_t = w_router[0].T.astype(BF16)
    zeros = lambda *shape: jnp.zeros(shape, x_prompt.dtype)

    assert sum(PROMPT_GROUP_SEQS) == bp and tp % TOKEN_TILE == 0 and n_s % TOKEN_TILE == 0
    seq_tiles = tp // TOKEN_TILE
    s_tiles = n_s // TOKEN_TILE
    x1_s, pool_s, conv_s, lru_s = _run_mixer(
        x_sample, state_pool[0], state_conv[0], state_lru[0], mw, start_pos=PAST_LEN, bblk=bs, tt=ts)
    y_p, y_s = None, None
    expert_w = (w_gu[0], w_dn[0])
    pools, convs, lrus = [], [], []
    seq0 = 0
    for g, seqs in enumerate(PROMPT_GROUP_SEQS):
        x1_g, pool_g, conv_g, lru_g = _run_mixer(
            x_prompt, zeros(seqs, POOL_HIST, d), zeros(seqs, CONV_HIST, d), zeros(seqs, d), mw,
            start_pos=0, bblk=1, tt=MIXER_TILE, seq0=seq0)
        pools.append(pool_g)
        convs.append(conv_g)
        lrus.append(lru_g)
        last = g == len(PROMPT_GROUP_SEQS) - 1
        g_tiles = seqs * seq_tiles
        parts, windows = [x1_g.reshape(seqs * tp, d)], [_Window(0, g_tiles)]
        out_shapes, out_windows, out_init = [(n_p, d)], [_Window(seq0 * seq_tiles, g_tiles)], [y_p]
        seq0 += seqs
        if last:
            parts.append(x1_s.reshape(n_s, d))
            windows.append(_Window(0, s_tiles))
            out_shapes.append((n_s, d))
            out_windows.append(_Window(0, s_tiles))
            out_init.append(None)
        yk, topw, expert_w = _moe_rows(parts, windows, norm_ffn[0], w_router_t, b_router[0], expert_w[0],
                                       b_gu[0], expert_w[1], b_dn[0])
        outs = _final_call(parts, windows, yk, topw.T, norm_final, out_shapes, out_windows, out_init,
                           tb=TOKEN_TILE)
        y_p = outs[0]
        if last:
            y_s = outs[1]

    cat = lambda xs: jnp.concatenate(xs, axis=0)[None]
    return (y_p.reshape(bp, tp, d), y_s.reshape(bs, ts, d), cat(pools), cat(convs), cat(lrus),
            pool_s[None], conv_s[None], lru_s[None])
```

```python
import functools
from typing import NamedTuple

import jax
import jax.numpy as jnp
from jax import lax
from jax.experimental import pallas as pl
from jax.experimental.pallas import tpu as pltpu
from jax.experimental.pallas import tpu_sc as plsc

BF16 = jnp.bfloat16
F32 = jnp.float32

POOL_WINDOWS = (2, 4, 8, 16)
POOL_HIST = max(POOL_WINDOWS) - 1
CONV_WIDTH = 4
CONV_HIST = CONV_WIDTH - 1
LRU_C = 8.0
GATE_PIECES = 4
LRU_SEGMENTS = 8
TOP_K = 4
SWIGLU_LIMIT = 7.0
SWIGLU_ALPHA = 1.702
NORM_EPS = 1e-6

SUBLANES = 8
LANES = 128
MXU_WIDTH = 256
POOL_PAD = -(-POOL_HIST // SUBLANES) * SUBLANES
CONV_PAD = -(-CONV_HIST // SUBLANES) * SUBLANES
V7X_VMEM_BYTES = 64 * 1024 * 1024
VMEM_LIMIT_BYTES = V7X_VMEM_BYTES * 7 // 8

PAST_LEN = 1024
MIXER_TILE = 256
TOKEN_TILE = 512
EXPERT_TILE = 1024
EXPERT_SUBTILE = 512
PROMPT_GROUP_SEQS = (10, 6)

SC_CORES = 2
SC_SUBCORES = 16
SC_WORKERS = SC_CORES * SC_SUBCORES
SC_MAX_STEP_ROWS = 128


def _rms_norm(x, g):
    ms = jnp.mean(x * x, axis=-1, keepdims=True)
    return (x * lax.rsqrt(ms + NORM_EPS)) * g


def _dot(a, b):
    return jnp.dot(a, b, preferred_element_type=F32)


def _softplus(x):
    return jnp.maximum(x, 0.0) + jnp.log1p(jnp.exp(-jnp.abs(x)))


def _gelu_tanh(x):
    c = 0.7978845608028654
    return 0.5 * x * (1.0 + jnp.tanh(c * (x + 0.044715 * (x * x * x))))


def _lru_scan(a_ref, b_ref, h_ref, h0, frames, chunks):
    tile = lambda ref, t: ref[t * chunks:(t + 1) * chunks, :]
    seg = frames // LRU_SEGMENTS
    assert seg * LRU_SEGMENTS == frames
    prods = [tile(a_ref, s * seg) for s in range(LRU_SEGMENTS)]
    sums = [tile(b_ref, s * seg) for s in range(LRU_SEGMENTS)]
    for t in range(1, seg):
        for s in range(LRU_SEGMENTS):
            a_t = tile(a_ref, s * seg + t)
            sums[s] = a_t * sums[s] + tile(b_ref, s * seg + t)
            prods[s] = a_t * prods[s]
            b_ref[(s * seg + t) * chunks:(s * seg + t + 1) * chunks, :] = sums[s]
            a_ref[(s * seg + t) * chunks:(s * seg + t + 1) * chunks, :] = prods[s]
    h_in = [h0]
    for s in range(LRU_SEGMENTS):
        h_in.append(prods[s] * h_in[s] + sums[s])
    for s in range(LRU_SEGMENTS):
        for t in range(seg):
            f = s * seg + t
            h_ref[f * chunks:(f + 1) * chunks, :] = tile(a_ref, f) * h_in[s] + tile(b_ref, f)
    return h_in[LRU_SEGMENTS]


def _mixer_kernel(x_ref, hp_ref, hc_ref, hl_ref, nmix_ref, win_ref, bgate_ref, poolw_ref, pscale_ref,
                  convw_ref, convb_ref, wai_ref, ba_ref, bi_ref, lam_ref, wbrp_ref, wbrl_ref, wout_ref,
                  x1_ref, npool_ref, nconv_ref, nlru_ref,
                  pool_ext, conv_ext, h_carry, frames_a, frames_b, frames_h, *, start_pos, bblk, tt, d):
    j = pl.program_id(1)

    @pl.when(j == 0)
    def _():
        pool_ext[:, 0:POOL_PAD, :] = hp_ref[...]
        conv_ext[:, 0:CONV_PAD * (d // LANES), :] = hc_ref[...]
        h_carry[...] = hl_ref[...]

    rows = bblk * tt
    chunks = d // LANES
    cat_rows = lambda parts: jnp.concatenate(parts, axis=0) if len(parts) > 1 else parts[0]
    x = x_ref[...].reshape(rows, d)
    xnb = _rms_norm(x, nmix_ref[...]).astype(BF16)

    u_x = _dot(xnb, win_ref[:, d:2 * d])
    xc_rows = []
    for b in range(bblk):
        _store_rows(conv_ext.at[b, pl.ds(CONV_PAD * chunks, tt * chunks)], u_x[b * tt:(b + 1) * tt, :], tt, chunks)
        ce = conv_ext[b].reshape(CONV_PAD + tt, chunks, LANES)
        y = convb_ref[...] + ce[CONV_PAD:CONV_PAD + tt] * convw_ref[CONV_WIDTH - 1]
        for k in range(1, CONV_WIDTH):
            y = y + ce[CONV_PAD - k:CONV_PAD - k + tt] * convw_ref[CONV_WIDTH - 1 - k]
        frames_h[...] = y.reshape(tt * chunks, LANES)
        xc_rows.append(_load_rows(frames_h, tt, chunks))
        nconv_ref[b] = conv_ext[b, (CONV_PAD + tt - CONV_HIST) * chunks:(CONV_PAD + tt) * chunks, :]
        conv_ext[b, 0:CONV_PAD * chunks, :] = conv_ext[b, tt * chunks:(tt + CONV_PAD) * chunks, :]
    xc = cat_rows(xc_rows)

    n_chunks = d // MXU_WIDTH
    pre_a, pre_i = [], []
    for c in range(n_chunks):
        ai = _dot(xc[:, c * MXU_WIDTH:(c + 1) * MXU_WIDTH].astype(BF16), wai_ref[c])
        pre_a.append(ai[:, 0:MXU_WIDTH])
        pre_i.append(ai[:, MXU_WIDTH:2 * MXU_WIDTH])

    u_pool = _dot(xnb, win_ref[:, 0:d])
    pos1 = lax.broadcasted_iota(jnp.int32, (tt, 1), 0) + (start_pos + 1) + j * tt
    gd = d // len(POOL_WINDOWS)
    y_pool_rows = []
    for b in range(bblk):
        pool_ext[b, POOL_PAD:POOL_PAD + tt, :] = u_pool[b * tt:(b + 1) * tt, :]
        parts = []
        for g, w in enumerate(POOL_WINDOWS):
            sl = slice(g * gd, (g + 1) * gd)
            s = pool_ext[b, :, sl]
            width = 1
            while width < w:
                s = s + pltpu.roll(s, width, axis=0)
                width *= 2
            cur = s[POOL_PAD:POOL_PAD + tt, :]
            inv_cnt = 1.0 / jnp.minimum(pos1, w).astype(F32)
            pooled = cur * inv_cnt - u_pool[b * tt:(b + 1) * tt, sl]
            mixed = _dot(pooled.astype(BF16), poolw_ref[g])
            parts.append(mixed * pscale_ref[:, sl])
        y_pool_rows.append(jnp.concatenate(parts, axis=1))
        npool_ref[b] = pool_ext[b, POOL_PAD + tt - POOL_HIST:POOL_PAD + tt, :]
        pool_ext[b, 0:POOL_PAD, :] = pool_ext[b, tt:tt + POOL_PAD, :]
    y_pool = cat_rows(y_pool_rows)

    pre_a = jnp.concatenate(pre_a, axis=1) + ba_ref[...]
    pre_i = jnp.concatenate(pre_i, axis=1) + bi_ref[...]
    neg_c_softplus = (-LRU_C) * _softplus(-lam_ref[...])
    rp, cp = rows // GATE_PIECES, d // GATE_PIECES
    a_parts, bb_parts, glp_parts, ug_parts = [], [], [], []
    for p in range(GATE_PIECES):
        rs = slice(p * rp, (p + 1) * rp)
        log_a = jax.nn.sigmoid(pre_a[rs]) * neg_c_softplus
        a_p = jnp.exp(log_a)
        mult = jnp.sqrt(-jnp.tanh(log_a) * (a_p * a_p + 1.0))
        a_parts.append(a_p)
        bb_parts.append(mult * (jax.nn.sigmoid(pre_i[rs]) * xc[rs]))
        glp_parts.append(_dot(xnb, win_ref[:, 3 * d + p * cp:3 * d + (p + 1) * cp]))
        ug_parts.append(_dot(xnb, win_ref[:, 2 * d + p * cp:2 * d + (p + 1) * cp]))
    a = jnp.concatenate(a_parts, axis=0)
    bb = jnp.concatenate(bb_parts, axis=0)
    gl_pool = jnp.concatenate(glp_parts, axis=1)
    u_gate = jnp.concatenate(ug_parts, axis=1)
    h_rows = []
    for b in range(bblk):
        _store_rows(frames_a, a[b * tt:(b + 1) * tt, :], tt, chunks)
        _store_rows(frames_b, bb[b * tt:(b + 1) * tt, :], tt, chunks)
        h_last = _lru_scan(frames_a, frames_b, frames_h, h_carry[b], tt, chunks)
        h_carry[b] = h_last
        nlru_ref[b] = h_last
        h_rows.append(_load_rows(frames_h, tt, chunks))
    h = cat_rows(h_rows)
    br_pool = _dot(y_pool.astype(BF16), wbrp_ref[...])
    gl_lru = _dot(xnb, win_ref[:, 4 * d:5 * d])

    y_lru = h * _gelu_tanh(u_gate)
    br_lru = _dot(y_lru.astype(BF16), wbrl_ref[...])
    g_pool = jax.nn.sigmoid(gl_pool + bgate_ref[:, 0:d])
    g_lru = jax.nn.sigmoid(gl_lru + bgate_ref[:, d:2 * d])
    acc = g_pool * br_pool + g_lru * br_lru
    out = _dot(acc.astype(BF16), wout_ref[...])
    x1_ref[...] = (x + out).reshape(bblk, tt, d)


def _const_spec(shape):
    nd = len(shape)
    return pl.BlockSpec(shape, lambda *_: (0,) * nd, pipeline_mode=pl.Buffered(1))


def _mixer_call(x, hist_pool, hist_conv, hist_lru, weights, *, start_pos, bblk, tt, seq0=0):
    _, t_len, d = x.shape
    bsz = hist_pool.shape[0]
    assert seq0 % bblk == 0
    grid = (bsz // bblk, t_len // tt)
    kern = functools.partial(_mixer_kernel, start_pos=start_pos, bblk=bblk, tt=tt, d=d)
    chunks = d // LANES
    seq_spec = lambda rows: pl.BlockSpec((bblk, rows, d), lambda b, j: (b, 0, 0))
    frame_spec = lambda frames: pl.BlockSpec((bblk, frames * chunks, LANES), lambda b, j: (b, 0, 0))
    in_specs = [
        pl.BlockSpec((bblk, tt, d), lambda b, j: (b + seq0 // bblk, j, 0)),
        seq_spec(POOL_PAD), frame_spec(CONV_PAD), frame_spec(1),
    ] + [_const_spec(w.shape) for w in weights]
    out_shape = (
        jax.ShapeDtypeStruct((bsz, t_len, d), F32),
        jax.ShapeDtypeStruct((bsz, POOL_HIST, d), F32),
        jax.ShapeDtypeStruct((bsz, CONV_HIST * chunks, LANES), F32),
        jax.ShapeDtypeStruct((bsz, chunks, LANES), F32),
    )
    out_specs = (
        pl.BlockSpec((bblk, tt, d), lambda b, j: (b, j, 0)),
        seq_spec(POOL_HIST), frame_spec(CONV_HIST), frame_spec(1),
    )
    return pl.pallas_call(
        kern,
        grid=grid,
        in_specs=in_specs,
        out_specs=out_specs,
        out_shape=out_shape,
        scratch_shapes=[
            pltpu.VMEM((bblk, POOL_PAD + tt, d), F32),
            pltpu.VMEM((bblk, (CONV_PAD + tt) * chunks, LANES), F32),
            pltpu.VMEM((bblk, chunks, LANES), F32),
            pltpu.VMEM((tt * chunks, LANES), F32),
            pltpu.VMEM((tt * chunks, LANES), F32),
            pltpu.VMEM((tt * chunks, LANES), F32),
        ],
        compiler_params=pltpu.CompilerParams(
            dimension_semantics=("arbitrary", "arbitrary"), vmem_limit_bytes=VMEM_LIMIT_BYTES),
        name="mixer",
    )(x, hist_pool, hist_conv, hist_lru, *weights)


def _load_rows(ref, rows, chunks):
    return jnp.concatenate([ref[pl.ds(s, rows, stride=chunks), :] for s in range(chunks)], axis=1)


def _store_rows(ref, val, rows, chunks):
    for s in range(chunks):
        ref[pl.ds(s, rows, stride=chunks), :] = val[:, s * LANES:(s + 1) * LANES]


def _pack_halves(x):
    half = x.shape[1] // 2
    bits = lax.bitcast_convert_type(x, jnp.uint32)
    return (bits[:, :half] >> 16) | (bits[:, half:] & jnp.uint32(0xFFFF0000))


def _unpack_halves(p):
    lo = lax.bitcast_convert_type(p << 16, F32)
    hi = lax.bitcast_convert_type(p & jnp.uint32(0xFFFF0000), F32)
    return jnp.concatenate([lo, hi], axis=1)


class _Window(NamedTuple):
    first: int
    count: int


def _window_specs(windows, tb, d):
    specs, start = [], 0
    for win in windows:
        specs.append(pl.BlockSpec(
            (tb, d), lambda i, win=win, start=start: (win.first + jnp.clip(i - start, 0, win.count - 1), 0)))
        start += win.count
    return specs


def _window_load(i, windows, refs):
    val, start = refs[-1][...], sum(w.count for w in windows[:-1])
    for win, ref in zip(reversed(windows[:-1]), reversed(refs[:-1])):
        val = jnp.where(i < start, ref[...], val)
        start -= win.count
    return val


def _window_store(i, windows, refs, val):
    start = 0
    for win, ref in zip(windows, refs):
        @pl.when((i >= start) & (i < start + win.count))
        def _(ref=ref):
            ref[...] = val
        start += win.count


def _router_kernel(*refs, tb, d, windows):
    x1_refs = refs[:len(windows)]
    (nffn_ref, wrt_ref, br_ref, tri_ref, xn_ref, topi_ref, topw_ref, rank_ref, cnt_ref, carry) = refs[len(windows):]
    i = pl.program_id(0)

    @pl.when(i == 0)
    def _():
        carry[...] = jnp.zeros_like(carry)

    xnb = _rms_norm(_window_load(i, windows, x1_refs), nffn_ref[...]).astype(BF16)
    xn_ref_chunks = d // (2 * LANES)
    _store_rows(xn_ref, _pack_halves(xnb.astype(F32)), tb, xn_ref_chunks)
    logits = lax.dot_general(wrt_ref[...], xnb, (((1,), (1,)), ((), ())),
                             preferred_element_type=F32) + br_ref[...]
    n_exp = logits.shape[0]
    iota_e = lax.broadcasted_iota(jnp.int32, (n_exp, tb), 0)
    work = logits
    vals, idxs, hots = [], [], []
    for _ in range(TOP_K):
        m = jnp.max(work, axis=0, keepdims=True)
        idx = jnp.min(jnp.where(work == m, iota_e, n_exp), axis=0, keepdims=True)
        hot = iota_e == idx
        vals.append(m)
        idxs.append(idx)
        hots.append(hot)
        work = jnp.where(hot, -jnp.inf, work)
    exps = [jnp.exp(v - vals[0]) for v in vals]
    denom = exps[0] + exps[1] + exps[2] + exps[3]
    topi_ref[...] = jnp.concatenate(idxs, axis=0)
    topw_ref[...] = jnp.concatenate([e / denom for e in exps], axis=0)

    sel = sum(jnp.where(h, 1.0, 0.0) for h in hots)
    before = _dot(sel.astype(BF16), tri_ref[...]) + carry[:, 0:1]
    ranks = [jnp.sum(jnp.where(h, before, 0.0), axis=0, keepdims=True) for h in hots]
    rank_ref[...] = jnp.concatenate(ranks, axis=0).astype(jnp.int32)
    carry[...] = carry[...] + jnp.sum(sel, axis=1, keepdims=True)
    cnt_ref[...] = carry[...]


def _router_call(x1_parts, windows, norm_ffn, w_router_t, b_router, *, tb):
    d = x1_parts[0].shape[1]
    n_tiles = sum(w.count for w in windows)
    n = n_tiles * tb
    chunks = d // (2 * LANES)
    n_exp = w_router_t.shape[0]
    tri = jnp.triu(jnp.ones((tb, tb), BF16), k=1)
    tok = lambda rows, dt: (pl.BlockSpec((rows, tb), lambda i: (0, i)), jax.ShapeDtypeStruct((rows, n), dt))
    (topi_spec, topi_shape), (topw_spec, topw_shape), (rank_spec, rank_shape) = (
        tok(TOP_K, jnp.int32), tok(TOP_K, F32), tok(TOP_K, jnp.int32))
    return pl.pallas_call(
        functools.partial(_router_kernel, tb=tb, d=d, windows=tuple(windows)),
        grid=(n_tiles,),
        in_specs=[
            *_window_specs(windows, tb, d),
            _const_spec((1, d)), _const_spec((n_exp, d)), _const_spec((n_exp, 1)), _const_spec((tb, tb)),
        ],
        out_specs=(
            pl.BlockSpec((tb * chunks, LANES), lambda i: (i, 0)),
            topi_spec, topw_spec, rank_spec,
            pl.BlockSpec((n_exp, LANES), lambda i: (0, 0)),
        ),
        out_shape=(
            jax.ShapeDtypeStruct((n * chunks, LANES), jnp.uint32),
            topi_shape, topw_shape, rank_shape,
            jax.ShapeDtypeStruct((n_exp, LANES), F32),
        ),
        scratch_shapes=[pltpu.VMEM((n_exp, LANES), F32)],
        compiler_params=pltpu.CompilerParams(
            dimension_semantics=("arbitrary",), vmem_limit_bytes=VMEM_LIMIT_BYTES),
        name="router",
    )(*x1_parts, norm_ffn.reshape(1, d), w_router_t, b_router.reshape(n_exp, 1), tri)


def _expert_kernel(te_ref, tv_ref, nu_ref, vs_ref, vf_ref, vn_ref, xs_ref, wgu_ref, bgu_ref, wdn_ref, bdn_ref,
                   ys_ref, *extra, tm, d, dff, round_weights):
    del nu_ref
    i = pl.program_id(0)
    if round_weights:
        wgu_out, wdn_out = extra
        prev = te_ref[jnp.maximum(i - 1, 0)]

        @pl.when((i == 0) | (te_ref[i] != prev))
        def _():
            wgu_out[...] = wgu_ref[...].astype(BF16)
            wdn_out[...] = wdn_ref[...].astype(BF16)

        w_gate_up = lambda: wgu_out[0]
        w_down = lambda: wdn_out[0]
    else:
        wgu_buf, wdn_buf, sems = extra
        slot = vs_ref[i]

        def copies(expert, into):
            return (pltpu.make_async_copy(wgu_ref.at[expert], wgu_buf.at[into], sems.at[0, into]),
                    pltpu.make_async_copy(wdn_ref.at[expert], wdn_buf.at[into], sems.at[1, into]))

        @pl.when(i == 0)
        def _():
            for c in copies(te_ref[0], slot):
                c.start()

        @pl.when(vf_ref[i] == 1)
        def _():
            for c in copies(te_ref[i], slot):
                c.wait()

            @pl.when(vn_ref[i] >= 0)
            def _():
                for c in copies(vn_ref[i], 1 - slot):
                    c.start()

        w_gate_up = lambda: wgu_buf[slot]
        w_down = lambda: wdn_buf[slot]

    xchunks = d // (2 * LANES)

    def ffn(row0, rows):
        window = pl.ds(row0 * xchunks, rows * xchunks)
        x = _unpack_halves(_load_rows(xs_ref.at[window], rows, xchunks)).astype(BF16)
        gu = _dot(x, w_gate_up()) + bgu_ref[0]
        gate = jnp.minimum(gu[:, :dff], SWIGLU_LIMIT)
        up = jnp.clip(gu[:, dff:], -SWIGLU_LIMIT, SWIGLU_LIMIT)
        hid = (up + 1.0) * gate * jax.nn.sigmoid(SWIGLU_ALPHA * gate)
        y = _dot(hid.astype(BF16), w_down()) + bdn_ref[0]
        _store_rows(ys_ref.at[window], _pack_halves(y.astype(BF16).astype(F32)), rows, xchunks)

    sub = EXPERT_SUBTILE
    for s in range(tm // sub):
        valid = tv_ref[i] - s * sub
        pl.when(valid > sub // 2)(functools.partial(ffn, s * sub, sub))
        pl.when((valid > 0) & (valid <= sub // 2))(functools.partial(ffn, s * sub, sub // 2))


def _expert_visits(tile_expert, n_used):
    n_tiles = tile_expert.shape[0]
    ids = jnp.arange(n_tiles, dtype=jnp.int32)
    live = ids < n_used[0]
    changed = jnp.concatenate([jnp.ones((1,), bool), tile_expert[1:] != tile_expert[:-1]])
    first = live & changed
    slot = (jnp.cumsum(first.astype(jnp.int32)) - 1) % 2
    later_first = first[None, :] & (ids[None, :] > ids[:, None])
    nxt_pos = jnp.min(jnp.where(later_first, ids[None, :], n_tiles), axis=1)
    nxt = jnp.sum(jnp.where(ids[None, :] == nxt_pos[:, None], tile_expert[None, :], 0), axis=1)
    nxt = jnp.where(nxt_pos < n_tiles, nxt, -1)
    return slot.astype(jnp.int32), first.astype(jnp.int32), nxt.astype(jnp.int32)


def _expert_call(tile_expert, tile_rows, n_used, xs, w_gu, b_gu, w_dn, b_dn, *, tm):
    n_exp, d, dff2 = w_gu.shape
    dff = dff2 // 2
    xchunks = d // (2 * LANES)
    p_rows = xs.shape[0] // xchunks
    round_weights = w_gu.dtype != BF16
    row_blk = lambda i, te, tv, nu, *_: (jnp.minimum(i, nu[0] - 1), 0)
    exp_blk = lambda i, te, *_: (te[i], 0, 0)
    out_specs = [pl.BlockSpec((tm * xchunks, LANES), row_blk)]
    out_shape = [jax.ShapeDtypeStruct((p_rows * xchunks, LANES), jnp.uint32)]
    if round_weights:
        w_specs = [pl.BlockSpec((1, d, dff2), exp_blk), pl.BlockSpec((1, dff, d), exp_blk)]
        out_specs += [pl.BlockSpec((1, d, dff2), exp_blk), pl.BlockSpec((1, dff, d), exp_blk)]
        out_shape += [jax.ShapeDtypeStruct(w_gu.shape, BF16), jax.ShapeDtypeStruct(w_dn.shape, BF16)]
        scratch = []
    else:
        w_specs = [pl.BlockSpec(memory_space=pl.ANY), pl.BlockSpec(memory_space=pl.ANY)]
        scratch = [pltpu.VMEM((2, d, dff2), BF16), pltpu.VMEM((2, dff, d), BF16), pltpu.SemaphoreType.DMA((2, 2))]
    visits = _expert_visits(tile_expert, n_used)
    return pl.pallas_call(
        functools.partial(_expert_kernel, tm=tm, d=d, dff=dff, round_weights=round_weights),
        grid_spec=pltpu.PrefetchScalarGridSpec(
            num_scalar_prefetch=6,
            grid=(p_rows // tm,),
            in_specs=[
                pl.BlockSpec((tm * xchunks, LANES), row_blk),
                w_specs[0],
                pl.BlockSpec((1, 1, dff2), exp_blk),
                w_specs[1],
                pl.BlockSpec((1, 1, d), exp_blk),
            ],
            out_specs=out_specs,
            scratch_shapes=scratch,
        ),
        out_shape=out_shape,
        compiler_params=pltpu.CompilerParams(
            dimension_semantics=("arbitrary",), vmem_limit_bytes=VMEM_LIMIT_BYTES),
        name="experts",
    )(tile_expert, tile_rows, n_used, *visits, xs, w_gu, b_gu.reshape(n_exp, 1, dff2), w_dn,
      b_dn.reshape(n_exp, 1, d))


def _final_kernel(*refs, tb, d, in_windows, out_windows, n_alias):
    n_in, n_out = len(in_windows), len(out_windows)
    x1_refs = refs[:n_in]
    yk_ref, w_ref, nfin_ref = refs[n_in:n_in + 3]
    out_refs = refs[n_in + 3 + n_alias:n_in + 3 + n_alias + n_out]
    i = pl.program_id(0)
    w = w_ref[...]
    chunks = d // (2 * LANES)
    moe = _unpack_halves(_load_rows(yk_ref.at[0], tb, chunks)) * w[:, 0:1]
    for k in range(1, TOP_K):
        moe = moe + _unpack_halves(_load_rows(yk_ref.at[k], tb, chunks)) * w[:, k:k + 1]
    out = _rms_norm(_window_load(i, in_windows, x1_refs) + moe, nfin_ref[...])
    _window_store(i, out_windows, out_refs, out)


def _final_call(x1_parts, in_windows, yk, topw_t, norm_final, out_shapes, out_windows, out_init, *, tb):
    d = x1_parts[0].shape[1]
    n_tiles = sum(w.count for w in in_windows)
    chunks = d // (2 * LANES)
    alias_in = [a for a in out_init if a is not None]
    n_fixed = len(x1_parts) + 3
    aliases, j = {}, 0
    for k, a in enumerate(out_init):
        if a is not None:
            aliases[n_fixed + j] = k
            j += 1
    return pl.pallas_call(
        functools.partial(_final_kernel, tb=tb, d=d, in_windows=tuple(in_windows),
                          out_windows=tuple(out_windows), n_alias=len(alias_in)),
        grid=(n_tiles,),
        in_specs=[
            *_window_specs(in_windows, tb, d),
            pl.BlockSpec((TOP_K, tb * chunks, LANES), lambda i: (0, i, 0)),
            pl.BlockSpec((tb, TOP_K), lambda i: (i, 0)),
            _const_spec((1, d)),
            *[pl.BlockSpec(memory_space=pl.ANY) for _ in alias_in],
        ],
        out_specs=_window_specs(out_windows, tb, d),
        out_shape=[jax.ShapeDtypeStruct(s, F32) for s in out_shapes],
        input_output_aliases=aliases,
        compiler_params=pltpu.CompilerParams(
            dimension_semantics=("arbitrary",), vmem_limit_bytes=VMEM_LIMIT_BYTES),
        name="final",
    )(*x1_parts, yk, topw_t, norm_final.reshape(1, d), *alias_in)


def _sc_step_rows(n):
    per_worker = n // SC_WORKERS
    assert per_worker * SC_WORKERS == n
    w = max(c for c in range(SUBLANES, SC_MAX_STEP_ROWS + 1, SUBLANES) if per_worker % c == 0)
    return w, per_worker // w


def _sc_mesh():
    return plsc.VectorSubcoreMesh(core_axis_name="c", subcore_axis_name="s",
                                  num_cores=SC_CORES, num_subcores=SC_SUBCORES)


def _dispatch_rows(x_tiles, pos_steps, p_rows):
    n, chunks, _ = x_tiles.shape
    w, steps = _sc_step_rows(n)

    @functools.partial(
        pl.kernel, mesh=_sc_mesh(),
        out_type=jax.ShapeDtypeStruct((p_rows, chunks, LANES), x_tiles.dtype),
        scratch_types=[pltpu.VMEM((w, chunks, LANES), x_tiles.dtype), pltpu.VMEM((TOP_K, w), jnp.int32)],
        name="dispatch",
    )
    def run(x_hbm, pos_hbm, out_hbm, buf, idx):
        wid = lax.axis_index("s") * SC_CORES + lax.axis_index("c")

        @pl.loop(0, steps)
        def _(i):
            step = wid * steps + i
            pltpu.sync_copy(x_hbm.at[pl.ds(pl.multiple_of(step * w, SUBLANES), w)], buf)
            pltpu.sync_copy(pos_hbm.at[step], idx)
            for k in range(TOP_K):
                pltpu.sync_copy(buf, out_hbm.at[idx.at[k]])

    return run(x_tiles, pos_steps)


def _combine_rows(y_tiles, pos_steps, n):
    _, chunks, _ = y_tiles.shape
    w, steps = _sc_step_rows(n)

    @functools.partial(
        pl.kernel, mesh=_sc_mesh(),
        out_type=jax.ShapeDtypeStruct((TOP_K, n, chunks, LANES), y_tiles.dtype),
        scratch_types=[pltpu.VMEM((w, chunks, LANES), y_tiles.dtype), pltpu.VMEM((TOP_K, w), jnp.int32)],
        name="combine",
    )
    def run(y_hbm, pos_hbm, out_hbm, buf, idx):
        wid = lax.axis_index("s") * SC_CORES + lax.axis_index("c")

        @pl.loop(0, steps)
        def _(i):
            step = wid * steps + i
            pltpu.sync_copy(pos_hbm.at[step], idx)
            for k in range(TOP_K):
                pltpu.sync_copy(y_hbm.at[idx.at[k]], buf)
                pltpu.sync_copy(buf, out_hbm.at[k, pl.ds(pl.multiple_of(step * w, SUBLANES), w)])

    return run(y_tiles, pos_steps)


def _block_diag(w, per_block):
    heads, n, _ = w.shape
    w4 = w.reshape(heads // per_block, per_block, n, n)
    eye = jnp.eye(per_block, dtype=w.dtype)
    bd = jnp.einsum('chij,hg->chigj', w4, eye)
    return bd.reshape(heads // per_block, per_block * n, per_block * n)


def _mixer_weights(norm_mix, w_in, b_gate, pool_w, pool_scale, conv_w, conv_b, lru_wa, lru_ba, lru_wi,
                   lru_bi, lru_lambda, w_br_pool, w_br_lru, w_out):
    row = lambda v: v.reshape(1, -1)
    d = norm_mix.shape[0]
    head_dim = lru_wa.shape[-1]
    per_block = MXU_WIDTH // head_dim
    w_ai = jnp.concatenate([_block_diag(lru_wa, per_block), _block_diag(lru_wi, per_block)], axis=-1)
    return (row(norm_mix), w_in.astype(BF16), row(b_gate), pool_w.astype(BF16), row(pool_scale),
            conv_w.reshape(-1, d // LANES, LANES), conv_b.reshape(d // LANES, LANES), w_ai.astype(BF16),
            row(lru_ba), row(lru_bi), row(lru_lambda),
            w_br_pool.astype(BF16), w_br_lru.astype(BF16), w_out.astype(BF16))


def _run_mixer(x, state_pool, state_conv, state_lru, weights, *, start_pos, bblk, tt, seq0=0):
    bsz, d = state_lru.shape
    chunks = d // LANES
    hp = jnp.pad(state_pool, ((0, 0), (POOL_PAD - POOL_HIST, 0), (0, 0)))
    hc = jnp.pad(state_conv, ((0, 0), (CONV_PAD - CONV_HIST, 0), (0, 0))).reshape(bsz, CONV_PAD * chunks, LANES)
    hl = state_lru.reshape(bsz, chunks, LANES)
    x1, npool, nconv, nlru = _mixer_call(x, hp, hc, hl, weights, start_pos=start_pos, bblk=bblk, tt=tt,
                                         seq0=seq0)
    return x1, npool, nconv.reshape(bsz, CONV_HIST, d), nlru.reshape(bsz, d)


def _routing_plan(topi, rank, counts, *, tm, n_tiles, min_tiles):
    n_exp = counts.shape[0]
    tiles_per = jnp.maximum((counts + tm - 1) // tm, min_tiles)
    tiles_cum = jnp.cumsum(tiles_per)
    tile_start = tiles_cum - tiles_per
    base = tile_start * tm
    experts = jnp.arange(n_exp, dtype=jnp.int32)[:, None, None]
    pos = rank + jnp.sum(jnp.where(topi[None] == experts, base[:, None, None], 0), axis=0)
    n_used = tiles_cum[-1]
    tile_ids = jnp.arange(n_tiles, dtype=jnp.int32)
    live = tile_ids < n_used
    owner = (tiles_cum[None, :] <= jnp.minimum(tile_ids, n_used - 1)[:, None])
    tile_expert = jnp.sum(owner.astype(jnp.int32), axis=1)
    is_owner = tile_expert[:, None] == jnp.arange(n_exp, dtype=jnp.int32)[None, :]
    rows_left = jnp.sum(jnp.where(is_owner, counts[None, :] - (tile_ids[:, None] - tile_start[None, :]) * tm, 0),
                        axis=1)
    tile_rows = jnp.where(live, jnp.clip(rows_left, 0, tm), 0).astype(jnp.int32)
    return pos, tile_expert, tile_rows, n_used.reshape(1).astype(jnp.int32)


def _moe_rows(x1_parts, windows, norm_ffn, w_router_t, b_router, w_gu, b_gu, w_dn, b_dn):
    d = x1_parts[0].shape[1]
    chunks = d // LANES
    n_exp = w_router_t.shape[0]
    n = sum(w.count for w in windows) * TOKEN_TILE
    xn2, topi, topw, rank, cnt = _router_call(x1_parts, windows, norm_ffn, w_router_t, b_router, tb=TOKEN_TILE)
    min_tiles = 0 if w_gu.dtype == BF16 else 1
    n_tiles = (n * TOP_K + n_exp * (EXPERT_TILE - 1)) // EXPERT_TILE + n_exp * min_tiles
    p_rows = n_tiles * EXPERT_TILE
    pos, tile_expert, tile_rows, n_used = _routing_plan(
        topi, rank, cnt[:, 0].astype(jnp.int32), tm=EXPERT_TILE, n_tiles=n_tiles, min_tiles=min_tiles)
    w, steps = _sc_step_rows(n)
    pos_steps = pos.reshape(TOP_K, SC_WORKERS * steps, w).transpose(1, 0, 2)
    xchunks = d // (2 * LANES)
    xs = _dispatch_rows(xn2.reshape(n, xchunks, LANES), pos_steps, p_rows)
    ys, *w_bf = _expert_call(tile_expert, tile_rows, n_used, xs.reshape(p_rows * xchunks, LANES), w_gu, b_gu,
                             w_dn, b_dn, tm=EXPERT_TILE)
    yk = _combine_rows(ys.reshape(p_rows, xchunks, LANES), pos_steps, n)
    return yk.reshape(TOP_K, n * xchunks, LANES), topw, (w_bf if w_bf else (w_gu, w_dn))


def kernel(x_prompt, x_sample, state_pool, state_conv, state_lru, norm_mix, w_in, b_gate, pool_w, pool_scale,
           conv_w, conv_b, lru_wa, lru_ba, lru_wi, lru_bi, lru_lambda, w_br_pool, w_br_lru, w_out, norm_ffn,
           w_router, b_router, w_gu, b_gu, w_dn, b_dn, norm_final):
    bp, tp, d = x_prompt.shape
    bs, ts, _ = x_sample.shape
    n_p, n_s = bp * tp, bs * ts
    mw = _mixer_weights(norm_mix[0], w_in[0], b_gate[0], pool_w[0], pool_scale[0], conv_w[0], conv_b[0],
                        lru_wa[0], lru_ba[0], lru_wi[0], lru_bi[0], lru_lambda[0], w_br_pool[0], w_br_lru[0],
                        w_out[0])
    w_router_t = w_router[0].T.astype(BF16)
    zeros = lambda *shape: jnp.zeros(shape, x_prompt.dtype)

    assert sum(PROMPT_GROUP_SEQS) == bp and tp % TOKEN_TILE == 0 and n_s % TOKEN_TILE == 0
    seq_tiles = tp // TOKEN_TILE
    s_tiles = n_s // TOKEN_TILE
    x1_s, pool_s, conv_s, lru_s = _run_mixer(
        x_sample, state_pool[0], state_conv[0], state_lru[0], mw, start_pos=PAST_LEN, bblk=bs, tt=ts)
    y_p, y_s = None, None
    expert_w = (w_gu[0], w_dn[0])
    pools, convs, lrus = [], [], []
    seq0 = 0
    for g, seqs in enumerate(PROMPT_GROUP_SEQS):
        x1_g, pool_g, conv_g, lru_g = _run_mixer(
            x_prompt, zeros(seqs, POOL_HIST, d), zeros(seqs, CONV_HIST, d), zeros(seqs, d), mw,
            start_pos=0, bblk=1, tt=MIXER_TILE, seq0=seq0)
        pools.append(pool_g)
        convs.append(conv_g)
        lrus.append(lru_g)
        last = g == len(PROMPT_GROUP_SEQS) - 1
        g_tiles = seqs * seq_tiles
        parts, windows = [x1_g.reshape(seqs * tp, d)], [_Window(0, g_tiles)]
        out_shapes, out_windows, out_init = [(n_p, d)], [_Window(seq0 * seq_tiles, g_tiles)], [y_p]
        seq0 += seqs
        if last:
            parts.append(x1_s.reshape(n_s, d))
            windows.append(_Window(0, s_tiles))
            out_shapes.append((n_s, d))
            out_windows.append(_Window(0, s_tiles))
            out_init.append(None)
        yk, topw, expert_w = _moe_rows(parts, windows, norm_ffn[0], w_router_t, b_router[0], expert_w[0],
                                       b_gu[0], expert_w[1], b_dn[0])
        outs = _final_call(parts, windows, yk, topw.T, norm_final, out_shapes, out_windows, out_init,
                           tb=TOKEN_TILE)
        y_p = outs[0]
        if last:
            y_s = outs[1]

    cat = lambda xs: jnp.concatenate(xs, axis=0)[None]
    return (y_p.reshape(bp, tp, d), y_s.reshape(bs, ts, d), cat(pools), cat(convs), cat(lrus),
            pool_s[None], conv_s[None], lru_s[None])
```
